```python
import jax, jax.numpy as jnp
from jax import lax
import numpy as np

D_MODEL = 1024
BATCH = 2
SEQ = 8192
DEPTH = 1

CHUNK = 64
PLE_DIM = 256
HEAD_DIM = 64
SWA_HEADS = 8
SWA_KV_HEADS = 2
SWA_GROUP = SWA_HEADS // SWA_KV_HEADS
WINDOW = 128
SWA_BLOCK = WINDOW
FOX_HEADS = 8
FOX_BLOCK = 128
D_FF = 4 * D_MODEL
N_BRANCH = 2
RMS_EPS = 1e-6

SWA_Q = SWA_HEADS * HEAD_DIM
SWA_KV = SWA_KV_HEADS * HEAD_DIM
FOX_W = FOX_HEADS * HEAD_DIM
GATE_W = N_BRANCH * D_MODEL
SPLIT_POINTS = tuple(np.cumsum([SWA_Q, SWA_KV, SWA_KV, FOX_W, FOX_W, FOX_W, FOX_HEADS]).tolist())
D_IN = SPLIT_POINTS[-1] + GATE_W

kernel_name = "hybrid_swa_sink_fox_gated_block"


def alibi_slopes(n_heads):
    return jnp.asarray(np.array([2.0 ** (-8.0 * (h + 1) / n_heads) for h in range(n_heads)], dtype=np.float32))


def rms_norm(x, g):
    x32 = x.astype(jnp.float32)
    y = x32 * lax.rsqrt(jnp.mean(jnp.square(x32), axis=-1, keepdims=True) + RMS_EPS)
    return (y * g.astype(jnp.float32)).astype(x.dtype)


def sliding_window_attention(q, k, v, sinks):
    B, S = q.shape[0], q.shape[1]
    nb = S // SWA_BLOCK
    SB = SWA_BLOCK
    qb = q.reshape(B, nb, SB, SWA_KV_HEADS, SWA_GROUP, HEAD_DIM)
    kb = k.reshape(B, nb, SB, SWA_KV_HEADS, HEAD_DIM)
    vb = v.reshape(B, nb, SB, SWA_KV_HEADS, HEAD_DIM)

    def band(t):
        prev = jnp.pad(t[:, :-1], ((0, 0), (1, 0), (0, 0), (0, 0), (0, 0)))
        return jnp.concatenate([prev, t], axis=2)

    k_band, v_band = band(kb), band(vb)
    s = jnp.einsum('bnqkgd,bnskd->bnkgqs', qb, k_band).astype(jnp.float32) * (HEAD_DIM ** -0.5)

    qi = jnp.arange(SB)[:, None] + SB
    si = jnp.arange(2 * SB)[None, :]
    chunk_diff = qi // CHUNK - si // CHUNK
    band_ok = (chunk_diff >= 0) & (chunk_diff <= WINDOW // CHUNK)
    real_key = (jnp.arange(nb)[:, None, None] > 0) | (si >= SB)[None]
    mask = band_ok[None] & real_key

    slopes = alibi_slopes(SWA_HEADS).reshape(SWA_KV_HEADS, SWA_GROUP)
    alibi = -slopes[:, :, None, None] * jnp.abs(qi - si).astype(jnp.float32)
    s = jnp.where(mask[None, :, None, None], s + alibi[None, None], -jnp.inf)

    sink = jnp.broadcast_to(sinks.astype(jnp.float32).reshape(SWA_KV_HEADS, SWA_GROUP)[None, None, :, :, None, None],
                            s.shape[:-1] + (1,))
    probs = jax.nn.softmax(jnp.concatenate([s, sink], axis=-1), axis=-1)[..., :-1]
    out = jnp.einsum('bnkgqs,bnskd->bnqkgd', probs.astype(v.dtype), v_band)
    return out.reshape(B, S, SWA_Q)


def forgetting_attention(q, k, v, f_logit):
    B, S = q.shape[0], q.shape[1]
    nb = S // FOX_BLOCK
    log_f = jax.nn.log_sigmoid(f_logit.astype(jnp.float32))
    c = jnp.cumsum(log_f, axis=1)
    c_k = c.transpose(0, 2, 1)
    kh = k.transpose(0, 2, 1, 3)
    vh = v.transpose(0, 2, 1, 3)
    q_blocks = q.reshape(B, nb, FOX_BLOCK, FOX_HEADS, HEAD_DIM).transpose(1, 0, 3, 2, 4)
    c_blocks = c.reshape(B, nb, FOX_BLOCK, FOX_HEADS).transpose(1, 0, 3, 2)
    k_pos = jnp.arange(S)
    scale = HEAD_DIM ** -0.5

    def one_block(args):
        qb, cq, n = args
        s = jnp.einsum('bhqd,bhsd->bhqs', qb, kh).astype(jnp.float32) * scale
        s = s + cq[..., None] - c_k[:, :, None, :]
        q_pos = n * FOX_BLOCK + jnp.arange(FOX_BLOCK)
        s = jnp.where((k_pos[None, :] <= q_pos[:, None])[None, None], s, -jnp.inf)
        probs = jax.nn.softmax(s, axis=-1)
        return jnp.einsum('bhqs,bhsd->bhqd', probs.astype(vh.dtype), vh)

    out = lax.map(one_block, (q_blocks, c_blocks, jnp.arange(nb, dtype=jnp.int32)))
    return out.transpose(1, 0, 3, 2, 4).reshape(B, S, FOX_W)


def setup_inputs(seed: int = 0) -> dict:
    key = jax.random.key(seed)
    ks = jax.random.split(key, 18)

    def dense(k, fan_in, fan_out):
        return jax.random.normal(k, (DEPTH, fan_in, fan_out), jnp.float32) * fan_in ** -0.5

    def gain(k, shape):
        return 1.0 + 0.02 * jax.random.normal(k, shape, jnp.float32)

    return {
        "x": jax.random.normal(ks[0], (BATCH, SEQ, D_MODEL), jnp.float32),
        "p": jax.random.normal(ks[1], (DEPTH, BATCH, SEQ, PLE_DIM), jnp.float32),
        "g_mix": gain(ks[2], (DEPTH, D_MODEL)),
        "w_in": dense(ks[3], D_MODEL, D_IN),
        "b_forget": 3.0 + 0.5 * jax.random.normal(ks[4], (DEPTH, FOX_HEADS), jnp.float32),
        "swa_sinks": 0.5 * jax.random.normal(ks[5], (DEPTH, SWA_HEADS), jnp.float32),
        "w_br_swa": dense(ks[6], SWA_Q, D_MODEL),
        "w_br_fox": dense(ks[7], FOX_W, D_MODEL),
        "w_mix_out": dense(ks[8], D_MODEL, D_MODEL),
        "g_mlp": gain(ks[9], (DEPTH, D_MODEL)),
        "w_ff1": dense(ks[10], D_MODEL, D_FF),
        "w_ff2": dense(ks[11], D_FF, D_MODEL),
        "g_ple": gain(ks[12], (DEPTH, D_MODEL)),
        "w_ple_gate": dense(ks[13], D_MODEL, D_MODEL),
        "w_ple_proj": dense(ks[14], PLE_DIM, D_MODEL),
        "g_final": gain(ks[15], (D_MODEL,)),
    }


def reference(x, p, g_mix, w_in, b_forget, swa_sinks, w_br_swa, w_br_fox, w_mix_out,
              g_mlp, w_ff1, w_ff2, g_ple, w_ple_gate, w_ple_proj, g_final):
    B, S = x.shape[0], x.shape[1]
    h = x
    for i in range(DEPTH):
        u = rms_norm(h, g_mix[i])
        z = u @ w_in[i]
        q_a, k_a, v_a, q_b, k_b, v_b, f_b, gate_logits = jnp.split(z, SPLIT_POINTS, axis=-1)
        y_a = sliding_window_attention(
            q_a.reshape(B, S, SWA_HEADS, HEAD_DIM),
            k_a.reshape(B, S, SWA_KV_HEADS, HEAD_DIM),
            v_a.reshape(B, S, SWA_KV_HEADS, HEAD_DIM),
            swa_sinks[i]) @ w_br_swa[i]
        y_b = forgetting_attention(
            q_b.reshape(B, S, FOX_HEADS, HEAD_DIM),
            k_b.reshape(B, S, FOX_HEADS, HEAD_DIM),
            v_b.reshape(B, S, FOX_HEADS, HEAD_DIM),
            f_b + b_forget[i]) @ w_br_fox[i]
        gates = jax.nn.sigmoid(gate_logits).reshape(B, S, N_BRANCH, D_MODEL)
        mixed = gates[:, :, 0] * y_a + gates[:, :, 1] * y_b
        h = h + mixed @ w_mix_out[i]
        u = rms_norm(h, g_mlp[i])
        h = h + jnp.square(jax.nn.relu(u @ w_ff1[i])) @ w_ff2[i]
        ple_gate = jax.nn.sigmoid(rms_norm(h, g_ple[i]) @ w_ple_gate[i])
        h = h + ple_gate * (p[i] @ w_ple_proj[i])
    return rms_norm(h, g_final)
```

```python
import functools

import numpy as np
import jax
import jax.numpy as jnp
from jax import lax
from jax.experimental import pallas as pl
from jax.experimental.pallas import tpu as pltpu

D_MODEL = 1024
CHUNK = 64
PLE_DIM = 256
HEAD_DIM = 64
SWA_HEADS = 8
SWA_KV_HEADS = 2
SWA_GROUP = SWA_HEADS // SWA_KV_HEADS
WINDOW = 128
SWA_BLOCK = WINDOW
FOX_HEADS = 8
D_FF = 4 * D_MODEL
RMS_EPS = 1e-6
SWA_Q = SWA_HEADS * HEAD_DIM
SWA_KV = SWA_KV_HEADS * HEAD_DIM
FOX_W = FOX_HEADS * HEAD_DIM
SCALE = HEAD_DIM ** -0.5

LANES = 128
NEG_BIG = -1e30
VMEM_LIMIT = 52 * 1024 * 1024

TM = 512
FOX_TQ = 512
FF_CHUNK = 1024

F32 = jnp.float32
BF16 = jnp.bfloat16
NT_DIMS = (((1,), (1,)), ((), ()))


def _rms(x, g):
    return x * lax.rsqrt(jnp.mean(x * x, axis=-1, keepdims=True) + RMS_EPS) * g


def _const_spec(shape):
    return pl.BlockSpec(shape, lambda *_: (0,) * len(shape), pipeline_mode=pl.Buffered(1))


def _in_proj_kernel(x_ref, g_ref, wa_ref, wb_ref, wf_ref, wg_ref, bf_ref,
                    qkva_ref, qkvb_ref, gates_ref, c_ref, carry_ref, *, tiles_per_seq):
    i = pl.program_id(0)

    @pl.when(i % tiles_per_seq == 0)
    def _():
        carry_ref[...] = jnp.zeros_like(carry_ref)

    u = _rms(x_ref[...], g_ref[...]).astype(BF16)
    qkva_ref[...] = jnp.dot(u, wa_ref[...], preferred_element_type=F32).astype(BF16)
    qkvb_ref[...] = jnp.dot(u, wb_ref[...], preferred_element_type=F32).astype(BF16)
    gl = jnp.dot(u, wg_ref[...], preferred_element_type=F32)
    gates_ref[...] = jax.nn.sigmoid(gl).astype(BF16)

    f = jnp.dot(u, wf_ref[...], preferred_element_type=F32) + bf_ref[...]
    logf = jnp.minimum(f, 0.0) - jnp.log1p(jnp.exp(-jnp.abs(f)))
    tm = logf.shape[0]
    row = lax.broadcasted_iota(jnp.int32, (tm, tm), 0)
    col = lax.broadcasted_iota(jnp.int32, (tm, tm), 1)
    tri = (col <= row).astype(BF16)
    hi = logf.astype(BF16)
    r1 = logf - hi.astype(F32)
    mid = r1.astype(BF16)
    lo = (r1 - mid.astype(F32)).astype(BF16)
    c = (jnp.dot(tri, hi, preferred_element_type=F32)
         + jnp.dot(tri, mid, preferred_element_type=F32)
         + jnp.dot(tri, lo, preferred_element_type=F32)) + carry_ref[...]
    c_ref[...] = c
    carry_ref[...] = c[tm - 1:tm, :]


def _in_proj(x2, g_mix, wa, wb, wf, wg, bf_pad, seq):
    t = x2.shape[0]
    kern = functools.partial(_in_proj_kernel, tiles_per_seq=seq // TM)
    row_spec = lambda w: pl.BlockSpec((TM, w), lambda i: (i, 0))
    return pl.pallas_call(
        kern,
        grid=(t // TM,),
        in_specs=[row_spec(D_MODEL), _const_spec((1, D_MODEL)), _const_spec(wa.shape), _const_spec(wb.shape),
                  _const_spec(wf.shape), _const_spec(wg.shape), _const_spec((1, LANES))],
        out_specs=[row_spec(wa.shape[1]), row_spec(wb.shape[1]), row_spec(wg.shape[1]), row_spec(LANES)],
        out_shape=[jax.ShapeDtypeStruct((t, wa.shape[1]), BF16), jax.ShapeDtypeStruct((t, wb.shape[1]), BF16),
                   jax.ShapeDtypeStruct((t, wg.shape[1]), BF16), jax.ShapeDtypeStruct((t, LANES), F32)],
        scratch_shapes=[pltpu.VMEM((1, LANES), F32)],
        compiler_params=pltpu.CompilerParams(dimension_semantics=("arbitrary",), vmem_limit_bytes=VMEM_LIMIT),
        name="in_proj",
    )(x2, g_mix, wa, wb, wf, wg, bf_pad)


def _swa_bias_table():
    sb = SWA_BLOCK
    qi = np.arange(sb)[:, None] + sb
    si = np.arange(2 * sb)[None, :]
    chunk_diff = qi // CHUNK - si // CHUNK
    band_ok = (chunk_diff >= 0) & (chunk_diff <= WINDOW // CHUNK)
    slopes = np.array([2.0 ** (-8.0 * (h + 1) / SWA_HEADS) for h in range(SWA_HEADS)], dtype=np.float32)
    alibi = -slopes[:, None, None] * np.abs(qi - si).astype(np.float32)[None]
    first = band_ok & (si >= sb)
    table = np.stack([np.where(first[None], alibi, NEG_BIG), np.where(band_ok[None], alibi, NEG_BIG)])
    return table.reshape(2, SWA_KV_HEADS, SWA_GROUP * sb, 2 * sb).astype(np.float32)


def _swa_kernel(sink_ref, q_ref, kvp_ref, kvc_ref, bias_ref, o_ref):
    sb = SWA_BLOCK
    q = q_ref[...]
    kvp = kvp_ref[...]
    kvc = kvc_ref[...]
    for kh in range(SWA_KV_HEADS):
        heads = [kh * SWA_GROUP + g for g in range(SWA_GROUP)]
        q4 = jnp.concatenate([q[:, h * HEAD_DIM:(h + 1) * HEAD_DIM] for h in heads], axis=0)
        ks = slice(kh * HEAD_DIM, (kh + 1) * HEAD_DIM)
        vs = slice(SWA_KV + kh * HEAD_DIM, SWA_KV + (kh + 1) * HEAD_DIM)
        kb = jnp.concatenate([kvp[:, ks], kvc[:, ks]], axis=0)
        vb = jnp.concatenate([kvp[:, vs], kvc[:, vs]], axis=0)
        s = lax.dot_general(q4, kb, NT_DIMS, preferred_element_type=F32) * SCALE + bias_ref[kh]
        sink = jnp.concatenate([jnp.full((sb, 1), sink_ref[0, h], F32) for h in heads], axis=0)
        m = jnp.maximum(jnp.max(s, axis=1, keepdims=True), sink)
        p = jnp.exp(s - m)
        denom = jnp.sum(p, axis=1, keepdims=True) + jnp.exp(sink - m)
        o = jnp.dot(p.astype(BF16), vb, preferred_element_type=F32) / denom
        for g, h in enumerate(heads):
            o_ref[:, h * HEAD_DIM:(h + 1) * HEAD_DIM] = o[g * sb:(g + 1) * sb, :].astype(BF16)


def _swa(qkva3, sinks, bias):
    b, s, _ = qkva3.shape
    sb = SWA_BLOCK
    kv_col = SWA_Q // (2 * SWA_KV)
    return pl.pallas_call(
        _swa_kernel,
        grid=(b, s // sb),
        in_specs=[pl.BlockSpec(memory_space=pltpu.SMEM),
                  pl.BlockSpec((None, sb, SWA_Q), lambda bi, n: (bi, n, 0)),
                  pl.BlockSpec((None, sb, 2 * SWA_KV), lambda bi, n: (bi, jnp.maximum(n - 1, 0), kv_col)),
                  pl.BlockSpec((None, sb, 2 * SWA_KV), lambda bi, n: (bi, n, kv_col)),
                  pl.BlockSpec((None,) + bias.shape[1:], lambda bi, n: (jnp.minimum(n, 1), 0, 0, 0))],
        out_specs=pl.BlockSpec((None, sb, SWA_Q), lambda bi, n: (bi, n, 0)),
        out_shape=jax.ShapeDtypeStruct((b, s, SWA_Q), BF16),
        compiler_params=pltpu.CompilerParams(dimension_semantics=("parallel", "parallel"),
                                             vmem_limit_bytes=VMEM_LIMIT),
        name="swa",
    )(sinks, qkva3, qkva3, qkva3, bias)


def _fox_kernel(q_ref, k_ref, v_ref, cq_ref, ck_ref, o_ref, m_sc, l_sc, acc_sc):
    tq = q_ref.shape[0]
    hp = pl.program_id(1)
    qi = pl.program_id(2)
    lane = lax.broadcasted_iota(jnp.int32, (1, LANES), 1)
    first_head = lane < HEAD_DIM
    q2 = q_ref[...] * SCALE
    zero = jnp.zeros_like(q2)
    qs = (jnp.where(first_head, q2, zero), jnp.where(first_head, zero, q2))
    c_tile = cq_ref[...]
    cqs = tuple(jnp.sum(jnp.where(lane == 2 * hp + hh, c_tile, 0.0), axis=1, keepdims=True) for hh in range(2))

    m_sc[...] = jnp.full_like(m_sc, NEG_BIG)
    l_sc[...] = jnp.zeros_like(l_sc)
    acc_sc[...] = jnp.zeros_like(acc_sc)

    def step(j, diagonal):
        start = pl.multiple_of(j * tq, tq)
        k2 = k_ref[pl.ds(start, tq), :]
        v2 = v_ref[pl.ds(start, tq), :]
        for hh in range(2):
            s = lax.dot_general(qs[hh], k2, NT_DIMS, preferred_element_type=F32)
            s = s + cqs[hh] - ck_ref[hh:hh + 1, pl.ds(start, tq)]
            if diagonal:
                row = lax.broadcasted_iota(jnp.int32, s.shape, 0)
                col = lax.broadcasted_iota(jnp.int32, s.shape, 1)
                s = jnp.where(col <= row, s, NEG_BIG)
            m_prev = m_sc[hh]
            m_new = jnp.maximum(m_prev, jnp.max(s, axis=1, keepdims=True))
            p = jnp.exp(s - m_new)
            alpha = jnp.exp(m_prev - m_new)
            l_sc[hh] = alpha * l_sc[hh] + jnp.sum(p, axis=1, keepdims=True)
            m_sc[hh] = m_new
            acc_sc[hh] = alpha * acc_sc[hh] + jnp.dot(p.astype(BF16), v2, preferred_element_type=F32)

    def off_diagonal(j, carry):
        step(j, False)
        return carry

    lax.fori_loop(0, qi, off_diagonal, 0)
    step(qi, True)
    out = jnp.where(first_head, acc_sc[0] / l_sc[0], acc_sc[1] / l_sc[1])
    o_ref[...] = out.astype(BF16)


def _fox(qkvb3, c3, ck4):
    b, s, _ = qkvb3.shape
    tq = FOX_TQ
    pairs = FOX_W // LANES
    return pl.pallas_call(
        _fox_kernel,
        grid=(b, pairs, s // tq),
        in_specs=[pl.BlockSpec((None, tq, LANES), lambda bi, hp, qi: (bi, qi, hp)),
                  pl.BlockSpec((None, s, LANES), lambda bi, hp, qi: (bi, 0, pairs + hp)),
                  pl.BlockSpec((None, s, LANES), lambda bi, hp, qi: (bi, 0, 2 * pairs + hp)),
                  pl.BlockSpec((None, tq, LANES), lambda bi, hp, qi: (bi, qi, 0)),
                  pl.BlockSpec((None, None, 2, s), lambda bi, hp, qi: (bi, hp, 0, 0))],
        out_specs=pl.BlockSpec((None, tq, LANES), lambda bi, hp, qi: (bi, qi, hp)),
        out_shape=jax.ShapeDtypeStruct((b, s, FOX_W), BF16),
        scratch_shapes=[pltpu.VMEM((2, tq, 1), F32), pltpu.VMEM((2, tq, 1), F32), pltpu.VMEM((2, tq, LANES), F32)],
        compiler_params=pltpu.CompilerParams(dimension_semantics=("parallel", "parallel", "parallel"),
                                             vmem_limit_bytes=VMEM_LIMIT),
        name="fox",
    )(qkvb3, qkvb3, qkvb3, c3, ck4)


def _mix_kernel(x_ref, ya_ref, yb_ref, gates_ref, wa_ref, wb_ref, wo_ref, h_ref):
    y_a = jnp.dot(ya_ref[...], wa_ref[...], preferred_element_type=F32)
    y_b = jnp.dot(yb_ref[...], wb_ref[...], preferred_element_type=F32)
    gates = gates_ref[...].astype(F32)
    mixed = gates[:, :D_MODEL] * y_a + gates[:, D_MODEL:] * y_b
    h_ref[...] = x_ref[...] + jnp.dot(mixed.astype(BF16), wo_ref[...], preferred_element_type=F32)


def _mix(x2, att_a, att_b, gates, w_a, w_b, w_o):
    t = x2.shape[0]
    row_spec = lambda w: pl.BlockSpec((TM, w), lambda i: (i, 0))
    return pl.pallas_call(
        _mix_kernel,
        grid=(t // TM,),
        in_specs=[row_spec(D_MODEL), row_spec(SWA_Q), row_spec(FOX_W), row_spec(2 * D_MODEL),
                  _const_spec(w_a.shape), _const_spec(w_b.shape), _const_spec(w_o.shape)],
        out_specs=row_spec(D_MODEL),
        out_shape=jax.ShapeDtypeStruct((t, D_MODEL), F32),
        compiler_params=pltpu.CompilerParams(dimension_semantics=("parallel",), vmem_limit_bytes=VMEM_LIMIT),
        name="mix",
    )(x2, att_a, att_b, gates, w_a, w_b, w_o)


def _mlp_ple_kernel(h_ref, p_ref, gm_ref, w1_ref, w2_ref, gp_ref, wpg_ref, wpp_ref, gf_ref, o_ref):
    h = h_ref[...]
    u = _rms(h, gm_ref[...]).astype(BF16)
    acc = jnp.zeros_like(h)
    for c in range(D_FF // FF_CHUNK):
        cols = slice(c * FF_CHUNK, (c + 1) * FF_CHUNK)
        a = jnp.dot(u, w1_ref[:, cols], preferred_element_type=F32)
        a = jnp.square(jnp.maximum(a, 0.0)).astype(BF16)
        acc = acc + jnp.dot(a, w2_ref[cols, :], preferred_element_type=F32)
    h = h + acc
    gate = jax.nn.sigmoid(jnp.dot(_rms(h, gp_ref[...]).astype(BF16), wpg_ref[...], preferred_element_type=F32))
    proj = jnp.dot(p_ref[...].astype(BF16), wpp_ref[...], preferred_element_type=F32)
    h = h + gate * proj
    o_ref[...] = _rms(h, gf_ref[...])


def _mlp_ple(h2, p2, g_mlp, w1, w2, g_ple, w_pg, w_pp, g_final):
    t = h2.shape[0]
    row_spec = lambda w: pl.BlockSpec((TM, w), lambda i: (i, 0))
    vec = _const_spec((1, D_MODEL))
    return pl.pallas_call(
        _mlp_ple_kernel,
        grid=(t // TM,),
        in_specs=[row_spec(D_MODEL), row_spec(PLE_DIM), vec, _const_spec(w1.shape), _const_spec(w2.shape),
                  vec, _const_spec(w_pg.shape), _const_spec(w_pp.shape), vec],
        out_specs=row_spec(D_MODEL),
        out_shape=jax.ShapeDtypeStruct((t, D_MODEL), F32),
        compiler_params=pltpu.CompilerParams(dimension_semantics=("parallel",), vmem_limit_bytes=VMEM_LIMIT),
        name="mlp_ple",
    )(h2, p2, g_mlp, w1, w2, g_ple, w_pg, w_pp, g_final)


def kernel(x, p, g_mix, w_in, b_forget, swa_sinks, w_br_swa, w_br_fox, w_mix_out,
           g_mlp, w_ff1, w_ff2, g_ple, w_ple_gate, w_ple_proj, g_final):
    b, s, d = x.shape
    assert d == D_MODEL and w_in.shape[0] == 1, "single-layer trunk with D_MODEL channels only"
    assert s % TM == 0 and s % FOX_TQ == 0 and s % SWA_BLOCK == 0
    t = b * s
    x2 = x.reshape(t, d)

    a_end = SWA_Q + 2 * SWA_KV
    b_end = a_end + 3 * FOX_W
    f_end = b_end + FOX_HEADS
    w = w_in[0]
    wa = w[:, :a_end].astype(BF16)
    wb = w[:, a_end:b_end].astype(BF16)
    wf = jnp.pad(w[:, b_end:f_end], ((0, 0), (0, LANES - FOX_HEADS))).astype(BF16)
    wg = w[:, f_end:].astype(BF16)
    bf_pad = jnp.pad(b_forget[0], (0, LANES - FOX_HEADS)).reshape(1, LANES)

    qkva, qkvb, gates, c = _in_proj(x2, g_mix[0].reshape(1, d), wa, wb, wf, wg, bf_pad, s)

    att_a = _swa(qkva.reshape(b, s, a_end), swa_sinks[0].reshape(1, SWA_HEADS), jnp.asarray(_swa_bias_table()))

    c3 = c.reshape(b, s, LANES)
    ck4 = c3[:, :, :FOX_HEADS].transpose(0, 2, 1).reshape(b, FOX_HEADS // 2, 2, s)
    att_b = _fox(qkvb.reshape(b, s, 3 * FOX_W), c3, ck4)

    h = _mix(x2, att_a.reshape(t, SWA_Q), att_b.reshape(t, FOX_W), gates,
             w_br_swa[0].astype(BF16), w_br_fox[0].astype(BF16), w_mix_out[0].astype(BF16))
    out = _mlp_ple(h, p[0].reshape(t, PLE_DIM), g_mlp[0].reshape(1, d), w_ff1[0].astype(BF16), w_ff2[0].astype(BF16),
                   g_ple[0].reshape(1, d), w_ple_gate[0].astype(BF16), w_ple_proj[0].astype(BF16),
                   g_final.reshape(1, d))
    return out.reshape(b, s, d)
```

```python
import functools

import numpy as np
import jax
import jax.numpy as jnp
from jax import lax
from jax.experimental import pallas as pl
from jax.experimental.pallas import tpu as pltpu

D_MODEL = 1024
CHUNK = 64
PLE_DIM = 256
HEAD_DIM = 64
SWA_HEADS = 8
SWA_KV_HEADS = 2
SWA_GROUP = SWA_HEADS // SWA_KV_HEADS
WINDOW = 128
SWA_BLOCK = WINDOW
FOX_HEADS = 8
D_FF = 4 * D_MODEL
RMS_EPS = 1e-6
SWA_Q = SWA_HEADS * HEAD_DIM
SWA_KV = SWA_KV_HEADS * HEAD_DIM
FOX_W = FOX_HEADS * HEAD_DIM
SCALE = HEAD_DIM ** -0.5

LANES = 128
NEG_BIG = -1e30
VMEM_LIMIT = 52 * 1024 * 1024

TM = 512
FOX_TQ = 512
FF_CHUNK = 1024

FOX_PAIRS = FOX_HEADS // 2
FOX_DEPTH = 2 * LANES
BIAS_SLOT = 8
C_PARTS = 3
VT_ROWS = 80

F32 = jnp.float32
BF16 = jnp.bfloat16
NT_DIMS = (((1,), (1,)), ((), ()))


def _rms(x, g):
    return x * lax.rsqrt(jnp.mean(x * x, axis=-1, keepdims=True) + RMS_EPS) * g


def _const_spec(shape):
    return pl.BlockSpec(shape, lambda *_: (0,) * len(shape), pipeline_mode=pl.Buffered(1))


def _split3(v):
    hi = v.astype(BF16)
    r1 = v - hi.astype(F32)
    mid = r1.astype(BF16)
    lo = (r1 - mid.astype(F32)).astype(BF16)
    return hi, mid, lo


def _bias_placement():
    width = FOX_PAIRS * LANES
    place = np.zeros((C_PARTS * LANES, 2 * width), np.float32)
    ones = np.zeros((1, 2 * width), np.float32)
    for h in range(FOX_HEADS):
        base = (h // 2) * LANES + (h % 2) * BIAS_SLOT
        for part in range(C_PARTS):
            place[part * LANES + h, base + part] = 1.0
            ones[0, base + C_PARTS + part] = 1.0
            ones[0, width + base + part] = 1.0
            place[part * LANES + h, width + base + C_PARTS + part] = -1.0
    return place, ones


def _in_proj_kernel(x_ref, g_ref, wa_ref, wq_ref, wk_ref, wvt_ref, wf_ref, wg_ref, bf_ref, place_ref, ones_ref,
                    vone_ref, qkva_ref, q3_ref, k3_ref, vt_ref, gates_ref, carry_ref, *, tiles_per_seq):
    i = pl.program_id(0)

    @pl.when(i % tiles_per_seq == 0)
    def _():
        carry_ref[...] = jnp.zeros_like(carry_ref)

    u = _rms(x_ref[...], g_ref[...]).astype(BF16)
    qkva_ref[...] = jnp.dot(u, wa_ref[...], preferred_element_type=F32).astype(BF16)
    q = (jnp.dot(u, wq_ref[...], preferred_element_type=F32) * SCALE).astype(BF16)
    k = jnp.dot(u, wk_ref[...], preferred_element_type=F32).astype(BF16)
    vt = lax.dot_general(wvt_ref[...], u, NT_DIMS, preferred_element_type=F32)
    vt_ref[...] = (vt + vone_ref[...]).astype(BF16)
    gl = jnp.dot(u, wg_ref[...], preferred_element_type=F32)
    gates_ref[...] = jax.nn.sigmoid(gl).astype(BF16)

    f = jnp.dot(u, wf_ref[...], preferred_element_type=F32) + bf_ref[...]
    logf = jnp.minimum(f, 0.0) - jnp.log1p(jnp.exp(-jnp.abs(f)))
    tm = logf.shape[0]
    row = lax.broadcasted_iota(jnp.int32, (tm, tm), 0)
    col = lax.broadcasted_iota(jnp.int32, (tm, tm), 1)
    tri = (col <= row).astype(BF16)
    c = sum(jnp.dot(tri, part, preferred_element_type=F32) for part in _split3(logf)) + carry_ref[...]
    carry_ref[...] = c[tm - 1:tm, :]
    parts = jnp.concatenate(_split3(c), axis=1)
    bias = (jnp.dot(parts, place_ref[...], preferred_element_type=F32) + ones_ref[...]).astype(BF16)
    width = FOX_PAIRS * LANES
    for hp in range(FOX_PAIRS):
        lanes = slice(hp * LANES, (hp + 1) * LANES)
        q3_ref[:, hp * FOX_DEPTH:hp * FOX_DEPTH + LANES] = q[:, lanes]
        q3_ref[:, hp * FOX_DEPTH + LANES:(hp + 1) * FOX_DEPTH] = bias[:, lanes]
        k3_ref[:, hp * FOX_DEPTH:hp * FOX_DEPTH + LANES] = k[:, lanes]
        k3_ref[:, hp * FOX_DEPTH + LANES:(hp + 1) * FOX_DEPTH] = bias[:, width + hp * LANES:width + (hp + 1) * LANES]


def _in_proj(x2, g_mix, wa, wq, wk, wvt, wf, wg, bf_pad, batch, seq):
    t = x2.shape[0]
    tps = seq // TM
    place, ones = _bias_placement()
    vone = np.zeros((FOX_HEADS * VT_ROWS, 1), np.float32)
    vone[HEAD_DIM::VT_ROWS] = 1.0
    consts = (jnp.asarray(place, BF16), jnp.asarray(ones), jnp.asarray(vone))
    kern = functools.partial(_in_proj_kernel, tiles_per_seq=tps)
    row_spec = lambda w: pl.BlockSpec((TM, w), lambda i: (i, 0))
    operands = (x2, g_mix, wa, wq, wk, wvt, wf, wg, bf_pad) + consts
    return pl.pallas_call(
        kern,
        grid=(t // TM,),
        in_specs=[row_spec(D_MODEL)] + [_const_spec(a.shape) for a in operands[1:]],
        out_specs=[row_spec(wa.shape[1]), row_spec(FOX_PAIRS * FOX_DEPTH), row_spec(FOX_PAIRS * FOX_DEPTH),
                   pl.BlockSpec((None, FOX_HEADS * VT_ROWS, TM), lambda i: (i // tps, 0, i % tps)),
                   row_spec(wg.shape[1])],
        out_shape=[jax.ShapeDtypeStruct((t, wa.shape[1]), BF16),
                   jax.ShapeDtypeStruct((t, FOX_PAIRS * FOX_DEPTH), BF16),
                   jax.ShapeDtypeStruct((t, FOX_PAIRS * FOX_DEPTH), BF16),
                   jax.ShapeDtypeStruct((batch, FOX_HEADS * VT_ROWS, seq), BF16),
                   jax.ShapeDtypeStruct((t, wg.shape[1]), BF16)],
        scratch_shapes=[pltpu.VMEM((1, LANES), F32)],
        compiler_params=pltpu.CompilerParams(dimension_semantics=("arbitrary",), vmem_limit_bytes=VMEM_LIMIT),
        name="in_proj",
    )(*operands)


def _swa_bias_table():
    sb = SWA_BLOCK
    qi = np.arange(sb)[:, None] + sb
    si = np.arange(2 * sb)[None, :]
    chunk_diff = qi // CHUNK - si // CHUNK
    band_ok = (chunk_diff >= 0) & (chunk_diff <= WINDOW // CHUNK)
    slopes = np.array([2.0 ** (-8.0 * (h + 1) / SWA_HEADS) for h in range(SWA_HEADS)], dtype=np.float32)
    alibi = -slopes[:, None, None] * np.abs(qi - si).astype(np.float32)[None]
    first = band_ok & (si >= sb)
    table = np.stack([np.where(first[None], alibi, NEG_BIG), np.where(band_ok[None], alibi, NEG_BIG)])
    return table.reshape(2, SWA_KV_HEADS, SWA_GROUP * sb, 2 * sb).astype(np.float32)


def _swa_kernel(sink_ref, q_ref, kvp_ref, kvc_ref, bias_ref, o_ref):
    sb = SWA_BLOCK
    q = q_ref[...]
    kvp = kvp_ref[...]
    kvc = kvc_ref[...]
    for kh in range(SWA_KV_HEADS):
        heads = [kh * SWA_GROUP + g for g in range(SWA_GROUP)]
        q4 = jnp.concatenate([q[:, h * HEAD_DIM:(h + 1) * HEAD_DIM] for h in heads], axis=0)
        ks = slice(kh * HEAD_DIM, (kh + 1) * HEAD_DIM)
        vs = slice(SWA_KV + kh * HEAD_DIM, SWA_KV + (kh + 1) * HEAD_DIM)
        kb = jnp.concatenate([kvp[:, ks], kvc[:, ks]], axis=0)
        vb = jnp.concatenate([kvp[:, vs], kvc[:, vs]], axis=0)
        s = lax.dot_general(q4, kb, NT_DIMS, preferred_element_type=F32) * SCALE + bias_ref[kh]
        sink = jnp.concatenate([jnp.full((sb, 1), sink_ref[0, h], F32) for h in heads], axis=0)
        m = jnp.maximum(jnp.max(s, axis=1, keepdims=True), sink)
        p = jnp.exp(s - m)
        denom = jnp.sum(p, axis=1, keepdims=True) + jnp.exp(sink - m)
        o = jnp.dot(p.astype(BF16), vb, preferred_element_type=F32) / denom
        for g, h in enumerate(heads):
            o_ref[:, h * HEAD_DIM:(h + 1) * HEAD_DIM] = o[g * sb:(g + 1) * sb, :].astype(BF16)


def _swa(qkva3, sinks, bias):
    b, s, _ = qkva3.shape
    sb = SWA_BLOCK
    kv_col = SWA_Q // (2 * SWA_KV)
    return pl.pallas_call(
        _swa_kernel,
        grid=(b, s // sb),
        in_specs=[pl.BlockSpec(memory_space=pltpu.SMEM),
                  pl.BlockSpec((None, sb, SWA_Q), lambda bi, n: (bi, n, 0)),
                  pl.BlockSpec((None, sb, 2 * SWA_KV), lambda bi, n: (bi, jnp.maximum(n - 1, 0), kv_col)),
                  pl.BlockSpec((None, sb, 2 * SWA_KV), lambda bi, n: (bi, n, kv_col)),
                  pl.BlockSpec((None,) + bias.shape[1:], lambda bi, n: (jnp.minimum(n, 1), 0, 0, 0))],
        out_specs=pl.BlockSpec((None, sb, SWA_Q), lambda bi, n: (bi, n, 0)),
        out_shape=jax.ShapeDtypeStruct((b, s, SWA_Q), BF16),
        compiler_params=pltpu.CompilerParams(dimension_semantics=("parallel", "parallel"),
                                             vmem_limit_bytes=VMEM_LIMIT),
        name="swa",
    )(sinks, qkva3, qkva3, qkva3, bias)


def _fox_kernel(q_ref, k_ref, vt_ref, o_ref, qt_sc, m_sc, acc_sc):
    tq = q_ref.shape[0]
    qi = pl.program_id(2)
    qt = q_ref[...].astype(F32).T
    row = lax.broadcasted_iota(jnp.int32, (FOX_DEPTH, 1), 0)
    for hh in range(2):
        own = ((row >= hh * HEAD_DIM) & (row < (hh + 1) * HEAD_DIM)) | \
              ((row >= LANES + hh * BIAS_SLOT) & (row < LANES + (hh + 1) * BIAS_SLOT))
        qt_sc[hh] = jnp.where(own, qt, 0.0).astype(BF16)
    m_sc[...] = jnp.full_like(m_sc, NEG_BIG)
    acc_sc[...] = jnp.zeros_like(acc_sc)

    def step(j, diagonal):
        start = pl.multiple_of(j * tq, tq)
        k3 = k_ref[pl.ds(start, tq), :]
        for hh in range(2):
            st = jnp.dot(k3, qt_sc[hh], preferred_element_type=F32)
            if diagonal:
                key = lax.broadcasted_iota(jnp.int32, st.shape, 0)
                qry = lax.broadcasted_iota(jnp.int32, st.shape, 1)
                st = jnp.where(key <= qry, st, NEG_BIG)
            m_prev = m_sc[hh]
            m_new = jnp.maximum(m_prev, jnp.max(st, axis=0, keepdims=True))
            p = jnp.exp(st - m_new).astype(BF16)
            alpha = jnp.exp(m_prev - m_new)
            pv = jnp.dot(vt_ref[hh, :, pl.ds(start, tq)], p, preferred_element_type=F32)
            acc_sc[hh] = alpha * acc_sc[hh] + pv
            m_sc[hh] = m_new

    def off_diagonal(j, carry):
        step(j, False)
        return carry

    lax.fori_loop(0, qi, off_diagonal, 0)
    step(qi, True)
    outs = [acc_sc[hh, :HEAD_DIM, :] / acc_sc[hh, HEAD_DIM:HEAD_DIM + 1, :] for hh in range(2)]
    o_ref[...] = jnp.concatenate(outs, axis=0).T.astype(BF16)


def _fox(q3, k3, vt4):
    b, s, _ = q3.shape
    tq = FOX_TQ
    return pl.pallas_call(
        _fox_kernel,
        grid=(b, FOX_PAIRS, s // tq),
        in_specs=[pl.BlockSpec((None, tq, FOX_DEPTH), lambda bi, hp, qi: (bi, qi, hp)),
                  pl.BlockSpec((None, s, FOX_DEPTH), lambda bi, hp, qi: (bi, 0, hp)),
                  pl.BlockSpec((None, 2, VT_ROWS, s), lambda bi, hp, qi: (bi, hp, 0, 0))],
        out_specs=pl.BlockSpec((None, tq, LANES), lambda bi, hp, qi: (bi, qi, hp)),
        out_shape=jax.ShapeDtypeStruct((b, s, FOX_W), BF16),
        scratch_shapes=[pltpu.VMEM((2, FOX_DEPTH, tq), BF16), pltpu.VMEM((2, 1, tq), F32),
                        pltpu.VMEM((2, VT_ROWS, tq), F32)],
        compiler_params=pltpu.CompilerParams(dimension_semantics=("parallel", "parallel", "parallel"),
                                             vmem_limit_bytes=VMEM_LIMIT),
        name="fox",
    )(q3, k3, vt4)


def _mix_kernel(x_ref, ya_ref, yb_ref, gates_ref, wa_ref, wb_ref, wo_ref, h_ref):
    y_a = jnp.dot(ya_ref[...], wa_ref[...], preferred_element_type=F32)
    y_b = jnp.dot(yb_ref[...], wb_ref[...], preferred_element_type=F32)
    gates = gates_ref[...].astype(F32)
    mixed = gates[:, :D_MODEL] * y_a + gates[:, D_MODEL:] * y_b
    h_ref[...] = x_ref[...] + jnp.dot(mixed.astype(BF16), wo_ref[...], preferred_element_type=F32)


def _mix(x2, att_a, att_b, gates, w_a, w_b, w_o):
    t = x2.shape[0]
    row_spec = lambda w: pl.BlockSpec((TM, w), lambda i: (i, 0))
    return pl.pallas_call(
        _mix_kernel,
        grid=(t // TM,),
        in_specs=[row_spec(D_MODEL), row_spec(SWA_Q), row_spec(FOX_W), row_spec(2 * D_MODEL),
                  _const_spec(w_a.shape), _const_spec(w_b.shape), _const_spec(w_o.shape)],
        out_specs=row_spec(D_MODEL),
        out_shape=jax.ShapeDtypeStruct((t, D_MODEL), F32),
        compiler_params=pltpu.CompilerParams(dimension_semantics=("parallel",), vmem_limit_bytes=VMEM_LIMIT),
        name="mix",
    )(x2, att_a, att_b, gates, w_a, w_b, w_o)


def _mlp_ple_kernel(h_ref, p_ref, gm_ref, w1_ref, w2_ref, gp_ref, wpg_ref, wpp_ref, gf_ref, o_ref):
    h = h_ref[...]
    u = _rms(h, gm_ref[...]).astype(BF16)
    acc = jnp.zeros_like(h)
    for c in range(D_FF // FF_CHUNK):
        cols = slice(c * FF_CHUNK, (c + 1) * FF_CHUNK)
        a = jnp.dot(u, w1_ref[:, cols], preferred_element_type=F32)
        a = jnp.square(jnp.maximum(a, 0.0)).astype(BF16)
        acc = acc + jnp.dot(a, w2_ref[cols, :], preferred_element_type=F32)
    h = h + acc
    gate = jax.nn.sigmoid(jnp.dot(_rms(h, gp_ref[...]).astype(BF16), wpg_ref[...], preferred_element_type=F32))
    proj = jnp.dot(p_ref[...].astype(BF16), wpp_ref[...], preferred_element_type=F32)
    h = h + gate * proj
    o_ref[...] = _rms(h, gf_ref[...])


def _mlp_ple(h2, p2, g_mlp, w1, w2, g_ple, w_pg, w_pp, g_final):
    t = h2.shape[0]
    row_spec = lambda w: pl.BlockSpec((TM, w), lambda i: (i, 0))
    vec = _const_spec((1, D_MODEL))
    return pl.pallas_call(
        _mlp_ple_kernel,
        grid=(t // TM,),
        in_specs=[row_spec(D_MODEL), row_spec(PLE_DIM), vec, _const_spec(w1.shape), _const_spec(w2.shape),
                  vec, _const_spec(w_pg.shape), _const_spec(w_pp.shape), vec],
        out_specs=row_spec(D_MODEL),
        out_shape=jax.ShapeDtypeStruct((t, D_MODEL), F32),
        compiler_params=pltpu.CompilerParams(dimension_semantics=("parallel",), vmem_limit_bytes=VMEM_LIMIT),
        name="mlp_ple",
    )(h2, p2, g_mlp, w1, w2, g_ple, w_pg, w_pp, g_final)


def kernel(x, p, g_mix, w_in, b_forget, swa_sinks, w_br_swa, w_br_fox, w_mix_out,
           g_mlp, w_ff1, w_ff2, g_ple, w_ple_gate, w_ple_proj, g_final):
    b, s, d = x.shape
    assert d == D_MODEL and w_in.shape[0] == 1, "single-layer trunk with D_MODEL channels only"
    assert s % TM == 0 and s % FOX_TQ == 0 and s % SWA_BLOCK == 0
    t = b * s
    x2 = x.reshape(t, d)

    a_end = SWA_Q + 2 * SWA_KV
    q_end = a_end + FOX_W
    k_end = q_end + FOX_W
    v_end = k_end + FOX_W
    f_end = v_end + FOX_HEADS
    w = w_in[0]
    wa = w[:, :a_end].astype(BF16)
    wq = w[:, a_end:q_end].astype(BF16)
    wk = w[:, q_end:k_end].astype(BF16)
    wv = w[:, k_end:v_end].T.reshape(FOX_HEADS, HEAD_DIM, d)
    wvt = jnp.pad(wv, ((0, 0), (0, VT_ROWS - HEAD_DIM), (0, 0))).reshape(FOX_HEADS * VT_ROWS, d).astype(BF16)
    wf = jnp.pad(w[:, v_end:f_end], ((0, 0), (0, LANES - FOX_HEADS))).astype(BF16)
    wg = w[:, f_end:].astype(BF16)
    bf_pad = jnp.pad(b_forget[0], (0, LANES - FOX_HEADS)).reshape(1, LANES)

    qkva, q3, k3, vt, gates = _in_proj(x2, g_mix[0].reshape(1, d), wa, wq, wk, wvt, wf, wg, bf_pad, b, s)

    att_a = _swa(qkva.reshape(b, s, a_end), swa_sinks[0].reshape(1, SWA_HEADS), jnp.asarray(_swa_bias_table()))
    att_b = _fox(q3.reshape(b, s, FOX_PAIRS * FOX_DEPTH), k3.reshape(b, s, FOX_PAIRS * FOX_DEPTH),
                 vt.reshape(b, FOX_HEADS, VT_ROWS, s))

    h = _mix(x2, att_a.reshape(t, SWA_Q), att_b.reshape(t, FOX_W), gates,
             w_br_swa[0].astype(BF16), w_br_fox[0].astype(BF16), w_mix_out[0].astype(BF16))
    out = _mlp_ple(h, p[0].reshape(t, PLE_DIM), g_mlp[0].reshape(1, d), w_ff1[0].astype(BF16), w_ff2[0].astype(BF16),
                   g_ple[0].reshape(1, d), w_ple_gate[0].astype(BF16), w_ple_proj[0].astype(BF16),
                   g_final.reshape(1, d))
    return out.reshape(b, s, d)
```

```python
import functools

import numpy as np
import jax
import jax.numpy as jnp
from jax import lax
from jax.experimental import pallas as pl
from jax.experimental.pallas import tpu as pltpu

D_MODEL = 1024
CHUNK = 64
PLE_DIM = 256
HEAD_DIM = 64
SWA_HEADS = 8
SWA_KV_HEADS = 2
SWA_GROUP = SWA_HEADS // SWA_KV_HEADS
WINDOW = 128
SWA_BLOCK = WINDOW
FOX_HEADS = 8
D_FF = 4 * D_MODEL
RMS_EPS = 1e-6
SWA_Q = SWA_HEADS * HEAD_DIM
SWA_KV = SWA_KV_HEADS * HEAD_DIM
FOX_W = FOX_HEADS * HEAD_DIM
SCALE = HEAD_DIM ** -0.5
LOG2E = float(np.log2(np.e))

LANES = 128
NEG_BIG = -1e30
VMEM_LIMIT = 52 * 1024 * 1024

TM = 512
FOX_TQ = 512
FF_CHUNK = 1024

FOX_PAIRS = FOX_HEADS // 2
FOX_DEPTH = 2 * LANES
BIAS_SLOT = 8
C_PARTS = 3
VT_ROWS = 80

F32 = jnp.float32
BF16 = jnp.bfloat16
NT_DIMS = (((1,), (1,)), ((), ()))


def _rms(x, g):
    return x * lax.rsqrt(jnp.mean(x * x, axis=-1, keepdims=True) + RMS_EPS) * g


def _const_spec(shape):
    return pl.BlockSpec(shape, lambda *_: (0,) * len(shape), pipeline_mode=pl.Buffered(1))


def _split3(v):
    hi = v.astype(BF16)
    r1 = v - hi.astype(F32)
    mid = r1.astype(BF16)
    lo = (r1 - mid.astype(F32)).astype(BF16)
    return hi, mid, lo


def _bias_placement():
    width = FOX_PAIRS * LANES
    place = np.zeros((C_PARTS * LANES, 2 * width), np.float32)
    ones = np.zeros((1, 2 * width), np.float32)
    for h in range(FOX_HEADS):
        base = (h // 2) * LANES + (h % 2) * BIAS_SLOT
        for part in range(C_PARTS):
            place[part * LANES + h, base + part] = 1.0
            ones[0, base + C_PARTS + part] = 1.0
            ones[0, width + base + part] = 1.0
            place[part * LANES + h, width + base + C_PARTS + part] = -1.0
    return place, ones


def _in_proj_kernel(x_ref, g_ref, wa_ref, wq_ref, wk_ref, wvt_ref, wf_ref, wg_ref, bf_ref, place_ref, ones_ref,
                    vone_ref, qkva_ref, q3_ref, k3_ref, vt_ref, gates_ref, carry_ref, *, tiles_per_seq):
    i = pl.program_id(0)

    @pl.when(i % tiles_per_seq == 0)
    def _():
        carry_ref[...] = jnp.zeros_like(carry_ref)

    u = _rms(x_ref[...], g_ref[...]).astype(BF16)
    qkva_ref[...] = jnp.dot(u, wa_ref[...], preferred_element_type=F32).astype(BF16)
    q = (jnp.dot(u, wq_ref[...], preferred_element_type=F32) * (SCALE * LOG2E)).astype(BF16)
    k = jnp.dot(u, wk_ref[...], preferred_element_type=F32).astype(BF16)
    vt = lax.dot_general(wvt_ref[...], u, NT_DIMS, preferred_element_type=F32)
    vt_ref[...] = (vt + vone_ref[...]).astype(BF16)
    gl = jnp.dot(u, wg_ref[...], preferred_element_type=F32)
    gates_ref[...] = jax.nn.sigmoid(gl).astype(BF16)

    f = jnp.dot(u, wf_ref[...], preferred_element_type=F32) + bf_ref[...]
    logf = jnp.minimum(f, 0.0) - jnp.log1p(jnp.exp(-jnp.abs(f)))
    tm = logf.shape[0]
    row = lax.broadcasted_iota(jnp.int32, (tm, tm), 0)
    col = lax.broadcasted_iota(jnp.int32, (tm, tm), 1)
    tri = (col <= row).astype(BF16)
    c = sum(jnp.dot(tri, part, preferred_element_type=F32) for part in _split3(logf)) + carry_ref[...]
    carry_ref[...] = c[tm - 1:tm, :]
    parts = jnp.concatenate(_split3(c * LOG2E), axis=1)
    bias = (jnp.dot(parts, place_ref[...], preferred_element_type=F32) + ones_ref[...]).astype(BF16)
    width = FOX_PAIRS * LANES
    for hp in range(FOX_PAIRS):
        lanes = slice(hp * LANES, (hp + 1) * LANES)
        q3_ref[:, hp * FOX_DEPTH:hp * FOX_DEPTH + LANES] = q[:, lanes]
        q3_ref[:, hp * FOX_DEPTH + LANES:(hp + 1) * FOX_DEPTH] = bias[:, lanes]
        k3_ref[:, hp * FOX_DEPTH:hp * FOX_DEPTH + LANES] = k[:, lanes]
        k3_ref[:, hp * FOX_DEPTH + LANES:(hp + 1) * FOX_DEPTH] = bias[:, width + hp * LANES:width + (hp + 1) * LANES]


def _in_proj(x2, g_mix, wa, wq, wk, wvt, wf, wg, bf_pad, batch, seq):
    t = x2.shape[0]
    tps = seq // TM
    place, ones = _bias_placement()
    vone = np.zeros((FOX_HEADS * VT_ROWS, 1), np.float32)
    vone[HEAD_DIM::VT_ROWS] = 1.0
    consts = (jnp.asarray(place, BF16), jnp.asarray(ones), jnp.asarray(vone))
    kern = functools.partial(_in_proj_kernel, tiles_per_seq=tps)
    row_spec = lambda w: pl.BlockSpec((TM, w), lambda i: (i, 0))
    operands = (x2, g_mix, wa, wq, wk, wvt, wf, wg, bf_pad) + consts
    return pl.pallas_call(
        kern,
        grid=(t // TM,),
        in_specs=[row_spec(D_MODEL)] + [_const_spec(a.shape) for a in operands[1:]],
        out_specs=[row_spec(wa.shape[1]), row_spec(FOX_PAIRS * FOX_DEPTH), row_spec(FOX_PAIRS * FOX_DEPTH),
                   pl.BlockSpec((None, FOX_HEADS * VT_ROWS, TM), lambda i: (i // tps, 0, i % tps)),
                   row_spec(wg.shape[1])],
        out_shape=[jax.ShapeDtypeStruct((t, wa.shape[1]), BF16),
                   jax.ShapeDtypeStruct((t, FOX_PAIRS * FOX_DEPTH), BF16),
                   jax.ShapeDtypeStruct((t, FOX_PAIRS * FOX_DEPTH), BF16),
                   jax.ShapeDtypeStruct((batch, FOX_HEADS * VT_ROWS, seq), BF16),
                   jax.ShapeDtypeStruct((t, wg.shape[1]), BF16)],
        scratch_shapes=[pltpu.VMEM((1, LANES), F32)],
        compiler_params=pltpu.CompilerParams(dimension_semantics=("arbitrary",), vmem_limit_bytes=VMEM_LIMIT),
        name="in_proj",
    )(*operands)


def _swa_bias_table():
    sb = SWA_BLOCK
    qi = np.arange(sb)[:, None] + sb
    si = np.arange(2 * sb)[None, :]
    chunk_diff = qi // CHUNK - si // CHUNK
    band_ok = (chunk_diff >= 0) & (chunk_diff <= WINDOW // CHUNK)
    slopes = np.array([2.0 ** (-8.0 * (h + 1) / SWA_HEADS) for h in range(SWA_HEADS)], dtype=np.float32)
    alibi = -slopes[:, None, None] * np.abs(qi - si).astype(np.float32)[None]
    first = band_ok & (si >= sb)
    table = np.stack([np.where(first[None], alibi, NEG_BIG), np.where(band_ok[None], alibi, NEG_BIG)])
    return table.reshape(2, SWA_KV_HEADS, SWA_GROUP * sb, 2 * sb).astype(np.float32)


def _swa_kernel(sink_ref, q_ref, kvp_ref, kvc_ref, bias_ref, o_ref):
    sb = SWA_BLOCK
    q = q_ref[...]
    kvp = kvp_ref[...]
    kvc = kvc_ref[...]
    for kh in range(SWA_KV_HEADS):
        heads = [kh * SWA_GROUP + g for g in range(SWA_GROUP)]
        q4 = jnp.concatenate([q[:, h * HEAD_DIM:(h + 1) * HEAD_DIM] for h in heads], axis=0)
        ks = slice(kh * HEAD_DIM, (kh + 1) * HEAD_DIM)
        vs = slice(SWA_KV + kh * HEAD_DIM, SWA_KV + (kh + 1) * HEAD_DIM)
        kb = jnp.concatenate([kvp[:, ks], kvc[:, ks]], axis=0)
        vb = jnp.concatenate([kvp[:, vs], kvc[:, vs]], axis=0)
        s = lax.dot_general(q4, kb, NT_DIMS, preferred_element_type=F32) * SCALE + bias_ref[kh]
        sink = jnp.concatenate([jnp.full((sb, 1), sink_ref[0, h], F32) for h in heads], axis=0)
        m = jnp.maximum(jnp.max(s, axis=1, keepdims=True), sink)
        p = jnp.exp(s - m)
        denom = jnp.sum(p, axis=1, keepdims=True) + jnp.exp(sink - m)
        o = jnp.dot(p.astype(BF16), vb, preferred_element_type=F32) / denom
        for g, h in enumerate(heads):
            o_ref[:, h * HEAD_DIM:(h + 1) * HEAD_DIM] = o[g * sb:(g + 1) * sb, :].astype(BF16)


def _swa(qkva3, sinks, bias):
    b, s, _ = qkva3.shape
    sb = SWA_BLOCK
    kv_col = SWA_Q // (2 * SWA_KV)
    return pl.pallas_call(
        _swa_kernel,
        grid=(b, s // sb),
        in_specs=[pl.BlockSpec(memory_space=pltpu.SMEM),
                  pl.BlockSpec((None, sb, SWA_Q), lambda bi, n: (bi, n, 0)),
                  pl.BlockSpec((None, sb, 2 * SWA_KV), lambda bi, n: (bi, jnp.maximum(n - 1, 0), kv_col)),
                  pl.BlockSpec((None, sb, 2 * SWA_KV), lambda bi, n: (bi, n, kv_col)),
                  pl.BlockSpec((None,) + bias.shape[1:], lambda bi, n: (jnp.minimum(n, 1), 0, 0, 0))],
        out_specs=pl.BlockSpec((None, sb, SWA_Q), lambda bi, n: (bi, n, 0)),
        out_shape=jax.ShapeDtypeStruct((b, s, SWA_Q), BF16),
        compiler_params=pltpu.CompilerParams(dimension_semantics=("parallel", "parallel"),
                                             vmem_limit_bytes=VMEM_LIMIT),
        name="swa",
    )(sinks, qkva3, qkva3, qkva3, bias)


def _fox_kernel(q_ref, k_ref, vt_ref, o_ref, qt_sc, st_sc, mb_sc, m_sc, acc_sc):
    tq = q_ref.shape[0]
    qi = pl.program_id(2)
    qt = q_ref[...].astype(F32).T
    row = lax.broadcasted_iota(jnp.int32, (FOX_DEPTH, 1), 0)
    for hh in range(2):
        own = ((row >= hh * HEAD_DIM) & (row < (hh + 1) * HEAD_DIM)) | \
              ((row >= LANES + hh * BIAS_SLOT) & (row < LANES + (hh + 1) * BIAS_SLOT))
        qt_sc[hh] = jnp.where(own, qt, 0.0).astype(BF16)
    m_sc[...] = jnp.full_like(m_sc, NEG_BIG)
    acc_sc[...] = jnp.zeros_like(acc_sc)

    def scores(block, slot, diagonal):
        start = pl.multiple_of(block * tq, tq)
        k3 = k_ref[pl.ds(start, tq), :]
        for hh in range(2):
            st = jnp.dot(k3, qt_sc[hh], preferred_element_type=F32)
            if diagonal:
                key = lax.broadcasted_iota(jnp.int32, st.shape, 0)
                qry = lax.broadcasted_iota(jnp.int32, st.shape, 1)
                st = jnp.where(key <= qry, st, NEG_BIG)
            st_sc[slot, hh] = st
            mb_sc[slot, hh] = jnp.max(st, axis=0, keepdims=True)

    def consume(block, slot):
        start = pl.multiple_of(block * tq, tq)
        for hh in range(2):
            m_prev = m_sc[hh]
            m_new = jnp.maximum(m_prev, mb_sc[slot, hh])
            p = jnp.exp2(st_sc[slot, hh] - m_new).astype(BF16)
            alpha = jnp.exp2(m_prev - m_new)
            pv = jnp.dot(vt_ref[hh, :, pl.ds(start, tq)], p, preferred_element_type=F32)
            acc_sc[hh] = alpha * acc_sc[hh] + pv
            m_sc[hh] = m_new

    scores(qi, 0, True)
    scores(0, 1, False)
    consume(qi, 0)

    def block_pair(t, carry):
        b0 = 2 * t
        scores(jnp.minimum(b0 + 1, qi), 0, False)
        consume(b0, 1)

        @pl.when(b0 + 1 < qi)
        def _():
            scores(jnp.minimum(b0 + 2, qi), 1, False)
            consume(b0 + 1, 0)

        return carry

    lax.fori_loop(0, (qi + 1) // 2, block_pair, 0)
    outs = [acc_sc[hh, :HEAD_DIM, :] / acc_sc[hh, HEAD_DIM:HEAD_DIM + 1, :] for hh in range(2)]
    o_ref[...] = jnp.concatenate(outs, axis=0).T.astype(BF16)


def _fox(q3, k3, vt4):
    b, s, _ = q3.shape
    tq = FOX_TQ
    return pl.pallas_call(
        _fox_kernel,
        grid=(b, FOX_PAIRS, s // tq),
        in_specs=[pl.BlockSpec((None, tq, FOX_DEPTH), lambda bi, hp, qi: (bi, qi, hp)),
                  pl.BlockSpec((None, s, FOX_DEPTH), lambda bi, hp, qi: (bi, 0, hp)),
                  pl.BlockSpec((None, 2, VT_ROWS, s), lambda bi, hp, qi: (bi, hp, 0, 0))],
        out_specs=pl.BlockSpec((None, tq, LANES), lambda bi, hp, qi: (bi, qi, hp)),
        out_shape=jax.ShapeDtypeStruct((b, s, FOX_W), BF16),
        scratch_shapes=[pltpu.VMEM((2, FOX_DEPTH, tq), BF16),
                        pltpu.VMEM((2, 2, tq, tq), F32),
                        pltpu.VMEM((2, 2, 1, tq), F32),
                        pltpu.VMEM((2, 1, tq), F32),
                        pltpu.VMEM((2, VT_ROWS, tq), F32)],
        compiler_params=pltpu.CompilerParams(dimension_semantics=("parallel", "parallel", "parallel"),
                                             vmem_limit_bytes=VMEM_LIMIT),
        name="fox",
    )(q3, k3, vt4)


def _mix_kernel(x_ref, ya_ref, yb_ref, gates_ref, wa_ref, wb_ref, wo_ref, h_ref):
    y_a = jnp.dot(ya_ref[...], wa_ref[...], preferred_element_type=F32)
    y_b = jnp.dot(yb_ref[...], wb_ref[...], preferred_element_type=F32)
    gates = gates_ref[...].astype(F32)
    mixed = gates[:, :D_MODEL] * y_a + gates[:, D_MODEL:] * y_b
    h_ref[...] = x_ref[...] + jnp.dot(mixed.astype(BF16), wo_ref[...], preferred_element_type=F32)


def _mix(x2, att_a, att_b, gates, w_a, w_b, w_o):
    t = x2.shape[0]
    row_spec = lambda w: pl.BlockSpec((TM, w), lambda i: (i, 0))
    return pl.pallas_call(
        _mix_kernel,
        grid=(t // TM,),
        in_specs=[row_spec(D_MODEL), row_spec(SWA_Q), row_spec(FOX_W), row_spec(2 * D_MODEL),
                  _const_spec(w_a.shape), _const_spec(w_b.shape), _const_spec(w_o.shape)],
        out_specs=row_spec(D_MODEL),
        out_shape=jax.ShapeDtypeStruct((t, D_MODEL), F32),
        compiler_params=pltpu.CompilerParams(dimension_semantics=("parallel",), vmem_limit_bytes=VMEM_LIMIT),
        name="mix",
    )(x2, att_a, att_b, gates, w_a, w_b, w_o)


def _mlp_ple_kernel(h_ref, p_ref, gm_ref, w1_ref, w2_ref, gp_ref, wpg_ref, wpp_ref, gf_ref, o_ref):
    h = h_ref[...]
    u = _rms(h, gm_ref[...]).astype(BF16)
    acc = jnp.zeros_like(h)
    for c in range(D_FF // FF_CHUNK):
        cols = slice(c * FF_CHUNK, (c + 1) * FF_CHUNK)
        a = jnp.dot(u, w1_ref[:, cols], preferred_element_type=F32)
        a = jnp.square(jnp.maximum(a, 0.0)).astype(BF16)
        acc = acc + jnp.dot(a, w2_ref[cols, :], preferred_element_type=F32)
    h = h + acc
    gate = jax.nn.sigmoid(jnp.dot(_rms(h, gp_ref[...]).astype(BF16), wpg_ref[...], preferred_element_type=F32))
    proj = jnp.dot(p_ref[...].astype(BF16), wpp_ref[...], preferred_element_type=F32)
    h = h + gate * proj
    o_ref[...] = _rms(h, gf_ref[...])


def _mlp_ple(h2, p2, g_mlp, w1, w2, g_ple, w_pg, w_pp, g_final):
    t = h2.shape[0]
    row_spec = lambda w: pl.BlockSpec((TM, w), lambda i: (i, 0))
    vec = _const_spec((1, D_MODEL))
    return pl.pallas_call(
        _mlp_ple_kernel,
        grid=(t // TM,),
        in_specs=[row_spec(D_MODEL), row_spec(PLE_DIM), vec, _const_spec(w1.shape), _const_spec(w2.shape),
                  vec, _const_spec(w_pg.shape), _const_spec(w_pp.shape), vec],
        out_specs=row_spec(D_MODEL),
        out_shape=jax.ShapeDtypeStruct((t, D_MODEL), F32),
        compiler_params=pltpu.CompilerParams(dimension_semantics=("parallel",), vmem_limit_bytes=VMEM_LIMIT),
        name="mlp_ple",
    )(h2, p2, g_mlp, w1, w2, g_ple, w_pg, w_pp, g_final)


def kernel(x, p, g_mix, w_in, b_forget, swa_sinks, w_br_swa, w_br_fox, w_mix_out,
           g_mlp, w_ff1, w_ff2, g_ple, w_ple_gate, w_ple_proj, g_final):
    b, s, d = x.shape
    assert d == D_MODEL and w_in.shape[0] == 1, "single-layer trunk with D_MODEL channels only"
    assert s % TM == 0 and s % FOX_TQ == 0 and s % SWA_BLOCK == 0
    t = b * s
    x2 = x.reshape(t, d)

    a_end = SWA_Q + 2 * SWA_KV
    q_end = a_end + FOX_W
    k_end = q_end + FOX_W
    v_end = k_end + FOX_W
    f_end = v_end + FOX_HEADS
    w = w_in[0]
    wa = w[:, :a_end].astype(BF16)
    wq = w[:, a_end:q_end].astype(BF16)
    wk = w[:, q_end:k_end].astype(BF16)
    wv = w[:, k_end:v_end].T.reshape(FOX_HEADS, HEAD_DIM, d)
    wvt = jnp.pad(wv, ((0, 0), (0, VT_ROWS - HEAD_DIM), (0, 0))).reshape(FOX_HEADS * VT_ROWS, d).astype(BF16)
    wf = jnp.pad(w[:, v_end:f_end], ((0, 0), (0, LANES - FOX_HEADS))).astype(BF16)
    wg = w[:, f_end:].astype(BF16)
    bf_pad = jnp.pad(b_forget[0], (0, LANES - FOX_HEADS)).reshape(1, LANES)

    qkva, q3, k3, vt, gates = _in_proj(x2, g_mix[0].reshape(1, d), wa, wq, wk, wvt, wf, wg, bf_pad, b, s)

    att_a = _swa(qkva.reshape(b, s, a_end), swa_sinks[0].reshape(1, SWA_HEADS), jnp.asarray(_swa_bias_table()))
    att_b = _fox(q3.reshape(b, s, FOX_PAIRS * FOX_DEPTH), k3.reshape(b, s, FOX_PAIRS * FOX_DEPTH),
                 vt.reshape(b, FOX_HEADS, VT_ROWS, s))

    h = _mix(x2, att_a.reshape(t, SWA_Q), att_b.reshape(t, FOX_W), gates,
             w_br_swa[0].astype(BF16), w_br_fox[0].astype(BF16), w_mix_out[0].astype(BF16))
    out = _mlp_ple(h, p[0].reshape(t, PLE_DIM), g_mlp[0].reshape(1, d), w_ff1[0].astype(BF16), w_ff2[0].astype(BF16),
                   g_ple[0].reshape(1, d), w_ple_gate[0].astype(BF16), w_ple_proj[0].astype(BF16),
                   g_final.reshape(1, d))
    return out.reshape(b, s, d)
```

```python
import functools

import numpy as np
import jax
import jax.numpy as jnp
from jax import lax
from jax.experimental import pallas as pl
from jax.experimental.pallas import tpu as pltpu

D_MODEL = 1024
CHUNK = 64
PLE_DIM = 256
HEAD_DIM = 64
SWA_HEADS = 8
SWA_KV_HEADS = 2
SWA_GROUP = SWA_HEADS // SWA_KV_HEADS
WINDOW = 128
SWA_BLOCK = WINDOW
FOX_HEADS = 8
D_FF = 4 * D_MODEL
RMS_EPS = 1e-6
SWA_Q = SWA_HEADS * HEAD_DIM
SWA_KV = SWA_KV_HEADS * HEAD_DIM
FOX_W = FOX_HEADS * HEAD_DIM
SCALE = HEAD_DIM ** -0.5
LOG2E = float(np.log2(np.e))

LANES = 128
NEG_BIG = -1e30
VMEM_LIMIT = 52 * 1024 * 1024

TM = 512
SWA_TILE = 512
FOX_TQ = 512
FF_CHUNK = 1024

FOX_PAIRS = FOX_HEADS // 2
FOX_DEPTH = 2 * LANES
BIAS_SLOT = 8
C_PARTS = 3
VT_ROWS = 80

F32 = jnp.float32
BF16 = jnp.bfloat16
NT_DIMS = (((1,), (1,)), ((), ()))


def _rms(x, g):
    return x * lax.rsqrt(jnp.mean(x * x, axis=-1, keepdims=True) + RMS_EPS) * g


def _const_spec(shape):
    return pl.BlockSpec(shape, lambda *_: (0,) * len(shape), pipeline_mode=pl.Buffered(1))


def _split3(v):
    hi = v.astype(BF16)
    r1 = v - hi.astype(F32)
    mid = r1.astype(BF16)
    lo = (r1 - mid.astype(F32)).astype(BF16)
    return hi, mid, lo


def _bias_placement():
    width = FOX_PAIRS * LANES
    place = np.zeros((C_PARTS * LANES, 2 * width), np.float32)
    ones = np.zeros((1, 2 * width), np.float32)
    for h in range(FOX_HEADS):
        base = (h // 2) * LANES + (h % 2) * BIAS_SLOT
        for part in range(C_PARTS):
            place[part * LANES + h, base + part] = 1.0
            ones[0, base + C_PARTS + part] = 1.0
            ones[0, width + base + part] = 1.0
            place[part * LANES + h, width + base + C_PARTS + part] = -1.0
    return place, ones


def _in_proj_kernel(x_ref, g_ref, wqa_ref, wka_ref, wq_ref, wk_ref, wvt_ref, wf_ref, wg_ref, bf_ref, place_ref, ones_ref,
                    vone_ref, qa_ref, ka_ref, vta_ref, q3_ref, k3_ref, vtb_ref, gates_ref, carry_ref, *, tiles_per_seq):
    i = pl.program_id(0)

    @pl.when(i % tiles_per_seq == 0)
    def _():
        carry_ref[...] = jnp.zeros_like(carry_ref)

    u = _rms(x_ref[...], g_ref[...]).astype(BF16)
    qa_ref[...] = (jnp.dot(u, wqa_ref[...], preferred_element_type=F32) * (SCALE * LOG2E)).astype(BF16)
    ka_ref[...] = jnp.dot(u, wka_ref[...], preferred_element_type=F32).astype(BF16)
    q = (jnp.dot(u, wq_ref[...], preferred_element_type=F32) * (SCALE * LOG2E)).astype(BF16)
    k = jnp.dot(u, wk_ref[...], preferred_element_type=F32).astype(BF16)
    vt = (lax.dot_general(wvt_ref[...], u, NT_DIMS, preferred_element_type=F32) + vone_ref[...]).astype(BF16)
    vta_ref[...] = vt[:SWA_KV_HEADS * VT_ROWS, :]
    vtb_ref[...] = vt[SWA_KV_HEADS * VT_ROWS:, :]
    gl = jnp.dot(u, wg_ref[...], preferred_element_type=F32)
    gates_ref[...] = jax.nn.sigmoid(gl).astype(BF16)

    f = jnp.dot(u, wf_ref[...], preferred_element_type=F32) + bf_ref[...]
    logf = jnp.minimum(f, 0.0) - jnp.log1p(jnp.exp(-jnp.abs(f)))
    tm = logf.shape[0]
    row = lax.broadcasted_iota(jnp.int32, (tm, tm), 0)
    col = lax.broadcasted_iota(jnp.int32, (tm, tm), 1)
    tri = (col <= row).astype(BF16)
    c = sum(jnp.dot(tri, part, preferred_element_type=F32) for part in _split3(logf)) + carry_ref[...]
    carry_ref[...] = c[tm - 1:tm, :]
    parts = jnp.concatenate(_split3(c * LOG2E), axis=1)
    bias = (jnp.dot(parts, place_ref[...], preferred_element_type=F32) + ones_ref[...]).astype(BF16)
    width = FOX_PAIRS * LANES
    for hp in range(FOX_PAIRS):
        lanes = slice(hp * LANES, (hp + 1) * LANES)
        q3_ref[:, hp * FOX_DEPTH:hp * FOX_DEPTH + LANES] = q[:, lanes]
        q3_ref[:, hp * FOX_DEPTH + LANES:(hp + 1) * FOX_DEPTH] = bias[:, lanes]
        k3_ref[:, hp * FOX_DEPTH:hp * FOX_DEPTH + LANES] = k[:, lanes]
        k3_ref[:, hp * FOX_DEPTH + LANES:(hp + 1) * FOX_DEPTH] = bias[:, width + hp * LANES:width + (hp + 1) * LANES]


def _in_proj(x2, g_mix, wqa, wka, wq, wk, wvt, wf, wg, bf_pad, batch, seq):
    t = x2.shape[0]
    tps = seq // TM
    place, ones = _bias_placement()
    n_vt = SWA_KV_HEADS + FOX_HEADS
    vone = np.zeros((n_vt * VT_ROWS, 1), np.float32)
    vone[HEAD_DIM::VT_ROWS] = 1.0
    consts = (jnp.asarray(place, BF16), jnp.asarray(ones), jnp.asarray(vone))
    kern = functools.partial(_in_proj_kernel, tiles_per_seq=tps)
    row_spec = lambda w: pl.BlockSpec((TM, w), lambda i: (i, 0))
    vt_spec = lambda heads: pl.BlockSpec((None, heads * VT_ROWS, TM), lambda i: (i // tps, 0, i % tps))
    operands = (x2, g_mix, wqa, wka, wq, wk, wvt, wf, wg, bf_pad) + consts
    widths = (SWA_Q, SWA_KV, FOX_PAIRS * FOX_DEPTH, FOX_PAIRS * FOX_DEPTH, wg.shape[1])
    qa, ka, q3, k3, gates = (jax.ShapeDtypeStruct((t, w), BF16) for w in widths)
    vta, vtb = (jax.ShapeDtypeStruct((batch, heads * VT_ROWS, seq), BF16) for heads in (SWA_KV_HEADS, FOX_HEADS))
    return pl.pallas_call(
        kern,
        grid=(t // TM,),
        in_specs=[row_spec(D_MODEL)] + [_const_spec(a.shape) for a in operands[1:]],
        out_specs=[row_spec(widths[0]), row_spec(widths[1]), vt_spec(SWA_KV_HEADS),
                   row_spec(widths[2]), row_spec(widths[3]), vt_spec(FOX_HEADS), row_spec(widths[4])],
        out_shape=[qa, ka, vta, q3, k3, vtb, gates],
        scratch_shapes=[pltpu.VMEM((1, LANES), F32)],
        compiler_params=pltpu.CompilerParams(dimension_semantics=("arbitrary",), vmem_limit_bytes=VMEM_LIMIT),
        name="in_proj",
    )(*operands)


def _swa_bias_table():
    sb = SWA_BLOCK
    qi = np.arange(sb)[None, :] + sb
    si = np.arange(2 * sb)[:, None]
    chunk_diff = qi // CHUNK - si // CHUNK
    band_ok = (chunk_diff >= 0) & (chunk_diff <= WINDOW // CHUNK)
    slopes = np.array([2.0 ** (-8.0 * (h + 1) / SWA_HEADS) for h in range(SWA_HEADS)], dtype=np.float32)
    alibi = -slopes[:, None, None] * np.abs(qi - si).astype(np.float32)[None] * np.float32(LOG2E)
    first = band_ok & (si >= sb)
    table = np.stack([np.where(first[None], alibi, NEG_BIG), np.where(band_ok[None], alibi, NEG_BIG)])
    table = table.reshape(2, SWA_KV_HEADS, SWA_GROUP, 2 * sb, sb).transpose(0, 1, 3, 2, 4)
    return np.ascontiguousarray(table.reshape(2, SWA_KV_HEADS, 2 * sb, SWA_GROUP * sb)).astype(np.float32)


def _swa_kernel(sink_ref, q_ref, kp_ref, kc_ref, vtp_ref, vtc_ref, bias_first_ref, bias_rest_ref, o_ref):
    sb = SWA_BLOCK
    qt = q_ref[...].astype(F32).T
    half = jnp.zeros((HEAD_DIM, SWA_GROUP * sb), F32)
    for j in range(SWA_TILE // sb):
        cols = slice(j * sb, (j + 1) * sb)
        if j == 0:
            kb = jnp.concatenate([kp_ref[...], kc_ref[0:sb, :]], axis=0)
            vtb = jnp.concatenate([vtp_ref[...], vtc_ref[:, 0:sb]], axis=1)
            bias_ref = bias_first_ref
        else:
            kb = kc_ref[(j - 1) * sb:(j + 1) * sb, :]
            vtb = vtc_ref[:, (j - 1) * sb:(j + 1) * sb]
            bias_ref = bias_rest_ref
        out_rows = []
        for kh in range(SWA_KV_HEADS):
            heads = [kh * SWA_GROUP + g for g in range(SWA_GROUP)]
            q4 = jnp.concatenate([qt[h * HEAD_DIM:(h + 1) * HEAD_DIM, cols] for h in heads], axis=1)
            q4t = jnp.concatenate([q4, half] if kh == 0 else [half, q4], axis=0).astype(BF16)
            st = jnp.dot(kb, q4t, preferred_element_type=F32) + bias_ref[kh]
            sink = jnp.concatenate([jnp.full((1, sb), sink_ref[0, h] * LOG2E, F32) for h in heads], axis=1)
            m = jnp.maximum(jnp.max(st, axis=0, keepdims=True), sink)
            p = jnp.exp2(st - m).astype(BF16)
            pv = jnp.dot(vtb[kh * VT_ROWS:(kh + 1) * VT_ROWS, :], p, preferred_element_type=F32)
            denom = pv[HEAD_DIM:HEAD_DIM + 1, :] + jnp.exp2(sink - m)
            o = pv[:HEAD_DIM, :] / denom
            out_rows += [o[:, g * sb:(g + 1) * sb] for g in range(SWA_GROUP)]
        o_ref[cols, :] = jnp.concatenate(out_rows, axis=0).T.astype(BF16)


def _swa(q3, k3, vt3, sinks, bias):
    b, s, _ = q3.shape
    sb, ts = SWA_BLOCK, SWA_TILE
    per = ts // sb
    prev = lambda n: jnp.maximum(n * per - 1, 0)
    bias_block = (None,) + bias.shape[1:]
    return pl.pallas_call(
        _swa_kernel,
        grid=(b, s // ts),
        in_specs=[pl.BlockSpec(memory_space=pltpu.SMEM),
                  pl.BlockSpec((None, ts, SWA_Q), lambda bi, n: (bi, n, 0)),
                  pl.BlockSpec((None, sb, SWA_KV), lambda bi, n: (bi, prev(n), 0)),
                  pl.BlockSpec((None, ts, SWA_KV), lambda bi, n: (bi, n, 0)),
                  pl.BlockSpec((None, SWA_KV_HEADS * VT_ROWS, sb), lambda bi, n: (bi, 0, prev(n))),
                  pl.BlockSpec((None, SWA_KV_HEADS * VT_ROWS, ts), lambda bi, n: (bi, 0, n)),
                  pl.BlockSpec(bias_block, lambda bi, n: (jnp.minimum(n, 1), 0, 0, 0)),
                  pl.BlockSpec(bias_block, lambda bi, n: (1, 0, 0, 0))],
        out_specs=pl.BlockSpec((None, ts, SWA_Q), lambda bi, n: (bi, n, 0)),
        out_shape=jax.ShapeDtypeStruct((b, s, SWA_Q), BF16),
        compiler_params=pltpu.CompilerParams(dimension_semantics=("parallel", "parallel"),
                                             vmem_limit_bytes=VMEM_LIMIT),
        name="swa",
    )(sinks, q3, k3, k3, vt3, vt3, bias, bias)


def _fox_kernel(q_ref, k_ref, vt_ref, o_ref, qt_sc, st_sc, mb_sc, m_sc, acc_sc):
    tq = q_ref.shape[0]
    qi = pl.program_id(2)
    qt = q_ref[...].astype(F32).T
    row = lax.broadcasted_iota(jnp.int32, (FOX_DEPTH, 1), 0)
    for hh in range(2):
        own = ((row >= hh * HEAD_DIM) & (row < (hh + 1) * HEAD_DIM)) | \
              ((row >= LANES + hh * BIAS_SLOT) & (row < LANES + (hh + 1) * BIAS_SLOT))
        qt_sc[hh] = jnp.where(own, qt, 0.0).astype(BF16)
    m_sc[...] = jnp.full_like(m_sc, NEG_BIG)
    acc_sc[...] = jnp.zeros_like(acc_sc)

    def scores(block, slot, diagonal):
        start = pl.multiple_of(block * tq, tq)
        k3 = k_ref[pl.ds(start, tq), :]
        for hh in range(2):
            st = jnp.dot(k3, qt_sc[hh], preferred_element_type=F32)
            if diagonal:
                key = lax.broadcasted_iota(jnp.int32, st.shape, 0)
                qry = lax.broadcasted_iota(jnp.int32, st.shape, 1)
                st = jnp.where(key <= qry, st, NEG_BIG)
            st_sc[slot, hh] = st
            mb_sc[slot, hh] = jnp.max(st, axis=0, keepdims=True)

    def consume(block, slot):
        start = pl.multiple_of(block * tq, tq)
        for hh in range(2):
            m_prev = m_sc[hh]
            m_new = jnp.maximum(m_prev, mb_sc[slot, hh])
            p = jnp.exp2(st_sc[slot, hh] - m_new).astype(BF16)
            alpha = jnp.exp2(m_prev - m_new)
            pv = jnp.dot(vt_ref[hh, :, pl.ds(start, tq)], p, preferred_element_type=F32)
            acc_sc[hh] = alpha * acc_sc[hh] + pv
            m_sc[hh] = m_new

    scores(qi, 0, True)
    scores(0, 1, False)
    consume(qi, 0)

    def block_pair(t, carry):
        b0 = 2 * t
        scores(jnp.minimum(b0 + 1, qi), 0, False)
        consume(b0, 1)

        @pl.when(b0 + 1 < qi)
        def _():
            scores(jnp.minimum(b0 + 2, qi), 1, False)
            consume(b0 + 1, 0)

        return carry

    lax.fori_loop(0, (qi + 1) // 2, block_pair, 0)
    outs = [acc_sc[hh, :HEAD_DIM, :] / acc_sc[hh, HEAD_DIM:HEAD_DIM + 1, :] for hh in range(2)]
    o_ref[...] = jnp.concatenate(outs, axis=0).T.astype(BF16)


def _fox(q3, k3, vt4):
    b, s, _ = q3.shape
    tq = FOX_TQ
    return pl.pallas_call(
        _fox_kernel,
        grid=(b, FOX_PAIRS, s // tq),
        in_specs=[pl.BlockSpec((None, tq, FOX_DEPTH), lambda bi, hp, qi: (bi, qi, hp)),
                  pl.BlockSpec((None, s, FOX_DEPTH), lambda bi, hp, qi: (bi, 0, hp)),
                  pl.BlockSpec((None, 2, VT_ROWS, s), lambda bi, hp, qi: (bi, hp, 0, 0))],
        out_specs=pl.BlockSpec((None, tq, LANES), lambda bi, hp, qi: (bi, qi, hp)),
        out_shape=jax.ShapeDtypeStruct((b, s, FOX_W), BF16),
        scratch_shapes=[pltpu.VMEM((2, FOX_DEPTH, tq), BF16),
                        pltpu.VMEM((2, 2, tq, tq), F32),
                        pltpu.VMEM((2, 2, 1, tq), F32),
                        pltpu.VMEM((2, 1, tq), F32),
                        pltpu.VMEM((2, VT_ROWS, tq), F32)],
        compiler_params=pltpu.CompilerParams(dimension_semantics=("parallel", "parallel", "parallel"),
                                             vmem_limit_bytes=VMEM_LIMIT),
        name="fox",
    )(q3, k3, vt4)


def _mix_kernel(x_ref, ya_ref, yb_ref, gates_ref, wa_ref, wb_ref, wo_ref, h_ref):
    y_a = jnp.dot(ya_ref[...], wa_ref[...], preferred_element_type=F32)
    y_b = jnp.dot(yb_ref[...], wb_ref[...], preferred_element_type=F32)
    gates = gates_ref[...].astype(F32)
    mixed = gates[:, :D_MODEL] * y_a + gates[:, D_MODEL:] * y_b
    h_ref[...] = x_ref[...] + jnp.dot(mixed.astype(BF16), wo_ref[...], preferred_element_type=F32)


def _mix(x2, att_a, att_b, gates, w_a, w_b, w_o):
    t = x2.shape[0]
    row_spec = lambda w: pl.BlockSpec((TM, w), lambda i: (i, 0))
    return pl.pallas_call(
        _mix_kernel,
        grid=(t // TM,),
        in_specs=[row_spec(D_MODEL), row_spec(SWA_Q), row_spec(FOX_W), row_spec(2 * D_MODEL),
                  _const_spec(w_a.shape), _const_spec(w_b.shape), _const_spec(w_o.shape)],
        out_specs=row_spec(D_MODEL),
        out_shape=jax.ShapeDtypeStruct((t, D_MODEL), F32),
        compiler_params=pltpu.CompilerParams(dimension_semantics=("parallel",), vmem_limit_bytes=VMEM_LIMIT),
        name="mix",
    )(x2, att_a, att_b, gates, w_a, w_b, w_o)


def _mlp_ple_kernel(h_ref, p_ref, gm_ref, w1_ref, w2_ref, gp_ref, wpg_ref, wpp_ref, gf_ref, o_ref):
    h = h_ref[...]
    u = _rms(h, gm_ref[...]).astype(BF16)
    acc = jnp.zeros_like(h)
    for c in range(D_FF // FF_CHUNK):
        cols = slice(c * FF_CHUNK, (c + 1) * FF_CHUNK)
        a = jnp.dot(u, w1_ref[:, cols], preferred_element_type=F32)
        a = jnp.square(jnp.maximum(a, 0.0)).astype(BF16)
        acc = acc + jnp.dot(a, w2_ref[cols, :], preferred_element_type=F32)
    h = h + acc
    gate = jax.nn.sigmoid(jnp.dot(_rms(h, gp_ref[...]).astype(BF16), wpg_ref[...], preferred_element_type=F32))
    proj = jnp.dot(p_ref[...].astype(BF16), wpp_ref[...], preferred_element_type=F32)
    h = h + gate * proj
    o_ref[...] = _rms(h, gf_ref[...])


def _mlp_ple(h2, p2, g_mlp, w1, w2, g_ple, w_pg, w_pp, g_final):
    t = h2.shape[0]
    row_spec = lambda w: pl.BlockSpec((TM, w), lambda i: (i, 0))
    vec = _const_spec((1, D_MODEL))
    return pl.pallas_call(
        _mlp_ple_kernel,
        grid=(t // TM,),
        in_specs=[row_spec(D_MODEL), row_spec(PLE_DIM), vec, _const_spec(w1.shape), _const_spec(w2.shape),
                  vec, _const_spec(w_pg.shape), _const_spec(w_pp.shape), vec],
        out_specs=row_spec(D_MODEL),
        out_shape=jax.ShapeDtypeStruct((t, D_MODEL), F32),
        compiler_params=pltpu.CompilerParams(dimension_semantics=("parallel",), vmem_limit_bytes=VMEM_LIMIT),
        name="mlp_ple",
    )(h2, p2, g_mlp, w1, w2, g_ple, w_pg, w_pp, g_final)


def kernel(x, p, g_mix, w_in, b_forget, swa_sinks, w_br_swa, w_br_fox, w_mix_out,
           g_mlp, w_ff1, w_ff2, g_ple, w_ple_gate, w_ple_proj, g_final):
    b, s, d = x.shape
    assert d == D_MODEL and w_in.shape[0] == 1, "single-layer trunk with D_MODEL channels only"
    assert s % TM == 0 and s % FOX_TQ == 0 and s % SWA_TILE == 0 and SWA_TILE % SWA_BLOCK == 0
    t = b * s
    x2 = x.reshape(t, d)

    ka_end = SWA_Q + SWA_KV
    a_end = ka_end + SWA_KV
    q_end = a_end + FOX_W
    k_end = q_end + FOX_W
    v_end = k_end + FOX_W
    f_end = v_end + FOX_HEADS
    w = w_in[0]
    wqa = w[:, :SWA_Q].astype(BF16)
    wka = w[:, SWA_Q:ka_end].astype(BF16)
    wq = w[:, a_end:q_end].astype(BF16)
    wk = w[:, q_end:k_end].astype(BF16)
    wv = jnp.concatenate([w[:, ka_end:a_end], w[:, k_end:v_end]], axis=1).T.reshape(-1, HEAD_DIM, d)
    wvt = jnp.pad(wv, ((0, 0), (0, VT_ROWS - HEAD_DIM), (0, 0))).reshape(-1, d).astype(BF16)
    wf = jnp.pad(w[:, v_end:f_end], ((0, 0), (0, LANES - FOX_HEADS))).astype(BF16)
    wg = w[:, f_end:].astype(BF16)
    bf_pad = jnp.pad(b_forget[0], (0, LANES - FOX_HEADS)).reshape(1, LANES)

    qa, ka, vta, q3, k3, vtb, gates = _in_proj(x2, g_mix[0].reshape(1, d), wqa, wka, wq, wk, wvt, wf, wg, bf_pad, b, s)

    att_a = _swa(qa.reshape(b, s, SWA_Q), ka.reshape(b, s, SWA_KV), vta, swa_sinks[0].reshape(1, SWA_HEADS),
                 jnp.asarray(_swa_bias_table()))
    att_b = _fox(q3.reshape(b, s, FOX_PAIRS * FOX_DEPTH), k3.reshape(b, s, FOX_PAIRS * FOX_DEPTH),
                 vtb.reshape(b, FOX_HEADS, VT_ROWS, s))

    h = _mix(x2, att_a.reshape(t, SWA_Q), att_b.reshape(t, FOX_W), gates,
             w_br_swa[0].astype(BF16), w_br_fox[0].astype(BF16), w_mix_out[0].astype(BF16))
    out = _mlp_ple(h, p[0].reshape(t, PLE_DIM), g_mlp[0].reshape(1, d), w_ff1[0].astype(BF16), w_ff2[0].astype(BF16),
                   g_ple[0].reshape(1, d), w_ple_gate[0].astype(BF16), w_ple_proj[0].astype(BF16),
                   g_final.reshape(1, d))
    return out.reshape(b, s, d)
```

```python
import functools

import numpy as np
import jax
import jax.numpy as jnp
from jax import lax
from jax.experimental import pallas as pl
from jax.experimental.pallas import tpu as pltpu

D_MODEL = 1024
CHUNK = 64
PLE_DIM = 256
HEAD_DIM = 64
SWA_HEADS = 8
SWA_KV_HEADS = 2
SWA_GROUP = SWA_HEADS // SWA_KV_HEADS
WINDOW = 128
SWA_BLOCK = WINDOW
FOX_HEADS = 8
D_FF = 4 * D_MODEL
RMS_EPS = 1e-6
SWA_Q = SWA_HEADS * HEAD_DIM
SWA_KV = SWA_KV_HEADS * HEAD_DIM
FOX_W = FOX_HEADS * HEAD_DIM
SCALE = HEAD_DIM ** -0.5
LOG2E = float(np.log2(np.e))

LANES = 128
MXU_COLS = 256
NEG_BIG = -1e30
VMEM_LIMIT = 52 * 1024 * 1024

TM = 512
SWA_TILE = 512
FOX_TQ = 512
FF_CHUNK = 1024

FOX_PAIRS = FOX_HEADS // 2
FOX_DEPTH = 2 * LANES
BIAS_SLOT = 8
C_PARTS = 3
VT_ROWS = 80

F32 = jnp.float32
BF16 = jnp.bfloat16
NT_DIMS = (((1,), (1,)), ((), ()))


def _rms(x, g):
    return x * lax.rsqrt(jnp.mean(x * x, axis=-1, keepdims=True) + RMS_EPS) * g


def _const_spec(shape):
    return pl.BlockSpec(shape, lambda *_: (0,) * len(shape), pipeline_mode=pl.Buffered(1))


def _split3(v):
    hi = v.astype(BF16)
    r1 = v - hi.astype(F32)
    mid = r1.astype(BF16)
    lo = (r1 - mid.astype(F32)).astype(BF16)
    return hi, mid, lo


def _bias_placement():
    width = FOX_PAIRS * LANES
    place = np.zeros((C_PARTS * LANES, 2 * width), np.float32)
    ones = np.zeros((1, 2 * width), np.float32)
    for h in range(FOX_HEADS):
        base = (h // 2) * LANES + (h % 2) * BIAS_SLOT
        for part in range(C_PARTS):
            place[part * LANES + h, base + part] = 1.0
            ones[0, base + C_PARTS + part] = 1.0
            ones[0, width + base + part] = 1.0
            place[part * LANES + h, width + base + C_PARTS + part] = -1.0
    return place, ones


def _in_proj_kernel(x_ref, g_ref, wqa_ref, wka_ref, wq_ref, wk_ref, wvt_ref, wf_ref, wg_ref, bf_ref, place_ref, ones_ref,
                    vone_ref, qa_ref, ka_ref, vta_ref, q3_ref, k3_ref, vtb_ref, gates_ref, carry_ref, *, tiles_per_seq):
    i = pl.program_id(0)

    @pl.when(i % tiles_per_seq == 0)
    def _():
        carry_ref[...] = jnp.zeros_like(carry_ref)

    u = _rms(x_ref[...], g_ref[...]).astype(BF16)
    qa_ref[...] = (jnp.dot(u, wqa_ref[...], preferred_element_type=F32) * (SCALE * LOG2E)).astype(BF16)
    ka_ref[...] = jnp.dot(u, wka_ref[...], preferred_element_type=F32).astype(BF16)
    q = (jnp.dot(u, wq_ref[...], preferred_element_type=F32) * (SCALE * LOG2E)).astype(BF16)
    k = jnp.dot(u, wk_ref[...], preferred_element_type=F32).astype(BF16)
    vt = (lax.dot_general(wvt_ref[...], u, NT_DIMS, preferred_element_type=F32) + vone_ref[...]).astype(BF16)
    vta_ref[...] = vt[:SWA_KV_HEADS * VT_ROWS, :]
    vtb_ref[...] = vt[SWA_KV_HEADS * VT_ROWS:, :]
    gl = jnp.dot(u, wg_ref[...], preferred_element_type=F32)
    gates_ref[...] = jax.nn.sigmoid(gl).astype(BF16)

    f = jnp.dot(u, wf_ref[...], preferred_element_type=F32) + bf_ref[...]
    logf = jnp.minimum(f, 0.0) - jnp.log1p(jnp.exp(-jnp.abs(f)))
    tm = logf.shape[0]
    row = lax.broadcasted_iota(jnp.int32, (tm, tm), 0)
    col = lax.broadcasted_iota(jnp.int32, (tm, tm), 1)
    tri = (col <= row).astype(BF16)
    c = sum(jnp.dot(tri, part, preferred_element_type=F32) for part in _split3(logf)) + carry_ref[...]
    carry_ref[...] = c[tm - 1:tm, :]
    parts = jnp.concatenate(_split3(c * LOG2E), axis=1)
    bias = (jnp.dot(parts, place_ref[...], preferred_element_type=F32) + ones_ref[...]).astype(BF16)
    width = FOX_PAIRS * LANES
    for hp in range(FOX_PAIRS):
        lanes = slice(hp * LANES, (hp + 1) * LANES)
        q3_ref[:, hp * FOX_DEPTH:hp * FOX_DEPTH + LANES] = q[:, lanes]
        q3_ref[:, hp * FOX_DEPTH + LANES:(hp + 1) * FOX_DEPTH] = bias[:, lanes]
        k3_ref[:, hp * FOX_DEPTH:hp * FOX_DEPTH + LANES] = k[:, lanes]
        k3_ref[:, hp * FOX_DEPTH + LANES:(hp + 1) * FOX_DEPTH] = bias[:, width + hp * LANES:width + (hp + 1) * LANES]


def _in_proj(x2, g_mix, wqa, wka, wq, wk, wvt, wf, wg, bf_pad, batch, seq):
    t = x2.shape[0]
    tps = seq // TM
    place, ones = _bias_placement()
    n_vt = SWA_KV_HEADS + FOX_HEADS
    vone = np.zeros((n_vt * VT_ROWS, 1), np.float32)
    vone[HEAD_DIM::VT_ROWS] = 1.0
    consts = (jnp.asarray(place, BF16), jnp.asarray(ones), jnp.asarray(vone))
    kern = functools.partial(_in_proj_kernel, tiles_per_seq=tps)
    row_spec = lambda w: pl.BlockSpec((TM, w), lambda i: (i, 0))
    vt_spec = lambda heads: pl.BlockSpec((None, heads * VT_ROWS, TM), lambda i: (i // tps, 0, i % tps))
    operands = (x2, g_mix, wqa, wka, wq, wk, wvt, wf, wg, bf_pad) + consts
    widths = (SWA_Q, SWA_KV, FOX_PAIRS * FOX_DEPTH, FOX_PAIRS * FOX_DEPTH, wg.shape[1])
    qa, ka, q3, k3, gates = (jax.ShapeDtypeStruct((t, w), BF16) for w in widths)
    vta, vtb = (jax.ShapeDtypeStruct((batch, heads * VT_ROWS, seq), BF16) for heads in (SWA_KV_HEADS, FOX_HEADS))
    return pl.pallas_call(
        kern,
        grid=(t // TM,),
        in_specs=[row_spec(D_MODEL)] + [_const_spec(a.shape) for a in operands[1:]],
        out_specs=[row_spec(widths[0]), row_spec(widths[1]), vt_spec(SWA_KV_HEADS),
                   row_spec(widths[2]), row_spec(widths[3]), vt_spec(FOX_HEADS), row_spec(widths[4])],
        out_shape=[qa, ka, vta, q3, k3, vtb, gates],
        scratch_shapes=[pltpu.VMEM((1, LANES), F32)],
        compiler_params=pltpu.CompilerParams(dimension_semantics=("arbitrary",), vmem_limit_bytes=VMEM_LIMIT),
        name="in_proj",
    )(*operands)


def _swa_bias_table():
    sb = SWA_BLOCK
    qi = np.arange(sb)[None, :] + sb
    si = np.arange(2 * sb)[:, None]
    chunk_diff = qi // CHUNK - si // CHUNK
    band_ok = (chunk_diff >= 0) & (chunk_diff <= WINDOW // CHUNK)
    slopes = np.array([2.0 ** (-8.0 * (h + 1) / SWA_HEADS) for h in range(SWA_HEADS)], dtype=np.float32)
    alibi = -slopes[:, None, None] * np.abs(qi - si).astype(np.float32)[None] * np.float32(LOG2E)
    first = band_ok & (si >= sb)
    table = np.stack([np.where(first[None], alibi, NEG_BIG), np.where(band_ok[None], alibi, NEG_BIG)])
    table = table.reshape(2, SWA_KV_HEADS, SWA_GROUP, 2 * sb, sb).transpose(0, 1, 3, 2, 4)
    return np.ascontiguousarray(table.reshape(2, SWA_KV_HEADS, 2 * sb, SWA_GROUP * sb)).astype(np.float32)


def _swa_kernel(sink_ref, q_ref, kp_ref, kc_ref, vtp_ref, vtc_ref, bias_first_ref, bias_rest_ref, o_ref):
    sb = SWA_BLOCK
    qt = q_ref[...].astype(F32).T
    half = jnp.zeros((HEAD_DIM, SWA_GROUP * sb), F32)
    for j in range(SWA_TILE // sb):
        cols = slice(j * sb, (j + 1) * sb)
        if j == 0:
            kb = jnp.concatenate([kp_ref[...], kc_ref[0:sb, :]], axis=0)
            vtb = jnp.concatenate([vtp_ref[...], vtc_ref[:, 0:sb]], axis=1)
            bias_ref = bias_first_ref
        else:
            kb = kc_ref[(j - 1) * sb:(j + 1) * sb, :]
            vtb = vtc_ref[:, (j - 1) * sb:(j + 1) * sb]
            bias_ref = bias_rest_ref
        out_rows = []
        for kh in range(SWA_KV_HEADS):
            heads = [kh * SWA_GROUP + g for g in range(SWA_GROUP)]
            q4 = jnp.concatenate([qt[h * HEAD_DIM:(h + 1) * HEAD_DIM, cols] for h in heads], axis=1)
            q4t = jnp.concatenate([q4, half] if kh == 0 else [half, q4], axis=0).astype(BF16)
            st = jnp.dot(kb, q4t, preferred_element_type=F32) + bias_ref[kh]
            sink = jnp.concatenate([jnp.full((1, sb), sink_ref[0, h] * LOG2E, F32) for h in heads], axis=1)
            m = jnp.maximum(jnp.max(st, axis=0, keepdims=True), sink)
            p = jnp.exp2(st - m).astype(BF16)
            pv = jnp.dot(vtb[kh * VT_ROWS:(kh + 1) * VT_ROWS, :], p, preferred_element_type=F32)
            denom = pv[HEAD_DIM:HEAD_DIM + 1, :] + jnp.exp2(sink - m)
            o = pv[:HEAD_DIM, :] / denom
            out_rows += [o[:, g * sb:(g + 1) * sb] for g in range(SWA_GROUP)]
        o_ref[cols, :] = jnp.concatenate(out_rows, axis=0).T.astype(BF16)


def _swa(q3, k3, vt3, sinks, bias):
    b, s, _ = q3.shape
    sb, ts = SWA_BLOCK, SWA_TILE
    per = ts // sb
    prev = lambda n: jnp.maximum(n * per - 1, 0)
    bias_block = (None,) + bias.shape[1:]
    return pl.pallas_call(
        _swa_kernel,
        grid=(b, s // ts),
        in_specs=[pl.BlockSpec(memory_space=pltpu.SMEM),
                  pl.BlockSpec((None, ts, SWA_Q), lambda bi, n: (bi, n, 0)),
                  pl.BlockSpec((None, sb, SWA_KV), lambda bi, n: (bi, prev(n), 0)),
                  pl.BlockSpec((None, ts, SWA_KV), lambda bi, n: (bi, n, 0)),
                  pl.BlockSpec((None, SWA_KV_HEADS * VT_ROWS, sb), lambda bi, n: (bi, 0, prev(n))),
                  pl.BlockSpec((None, SWA_KV_HEADS * VT_ROWS, ts), lambda bi, n: (bi, 0, n)),
                  pl.BlockSpec(bias_block, lambda bi, n: (jnp.minimum(n, 1), 0, 0, 0)),
                  pl.BlockSpec(bias_block, lambda bi, n: (1, 0, 0, 0))],
        out_specs=pl.BlockSpec((None, ts, SWA_Q), lambda bi, n: (bi, n, 0)),
        out_shape=jax.ShapeDtypeStruct((b, s, SWA_Q), BF16),
        compiler_params=pltpu.CompilerParams(dimension_semantics=("parallel", "parallel"),
                                             vmem_limit_bytes=VMEM_LIMIT),
        name="swa",
    )(sinks, q3, k3, k3, vt3, vt3, bias, bias)


def _fox_kernel(q_ref, k_ref, vt_ref, o_ref, qt_sc, st_sc, mb_sc, m_sc, acc_sc):
    tq = q_ref.shape[0]
    qi = pl.program_id(2)
    qt = q_ref[...].astype(F32).T
    row = lax.broadcasted_iota(jnp.int32, (FOX_DEPTH, 1), 0)
    for hh in range(2):
        own = ((row >= hh * HEAD_DIM) & (row < (hh + 1) * HEAD_DIM)) | \
              ((row >= LANES + hh * BIAS_SLOT) & (row < LANES + (hh + 1) * BIAS_SLOT))
        qt_sc[hh] = jnp.where(own, qt, 0.0).astype(BF16)
    m_sc[...] = jnp.full_like(m_sc, NEG_BIG)
    acc_sc[...] = jnp.zeros_like(acc_sc)

    units = [(hh, slice(c * MXU_COLS, (c + 1) * MXU_COLS)) for hh in range(2) for c in range(tq // MXU_COLS)]

    def scores(block, slot, unit, diagonal):
        hh, cols = unit
        start = pl.multiple_of(block * tq, tq)
        k3 = k_ref[pl.ds(start, tq), :]
        st = jnp.dot(k3, qt_sc[hh, :, cols], preferred_element_type=F32)
        if diagonal:
            key = lax.broadcasted_iota(jnp.int32, st.shape, 0)
            qry = lax.broadcasted_iota(jnp.int32, st.shape, 1) + cols.start
            st = jnp.where(key <= qry, st, NEG_BIG)
        st_sc[slot, hh, :, cols] = st
        mb_sc[slot, hh, :, cols] = jnp.max(st, axis=0, keepdims=True)

    def consume(block, slot, unit):
        hh, cols = unit
        start = pl.multiple_of(block * tq, tq)
        m_prev = m_sc[hh, :, cols]
        m_new = jnp.maximum(m_prev, mb_sc[slot, hh, :, cols])
        p = jnp.exp2(st_sc[slot, hh, :, cols] - m_new).astype(BF16)
        alpha = jnp.exp2(m_prev - m_new)
        pv = jnp.dot(vt_ref[hh, :, pl.ds(start, tq)], p, preferred_element_type=F32)
        acc_sc[hh, :, cols] = alpha * acc_sc[hh, :, cols] + pv
        m_sc[hh, :, cols] = m_new

    def stage(next_block, next_slot, block, slot):
        for unit in units:
            scores(next_block, next_slot, unit, False)
            consume(block, slot, unit)

    for unit in units:
        scores(qi, 0, unit, True)
    stage(0, 1, qi, 0)

    def block_pair(t, carry):
        b0 = 2 * t
        stage(b0 + 1, 0, b0, 1)
        stage(jnp.minimum(b0 + 2, qi), 1, b0 + 1, 0)
        return carry

    lax.fori_loop(0, qi // 2, block_pair, 0)

    @pl.when(qi % 2 == 1)
    def _():
        for unit in units:
            consume(qi - 1, 1, unit)

    outs = [acc_sc[hh, :HEAD_DIM, :] / acc_sc[hh, HEAD_DIM:HEAD_DIM + 1, :] for hh in range(2)]
    o_ref[...] = jnp.concatenate(outs, axis=0).T.astype(BF16)


def _fox(q3, k3, vt4):
    b, s, _ = q3.shape
    tq = FOX_TQ
    return pl.pallas_call(
        _fox_kernel,
        grid=(b, FOX_PAIRS, s // tq),
        in_specs=[pl.BlockSpec((None, tq, FOX_DEPTH), lambda bi, hp, qi: (bi, qi, hp)),
                  pl.BlockSpec((None, s, FOX_DEPTH), lambda bi, hp, qi: (bi, 0, hp)),
                  pl.BlockSpec((None, 2, VT_ROWS, s), lambda bi, hp, qi: (bi, hp, 0, 0))],
        out_specs=pl.BlockSpec((None, tq, LANES), lambda bi, hp, qi: (bi, qi, hp)),
        out_shape=jax.ShapeDtypeStruct((b, s, FOX_W), BF16),
        scratch_shapes=[pltpu.VMEM((2, FOX_DEPTH, tq), BF16),
                        pltpu.VMEM((2, 2, tq, tq), F32),
                        pltpu.VMEM((2, 2, 1, tq), F32),
                        pltpu.VMEM((2, 1, tq), F32),
                        pltpu.VMEM((2, VT_ROWS, tq), F32)],
        compiler_params=pltpu.CompilerParams(dimension_semantics=("parallel", "parallel", "parallel"),
                                             vmem_limit_bytes=VMEM_LIMIT),
        name="fox",
    )(q3, k3, vt4)


def _mix_kernel(x_ref, ya_ref, yb_ref, gates_ref, wa_ref, wb_ref, wo_ref, h_ref):
    y_a = jnp.dot(ya_ref[...], wa_ref[...], preferred_element_type=F32)
    y_b = jnp.dot(yb_ref[...], wb_ref[...], preferred_element_type=F32)
    gates = gates_ref[...].astype(F32)
    mixed = gates[:, :D_MODEL] * y_a + gates[:, D_MODEL:] * y_b
    h_ref[...] = x_ref[...] + jnp.dot(mixed.astype(BF16), wo_ref[...], preferred_element_type=F32)


def _mix(x2, att_a, att_b, gates, w_a, w_b, w_o):
    t = x2.shape[0]
    row_spec = lambda w: pl.BlockSpec((TM, w), lambda i: (i, 0))
    return pl.pallas_call(
        _mix_kernel,
        grid=(t // TM,),
        in_specs=[row_spec(D_MODEL), row_spec(SWA_Q), row_spec(FOX_W), row_spec(2 * D_MODEL),
                  _const_spec(w_a.shape), _const_spec(w_b.shape), _const_spec(w_o.shape)],
        out_specs=row_spec(D_MODEL),
        out_shape=jax.ShapeDtypeStruct((t, D_MODEL), F32),
        compiler_params=pltpu.CompilerParams(dimension_semantics=("parallel",), vmem_limit_bytes=VMEM_LIMIT),
        name="mix",
    )(x2, att_a, att_b, gates, w_a, w_b, w_o)


def _mlp_ple_kernel(h_ref, p_ref, gm_ref, w1_ref, w2_ref, gp_ref, wpg_ref, wpp_ref, gf_ref, o_ref):
    h = h_ref[...]
    u = _rms(h, gm_ref[...]).astype(BF16)
    acc = jnp.zeros_like(h)
    for c in range(D_FF // FF_CHUNK):
        cols = slice(c * FF_CHUNK, (c + 1) * FF_CHUNK)
        a = jnp.dot(u, w1_ref[:, cols], preferred_element_type=F32)
        a = jnp.square(jnp.maximum(a, 0.0)).astype(BF16)
        acc = acc + jnp.dot(a, w2_ref[cols, :], preferred_element_type=F32)
    h = h + acc
    gate = jax.nn.sigmoid(jnp.dot(_rms(h, gp_ref[...]).astype(BF16), wpg_ref[...], preferred_element_type=F32))
    proj = jnp.dot(p_ref[...].astype(BF16), wpp_ref[...], preferred_element_type=F32)
    h = h + gate * proj
    o_ref[...] = _rms(h, gf_ref[...])


def _mlp_ple(h2, p2, g_mlp, w1, w2, g_ple, w_pg, w_pp, g_final):
    t = h2.shape[0]
    row_spec = lambda w: pl.BlockSpec((TM, w), lambda i: (i, 0))
    vec = _const_spec((1, D_MODEL))
    return pl.pallas_call(
        _mlp_ple_kernel,
        grid=(t // TM,),
        in_specs=[row_spec(D_MODEL), row_spec(PLE_DIM), vec, _const_spec(w1.shape), _const_spec(w2.shape),
                  vec, _const_spec(w_pg.shape), _const_spec(w_pp.shape), vec],
        out_specs=row_spec(D_MODEL),
        out_shape=jax.ShapeDtypeStruct((t, D_MODEL), F32),
        compiler_params=pltpu.CompilerParams(dimension_semantics=("parallel",), vmem_limit_bytes=VMEM_LIMIT),
        name="mlp_ple",
    )(h2, p2, g_mlp, w1, w2, g_ple, w_pg, w_pp, g_final)


def kernel(x, p, g_mix, w_in, b_forget, swa_sinks, w_br_swa, w_br_fox, w_mix_out,
           g_mlp, w_ff1, w_ff2, g_ple, w_ple_gate, w_ple_proj, g_final):
    b, s, d = x.shape
    assert d == D_MODEL and w_in.shape[0] == 1, "single-layer trunk with D_MODEL channels only"
    assert s % TM == 0 and s % FOX_TQ == 0 and s % SWA_TILE == 0 and SWA_TILE % SWA_BLOCK == 0
    t = b * s
    x2 = x.reshape(t, d)

    ka_end = SWA_Q + SWA_KV
    a_end = ka_end + SWA_KV
    q_end = a_end + FOX_W
    k_end = q_end + FOX_W
    v_end = k_end + FOX_W
    f_end = v_end + FOX_HEADS
    w = w_in[0]
    wqa = w[:, :SWA_Q].astype(BF16)
    wka = w[:, SWA_Q:ka_end].astype(BF16)
    wq = w[:, a_end:q_end].astype(BF16)
    wk = w[:, q_end:k_end].astype(BF16)
    wv = jnp.concatenate([w[:, ka_end:a_end], w[:, k_end:v_end]], axis=1).T.reshape(-1, HEAD_DIM, d)
    wvt = jnp.pad(wv, ((0, 0), (0, VT_ROWS - HEAD_DIM), (0, 0))).reshape(-1, d).astype(BF16)
    wf = jnp.pad(w[:, v_end:f_end], ((0, 0), (0, LANES - FOX_HEADS))).astype(BF16)
    wg = w[:, f_end:].astype(BF16)
    bf_pad = jnp.pad(b_forget[0], (0, LANES - FOX_HEADS)).reshape(1, LANES)

    qa, ka, vta, q3, k3, vtb, gates = _in_proj(x2, g_mix[0].reshape(1, d), wqa, wka, wq, wk, wvt, wf, wg, bf_pad, b, s)

    att_a = _swa(qa.reshape(b, s, SWA_Q), ka.reshape(b, s, SWA_KV), vta, swa_sinks[0].reshape(1, SWA_HEADS),
                 jnp.asarray(_swa_bias_table()))
    att_b = _fox(q3.reshape(b, s, FOX_PAIRS * FOX_DEPTH), k3.reshape(b, s, FOX_PAIRS * FOX_DEPTH),
                 vtb.reshape(b, FOX_HEADS, VT_ROWS, s))

    h = _mix(x2, att_a.reshape(t, SWA_Q), att_b.reshape(t, FOX_W), gates,
             w_br_swa[0].astype(BF16), w_br_fox[0].astype(BF16), w_mix_out[0].astype(BF16))
    out = _mlp_ple(h, p[0].reshape(t, PLE_DIM), g_mlp[0].reshape(1, d), w_ff1[0].astype(BF16), w_ff2[0].astype(BF16),
                   g_ple[0].reshape(1, d), w_ple_gate[0].astype(BF16), w_ple_proj[0].astype(BF16),
                   g_final.reshape(1, d))
    return out.reshape(b, s, d)
```

```python
import functools

import numpy as np
import jax
import jax.numpy as jnp
from jax import lax
from jax.experimental import pallas as pl
from jax.experimental.pallas import tpu as pltpu

D_MODEL = 1024
CHUNK = 64
PLE_DIM = 256
HEAD_DIM = 64
SWA_HEADS = 8
SWA_KV_HEADS = 2
SWA_GROUP = SWA_HEADS // SWA_KV_HEADS
WINDOW = 128
SWA_BLOCK = WINDOW
FOX_HEADS = 8
D_FF = 4 * D_MODEL
RMS_EPS = 1e-6
SWA_Q = SWA_HEADS * HEAD_DIM
SWA_KV = SWA_KV_HEADS * HEAD_DIM
FOX_W = FOX_HEADS * HEAD_DIM
SCALE = HEAD_DIM ** -0.5
LOG2E = float(np.log2(np.e))

LANES = 128
MXU_COLS = 256
NEG_BIG = -1e30
VMEM_LIMIT = 52 * 1024 * 1024

TM = 512
SWA_TILE = 512
SWA_SLOTS = 6
FOX_TQ = 512
FF_CHUNK = 1024

FOX_PAIRS = FOX_HEADS // 2
FOX_DEPTH = 2 * LANES
BIAS_SLOT = 8
C_PARTS = 3
VT_ROWS = 80

F32 = jnp.float32
BF16 = jnp.bfloat16
NT_DIMS = (((1,), (1,)), ((), ()))


def _rms(x, g):
    return x * lax.rsqrt(jnp.mean(x * x, axis=-1, keepdims=True) + RMS_EPS) * g


def _const_spec(shape):
    return pl.BlockSpec(shape, lambda *_: (0,) * len(shape), pipeline_mode=pl.Buffered(1))


def _split3(v):
    hi = v.astype(BF16)
    r1 = v - hi.astype(F32)
    mid = r1.astype(BF16)
    lo = (r1 - mid.astype(F32)).astype(BF16)
    return hi, mid, lo


def _bias_placement():
    place_qt = np.zeros((LANES, C_PARTS * LANES), np.float32)
    ones_qt = np.zeros((LANES, 1), np.float32)
    place_k = np.zeros((C_PARTS * LANES, LANES), np.float32)
    ones_k = np.zeros((1, LANES), np.float32)
    for h in range(FOX_HEADS):
        for part in range(C_PARTS):
            place_qt[h * BIAS_SLOT + part, part * LANES + h] = 1.0
            ones_qt[h * BIAS_SLOT + C_PARTS + part, 0] = 1.0
            ones_k[0, h * BIAS_SLOT + part] = 1.0
            place_k[part * LANES + h, h * BIAS_SLOT + C_PARTS + part] = -1.0
    return (place_qt, ones_qt), (place_k, ones_k)


def _in_proj_kernel(x_ref, g_ref, wt_ref, wkf_ref, wk_ref, wg_ref, bf_ref, pqt_ref, oqt_ref, pk_ref, ok_ref, vone_ref,
                    qat_ref, ka_ref, vta_ref, q3t_ref, k3_ref, vtb_ref, gates_ref, carry_ref, *, tiles_per_seq):
    i = pl.program_id(0)

    @pl.when(i % tiles_per_seq == 0)
    def _():
        carry_ref[...] = jnp.zeros_like(carry_ref)

    u = _rms(x_ref[...], g_ref[...]).astype(BF16)
    tt = lax.dot_general(wt_ref[...], u, NT_DIMS, preferred_element_type=F32)
    qat_ref[...] = (tt[:SWA_Q] * (SCALE * LOG2E)).astype(BF16)
    qt = (tt[SWA_Q:SWA_Q + FOX_W] * (SCALE * LOG2E)).astype(BF16)
    vt = (tt[SWA_Q + FOX_W:] + vone_ref[...]).astype(BF16)
    vta_ref[...] = vt[:SWA_KV_HEADS * VT_ROWS]
    vtb_ref[...] = vt[SWA_KV_HEADS * VT_ROWS:]
    kf = jnp.dot(u, wkf_ref[...], preferred_element_type=F32)
    ka_ref[...] = kf[:, :SWA_KV].astype(BF16)
    k = jnp.dot(u, wk_ref[...], preferred_element_type=F32).astype(BF16)
    gl = jnp.dot(u, wg_ref[...], preferred_element_type=F32)
    gates_ref[...] = jax.nn.sigmoid(gl).astype(BF16)

    f = kf[:, SWA_KV:] + bf_ref[...]
    logf = jnp.minimum(f, 0.0) - jnp.log1p(jnp.exp(-jnp.abs(f)))
    tm = logf.shape[0]
    row = lax.broadcasted_iota(jnp.int32, (tm, tm), 0)
    col = lax.broadcasted_iota(jnp.int32, (tm, tm), 1)
    tri = (col <= row).astype(BF16)
    cs = jnp.dot(tri, jnp.concatenate(_split3(logf), axis=1), preferred_element_type=F32)
    c = cs[:, :LANES] + cs[:, LANES:2 * LANES] + cs[:, 2 * LANES:] + carry_ref[...]
    carry_ref[...] = c[tm - 1:tm, :]
    parts = jnp.concatenate(_split3(c * LOG2E), axis=1)
    kbias = (jnp.dot(parts, pk_ref[...], preferred_element_type=F32) + ok_ref[...]).astype(BF16)
    qbias_t = (lax.dot_general(pqt_ref[...], parts, NT_DIMS, preferred_element_type=F32)
               + oqt_ref[...]).astype(BF16)
    for hp in range(FOX_PAIRS):
        lanes = slice(hp * LANES, (hp + 1) * LANES)
        q3t_ref[hp, :LANES, :] = qt[lanes]
        q3t_ref[hp, LANES:, :] = qbias_t
        k3_ref[:, hp * FOX_DEPTH:hp * FOX_DEPTH + LANES] = k[:, lanes]
        k3_ref[:, hp * FOX_DEPTH + LANES:(hp + 1) * FOX_DEPTH] = kbias


def _in_proj(x2, g_mix, wt, wkf, wk, wg, bf_pad, batch, seq):
    t = x2.shape[0]
    tps = seq // TM
    (place_qt, ones_qt), (place_k, ones_k) = _bias_placement()
    n_vt = SWA_KV_HEADS + FOX_HEADS
    vone = np.zeros((n_vt * VT_ROWS, 1), np.float32)
    vone[HEAD_DIM::VT_ROWS] = 1.0
    consts = (jnp.asarray(place_qt, BF16), jnp.asarray(ones_qt), jnp.asarray(place_k, BF16), jnp.asarray(ones_k),
              jnp.asarray(vone))
    kern = functools.partial(_in_proj_kernel, tiles_per_seq=tps)
    row_spec = lambda w: pl.BlockSpec((TM, w), lambda i: (i, 0))
    col_spec = lambda *lead: pl.BlockSpec((None,) + lead + (TM,), lambda i: (i // tps,) + (0,) * len(lead) + (i % tps,))
    operands = (x2, g_mix, wt, wkf, wk, wg, bf_pad) + consts
    ka, k3, gates = (jax.ShapeDtypeStruct((t, w), BF16) for w in (SWA_KV, FOX_PAIRS * FOX_DEPTH, wg.shape[1]))
    qat = jax.ShapeDtypeStruct((batch, SWA_Q, seq), BF16)
    q3t = jax.ShapeDtypeStruct((batch, FOX_PAIRS, FOX_DEPTH, seq), BF16)
    vta, vtb = (jax.ShapeDtypeStruct((batch, heads * VT_ROWS, seq), BF16) for heads in (SWA_KV_HEADS, FOX_HEADS))
    return pl.pallas_call(
        kern,
        grid=(t // TM,),
        in_specs=[row_spec(D_MODEL)] + [_const_spec(a.shape) for a in operands[1:]],
        out_specs=[col_spec(SWA_Q), row_spec(SWA_KV), col_spec(SWA_KV_HEADS * VT_ROWS),
                   col_spec(FOX_PAIRS, FOX_DEPTH), row_spec(FOX_PAIRS * FOX_DEPTH), col_spec(FOX_HEADS * VT_ROWS),
                   row_spec(wg.shape[1])],
        out_shape=[qat, ka, vta, q3t, k3, vtb, gates],
        scratch_shapes=[pltpu.VMEM((1, LANES), F32)],
        compiler_params=pltpu.CompilerParams(dimension_semantics=("arbitrary",), vmem_limit_bytes=VMEM_LIMIT),
        name="in_proj",
    )(*operands)


def _swa_bias_table():
    sb = SWA_BLOCK
    qi = np.arange(sb)[None, :] + sb
    si = np.arange(2 * sb)[:, None]
    chunk_diff = qi // CHUNK - si // CHUNK
    band_ok = (chunk_diff >= 0) & (chunk_diff <= WINDOW // CHUNK)
    slopes = np.array([2.0 ** (-8.0 * (h + 1) / SWA_HEADS) for h in range(SWA_HEADS)], dtype=np.float32)
    alibi = -slopes[:, None, None] * np.abs(qi - si).astype(np.float32)[None] * np.float32(LOG2E)
    first = band_ok & (si >= sb)
    table = np.stack([np.where(first[None], alibi, NEG_BIG), np.where(band_ok[None], alibi, NEG_BIG)])
    table = table.reshape(2, SWA_KV_HEADS, SWA_GROUP, 2 * sb, sb).transpose(0, 1, 3, 2, 4)
    return np.ascontiguousarray(table.reshape(2, SWA_KV_HEADS, 2 * sb, SWA_GROUP * sb)).astype(np.float32)


def _swa_kernel(sink_ref, qt_ref, kp_ref, kc_ref, vtp_ref, vtc_ref, bias_first_ref, bias_rest_ref, o_ref,
                st_sc, mb_sc, ot_sc):
    sb = SWA_BLOCK
    per_unit = MXU_COLS // sb
    half = jnp.zeros((HEAD_DIM, MXU_COLS), BF16)
    units = [(j, kh, c) for j in range(SWA_TILE // sb) for kh in range(SWA_KV_HEADS)
             for c in range(SWA_GROUP // per_unit)]

    def band(j):
        if j == 0:
            kb = jnp.concatenate([kp_ref[...], kc_ref[0:sb, :]], axis=0)
            vtb = jnp.concatenate([vtp_ref[...], vtc_ref[:, 0:sb]], axis=1)
            return kb, vtb, bias_first_ref
        return kc_ref[(j - 1) * sb:(j + 1) * sb, :], vtc_ref[:, (j - 1) * sb:(j + 1) * sb], bias_rest_ref

    def unit_heads(kh, c):
        return [kh * SWA_GROUP + c * per_unit + g for g in range(per_unit)]

    def scores(i, slot):
        j, kh, c = units[i]
        kb, _, bias_ref = band(j)
        q2 = jnp.concatenate([qt_ref[h * HEAD_DIM:(h + 1) * HEAD_DIM, j * sb:(j + 1) * sb] for h in unit_heads(kh, c)],
                             axis=1)
        q2t = jnp.concatenate([q2, half] if kh == 0 else [half, q2], axis=0)
        st = jnp.dot(kb, q2t, preferred_element_type=F32) + bias_ref[kh, :, c * MXU_COLS:(c + 1) * MXU_COLS]
        st_sc[slot] = st
        mb_sc[slot] = jnp.max(st, axis=0, keepdims=True)

    def consume(unit, slot):
        j, kh, c = unit
        _, vtb, _ = band(j)
        heads = unit_heads(kh, c)
        sink = jnp.concatenate([jnp.full((1, sb), sink_ref[0, h] * LOG2E, F32) for h in heads], axis=1)
        m = jnp.maximum(mb_sc[slot], sink)
        p = jnp.exp2(st_sc[slot] - m).astype(BF16)
        pv = jnp.dot(vtb[kh * VT_ROWS:(kh + 1) * VT_ROWS, :], p, preferred_element_type=F32)
        denom = pv[HEAD_DIM:HEAD_DIM + 1, :] + jnp.exp2(sink - m)
        o = pv[:HEAD_DIM, :] / denom
        for g, h in enumerate(heads):
            ot_sc[h * HEAD_DIM:(h + 1) * HEAD_DIM, j * sb:(j + 1) * sb] = o[:, g * sb:(g + 1) * sb]

    slots = st_sc.shape[0]
    for i in range(slots - 1):
        scores(i, i)
    for i, unit in enumerate(units):
        ahead = i + slots - 1
        if ahead < len(units):
            scores(ahead, ahead % slots)
        consume(unit, i % slots)
    o_ref[...] = ot_sc[...].T.astype(BF16)


def _swa(qt3, k3, vt3, sinks, bias):
    b, s, _ = k3.shape
    sb, ts = SWA_BLOCK, SWA_TILE
    per = ts // sb
    prev = lambda n: jnp.maximum(n * per - 1, 0)
    bias_block = (None,) + bias.shape[1:]
    return pl.pallas_call(
        _swa_kernel,
        grid=(b, s // ts),
        in_specs=[pl.BlockSpec(memory_space=pltpu.SMEM),
                  pl.BlockSpec((None, SWA_Q, ts), lambda bi, n: (bi, 0, n)),
                  pl.BlockSpec((None, sb, SWA_KV), lambda bi, n: (bi, prev(n), 0)),
                  pl.BlockSpec((None, ts, SWA_KV), lambda bi, n: (bi, n, 0)),
                  pl.BlockSpec((None, SWA_KV_HEADS * VT_ROWS, sb), lambda bi, n: (bi, 0, prev(n))),
                  pl.BlockSpec((None, SWA_KV_HEADS * VT_ROWS, ts), lambda bi, n: (bi, 0, n)),
                  pl.BlockSpec(bias_block, lambda bi, n: (jnp.minimum(n, 1), 0, 0, 0)),
                  pl.BlockSpec(bias_block, lambda bi, n: (1, 0, 0, 0))],
        out_specs=pl.BlockSpec((None, ts, SWA_Q), lambda bi, n: (bi, n, 0)),
        out_shape=jax.ShapeDtypeStruct((b, s, SWA_Q), BF16),
        scratch_shapes=[pltpu.VMEM((SWA_SLOTS, 2 * sb, MXU_COLS), F32),
                        pltpu.VMEM((SWA_SLOTS, 1, MXU_COLS), F32),
                        pltpu.VMEM((SWA_Q, ts), F32)],
        compiler_params=pltpu.CompilerParams(dimension_semantics=("parallel", "parallel"),
                                             vmem_limit_bytes=VMEM_LIMIT),
        name="swa",
    )(sinks, qt3, k3, k3, vt3, vt3, bias, bias)


def _fox_kernel(qt_ref, k_ref, vt_ref, o_ref, qt_sc, st_sc, mb_sc, m_sc, acc_sc):
    tq = qt_ref.shape[1]
    hp = pl.program_id(1)
    qi = pl.program_id(2)
    qt = qt_ref[...]
    row = lax.broadcasted_iota(jnp.int32, (FOX_DEPTH, 1), 0)
    for hh in range(2):
        bias_lo = LANES + (2 * hp + hh) * BIAS_SLOT
        own = ((row >= hh * HEAD_DIM) & (row < (hh + 1) * HEAD_DIM)) | ((row >= bias_lo) & (row < bias_lo + BIAS_SLOT))
        qt_sc[hh] = jnp.where(own, qt, jnp.zeros_like(qt))
    m_sc[...] = jnp.full_like(m_sc, NEG_BIG)
    acc_sc[...] = jnp.zeros_like(acc_sc)

    units = [(hh, slice(c * MXU_COLS, (c + 1) * MXU_COLS)) for hh in range(2) for c in range(tq // MXU_COLS)]

    def scores(block, slot, unit, diagonal):
        hh, cols = unit
        start = pl.multiple_of(block * tq, tq)
        k3 = k_ref[pl.ds(start, tq), :]
        st = jnp.dot(k3, qt_sc[hh, :, cols], preferred_element_type=F32)
        if diagonal:
            key = lax.broadcasted_iota(jnp.int32, st.shape, 0)
            qry = lax.broadcasted_iota(jnp.int32, st.shape, 1) + cols.start
            st = jnp.where(key <= qry, st, NEG_BIG)
        st_sc[slot, hh, :, cols] = st
        mb_sc[slot, hh, :, cols] = jnp.max(st, axis=0, keepdims=True)

    def consume(block, slot, unit):
        hh, cols = unit
        start = pl.multiple_of(block * tq, tq)
        m_prev = m_sc[hh, :, cols]
        m_new = jnp.maximum(m_prev, mb_sc[slot, hh, :, cols])
        p = jnp.exp2(st_sc[slot, hh, :, cols] - m_new).astype(BF16)
        alpha = jnp.exp2(m_prev - m_new)
        pv = jnp.dot(vt_ref[hh, :, pl.ds(start, tq)], p, preferred_element_type=F32)
        acc_sc[hh, :, cols] = alpha * acc_sc[hh, :, cols] + pv
        m_sc[hh, :, cols] = m_new

    def stage(next_block, next_slot, block, slot):
        for unit in units:
            scores(next_block, next_slot, unit, False)
            consume(block, slot, unit)

    for unit in units:
        scores(qi, 0, unit, True)
    stage(0, 1, qi, 0)

    def block_pair(t, carry):
        b0 = 2 * t
        stage(b0 + 1, 0, b0, 1)
        stage(jnp.minimum(b0 + 2, qi), 1, b0 + 1, 0)
        return carry

    lax.fori_loop(0, qi // 2, block_pair, 0)

    @pl.when(qi % 2 == 1)
    def _():
        for unit in units:
            consume(qi - 1, 1, unit)

    outs = [acc_sc[hh, :HEAD_DIM, :] / acc_sc[hh, HEAD_DIM:HEAD_DIM + 1, :] for hh in range(2)]
    o_ref[...] = jnp.concatenate(outs, axis=0).T.astype(BF16)


def _fox(q3t, k3, vt4):
    b, s, _ = k3.shape
    tq = FOX_TQ
    return pl.pallas_call(
        _fox_kernel,
        grid=(b, FOX_PAIRS, s // tq),
        in_specs=[pl.BlockSpec((None, None, FOX_DEPTH, tq), lambda bi, hp, qi: (bi, hp, 0, qi)),
                  pl.BlockSpec((None, s, FOX_DEPTH), lambda bi, hp, qi: (bi, 0, hp)),
                  pl.BlockSpec((None, 2, VT_ROWS, s), lambda bi, hp, qi: (bi, hp, 0, 0))],
        out_specs=pl.BlockSpec((None, tq, LANES), lambda bi, hp, qi: (bi, qi, hp)),
        out_shape=jax.ShapeDtypeStruct((b, s, FOX_W), BF16),
        scratch_shapes=[pltpu.VMEM((2, FOX_DEPTH, tq), BF16),
                        pltpu.VMEM((2, 2, tq, tq), F32),
                        pltpu.VMEM((2, 2, 1, tq), F32),
                        pltpu.VMEM((2, 1, tq), F32),
                        pltpu.VMEM((2, VT_ROWS, tq), F32)],
        compiler_params=pltpu.CompilerParams(dimension_semantics=("parallel", "parallel", "parallel"),
                                             vmem_limit_bytes=VMEM_LIMIT),
        name="fox",
    )(q3t, k3, vt4)


def _mix_kernel(x_ref, ya_ref, yb_ref, gates_ref, wa_ref, wb_ref, wo_ref, h_ref):
    y_a = jnp.dot(ya_ref[...], wa_ref[...], preferred_element_type=F32)
    y_b = jnp.dot(yb_ref[...], wb_ref[...], preferred_element_type=F32)
    gates = gates_ref[...].astype(F32)
    mixed = gates[:, :D_MODEL] * y_a + gates[:, D_MODEL:] * y_b
    h_ref[...] = x_ref[...] + jnp.dot(mixed.astype(BF16), wo_ref[...], preferred_element_type=F32)


def _mix(x2, att_a, att_b, gates, w_a, w_b, w_o):
    t = x2.shape[0]
    row_spec = lambda w: pl.BlockSpec((TM, w), lambda i: (i, 0))
    return pl.pallas_call(
        _mix_kernel,
        grid=(t // TM,),
        in_specs=[row_spec(D_MODEL), row_spec(SWA_Q), row_spec(FOX_W), row_spec(2 * D_MODEL),
                  _const_spec(w_a.shape), _const_spec(w_b.shape), _const_spec(w_o.shape)],
        out_specs=row_spec(D_MODEL),
        out_shape=jax.ShapeDtypeStruct((t, D_MODEL), F32),
        compiler_params=pltpu.CompilerParams(dimension_semantics=("parallel",), vmem_limit_bytes=VMEM_LIMIT),
        name="mix",
    )(x2, att_a, att_b, gates, w_a, w_b, w_o)


def _mlp_ple_kernel(h_ref, p_ref, gm_ref, w1_ref, w2_ref, gp_ref, wpg_ref, wpp_ref, gf_ref, o_ref):
    h = h_ref[...]
    u = _rms(h, gm_ref[...]).astype(BF16)
    acc = jnp.zeros_like(h)
    for c in range(D_FF // FF_CHUNK):
        cols = slice(c * FF_CHUNK, (c + 1) * FF_CHUNK)
        a = jnp.dot(u, w1_ref[:, cols], preferred_element_type=F32)
        a = jnp.square(jnp.maximum(a, 0.0)).astype(BF16)
        acc = acc + jnp.dot(a, w2_ref[cols, :], preferred_element_type=F32)
    h = h + acc
    gate = jax.nn.sigmoid(jnp.dot(_rms(h, gp_ref[...]).astype(BF16), wpg_ref[...], preferred_element_type=F32))
    proj = jnp.dot(p_ref[...].astype(BF16), wpp_ref[...], preferred_element_type=F32)
    h = h + gate * proj
    o_ref[...] = _rms(h, gf_ref[...])


def _mlp_ple(h2, p2, g_mlp, w1, w2, g_ple, w_pg, w_pp, g_final):
    t = h2.shape[0]
    row_spec = lambda w: pl.BlockSpec((TM, w), lambda i: (i, 0))
    vec = _const_spec((1, D_MODEL))
    return pl.pallas_call(
        _mlp_ple_kernel,
        grid=(t // TM,),
        in_specs=[row_spec(D_MODEL), row_spec(PLE_DIM), vec, _const_spec(w1.shape), _const_spec(w2.shape),
                  vec, _const_spec(w_pg.shape), _const_spec(w_pp.shape), vec],
        out_specs=row_spec(D_MODEL),
        out_shape=jax.ShapeDtypeStruct((t, D_MODEL), F32),
        compiler_params=pltpu.CompilerParams(dimension_semantics=("parallel",), vmem_limit_bytes=VMEM_LIMIT),
        name="mlp_ple",
    )(h2, p2, g_mlp, w1, w2, g_ple, w_pg, w_pp, g_final)


def kernel(x, p, g_mix, w_in, b_forget, swa_sinks, w_br_swa, w_br_fox, w_mix_out,
           g_mlp, w_ff1, w_ff2, g_ple, w_ple_gate, w_ple_proj, g_final):
    b, s, d = x.shape
    assert d == D_MODEL and w_in.shape[0] == 1, "single-layer trunk with D_MODEL channels only"
    assert s % TM == 0 and s % FOX_TQ == 0 and s % SWA_TILE == 0 and SWA_TILE % SWA_BLOCK == 0
    t = b * s
    x2 = x.reshape(t, d)

    ka_end = SWA_Q + SWA_KV
    a_end = ka_end + SWA_KV
    q_end = a_end + FOX_W
    k_end = q_end + FOX_W
    v_end = k_end + FOX_W
    f_end = v_end + FOX_HEADS
    w = w_in[0]
    wv = jnp.concatenate([w[:, ka_end:a_end], w[:, k_end:v_end]], axis=1).T.reshape(-1, HEAD_DIM, d)
    wvt = jnp.pad(wv, ((0, 0), (0, VT_ROWS - HEAD_DIM), (0, 0))).reshape(-1, d)
    wt = jnp.concatenate([w[:, :SWA_Q].T, w[:, a_end:q_end].T, wvt], axis=0).astype(BF16)
    wf = jnp.pad(w[:, v_end:f_end], ((0, 0), (0, LANES - FOX_HEADS)))
    wkf = jnp.concatenate([w[:, SWA_Q:ka_end], wf], axis=1).astype(BF16)
    wk = w[:, q_end:k_end].astype(BF16)
    wg = w[:, f_end:].astype(BF16)
    bf_pad = jnp.pad(b_forget[0], (0, LANES - FOX_HEADS)).reshape(1, LANES)

    qat, ka, vta, q3t, k3, vtb, gates = _in_proj(x2, g_mix[0].reshape(1, d), wt, wkf, wk, wg, bf_pad, b, s)

    att_a = _swa(qat, ka.reshape(b, s, SWA_KV), vta, swa_sinks[0].reshape(1, SWA_HEADS),
                 jnp.asarray(_swa_bias_table()))
    att_b = _fox(q3t, k3.reshape(b, s, FOX_PAIRS * FOX_DEPTH), vtb.reshape(b, FOX_HEADS, VT_ROWS, s))

    h = _mix(x2, att_a.reshape(t, SWA_Q), att_b.reshape(t, FOX_W), gates,
             w_br_swa[0].astype(BF16), w_br_fox[0].astype(BF16), w_mix_out[0].astype(BF16))
    out = _mlp_ple(h, p[0].reshape(t, PLE_DIM), g_mlp[0].reshape(1, d), w_ff1[0].astype(BF16), w_ff2[0].astype(BF16),
                   g_ple[0].reshape(1, d), w_ple_gate[0].astype(BF16), w_ple_proj[0].astype(BF16),
                   g_final.reshape(1, d))
    return out.reshape(b, s, d)
```

```python
import functools

import numpy as np
import jax
import jax.numpy as jnp
from jax import lax
from jax.experimental import pallas as pl
from jax.experimental.pallas import tpu as pltpu

D_MODEL = 1024
CHUNK = 64
PLE_DIM = 256
HEAD_DIM = 64
SWA_HEADS = 8
SWA_KV_HEADS = 2
SWA_GROUP = SWA_HEADS // SWA_KV_HEADS
WINDOW = 128
SWA_BLOCK = WINDOW
FOX_HEADS = 8
D_FF = 4 * D_MODEL
RMS_EPS = 1e-6
SWA_Q = SWA_HEADS * HEAD_DIM
SWA_KV = SWA_KV_HEADS * HEAD_DIM
FOX_W = FOX_HEADS * HEAD_DIM
SCALE = HEAD_DIM ** -0.5
LOG2E = float(np.log2(np.e))

LANES = 128
MXU_COLS = 256
NEG_BIG = -1e30
VMEM_LIMIT = 52 * 1024 * 1024

TM = 512
SWA_TILE = 512
SWA_SLOTS = 6
FOX_TQ = 512
FF_CHUNK = 1024

FOX_PAIRS = FOX_HEADS // 2
FOX_DEPTH = LANES
BIAS_SLOT = 8
C_PARTS = 3
VT_ROWS = 80

F32 = jnp.float32
BF16 = jnp.bfloat16
NT_DIMS = (((1,), (1,)), ((), ()))


def _rms(x, g):
    return x * lax.rsqrt(jnp.mean(x * x, axis=-1, keepdims=True) + RMS_EPS) * g


def _const_spec(shape):
    return pl.BlockSpec(shape, lambda *_: (0,) * len(shape), pipeline_mode=pl.Buffered(1))


def _split3(v):
    hi = v.astype(BF16)
    r1 = v - hi.astype(F32)
    mid = r1.astype(BF16)
    lo = (r1 - mid.astype(F32)).astype(BF16)
    return hi, mid, lo


def _bias_placement():
    place_qt = np.zeros((LANES, C_PARTS * LANES), np.float32)
    ones_qt = np.zeros((LANES, 1), np.float32)
    place_k = np.zeros((C_PARTS * LANES, LANES), np.float32)
    ones_k = np.zeros((1, LANES), np.float32)
    for h in range(FOX_HEADS):
        for part in range(C_PARTS):
            place_qt[h * BIAS_SLOT + part, part * LANES + h] = 1.0
            ones_qt[h * BIAS_SLOT + C_PARTS + part, 0] = 1.0
            ones_k[0, h * BIAS_SLOT + part] = 1.0
            place_k[part * LANES + h, h * BIAS_SLOT + C_PARTS + part] = -1.0
    return (place_qt, ones_qt), (place_k, ones_k)


def _in_proj_kernel(x_ref, g_ref, wt_ref, wkf_ref, wk_ref, wg_ref, bf_ref, pqt_ref, oqt_ref, pk_ref, ok_ref, vone_ref,
                    qat_ref, ka_ref, vta_ref, q3t_ref, k3_ref, vtb_ref, gates_ref, carry_ref, *, tiles_per_seq):
    i = pl.program_id(0)

    @pl.when(i % tiles_per_seq == 0)
    def _():
        carry_ref[...] = jnp.zeros_like(carry_ref)

    u = _rms(x_ref[...], g_ref[...]).astype(BF16)
    tt = lax.dot_general(wt_ref[...], u, NT_DIMS, preferred_element_type=F32)
    qat_ref[...] = (tt[:SWA_Q] * (SCALE * LOG2E)).astype(BF16)
    qt = (tt[SWA_Q:SWA_Q + FOX_W] * (SCALE * LOG2E)).astype(BF16)
    vt = (tt[SWA_Q + FOX_W:] + vone_ref[...]).astype(BF16)
    vta_ref[...] = vt[:SWA_KV_HEADS * VT_ROWS]
    vtb_ref[...] = vt[SWA_KV_HEADS * VT_ROWS:]
    kf = jnp.dot(u, wkf_ref[...], preferred_element_type=F32)
    ka_ref[...] = kf[:, :SWA_KV].astype(BF16)
    k = jnp.dot(u, wk_ref[...], preferred_element_type=F32)
    gl = jnp.dot(u, wg_ref[...], preferred_element_type=F32)
    gates_ref[...] = jax.nn.sigmoid(gl).astype(BF16)

    f = kf[:, SWA_KV:] + bf_ref[...]
    logf = jnp.minimum(f, 0.0) - jnp.log1p(jnp.exp(-jnp.abs(f)))
    tm = logf.shape[0]
    row = lax.broadcasted_iota(jnp.int32, (tm, tm), 0)
    col = lax.broadcasted_iota(jnp.int32, (tm, tm), 1)
    tri = (col <= row).astype(BF16)
    cs = jnp.dot(tri, jnp.concatenate(_split3(logf), axis=1), preferred_element_type=F32)
    c = cs[:, :LANES] + cs[:, LANES:2 * LANES] + cs[:, 2 * LANES:] + carry_ref[...]
    carry_ref[...] = c[tm - 1:tm, :]
    parts = jnp.concatenate(_split3(c * LOG2E), axis=1)
    kbias = jnp.dot(parts, pk_ref[...], preferred_element_type=F32) + ok_ref[...]
    qbias_t = lax.dot_general(pqt_ref[...], parts, NT_DIMS, preferred_element_type=F32) + oqt_ref[...]
    lane = lax.broadcasted_iota(jnp.int32, (1, LANES), 1)
    zero_rows = jnp.zeros((HEAD_DIM - BIAS_SLOT, tm), F32)
    for h in range(FOX_HEADS):
        q3t_ref[h, :HEAD_DIM, :] = qt[h * HEAD_DIM:(h + 1) * HEAD_DIM]
        own = jnp.concatenate([qbias_t[h * BIAS_SLOT:(h + 1) * BIAS_SLOT], zero_rows], axis=0)
        q3t_ref[h, HEAD_DIM:, :] = own.astype(BF16)
        pair = k[:, (h // 2) * LANES:(h // 2 + 1) * LANES]
        dims = pair if h % 2 == 0 else pltpu.roll(pair, HEAD_DIM, 1)
        bias = pltpu.roll(kbias, HEAD_DIM - h * BIAS_SLOT, 1)
        k3_ref[:, h * FOX_DEPTH:(h + 1) * FOX_DEPTH] = jnp.where(lane < HEAD_DIM, dims, bias).astype(BF16)


def _in_proj(x2, g_mix, wt, wkf, wk, wg, bf_pad, batch, seq):
    t = x2.shape[0]
    tps = seq // TM
    (place_qt, ones_qt), (place_k, ones_k) = _bias_placement()
    n_vt = SWA_KV_HEADS + FOX_HEADS
    vone = np.zeros((n_vt * VT_ROWS, 1), np.float32)
    vone[HEAD_DIM::VT_ROWS] = 1.0
    consts = (jnp.asarray(place_qt, BF16), jnp.asarray(ones_qt), jnp.asarray(place_k, BF16), jnp.asarray(ones_k),
              jnp.asarray(vone))
    kern = functools.partial(_in_proj_kernel, tiles_per_seq=tps)
    row_spec = lambda w: pl.BlockSpec((TM, w), lambda i: (i, 0))
    col_spec = lambda *lead: pl.BlockSpec((None,) + lead + (TM,), lambda i: (i // tps,) + (0,) * len(lead) + (i % tps,))
    operands = (x2, g_mix, wt, wkf, wk, wg, bf_pad) + consts
    ka, k3, gates = (jax.ShapeDtypeStruct((t, w), BF16) for w in (SWA_KV, FOX_HEADS * FOX_DEPTH, wg.shape[1]))
    qat = jax.ShapeDtypeStruct((batch, SWA_Q, seq), BF16)
    q3t = jax.ShapeDtypeStruct((batch, FOX_HEADS, FOX_DEPTH, seq), BF16)
    vta, vtb = (jax.ShapeDtypeStruct((batch, heads * VT_ROWS, seq), BF16) for heads in (SWA_KV_HEADS, FOX_HEADS))
    return pl.pallas_call(
        kern,
        grid=(t // TM,),
        in_specs=[row_spec(D_MODEL)] + [_const_spec(a.shape) for a in operands[1:]],
        out_specs=[col_spec(SWA_Q), row_spec(SWA_KV), col_spec(SWA_KV_HEADS * VT_ROWS),
                   col_spec(FOX_HEADS, FOX_DEPTH), row_spec(FOX_HEADS * FOX_DEPTH), col_spec(FOX_HEADS * VT_ROWS),
                   row_spec(wg.shape[1])],
        out_shape=[qat, ka, vta, q3t, k3, vtb, gates],
        scratch_shapes=[pltpu.VMEM((1, LANES), F32)],
        compiler_params=pltpu.CompilerParams(dimension_semantics=("arbitrary",), vmem_limit_bytes=VMEM_LIMIT),
        name="in_proj",
    )(*operands)


def _swa_bias_table():
    sb = SWA_BLOCK
    qi = np.arange(sb)[None, :] + sb
    si = np.arange(2 * sb)[:, None]
    chunk_diff = qi // CHUNK - si // CHUNK
    band_ok = (chunk_diff >= 0) & (chunk_diff <= WINDOW // CHUNK)
    slopes = np.array([2.0 ** (-8.0 * (h + 1) / SWA_HEADS) for h in range(SWA_HEADS)], dtype=np.float32)
    alibi = -slopes[:, None, None] * np.abs(qi - si).astype(np.float32)[None] * np.float32(LOG2E)
    first = band_ok & (si >= sb)
    table = np.stack([np.where(first[None], alibi, NEG_BIG), np.where(band_ok[None], alibi, NEG_BIG)])
    table = table.reshape(2, SWA_KV_HEADS, SWA_GROUP, 2 * sb, sb).transpose(0, 1, 3, 2, 4)
    return np.ascontiguousarray(table.reshape(2, SWA_KV_HEADS, 2 * sb, SWA_GROUP * sb)).astype(np.float32)


def _swa_kernel(sink_ref, qt_ref, kp_ref, kc_ref, vtp_ref, vtc_ref, bias_first_ref, bias_rest_ref, o_ref,
                st_sc, mb_sc, ot_sc):
    sb = SWA_BLOCK
    per_unit = MXU_COLS // sb
    half = jnp.zeros((HEAD_DIM, MXU_COLS), BF16)
    units = [(j, kh, c) for j in range(SWA_TILE // sb) for kh in range(SWA_KV_HEADS)
             for c in range(SWA_GROUP // per_unit)]

    def band(j):
        if j == 0:
            kb = jnp.concatenate([kp_ref[...], kc_ref[0:sb, :]], axis=0)
            vtb = jnp.concatenate([vtp_ref[...], vtc_ref[:, 0:sb]], axis=1)
            return kb, vtb, bias_first_ref
        return kc_ref[(j - 1) * sb:(j + 1) * sb, :], vtc_ref[:, (j - 1) * sb:(j + 1) * sb], bias_rest_ref

    def unit_heads(kh, c):
        return [kh * SWA_GROUP + c * per_unit + g for g in range(per_unit)]

    def scores(i, slot):
        j, kh, c = units[i]
        kb, _, bias_ref = band(j)
        q2 = jnp.concatenate([qt_ref[h * HEAD_DIM:(h + 1) * HEAD_DIM, j * sb:(j + 1) * sb] for h in unit_heads(kh, c)],
                             axis=1)
        q2t = jnp.concatenate([q2, half] if kh == 0 else [half, q2], axis=0)
        st = jnp.dot(kb, q2t, preferred_element_type=F32) + bias_ref[kh, :, c * MXU_COLS:(c + 1) * MXU_COLS]
        st_sc[slot] = st
        mb_sc[slot] = jnp.max(st, axis=0, keepdims=True)

    def consume(unit, slot):
        j, kh, c = unit
        _, vtb, _ = band(j)
        heads = unit_heads(kh, c)
        sink = jnp.concatenate([jnp.full((1, sb), sink_ref[0, h] * LOG2E, F32) for h in heads], axis=1)
        m = jnp.maximum(mb_sc[slot], sink)
        p = jnp.exp2(st_sc[slot] - m).astype(BF16)
        pv = jnp.dot(vtb[kh * VT_ROWS:(kh + 1) * VT_ROWS, :], p, preferred_element_type=F32)
        denom = pv[HEAD_DIM:HEAD_DIM + 1, :] + jnp.exp2(sink - m)
        o = pv[:HEAD_DIM, :] / denom
        for g, h in enumerate(heads):
            ot_sc[h * HEAD_DIM:(h + 1) * HEAD_DIM, j * sb:(j + 1) * sb] = o[:, g * sb:(g + 1) * sb]

    slots = st_sc.shape[0]
    for i in range(slots - 1):
        scores(i, i)
    for i, unit in enumerate(units):
        ahead = i + slots - 1
        if ahead < len(units):
            scores(ahead, ahead % slots)
        consume(unit, i % slots)
    o_ref[...] = ot_sc[...].T.astype(BF16)


def _swa(qt3, k3, vt3, sinks, bias):
    b, s, _ = k3.shape
    sb, ts = SWA_BLOCK, SWA_TILE
    per = ts // sb
    prev = lambda n: jnp.maximum(n * per - 1, 0)
    bias_block = (None,) + bias.shape[1:]
    return pl.pallas_call(
        _swa_kernel,
        grid=(b, s // ts),
        in_specs=[pl.BlockSpec(memory_space=pltpu.SMEM),
                  pl.BlockSpec((None, SWA_Q, ts), lambda bi, n: (bi, 0, n)),
                  pl.BlockSpec((None, sb, SWA_KV), lambda bi, n: (bi, prev(n), 0)),
                  pl.BlockSpec((None, ts, SWA_KV), lambda bi, n: (bi, n, 0)),
                  pl.BlockSpec((None, SWA_KV_HEADS * VT_ROWS, sb), lambda bi, n: (bi, 0, prev(n))),
                  pl.BlockSpec((None, SWA_KV_HEADS * VT_ROWS, ts), lambda bi, n: (bi, 0, n)),
                  pl.BlockSpec(bias_block, lambda bi, n: (jnp.minimum(n, 1), 0, 0, 0)),
                  pl.BlockSpec(bias_block, lambda bi, n: (1, 0, 0, 0))],
        out_specs=pl.BlockSpec((None, ts, SWA_Q), lambda bi, n: (bi, n, 0)),
        out_shape=jax.ShapeDtypeStruct((b, s, SWA_Q), BF16),
        scratch_shapes=[pltpu.VMEM((SWA_SLOTS, 2 * sb, MXU_COLS), F32),
                        pltpu.VMEM((SWA_SLOTS, 1, MXU_COLS), F32),
                        pltpu.VMEM((SWA_Q, ts), F32)],
        compiler_params=pltpu.CompilerParams(dimension_semantics=("parallel", "parallel"),
                                             vmem_limit_bytes=VMEM_LIMIT),
        name="swa",
    )(sinks, qt3, k3, k3, vt3, vt3, bias, bias)


def _fox_kernel(qt_ref, k_ref, vt_ref, o_ref, st_sc, mb_sc, m_sc, acc_sc):
    tq = qt_ref.shape[2]
    qi = pl.program_id(2)
    m_sc[...] = jnp.full_like(m_sc, NEG_BIG)
    acc_sc[...] = jnp.zeros_like(acc_sc)

    units = [(hh, slice(c * MXU_COLS, (c + 1) * MXU_COLS)) for hh in range(2) for c in range(tq // MXU_COLS)]

    def scores(block, slot, unit, diagonal):
        hh, cols = unit
        start = pl.multiple_of(block * tq, tq)
        k3 = k_ref[pl.ds(start, tq), hh * FOX_DEPTH:(hh + 1) * FOX_DEPTH]
        st = jnp.dot(k3, qt_ref[hh, :, cols], preferred_element_type=F32)
        if diagonal:
            key = lax.broadcasted_iota(jnp.int32, st.shape, 0)
            qry = lax.broadcasted_iota(jnp.int32, st.shape, 1) + cols.start
            st = jnp.where(key <= qry, st, NEG_BIG)
        st_sc[slot, hh, :, cols] = st
        mb_sc[slot, hh, :, cols] = jnp.max(st, axis=0, keepdims=True)

    def consume(block, slot, unit):
        hh, cols = unit
        start = pl.multiple_of(block * tq, tq)
        m_prev = m_sc[hh, :, cols]
        m_new = jnp.maximum(m_prev, mb_sc[slot, hh, :, cols])
        p = jnp.exp2(st_sc[slot, hh, :, cols] - m_new).astype(BF16)
        alpha = jnp.exp2(m_prev - m_new)
        pv = jnp.dot(vt_ref[hh, :, pl.ds(start, tq)], p, preferred_element_type=F32)
        acc_sc[hh, :, cols] = alpha * acc_sc[hh, :, cols] + pv
        m_sc[hh, :, cols] = m_new

    def stage(next_block, next_slot, block, slot):
        for unit in units:
            scores(next_block, next_slot, unit, False)
            consume(block, slot, unit)

    for unit in units:
        scores(qi, 0, unit, True)
    stage(0, 1, qi, 0)

    def block_pair(b0):
        stage(b0 + 1, 0, b0, 1)
        stage(jnp.minimum(b0 + 2, qi), 1, b0 + 1, 0)

    def block_quad(t, carry):
        block_pair(4 * t)
        block_pair(4 * t + 2)
        return carry

    lax.fori_loop(0, qi // 4, block_quad, 0)

    @pl.when(qi % 4 >= 2)
    def _():
        block_pair(qi // 4 * 4)

    @pl.when(qi % 2 == 1)
    def _():
        for unit in units:
            consume(qi - 1, 1, unit)

    outs = [acc_sc[hh, :HEAD_DIM, :] / acc_sc[hh, HEAD_DIM:HEAD_DIM + 1, :] for hh in range(2)]
    o_ref[...] = jnp.concatenate(outs, axis=0).T.astype(BF16)


def _fox(q3t, k3, vt4):
    b, s, _ = k3.shape
    tq = FOX_TQ
    return pl.pallas_call(
        _fox_kernel,
        grid=(b, FOX_PAIRS, s // tq),
        in_specs=[pl.BlockSpec((None, 2, FOX_DEPTH, tq), lambda bi, hp, qi: (bi, hp, 0, qi)),
                  pl.BlockSpec((None, s, 2 * FOX_DEPTH), lambda bi, hp, qi: (bi, 0, hp)),
                  pl.BlockSpec((None, 2, VT_ROWS, s), lambda bi, hp, qi: (bi, hp, 0, 0))],
        out_specs=pl.BlockSpec((None, tq, LANES), lambda bi, hp, qi: (bi, qi, hp)),
        out_shape=jax.ShapeDtypeStruct((b, s, FOX_W), BF16),
        scratch_shapes=[pltpu.VMEM((2, 2, tq, tq), F32),
                        pltpu.VMEM((2, 2, 1, tq), F32),
                        pltpu.VMEM((2, 1, tq), F32),
                        pltpu.VMEM((2, VT_ROWS, tq), F32)],
        compiler_params=pltpu.CompilerParams(dimension_semantics=("parallel", "parallel", "parallel"),
                                             vmem_limit_bytes=VMEM_LIMIT),
        name="fox",
    )(q3t, k3, vt4)


def _mix_kernel(x_ref, ya_ref, yb_ref, gates_ref, wa_ref, wb_ref, wo_ref, h_ref):
    y_a = jnp.dot(ya_ref[...], wa_ref[...], preferred_element_type=F32)
    y_b = jnp.dot(yb_ref[...], wb_ref[...], preferred_element_type=F32)
    gates = gates_ref[...].astype(F32)
    mixed = gates[:, :D_MODEL] * y_a + gates[:, D_MODEL:] * y_b
    h_ref[...] = x_ref[...] + jnp.dot(mixed.astype(BF16), wo_ref[...], preferred_element_type=F32)


def _mix(x2, att_a, att_b, gates, w_a, w_b, w_o):
    t = x2.shape[0]
    row_spec = lambda w: pl.BlockSpec((TM, w), lambda i: (i, 0))
    return pl.pallas_call(
        _mix_kernel,
        grid=(t // TM,),
        in_specs=[row_spec(D_MODEL), row_spec(SWA_Q), row_spec(FOX_W), row_spec(2 * D_MODEL),
                  _const_spec(w_a.shape), _const_spec(w_b.shape), _const_spec(w_o.shape)],
        out_specs=row_spec(D_MODEL),
        out_shape=jax.ShapeDtypeStruct((t, D_MODEL), F32),
        compiler_params=pltpu.CompilerParams(dimension_semantics=("parallel",), vmem_limit_bytes=VMEM_LIMIT),
        name="mix",
    )(x2, att_a, att_b, gates, w_a, w_b, w_o)


def _mlp_ple_kernel(h_ref, p_ref, gm_ref, w1_ref, w2_ref, gp_ref, wpg_ref, wpp_ref, gf_ref, o_ref):
    h = h_ref[...]
    u = _rms(h, gm_ref[...]).astype(BF16)
    acc = jnp.zeros_like(h)
    for c in range(D_FF // FF_CHUNK):
        cols = slice(c * FF_CHUNK, (c + 1) * FF_CHUNK)
        a = jnp.dot(u, w1_ref[:, cols], preferred_element_type=F32)
        a = jnp.square(jnp.maximum(a, 0.0)).astype(BF16)
        acc = acc + jnp.dot(a, w2_ref[cols, :], preferred_element_type=F32)
    h = h + acc
    gate = jax.nn.sigmoid(jnp.dot(_rms(h, gp_ref[...]).astype(BF16), wpg_ref[...], preferred_element_type=F32))
    proj = jnp.dot(p_ref[...].astype(BF16), wpp_ref[...], preferred_element_type=F32)
    h = h + gate * proj
    o_ref[...] = _rms(h, gf_ref[...])


def _mlp_ple(h2, p2, g_mlp, w1, w2, g_ple, w_pg, w_pp, g_final):
    t = h2.shape[0]
    row_spec = lambda w: pl.BlockSpec((TM, w), lambda i: (i, 0))
    vec = _const_spec((1, D_MODEL))
    return pl.pallas_call(
        _mlp_ple_kernel,
        grid=(t // TM,),
        in_specs=[row_spec(D_MODEL), row_spec(PLE_DIM), vec, _const_spec(w1.shape), _const_spec(w2.shape),
                  vec, _const_spec(w_pg.shape), _const_spec(w_pp.shape), vec],
        out_specs=row_spec(D_MODEL),
        out_shape=jax.ShapeDtypeStruct((t, D_MODEL), F32),
        compiler_params=pltpu.CompilerParams(dimension_semantics=("parallel",), vmem_limit_bytes=VMEM_LIMIT),
        name="mlp_ple",
    )(h2, p2, g_mlp, w1, w2, g_ple, w_pg, w_pp, g_final)


def kernel(x, p, g_mix, w_in, b_forget, swa_sinks, w_br_swa, w_br_fox, w_mix_out,
           g_mlp, w_ff1, w_ff2, g_ple, w_ple_gate, w_ple_proj, g_final):
    b, s, d = x.shape
    assert d == D_MODEL and w_in.shape[0] == 1, "single-layer trunk with D_MODEL channels only"
    assert s % TM == 0 and s % FOX_TQ == 0 and s % SWA_TILE == 0 and SWA_TILE % SWA_BLOCK == 0
    t = b * s
    x2 = x.reshape(t, d)

    ka_end = SWA_Q + SWA_KV
    a_end = ka_end + SWA_KV
    q_end = a_end + FOX_W
    k_end = q_end + FOX_W
    v_end = k_end + FOX_W
    f_end = v_end + FOX_HEADS
    w = w_in[0]
    wv = jnp.concatenate([w[:, ka_end:a_end], w[:, k_end:v_end]], axis=1).T.reshape(-1, HEAD_DIM, d)
    wvt = jnp.pad(wv, ((0, 0), (0, VT_ROWS - HEAD_DIM), (0, 0))).reshape(-1, d)
    wt = jnp.concatenate([w[:, :SWA_Q].T, w[:, a_end:q_end].T, wvt], axis=0).astype(BF16)
    wf = jnp.pad(w[:, v_end:f_end], ((0, 0), (0, LANES - FOX_HEADS)))
    wkf = jnp.concatenate([w[:, SWA_Q:ka_end], wf], axis=1).astype(BF16)
    wk = w[:, q_end:k_end].astype(BF16)
    wg = w[:, f_end:].astype(BF16)
    bf_pad = jnp.pad(b_forget[0], (0, LANES - FOX_HEADS)).reshape(1, LANES)

    qat, ka, vta, q3t, k3, vtb, gates = _in_proj(x2, g_mix[0].reshape(1, d), wt, wkf, wk, wg, bf_pad, b, s)

    att_a = _swa(qat, ka.reshape(b, s, SWA_KV), vta, swa_sinks[0].reshape(1, SWA_HEADS),
                 jnp.asarray(_swa_bias_table()))
    att_b = _fox(q3t, k3.reshape(b, s, FOX_HEADS * FOX_DEPTH), vtb.reshape(b, FOX_HEADS, VT_ROWS, s))

    h = _mix(x2, att_a.reshape(t, SWA_Q), att_b.reshape(t, FOX_W), gates,
             w_br_swa[0].astype(BF16), w_br_fox[0].astype(BF16), w_mix_out[0].astype(BF16))
    out = _mlp_ple(h, p[0].reshape(t, PLE_DIM), g_mlp[0].reshape(1, d), w_ff1[0].astype(BF16), w_ff2[0].astype(BF16),
                   g_ple[0].reshape(1, d), w_ple_gate[0].astype(BF16), w_ple_proj[0].astype(BF16),
                   g_final.reshape(1, d))
    return out.reshape(b, s, d)
```

```python
import functools

import numpy as np
import jax
import jax.numpy as jnp
from jax import lax
from jax.experimental import pallas as pl
from jax.experimental.pallas import tpu as pltpu

D_MODEL = 1024
CHUNK = 64
PLE_DIM = 256
HEAD_DIM = 64
SWA_HEADS = 8
SWA_KV_HEADS = 2
SWA_GROUP = SWA_HEADS // SWA_KV_HEADS
WINDOW = 128
SWA_BLOCK = WINDOW
FOX_HEADS = 8
D_FF = 4 * D_MODEL
RMS_EPS = 1e-6
SWA_Q = SWA_HEADS * HEAD_DIM
SWA_KV = SWA_KV_HEADS * HEAD_DIM
FOX_W = FOX_HEADS * HEAD_DIM
SCALE = HEAD_DIM ** -0.5
LOG2E = float(np.log2(np.e))

LANES = 128
MXU_COLS = 256
NEG_BIG = -1e30
VMEM_LIMIT = 52 * 1024 * 1024

TM = 512
SWA_TILE = 512
SWA_SLOTS = 6
FOX_TQ = 512
FF_CHUNK = 1024

FOX_PAIRS = FOX_HEADS // 2
FOX_DEPTH = LANES
BIAS_SLOT = 8
C_PARTS = 3
VT_ROWS = 80

F32 = jnp.float32
BF16 = jnp.bfloat16
NT_DIMS = (((1,), (1,)), ((), ()))


def _rms(x, g):
    return x * lax.rsqrt(jnp.mean(x * x, axis=-1, keepdims=True) + RMS_EPS) * g


def _const_spec(shape):
    return pl.BlockSpec(shape, lambda *_: (0,) * len(shape), pipeline_mode=pl.Buffered(1))


def _split3(v):
    hi = v.astype(BF16)
    r1 = v - hi.astype(F32)
    mid = r1.astype(BF16)
    lo = (r1 - mid.astype(F32)).astype(BF16)
    return hi, mid, lo


def _bias_placement():
    place_qt = np.zeros((LANES, C_PARTS * LANES), np.float32)
    ones_qt = np.zeros((LANES, 1), np.float32)
    place_k = np.zeros((C_PARTS * LANES, LANES), np.float32)
    ones_k = np.zeros((1, LANES), np.float32)
    for h in range(FOX_HEADS):
        for part in range(C_PARTS):
            place_qt[h * BIAS_SLOT + part, part * LANES + h] = 1.0
            ones_qt[h * BIAS_SLOT + C_PARTS + part, 0] = 1.0
            ones_k[0, h * BIAS_SLOT + part] = 1.0
            place_k[part * LANES + h, h * BIAS_SLOT + C_PARTS + part] = -1.0
    return (place_qt, ones_qt), (place_k, ones_k)


def _in_proj_kernel(x_ref, g_ref, wt_ref, wkf_ref, wk_ref, wg_ref, bf_ref, pqt_ref, oqt_ref, pk_ref, ok_ref, vone_ref,
                    qat_ref, ka_ref, vta_ref, q3t_ref, k3_ref, vtb_ref, gates_ref, carry_ref, *, tiles_per_seq):
    i = pl.program_id(0)

    @pl.when(i % tiles_per_seq == 0)
    def _():
        carry_ref[...] = jnp.zeros_like(carry_ref)

    u = _rms(x_ref[...], g_ref[...]).astype(BF16)
    tm = u.shape[0]
    kf = jnp.dot(u, wkf_ref[...], preferred_element_type=F32)
    ka_ref[...] = kf[:, :SWA_KV].astype(BF16)
    f = kf[:, SWA_KV:] + bf_ref[...]
    logf = jnp.minimum(f, 0.0) - jnp.log1p(jnp.exp(-jnp.abs(f)))

    gl = jnp.dot(u, wg_ref[...], preferred_element_type=F32)
    gates_ref[...] = jax.nn.sigmoid(gl).astype(BF16)

    row = lax.broadcasted_iota(jnp.int32, (tm, tm), 0)
    col = lax.broadcasted_iota(jnp.int32, (tm, tm), 1)
    tri = (col <= row).astype(BF16)
    cs = jnp.dot(tri, jnp.concatenate(_split3(logf), axis=1), preferred_element_type=F32)
    c = cs[:, :LANES] + cs[:, LANES:2 * LANES] + cs[:, 2 * LANES:] + carry_ref[...]
    carry_ref[...] = c[tm - 1:tm, :]

    tt = lax.dot_general(wt_ref[...], u, NT_DIMS, preferred_element_type=F32)
    qat_ref[...] = (tt[:SWA_Q] * (SCALE * LOG2E)).astype(BF16)
    qt = (tt[SWA_Q:SWA_Q + FOX_W] * (SCALE * LOG2E)).astype(BF16)
    vt = (tt[SWA_Q + FOX_W:] + vone_ref[...]).astype(BF16)
    vta_ref[...] = vt[:SWA_KV_HEADS * VT_ROWS]
    vtb_ref[...] = vt[SWA_KV_HEADS * VT_ROWS:]

    parts = jnp.concatenate(_split3(c * LOG2E), axis=1)
    kbias = jnp.dot(parts, pk_ref[...], preferred_element_type=F32) + ok_ref[...]
    qbias_t = lax.dot_general(pqt_ref[...], parts, NT_DIMS, preferred_element_type=F32) + oqt_ref[...]
    zero_rows = jnp.zeros((HEAD_DIM - BIAS_SLOT, tm), F32)
    for h in range(FOX_HEADS):
        q3t_ref[h, :HEAD_DIM, :] = qt[h * HEAD_DIM:(h + 1) * HEAD_DIM]
        own = jnp.concatenate([qbias_t[h * BIAS_SLOT:(h + 1) * BIAS_SLOT], zero_rows], axis=0)
        q3t_ref[h, HEAD_DIM:, :] = own.astype(BF16)

    k = jnp.dot(u, wk_ref[...], preferred_element_type=F32)
    lane = lax.broadcasted_iota(jnp.int32, (1, LANES), 1)
    for h in range(FOX_HEADS):
        pair = k[:, (h // 2) * LANES:(h // 2 + 1) * LANES]
        dims = pair if h % 2 == 0 else pltpu.roll(pair, HEAD_DIM, 1)
        bias = pltpu.roll(kbias, HEAD_DIM - h * BIAS_SLOT, 1)
        k3_ref[:, h * FOX_DEPTH:(h + 1) * FOX_DEPTH] = jnp.where(lane < HEAD_DIM, dims, bias).astype(BF16)


def _in_proj(x2, g_mix, wt, wkf, wk, wg, bf_pad, batch, seq):
    t = x2.shape[0]
    tps = seq // TM
    (place_qt, ones_qt), (place_k, ones_k) = _bias_placement()
    n_vt = SWA_KV_HEADS + FOX_HEADS
    vone = np.zeros((n_vt * VT_ROWS, 1), np.float32)
    vone[HEAD_DIM::VT_ROWS] = 1.0
    consts = (jnp.asarray(place_qt, BF16), jnp.asarray(ones_qt), jnp.asarray(place_k, BF16), jnp.asarray(ones_k),
              jnp.asarray(vone))
    kern = functools.partial(_in_proj_kernel, tiles_per_seq=tps)
    row_spec = lambda w: pl.BlockSpec((TM, w), lambda i: (i, 0))
    col_spec = lambda *lead: pl.BlockSpec((None,) + lead + (TM,), lambda i: (i // tps,) + (0,) * len(lead) + (i % tps,))
    operands = (x2, g_mix, wt, wkf, wk, wg, bf_pad) + consts
    ka, k3, gates = (jax.ShapeDtypeStruct((t, w), BF16) for w in (SWA_KV, FOX_HEADS * FOX_DEPTH, wg.shape[1]))
    qat = jax.ShapeDtypeStruct((batch, SWA_Q, seq), BF16)
    q3t = jax.ShapeDtypeStruct((batch, FOX_HEADS, FOX_DEPTH, seq), BF16)
    vta, vtb = (jax.ShapeDtypeStruct((batch, heads * VT_ROWS, seq), BF16) for heads in (SWA_KV_HEADS, FOX_HEADS))
    return pl.pallas_call(
        kern,
        grid=(t // TM,),
        in_specs=[row_spec(D_MODEL)] + [_const_spec(a.shape) for a in operands[1:]],
        out_specs=[col_spec(SWA_Q), row_spec(SWA_KV), col_spec(SWA_KV_HEADS * VT_ROWS),
                   col_spec(FOX_HEADS, FOX_DEPTH), row_spec(FOX_HEADS * FOX_DEPTH), col_spec(FOX_HEADS * VT_ROWS),
                   row_spec(wg.shape[1])],
        out_shape=[qat, ka, vta, q3t, k3, vtb, gates],
        scratch_shapes=[pltpu.VMEM((1, LANES), F32)],
        compiler_params=pltpu.CompilerParams(dimension_semantics=("arbitrary",), vmem_limit_bytes=VMEM_LIMIT),
        name="in_proj",
    )(*operands)


def _swa_bias_table():
    sb = SWA_BLOCK
    qi = np.arange(sb)[None, :] + sb
    si = np.arange(2 * sb)[:, None]
    chunk_diff = qi // CHUNK - si // CHUNK
    band_ok = (chunk_diff >= 0) & (chunk_diff <= WINDOW // CHUNK)
    slopes = np.array([2.0 ** (-8.0 * (h + 1) / SWA_HEADS) for h in range(SWA_HEADS)], dtype=np.float32)
    alibi = -slopes[:, None, None] * np.abs(qi - si).astype(np.float32)[None] * np.float32(LOG2E)
    first = band_ok & (si >= sb)
    table = np.stack([np.where(first[None], alibi, NEG_BIG), np.where(band_ok[None], alibi, NEG_BIG)])
    table = table.reshape(2, SWA_KV_HEADS, SWA_GROUP, 2 * sb, sb).transpose(0, 1, 3, 2, 4)
    return np.ascontiguousarray(table.reshape(2, SWA_KV_HEADS, 2 * sb, SWA_GROUP * sb)).astype(np.float32)


def _swa_kernel(sink_ref, qt_ref, kp_ref, kc_ref, vtp_ref, vtc_ref, bias_first_ref, bias_rest_ref, o_ref,
                st_sc, mb_sc, ot_sc):
    sb = SWA_BLOCK
    per_unit = MXU_COLS // sb
    half = jnp.zeros((HEAD_DIM, MXU_COLS), BF16)
    units = [(j, kh, c) for j in range(SWA_TILE // sb) for kh in range(SWA_KV_HEADS)
             for c in range(SWA_GROUP // per_unit)]

    def band(j):
        if j == 0:
            kb = jnp.concatenate([kp_ref[...], kc_ref[0:sb, :]], axis=0)
            vtb = jnp.concatenate([vtp_ref[...], vtc_ref[:, 0:sb]], axis=1)
            return kb, vtb, bias_first_ref
        return kc_ref[(j - 1) * sb:(j + 1) * sb, :], vtc_ref[:, (j - 1) * sb:(j + 1) * sb], bias_rest_ref

    def unit_heads(kh, c):
        return [kh * SWA_GROUP + c * per_unit + g for g in range(per_unit)]

    def scores(i, slot):
        j, kh, c = units[i]
        kb, _, bias_ref = band(j)
        q2 = jnp.concatenate([qt_ref[h * HEAD_DIM:(h + 1) * HEAD_DIM, j * sb:(j + 1) * sb] for h in unit_heads(kh, c)],
                             axis=1)
        q2t = jnp.concatenate([q2, half] if kh == 0 else [half, q2], axis=0)
        st = jnp.dot(kb, q2t, preferred_element_type=F32) + bias_ref[kh, :, c * MXU_COLS:(c + 1) * MXU_COLS]
        st_sc[slot] = st
        mb_sc[slot] = jnp.max(st, axis=0, keepdims=True)

    def consume(unit, slot):
        j, kh, c = unit
        _, vtb, _ = band(j)
        heads = unit_heads(kh, c)
        sink = jnp.concatenate([jnp.full((1, sb), sink_ref[0, h] * LOG2E, F32) for h in heads], axis=1)
        m = jnp.maximum(mb_sc[slot], sink)
        p = jnp.exp2(st_sc[slot] - m).astype(BF16)
        pv = jnp.dot(vtb[kh * VT_ROWS:(kh + 1) * VT_ROWS, :], p, preferred_element_type=F32)
        denom = pv[HEAD_DIM:HEAD_DIM + 1, :] + jnp.exp2(sink - m)
        o = pv[:HEAD_DIM, :] / denom
        for g, h in enumerate(heads):
            ot_sc[h * HEAD_DIM:(h + 1) * HEAD_DIM, j * sb:(j + 1) * sb] = o[:, g * sb:(g + 1) * sb]

    slots = st_sc.shape[0]
    for i in range(slots - 1):
        scores(i, i)
    for i, unit in enumerate(units):
        ahead = i + slots - 1
        if ahead < len(units):
            scores(ahead, ahead % slots)
        consume(unit, i % slots)
    o_ref[...] = ot_sc[...].T.astype(BF16)


def _swa(qt3, k3, vt3, sinks, bias):
    b, s, _ = k3.shape
    sb, ts = SWA_BLOCK, SWA_TILE
    per = ts // sb
    prev = lambda n: jnp.maximum(n * per - 1, 0)
    bias_block = (None,) + bias.shape[1:]
    return pl.pallas_call(
        _swa_kernel,
        grid=(b, s // ts),
        in_specs=[pl.BlockSpec(memory_space=pltpu.SMEM),
                  pl.BlockSpec((None, SWA_Q, ts), lambda bi, n: (bi, 0, n)),
                  pl.BlockSpec((None, sb, SWA_KV), lambda bi, n: (bi, prev(n), 0)),
                  pl.BlockSpec((None, ts, SWA_KV), lambda bi, n: (bi, n, 0)),
                  pl.BlockSpec((None, SWA_KV_HEADS * VT_ROWS, sb), lambda bi, n: (bi, 0, prev(n))),
                  pl.BlockSpec((None, SWA_KV_HEADS * VT_ROWS, ts), lambda bi, n: (bi, 0, n)),
                  pl.BlockSpec(bias_block, lambda bi, n: (jnp.minimum(n, 1), 0, 0, 0)),
                  pl.BlockSpec(bias_block, lambda bi, n: (1, 0, 0, 0))],
        out_specs=pl.BlockSpec((None, ts, SWA_Q), lambda bi, n: (bi, n, 0)),
        out_shape=jax.ShapeDtypeStruct((b, s, SWA_Q), BF16),
        scratch_shapes=[pltpu.VMEM((SWA_SLOTS, 2 * sb, MXU_COLS), F32),
                        pltpu.VMEM((SWA_SLOTS, 1, MXU_COLS), F32),
                        pltpu.VMEM((SWA_Q, ts), F32)],
        compiler_params=pltpu.CompilerParams(dimension_semantics=("parallel", "parallel"),
                                             vmem_limit_bytes=VMEM_LIMIT),
        name="swa",
    )(sinks, qt3, k3, k3, vt3, vt3, bias, bias)


def _fox_kernel(qt_ref, k_ref, vt_ref, o_ref, st_sc, mb_sc, m_sc, acc_sc):
    tq = qt_ref.shape[2]
    qi = pl.program_id(2)
    m_sc[...] = jnp.full_like(m_sc, NEG_BIG)
    acc_sc[...] = jnp.zeros_like(acc_sc)

    units = [(hh, slice(c * MXU_COLS, (c + 1) * MXU_COLS)) for hh in range(2) for c in range(tq // MXU_COLS)]

    def scores(block, slot, unit, diagonal):
        hh, cols = unit
        keys = cols.stop if diagonal else tq
        start = pl.multiple_of(block * tq, tq)
        k3 = k_ref[pl.ds(start, keys), hh * FOX_DEPTH:(hh + 1) * FOX_DEPTH]
        st = jnp.dot(k3, qt_ref[hh, :, cols], preferred_element_type=F32)
        if diagonal:
            key = lax.broadcasted_iota(jnp.int32, st.shape, 0)
            qry = lax.broadcasted_iota(jnp.int32, st.shape, 1) + cols.start
            st = jnp.where(key <= qry, st, NEG_BIG)
        st_sc[slot, hh, :keys, cols] = st
        mb_sc[slot, hh, :, cols] = jnp.max(st, axis=0, keepdims=True)

    def consume(block, slot, unit, diagonal=False):
        hh, cols = unit
        keys = cols.stop if diagonal else tq
        start = pl.multiple_of(block * tq, tq)
        m_prev = m_sc[hh, :, cols]
        m_new = jnp.maximum(m_prev, mb_sc[slot, hh, :, cols])
        p = jnp.exp2(st_sc[slot, hh, :keys, cols] - m_new).astype(BF16)
        alpha = jnp.exp2(m_prev - m_new)
        pv = jnp.dot(vt_ref[hh, :, pl.ds(start, keys)], p, preferred_element_type=F32)
        acc_sc[hh, :, cols] = alpha * acc_sc[hh, :, cols] + pv
        m_sc[hh, :, cols] = m_new

    def stage(next_block, next_slot, block, slot, diagonal=False):
        for unit in units:
            scores(next_block, next_slot, unit, False)
            consume(block, slot, unit, diagonal)

    for unit in units:
        scores(qi, 0, unit, True)
    stage(0, 1, qi, 0, diagonal=True)

    def block_pair(b0):
        stage(b0 + 1, 0, b0, 1)
        stage(jnp.minimum(b0 + 2, qi), 1, b0 + 1, 0)

    def block_quad(t, carry):
        block_pair(4 * t)
        block_pair(4 * t + 2)
        return carry

    lax.fori_loop(0, qi // 4, block_quad, 0)

    @pl.when(qi % 4 >= 2)
    def _():
        block_pair(qi // 4 * 4)

    @pl.when(qi % 2 == 1)
    def _():
        for unit in units:
            consume(qi - 1, 1, unit)

    outs = [acc_sc[hh, :HEAD_DIM, :] / acc_sc[hh, HEAD_DIM:HEAD_DIM + 1, :] for hh in range(2)]
    o_ref[...] = jnp.concatenate(outs, axis=0).T.astype(BF16)


def _fox(q3t, k3, vt4):
    b, s, _ = k3.shape
    tq = FOX_TQ
    return pl.pallas_call(
        _fox_kernel,
        grid=(b, FOX_PAIRS, s // tq),
        in_specs=[pl.BlockSpec((None, 2, FOX_DEPTH, tq), lambda bi, hp, qi: (bi, hp, 0, qi)),
                  pl.BlockSpec((None, s, 2 * FOX_DEPTH), lambda bi, hp, qi: (bi, 0, hp)),
                  pl.BlockSpec((None, 2, VT_ROWS, s), lambda bi, hp, qi: (bi, hp, 0, 0))],
        out_specs=pl.BlockSpec((None, tq, LANES), lambda bi, hp, qi: (bi, qi, hp)),
        out_shape=jax.ShapeDtypeStruct((b, s, FOX_W), BF16),
        scratch_shapes=[pltpu.VMEM((2, 2, tq, tq), F32),
                        pltpu.VMEM((2, 2, 1, tq), F32),
                        pltpu.VMEM((2, 1, tq), F32),
                        pltpu.VMEM((2, VT_ROWS, tq), F32)],
        compiler_params=pltpu.CompilerParams(dimension_semantics=("parallel", "parallel", "parallel"),
                                             vmem_limit_bytes=VMEM_LIMIT),
        name="fox",
    )(q3t, k3, vt4)


def _mix_kernel(x_ref, ya_ref, yb_ref, gates_ref, wa_ref, wb_ref, wo_ref, h_ref):
    y_a = jnp.dot(ya_ref[...], wa_ref[...], preferred_element_type=F32)
    y_b = jnp.dot(yb_ref[...], wb_ref[...], preferred_element_type=F32)
    gates = gates_ref[...].astype(F32)
    mixed = gates[:, :D_MODEL] * y_a + gates[:, D_MODEL:] * y_b
    h_ref[...] = x_ref[...] + jnp.dot(mixed.astype(BF16), wo_ref[...], preferred_element_type=F32)


def _mix(x2, att_a, att_b, gates, w_a, w_b, w_o):
    t = x2.shape[0]
    row_spec = lambda w: pl.BlockSpec((TM, w), lambda i: (i, 0))
    return pl.pallas_call(
        _mix_kernel,
        grid=(t // TM,),
        in_specs=[row_spec(D_MODEL), row_spec(SWA_Q), row_spec(FOX_W), row_spec(2 * D_MODEL),
                  _const_spec(w_a.shape), _const_spec(w_b.shape), _const_spec(w_o.shape)],
        out_specs=row_spec(D_MODEL),
        out_shape=jax.ShapeDtypeStruct((t, D_MODEL), F32),
        compiler_params=pltpu.CompilerParams(dimension_semantics=("parallel",), vmem_limit_bytes=VMEM_LIMIT),
        name="mix",
    )(x2, att_a, att_b, gates, w_a, w_b, w_o)


def _mlp_ple_kernel(h_ref, p_ref, gm_ref, w1_ref, w2_ref, gp_ref, wpg_ref, wpp_ref, gf_ref, o_ref):
    h = h_ref[...]
    u = _rms(h, gm_ref[...]).astype(BF16)
    acc = jnp.zeros_like(h)
    for c in range(D_FF // FF_CHUNK):
        cols = slice(c * FF_CHUNK, (c + 1) * FF_CHUNK)
        a = jnp.dot(u, w1_ref[:, cols], preferred_element_type=F32)
        a = jnp.square(jnp.maximum(a, 0.0)).astype(BF16)
        acc = acc + jnp.dot(a, w2_ref[cols, :], preferred_element_type=F32)
    h = h + acc
    gate = jax.nn.sigmoid(jnp.dot(_rms(h, gp_ref[...]).astype(BF16), wpg_ref[...], preferred_element_type=F32))
    proj = jnp.dot(p_ref[...].astype(BF16), wpp_ref[...], preferred_element_type=F32)
    h = h + gate * proj
    o_ref[...] = _rms(h, gf_ref[...])


def _mlp_ple(h2, p2, g_mlp, w1, w2, g_ple, w_pg, w_pp, g_final):
    t = h2.shape[0]
    row_spec = lambda w: pl.BlockSpec((TM, w), lambda i: (i, 0))
    vec = _const_spec((1, D_MODEL))
    return pl.pallas_call(
        _mlp_ple_kernel,
        grid=(t // TM,),
        in_specs=[row_spec(D_MODEL), row_spec(PLE_DIM), vec, _const_spec(w1.shape), _const_spec(w2.shape),
                  vec, _const_spec(w_pg.shape), _const_spec(w_pp.shape), vec],
        out_specs=row_spec(D_MODEL),
        out_shape=jax.ShapeDtypeStruct((t, D_MODEL), F32),
        compiler_params=pltpu.CompilerParams(dimension_semantics=("parallel",), vmem_limit_bytes=VMEM_LIMIT),
        name="mlp_ple",
    )(h2, p2, g_mlp, w1, w2, g_ple, w_pg, w_pp, g_final)


def kernel(x, p, g_mix, w_in, b_forget, swa_sinks, w_br_swa, w_br_fox, w_mix_out,
           g_mlp, w_ff1, w_ff2, g_ple, w_ple_gate, w_ple_proj, g_final):
    b, s, d = x.shape
    assert d == D_MODEL and w_in.shape[0] == 1, "single-layer trunk with D_MODEL channels only"
    assert s % TM == 0 and s % FOX_TQ == 0 and s % SWA_TILE == 0 and SWA_TILE % SWA_BLOCK == 0
    t = b * s
    x2 = x.reshape(t, d)

    ka_end = SWA_Q + SWA_KV
    a_end = ka_end + SWA_KV
    q_end = a_end + FOX_W
    k_end = q_end + FOX_W
    v_end = k_end + FOX_W
    f_end = v_end + FOX_HEADS
    w = w_in[0]
    wv = jnp.concatenate([w[:, ka_end:a_end], w[:, k_end:v_end]], axis=1).T.reshape(-1, HEAD_DIM, d)
    wvt = jnp.pad(wv, ((0, 0), (0, VT_ROWS - HEAD_DIM), (0, 0))).reshape(-1, d)
    wt = jnp.concatenate([w[:, :SWA_Q].T, w[:, a_end:q_end].T, wvt], axis=0).astype(BF16)
    wf = jnp.pad(w[:, v_end:f_end], ((0, 0), (0, LANES - FOX_HEADS)))
    wkf = jnp.concatenate([w[:, SWA_Q:ka_end], wf], axis=1).astype(BF16)
    wk = w[:, q_end:k_end].astype(BF16)
    wg = w[:, f_end:].astype(BF16)
    bf_pad = jnp.pad(b_forget[0], (0, LANES - FOX_HEADS)).reshape(1, LANES)

    qat, ka, vta, q3t, k3, vtb, gates = _in_proj(x2, g_mix[0].reshape(1, d), wt, wkf, wk, wg, bf_pad, b, s)

    att_a = _swa(qat, ka.reshape(b, s, SWA_KV), vta, swa_sinks[0].reshape(1, SWA_HEADS),
                 jnp.asarray(_swa_bias_table()))
    att_b = _fox(q3t, k3.reshape(b, s, FOX_HEADS * FOX_DEPTH), vtb.reshape(b, FOX_HEADS, VT_ROWS, s))

    h = _mix(x2, att_a.reshape(t, SWA_Q), att_b.reshape(t, FOX_W), gates,
             w_br_swa[0].astype(BF16), w_br_fox[0].astype(BF16), w_mix_out[0].astype(BF16))
    out = _mlp_ple(h, p[0].reshape(t, PLE_DIM), g_mlp[0].reshape(1, d), w_ff1[0].astype(BF16), w_ff2[0].astype(BF16),
                   g_ple[0].reshape(1, d), w_ple_gate[0].astype(BF16), w_ple_proj[0].astype(BF16),
                   g_final.reshape(1, d))
    return out.reshape(b, s, d)
```

```python
import functools

import numpy as np
import jax
import jax.numpy as jnp
from jax import lax
from jax.experimental import pallas as pl
from jax.experimental.pallas import tpu as pltpu

D_MODEL = 1024
CHUNK = 64
PLE_DIM = 256
HEAD_DIM = 64
SWA_HEADS = 8
SWA_KV_HEADS = 2
SWA_GROUP = SWA_HEADS // SWA_KV_HEADS
WINDOW = 128
SWA_BLOCK = WINDOW
FOX_HEADS = 8
D_FF = 4 * D_MODEL
RMS_EPS = 1e-6
SWA_Q = SWA_HEADS * HEAD_DIM
SWA_KV = SWA_KV_HEADS * HEAD_DIM
FOX_W = FOX_HEADS * HEAD_DIM
SCALE = HEAD_DIM ** -0.5
LOG2E = float(np.log2(np.e))

LANES = 128
BF16_SUBLANES = 16
MXU_COLS = 256
NEG_BIG = -1e30
VMEM_LIMIT = 52 * 1024 * 1024

TM = 512
MIX_TM = 1024
SWA_TILE = 512
SWA_SLOTS = 6
FOX_TQ = 512
FF_CHUNK = 1024

FOX_PAIRS = FOX_HEADS // 2
FOX_DEPTH = LANES
BIAS_SLOT = 8
C_PARTS = 3
VT_ROWS = 80

F32 = jnp.float32
BF16 = jnp.bfloat16
NT_DIMS = (((1,), (1,)), ((), ()))


def _rms(x, g):
    return x * lax.rsqrt(jnp.mean(x * x, axis=-1, keepdims=True) + RMS_EPS) * g


def _const_spec(shape):
    return pl.BlockSpec(shape, lambda *_: (0,) * len(shape), pipeline_mode=pl.Buffered(1))


def _split3(v):
    hi = v.astype(BF16)
    r1 = v - hi.astype(F32)
    mid = r1.astype(BF16)
    lo = (r1 - mid.astype(F32)).astype(BF16)
    return hi, mid, lo


def _bias_placement():
    place_qt = np.zeros((LANES, C_PARTS * LANES), np.float32)
    ones_qt = np.zeros((LANES, 1), np.float32)
    place_k = np.zeros((C_PARTS * LANES, LANES), np.float32)
    ones_k = np.zeros((1, LANES), np.float32)
    for h in range(FOX_HEADS):
        for part in range(C_PARTS):
            place_qt[h * BIAS_SLOT + part, part * LANES + h] = 1.0
            ones_qt[h * BIAS_SLOT + C_PARTS + part, 0] = 1.0
            ones_k[0, h * BIAS_SLOT + part] = 1.0
            place_k[part * LANES + h, h * BIAS_SLOT + C_PARTS + part] = -1.0
    return (place_qt, ones_qt), (place_k, ones_k)


def _in_proj_kernel(*refs, tiles_per_seq, n_cast):
    (x_ref, g_ref, wt_ref, wkf_ref, wk_ref, wg_ref, bf_ref, pqt_ref, oqt_ref, pk_ref, ok_ref, vone_ref) = refs[:12]
    cast_in = refs[12:12 + n_cast]
    qat_ref, ka_ref, vta_ref, q3t_ref, k3_ref, vtb_ref, gates_ref = refs[12 + n_cast:19 + n_cast]
    cast_out = refs[19 + n_cast:19 + 2 * n_cast]
    carry_ref = refs[19 + 2 * n_cast]
    i = pl.program_id(0)
    for src, dst in zip(cast_in, cast_out):
        dst[...] = src[...].astype(BF16)

    @pl.when(i % tiles_per_seq == 0)
    def _():
        carry_ref[...] = jnp.zeros_like(carry_ref)

    u = _rms(x_ref[...], g_ref[...]).astype(BF16)
    tm = u.shape[0]
    kf = jnp.dot(u, wkf_ref[...], preferred_element_type=F32)
    ka_ref[...] = kf[:, :SWA_KV].astype(BF16)
    f = kf[:, SWA_KV:] + bf_ref[...]
    logf = jnp.minimum(f, 0.0) - jnp.log1p(jnp.exp(-jnp.abs(f)))

    gl = jnp.dot(u, wg_ref[...], preferred_element_type=F32)
    gates_ref[...] = jax.nn.sigmoid(gl).astype(BF16)

    row = lax.broadcasted_iota(jnp.int32, (tm, tm), 0)
    col = lax.broadcasted_iota(jnp.int32, (tm, tm), 1)
    tri = (col <= row).astype(BF16)
    cs = jnp.dot(tri, jnp.concatenate(_split3(logf), axis=1), preferred_element_type=F32)
    c = cs[:, :LANES] + cs[:, LANES:2 * LANES] + cs[:, 2 * LANES:] + carry_ref[...]
    carry_ref[...] = c[tm - 1:tm, :]

    tt = lax.dot_general(wt_ref[...], u, NT_DIMS, preferred_element_type=F32)
    qat_ref[...] = (tt[:SWA_Q] * (SCALE * LOG2E)).astype(BF16)
    qt = (tt[SWA_Q:SWA_Q + FOX_W] * (SCALE * LOG2E)).astype(BF16)
    vt = (tt[SWA_Q + FOX_W:] + vone_ref[...]).astype(BF16)
    vta_ref[...] = vt[:SWA_KV_HEADS * VT_ROWS]
    vtb_ref[...] = vt[SWA_KV_HEADS * VT_ROWS:]

    parts = jnp.concatenate(_split3(c * LOG2E), axis=1)
    kbias = jnp.dot(parts, pk_ref[...], preferred_element_type=F32) + ok_ref[...]
    qbias_t = lax.dot_general(pqt_ref[...], parts, NT_DIMS, preferred_element_type=F32) + oqt_ref[...]
    zero_rows = jnp.zeros((HEAD_DIM - BIAS_SLOT, tm), F32)
    for h in range(FOX_HEADS):
        q3t_ref[h, :HEAD_DIM, :] = qt[h * HEAD_DIM:(h + 1) * HEAD_DIM]
        own = jnp.concatenate([qbias_t[h * BIAS_SLOT:(h + 1) * BIAS_SLOT], zero_rows], axis=0)
        q3t_ref[h, HEAD_DIM:, :] = own.astype(BF16)

    k = jnp.dot(u, wk_ref[...], preferred_element_type=F32)
    lane = lax.broadcasted_iota(jnp.int32, (1, LANES), 1)
    for h in range(FOX_HEADS):
        pair = k[:, (h // 2) * LANES:(h // 2 + 1) * LANES]
        dims = pair if h % 2 == 0 else pltpu.roll(pair, HEAD_DIM, 1)
        bias = pltpu.roll(kbias, HEAD_DIM - h * BIAS_SLOT, 1)
        k3_ref[:, h * FOX_DEPTH:(h + 1) * FOX_DEPTH] = jnp.where(lane < HEAD_DIM, dims, bias).astype(BF16)


def _in_proj(x2, g_mix, wt, wkf, wk, wg, bf_pad, batch, seq, later_weights):
    t = x2.shape[0]
    tps = seq // TM
    steps = t // TM
    (place_qt, ones_qt), (place_k, ones_k) = _bias_placement()
    n_vt = SWA_KV_HEADS + FOX_HEADS
    vone = np.zeros((n_vt * VT_ROWS, 1), np.float32)
    vone[HEAD_DIM::VT_ROWS] = 1.0
    consts = (jnp.asarray(place_qt, BF16), jnp.asarray(ones_qt), jnp.asarray(place_k, BF16), jnp.asarray(ones_k),
              jnp.asarray(vone))
    kern = functools.partial(_in_proj_kernel, tiles_per_seq=tps, n_cast=len(later_weights))
    row_spec = lambda w: pl.BlockSpec((TM, w), lambda i: (i, 0))
    col_spec = lambda *lead: pl.BlockSpec((None,) + lead + (TM,), lambda i: (i // tps,) + (0,) * len(lead) + (i % tps,))
    for w in later_weights:
        assert w.ndim == 2 and w.shape[0] % (steps * BF16_SUBLANES) == 0, w.shape
    slab_specs = [pl.BlockSpec((w.shape[0] // steps, w.shape[1]), lambda i: (i, 0)) for w in later_weights]
    operands = (x2, g_mix, wt, wkf, wk, wg, bf_pad) + consts
    ka, k3, gates = (jax.ShapeDtypeStruct((t, w), BF16) for w in (SWA_KV, FOX_HEADS * FOX_DEPTH, wg.shape[1]))
    qat = jax.ShapeDtypeStruct((batch, SWA_Q, seq), BF16)
    q3t = jax.ShapeDtypeStruct((batch, FOX_HEADS, FOX_DEPTH, seq), BF16)
    vta, vtb = (jax.ShapeDtypeStruct((batch, heads * VT_ROWS, seq), BF16) for heads in (SWA_KV_HEADS, FOX_HEADS))
    return pl.pallas_call(
        kern,
        grid=(steps,),
        in_specs=[row_spec(D_MODEL)] + [_const_spec(a.shape) for a in operands[1:]] + slab_specs,
        out_specs=[col_spec(SWA_Q), row_spec(SWA_KV), col_spec(SWA_KV_HEADS * VT_ROWS),
                   col_spec(FOX_HEADS, FOX_DEPTH), row_spec(FOX_HEADS * FOX_DEPTH), col_spec(FOX_HEADS * VT_ROWS),
                   row_spec(wg.shape[1])] + slab_specs,
        out_shape=[qat, ka, vta, q3t, k3, vtb, gates] + [jax.ShapeDtypeStruct(w.shape, BF16) for w in later_weights],
        scratch_shapes=[pltpu.VMEM((1, LANES), F32)],
        compiler_params=pltpu.CompilerParams(dimension_semantics=("arbitrary",), vmem_limit_bytes=VMEM_LIMIT),
        name="in_proj",
    )(*operands, *later_weights)


def _swa_bias_table():
    sb = SWA_BLOCK
    qi = np.arange(sb)[None, :] + sb
    si = np.arange(2 * sb)[:, None]
    chunk_diff = qi // CHUNK - si // CHUNK
    band_ok = (chunk_diff >= 0) & (chunk_diff <= WINDOW // CHUNK)
    slopes = np.array([2.0 ** (-8.0 * (h + 1) / SWA_HEADS) for h in range(SWA_HEADS)], dtype=np.float32)
    alibi = -slopes[:, None, None] * np.abs(qi - si).astype(np.float32)[None] * np.float32(LOG2E)
    first = band_ok & (si >= sb)
    table = np.stack([np.where(first[None], alibi, NEG_BIG), np.where(band_ok[None], alibi, NEG_BIG)])
    table = table.reshape(2, SWA_KV_HEADS, SWA_GROUP, 2 * sb, sb).transpose(0, 1, 3, 2, 4)
    return np.ascontiguousarray(table.reshape(2, SWA_KV_HEADS, 2 * sb, SWA_GROUP * sb)).astype(np.float32)


def _swa_kernel(sink_ref, qt_ref, kp_ref, kc_ref, vtp_ref, vtc_ref, bias_first_ref, bias_rest_ref, o_ref,
                st_sc, mb_sc, ot_sc):
    sb = SWA_BLOCK
    per_unit = MXU_COLS // sb
    half = jnp.zeros((HEAD_DIM, MXU_COLS), BF16)
    units = [(j, kh, c) for j in range(SWA_TILE // sb) for kh in range(SWA_KV_HEADS)
             for c in range(SWA_GROUP // per_unit)]

    def band(j):
        if j == 0:
            kb = jnp.concatenate([kp_ref[...], kc_ref[0:sb, :]], axis=0)
            vtb = jnp.concatenate([vtp_ref[...], vtc_ref[:, 0:sb]], axis=1)
            return kb, vtb, bias_first_ref
        return kc_ref[(j - 1) * sb:(j + 1) * sb, :], vtc_ref[:, (j - 1) * sb:(j + 1) * sb], bias_rest_ref

    def unit_heads(kh, c):
        return [kh * SWA_GROUP + c * per_unit + g for g in range(per_unit)]

    def scores(i, slot):
        j, kh, c = units[i]
        kb, _, bias_ref = band(j)
        q2 = jnp.concatenate([qt_ref[h * HEAD_DIM:(h + 1) * HEAD_DIM, j * sb:(j + 1) * sb] for h in unit_heads(kh, c)],
                             axis=1)
        q2t = jnp.concatenate([q2, half] if kh == 0 else [half, q2], axis=0)
        st = jnp.dot(kb, q2t, preferred_element_type=F32) + bias_ref[kh, :, c * MXU_COLS:(c + 1) * MXU_COLS]
        st_sc[slot] = st
        mb_sc[slot] = jnp.max(st, axis=0, keepdims=True)

    def consume(unit, slot):
        j, kh, c = unit
        _, vtb, _ = band(j)
        heads = unit_heads(kh, c)
        sink = jnp.concatenate([jnp.full((1, sb), sink_ref[0, h] * LOG2E, F32) for h in heads], axis=1)
        m = jnp.maximum(mb_sc[slot], sink)
        p = jnp.exp2(st_sc[slot] - m).astype(BF16)
        pv = jnp.dot(vtb[kh * VT_ROWS:(kh + 1) * VT_ROWS, :], p, preferred_element_type=F32)
        denom = pv[HEAD_DIM:HEAD_DIM + 1, :] + jnp.exp2(sink - m)
        o = pv[:HEAD_DIM, :] / denom
        for g, h in enumerate(heads):
            ot_sc[h * HEAD_DIM:(h + 1) * HEAD_DIM, j * sb:(j + 1) * sb] = o[:, g * sb:(g + 1) * sb]

    slots = st_sc.shape[0]
    for i in range(slots - 1):
        scores(i, i)
    for i, unit in enumerate(units):
        ahead = i + slots - 1
        if ahead < len(units):
            scores(ahead, ahead % slots)
        consume(unit, i % slots)
    o_ref[...] = ot_sc[...].T.astype(BF16)


def _swa(qt3, k3, vt3, sinks, bias):
    b, s, _ = k3.shape
    sb, ts = SWA_BLOCK, SWA_TILE
    per = ts // sb
    prev = lambda n: jnp.maximum(n * per - 1, 0)
    bias_block = (None,) + bias.shape[1:]
    return pl.pallas_call(
        _swa_kernel,
        grid=(b, s // ts),
        in_specs=[pl.BlockSpec(memory_space=pltpu.SMEM),
                  pl.BlockSpec((None, SWA_Q, ts), lambda bi, n: (bi, 0, n)),
                  pl.BlockSpec((None, sb, SWA_KV), lambda bi, n: (bi, prev(n), 0)),
                  pl.BlockSpec((None, ts, SWA_KV), lambda bi, n: (bi, n, 0)),
                  pl.BlockSpec((None, SWA_KV_HEADS * VT_ROWS, sb), lambda bi, n: (bi, 0, prev(n))),
                  pl.BlockSpec((None, SWA_KV_HEADS * VT_ROWS, ts), lambda bi, n: (bi, 0, n)),
                  pl.BlockSpec(bias_block, lambda bi, n: (jnp.minimum(n, 1), 0, 0, 0)),
                  pl.BlockSpec(bias_block, lambda bi, n: (1, 0, 0, 0))],
        out_specs=pl.BlockSpec((None, ts, SWA_Q), lambda bi, n: (bi, n, 0)),
        out_shape=jax.ShapeDtypeStruct((b, s, SWA_Q), BF16),
        scratch_shapes=[pltpu.VMEM((SWA_SLOTS, 2 * sb, MXU_COLS), F32),
                        pltpu.VMEM((SWA_SLOTS, 1, MXU_COLS), F32),
                        pltpu.VMEM((SWA_Q, ts), F32)],
        compiler_params=pltpu.CompilerParams(dimension_semantics=("parallel", "parallel"),
                                             vmem_limit_bytes=VMEM_LIMIT),
        name="swa",
    )(sinks, qt3, k3, k3, vt3, vt3, bias, bias)


def _fox_kernel(qt_ref, k_ref, vt_ref, o_ref, st_sc, mb_sc, m_sc, acc_sc):
    tq = qt_ref.shape[2]
    qi = pl.program_id(2)
    m_sc[...] = jnp.full_like(m_sc, NEG_BIG)
    acc_sc[...] = jnp.zeros_like(acc_sc)

    units = [(hh, slice(c * MXU_COLS, (c + 1) * MXU_COLS)) for hh in range(2) for c in range(tq // MXU_COLS)]

    def scores(block, slot, unit, diagonal):
        hh, cols = unit
        keys = cols.stop if diagonal else tq
        start = pl.multiple_of(block * tq, tq)
        k3 = k_ref[pl.ds(start, keys), hh * FOX_DEPTH:(hh + 1) * FOX_DEPTH]
        st = jnp.dot(k3, qt_ref[hh, :, cols], preferred_element_type=F32)
        if diagonal:
            key = lax.broadcasted_iota(jnp.int32, st.shape, 0)
            qry = lax.broadcasted_iota(jnp.int32, st.shape, 1) + cols.start
            st = jnp.where(key <= qry, st, NEG_BIG)
        st_sc[slot, hh, :keys, cols] = st
        mb_sc[slot, hh, :, cols] = jnp.max(st, axis=0, keepdims=True)

    def consume(block, slot, unit, diagonal=False):
        hh, cols = unit
        keys = cols.stop if diagonal else tq
        start = pl.multiple_of(block * tq, tq)
        m_prev = m_sc[hh, :, cols]
        m_new = jnp.maximum(m_prev, mb_sc[slot, hh, :, cols])
        p = jnp.exp2(st_sc[slot, hh, :keys, cols] - m_new).astype(BF16)
        alpha = jnp.exp2(m_prev - m_new)
        pv = jnp.dot(vt_ref[hh, :, pl.ds(start, keys)], p, preferred_element_type=F32)
        acc_sc[hh, :, cols] = alpha * acc_sc[hh, :, cols] + pv
        m_sc[hh, :, cols] = m_new

    def stage(next_block, next_slot, block, slot, diagonal=False):
        for unit in units:
            scores(next_block, next_slot, unit, False)
            consume(block, slot, unit, diagonal)

    for unit in units:
        scores(qi, 0, unit, True)
    stage(0, 1, qi, 0, diagonal=True)

    def block_pair(b0):
        stage(b0 + 1, 0, b0, 1)
        stage(jnp.minimum(b0 + 2, qi), 1, b0 + 1, 0)

    def block_quad(t, carry):
        block_pair(4 * t)
        block_pair(4 * t + 2)
        return carry

    lax.fori_loop(0, qi // 4, block_quad, 0)

    @pl.when(qi % 4 >= 2)
    def _():
        block_pair(qi // 4 * 4)

    @pl.when(qi % 2 == 1)
    def _():
        for unit in units:
            consume(qi - 1, 1, unit)

    outs = [acc_sc[hh, :HEAD_DIM, :] / acc_sc[hh, HEAD_DIM:HEAD_DIM + 1, :] for hh in range(2)]
    o_ref[...] = jnp.concatenate(outs, axis=0).T.astype(BF16)


def _fox(q3t, k3, vt4):
    b, s, _ = k3.shape
    tq = FOX_TQ
    return pl.pallas_call(
        _fox_kernel,
        grid=(b, FOX_PAIRS, s // tq),
        in_specs=[pl.BlockSpec((None, 2, FOX_DEPTH, tq), lambda bi, hp, qi: (bi, hp, 0, qi)),
                  pl.BlockSpec((None, s, 2 * FOX_DEPTH), lambda bi, hp, qi: (bi, 0, hp)),
                  pl.BlockSpec((None, 2, VT_ROWS, s), lambda bi, hp, qi: (bi, hp, 0, 0))],
        out_specs=pl.BlockSpec((None, tq, LANES), lambda bi, hp, qi: (bi, qi, hp)),
        out_shape=jax.ShapeDtypeStruct((b, s, FOX_W), BF16),
        scratch_shapes=[pltpu.VMEM((2, 2, tq, tq), F32),
                        pltpu.VMEM((2, 2, 1, tq), F32),
                        pltpu.VMEM((2, 1, tq), F32),
                        pltpu.VMEM((2, VT_ROWS, tq), F32)],
        compiler_params=pltpu.CompilerParams(dimension_semantics=("parallel", "parallel", "parallel"),
                                             vmem_limit_bytes=VMEM_LIMIT),
        name="fox",
    )(q3t, k3, vt4)


def _mix_kernel(x_ref, ya_ref, yb_ref, gates_ref, wa_ref, wb_ref, wo_ref, h_ref):
    slabs = [slice(r, r + TM) for r in range(0, MIX_TM, TM)]

    def branches(rows):
        y_a = jnp.dot(ya_ref[rows, :], wa_ref[...], preferred_element_type=F32)
        y_b = jnp.dot(yb_ref[rows, :], wb_ref[...], preferred_element_type=F32)
        return y_a, y_b

    pending = branches(slabs[0])
    for n, rows in enumerate(slabs):
        y_a, y_b = pending
        if n + 1 < len(slabs):
            pending = branches(slabs[n + 1])
        gates = gates_ref[rows, :].astype(F32)
        mixed = gates[:, :D_MODEL] * y_a + gates[:, D_MODEL:] * y_b
        h_ref[rows, :] = x_ref[rows, :] + jnp.dot(mixed.astype(BF16), wo_ref[...], preferred_element_type=F32)


def _mix(x2, att_a, att_b, gates, w_a, w_b, w_o):
    t = x2.shape[0]
    assert t % MIX_TM == 0 and MIX_TM % TM == 0
    row_spec = lambda w: pl.BlockSpec((MIX_TM, w), lambda i: (i, 0))
    return pl.pallas_call(
        _mix_kernel,
        grid=(t // MIX_TM,),
        in_specs=[row_spec(D_MODEL), row_spec(SWA_Q), row_spec(FOX_W), row_spec(2 * D_MODEL),
                  _const_spec(w_a.shape), _const_spec(w_b.shape), _const_spec(w_o.shape)],
        out_specs=row_spec(D_MODEL),
        out_shape=jax.ShapeDtypeStruct((t, D_MODEL), F32),
        compiler_params=pltpu.CompilerParams(dimension_semantics=("parallel",), vmem_limit_bytes=VMEM_LIMIT),
        name="mix",
    )(x2, att_a, att_b, gates, w_a, w_b, w_o)


def _mlp_ple_kernel(h_ref, p_ref, gm_ref, w1_ref, w2_ref, gp_ref, wpg_ref, wpp_ref, gf_ref, o_ref):
    h = h_ref[...]
    u = _rms(h, gm_ref[...]).astype(BF16)
    acc = jnp.zeros_like(h)
    for c in range(D_FF // FF_CHUNK):
        cols = slice(c * FF_CHUNK, (c + 1) * FF_CHUNK)
        a = jnp.dot(u, w1_ref[:, cols], preferred_element_type=F32)
        a = jnp.square(jnp.maximum(a, 0.0)).astype(BF16)
        acc = acc + jnp.dot(a, w2_ref[cols, :], preferred_element_type=F32)
    h = h + acc
    gate = jax.nn.sigmoid(jnp.dot(_rms(h, gp_ref[...]).astype(BF16), wpg_ref[...], preferred_element_type=F32))
    proj = jnp.dot(p_ref[...].astype(BF16), wpp_ref[...], preferred_element_type=F32)
    h = h + gate * proj
    o_ref[...] = _rms(h, gf_ref[...])


def _mlp_ple(h2, p2, g_mlp, w1, w2, g_ple, w_pg, w_pp, g_final):
    t = h2.shape[0]
    row_spec = lambda w: pl.BlockSpec((TM, w), lambda i: (i, 0))
    vec = _const_spec((1, D_MODEL))
    return pl.pallas_call(
        _mlp_ple_kernel,
        grid=(t // TM,),
        in_specs=[row_spec(D_MODEL), row_spec(PLE_DIM), vec, _const_spec(w1.shape), _const_spec(w2.shape),
                  vec, _const_spec(w_pg.shape), _const_spec(w_pp.shape), vec],
        out_specs=row_spec(D_MODEL),
        out_shape=jax.ShapeDtypeStruct((t, D_MODEL), F32),
        compiler_params=pltpu.CompilerParams(dimension_semantics=("parallel",), vmem_limit_bytes=VMEM_LIMIT),
        name="mlp_ple",
    )(h2, p2, g_mlp, w1, w2, g_ple, w_pg, w_pp, g_final)


def kernel(x, p, g_mix, w_in, b_forget, swa_sinks, w_br_swa, w_br_fox, w_mix_out,
           g_mlp, w_ff1, w_ff2, g_ple, w_ple_gate, w_ple_proj, g_final):
    b, s, d = x.shape
    assert d == D_MODEL and w_in.shape[0] == 1, "single-layer trunk with D_MODEL channels only"
    assert s % TM == 0 and s % FOX_TQ == 0 and s % SWA_TILE == 0 and SWA_TILE % SWA_BLOCK == 0
    t = b * s
    x2 = x.reshape(t, d)

    ka_end = SWA_Q + SWA_KV
    a_end = ka_end + SWA_KV
    q_end = a_end + FOX_W
    k_end = q_end + FOX_W
    v_end = k_end + FOX_W
    f_end = v_end + FOX_HEADS
    w = w_in[0]
    wv = jnp.concatenate([w[:, ka_end:a_end], w[:, k_end:v_end]], axis=1).T.reshape(-1, HEAD_DIM, d)
    wvt = jnp.pad(wv, ((0, 0), (0, VT_ROWS - HEAD_DIM), (0, 0))).reshape(-1, d)
    wt = jnp.concatenate([w[:, :SWA_Q].T, w[:, a_end:q_end].T, wvt], axis=0).astype(BF16)
    wf = jnp.pad(w[:, v_end:f_end], ((0, 0), (0, LANES - FOX_HEADS)))
    wkf = jnp.concatenate([w[:, SWA_Q:ka_end], wf], axis=1).astype(BF16)
    wk = w[:, q_end:k_end].astype(BF16)
    wg = w[:, f_end:].astype(BF16)
    bf_pad = jnp.pad(b_forget[0], (0, LANES - FOX_HEADS)).reshape(1, LANES)

    later = (w_br_swa[0], w_br_fox[0], w_mix_out[0], w_ff1[0], w_ff2[0], w_ple_gate[0])
    qat, ka, vta, q3t, k3, vtb, gates, *later_bf16 = _in_proj(x2, g_mix[0].reshape(1, d), wt, wkf, wk, wg, bf_pad, b, s, later)
    wa_b, wb_b, wo_b, w1_b, w2_b, wpg_b = later_bf16

    att_a = _swa(qat, ka.reshape(b, s, SWA_KV), vta, swa_sinks[0].reshape(1, SWA_HEADS),
                 jnp.asarray(_swa_bias_table()))
    att_b = _fox(q3t, k3.reshape(b, s, FOX_HEADS * FOX_DEPTH), vtb.reshape(b, FOX_HEADS, VT_ROWS, s))

    h = _mix(x2, att_a.reshape(t, SWA_Q), att_b.reshape(t, FOX_W), gates, wa_b, wb_b, wo_b)
    out = _mlp_ple(h, p[0].reshape(t, PLE_DIM), g_mlp[0].reshape(1, d), w1_b, w2_b,
                   g_ple[0].reshape(1, d), wpg_b, w_ple_proj[0].astype(BF16), g_final.reshape(1, d))
    return out.reshape(b, s, d)
```

```python
import functools

import numpy as np
import jax
import jax.numpy as jnp
from jax import lax
from jax.experimental import pallas as pl
from jax.experimental.pallas import tpu as pltpu

D_MODEL = 1024
CHUNK = 64
PLE_DIM = 256
HEAD_DIM = 64
SWA_HEADS = 8
SWA_KV_HEADS = 2
SWA_GROUP = SWA_HEADS // SWA_KV_HEADS
WINDOW = 128
SWA_BLOCK = WINDOW
FOX_HEADS = 8
D_FF = 4 * D_MODEL
RMS_EPS = 1e-6
SWA_Q = SWA_HEADS * HEAD_DIM
SWA_KV = SWA_KV_HEADS * HEAD_DIM
FOX_W = FOX_HEADS * HEAD_DIM
SCALE = HEAD_DIM ** -0.5
LOG2E = float(np.log2(np.e))

LANES = 128
BF16_SUBLANES = 16
MXU_COLS = 256
NEG_BIG = -1e30
VMEM_LIMIT = 52 * 1024 * 1024

TM = 512
MIX_TM = 1024
SWA_TILE = 512
SWA_SLOTS = 6
FOX_TQ = 512
FF_CHUNK = 1024

FOX_PAIRS = FOX_HEADS // 2
FOX_DEPTH = LANES
BIAS_SLOT = 8
C_PARTS = 3
VT_ROWS = 80

F32 = jnp.float32
BF16 = jnp.bfloat16
NT_DIMS = (((1,), (1,)), ((), ()))


def _rms(x, g):
    return x * lax.rsqrt(jnp.mean(x * x, axis=-1, keepdims=True) + RMS_EPS) * g


def _const_spec(shape):
    return pl.BlockSpec(shape, lambda *_: (0,) * len(shape), pipeline_mode=pl.Buffered(1))


def _split3(v):
    hi = v.astype(BF16)
    r1 = v - hi.astype(F32)
    mid = r1.astype(BF16)
    lo = (r1 - mid.astype(F32)).astype(BF16)
    return hi, mid, lo


def _bias_placement():
    place_qt = np.zeros((LANES, C_PARTS * LANES), np.float32)
    ones_qt = np.zeros((LANES, 1), np.float32)
    place_k = np.zeros((C_PARTS * LANES, LANES), np.float32)
    ones_k = np.zeros((1, LANES), np.float32)
    for h in range(FOX_HEADS):
        for part in range(C_PARTS):
            place_qt[h * BIAS_SLOT + part, part * LANES + h] = 1.0
            ones_qt[h * BIAS_SLOT + C_PARTS + part, 0] = 1.0
            ones_k[0, h * BIAS_SLOT + part] = 1.0
            place_k[part * LANES + h, h * BIAS_SLOT + C_PARTS + part] = -1.0
    return (place_qt, ones_qt), (place_k, ones_k)


def _in_proj_kernel(*refs, tiles_per_seq, n_cast):
    (x_ref, g_ref, wt_ref, wkf_ref, wk_ref, wg_ref, bf_ref, pqt_ref, oqt_ref, pk_ref, ok_ref, vone_ref) = refs[:12]
    cast_in = refs[12:12 + n_cast]
    qat_ref, ka_ref, vta_ref, q3t_ref, k3_ref, vtb_ref, gates_ref = refs[12 + n_cast:19 + n_cast]
    cast_out = refs[19 + n_cast:19 + 2 * n_cast]
    carry_ref = refs[19 + 2 * n_cast]
    i = pl.program_id(0)
    for src, dst in zip(cast_in, cast_out):
        dst[...] = src[...].astype(BF16)

    @pl.when(i % tiles_per_seq == 0)
    def _():
        carry_ref[...] = jnp.zeros_like(carry_ref)

    u = _rms(x_ref[...], g_ref[...]).astype(BF16)
    tm = u.shape[0]
    kf = jnp.dot(u, wkf_ref[...], preferred_element_type=F32)
    ka_ref[...] = kf[:, :SWA_KV].astype(BF16)
    f = kf[:, SWA_KV:] + bf_ref[...]
    logf = jnp.minimum(f, 0.0) - jnp.log1p(jnp.exp(-jnp.abs(f)))

    gl = jnp.dot(u, wg_ref[...], preferred_element_type=F32)
    gates_ref[...] = jax.nn.sigmoid(gl).astype(BF16)

    row = lax.broadcasted_iota(jnp.int32, (tm, tm), 0)
    col = lax.broadcasted_iota(jnp.int32, (tm, tm), 1)
    tri = (col <= row).astype(BF16)
    cs = jnp.dot(tri, jnp.concatenate(_split3(logf), axis=1), preferred_element_type=F32)
    c = cs[:, :LANES] + cs[:, LANES:2 * LANES] + cs[:, 2 * LANES:] + carry_ref[...]
    carry_ref[...] = c[tm - 1:tm, :]

    tt = lax.dot_general(wt_ref[...], u, NT_DIMS, preferred_element_type=F32)
    qat_ref[...] = (tt[:SWA_Q] * (SCALE * LOG2E)).astype(BF16)
    qt = (tt[SWA_Q:SWA_Q + FOX_W] * (SCALE * LOG2E)).astype(BF16)
    vt = (tt[SWA_Q + FOX_W:] + vone_ref[...]).astype(BF16)
    vta_ref[...] = vt[:SWA_KV_HEADS * VT_ROWS]
    vtb_ref[...] = vt[SWA_KV_HEADS * VT_ROWS:]

    parts = jnp.concatenate(_split3(c * LOG2E), axis=1)
    kbias = jnp.dot(parts, pk_ref[...], preferred_element_type=F32) + ok_ref[...]
    qbias_t = lax.dot_general(pqt_ref[...], parts, NT_DIMS, preferred_element_type=F32) + oqt_ref[...]
    zero_rows = jnp.zeros((HEAD_DIM - BIAS_SLOT, tm), F32)
    for h in range(FOX_HEADS):
        q3t_ref[h, :HEAD_DIM, :] = qt[h * HEAD_DIM:(h + 1) * HEAD_DIM]
        own = jnp.concatenate([qbias_t[h * BIAS_SLOT:(h + 1) * BIAS_SLOT], zero_rows], axis=0)
        q3t_ref[h, HEAD_DIM:, :] = own.astype(BF16)

    k = jnp.dot(u, wk_ref[...], preferred_element_type=F32)
    lane = lax.broadcasted_iota(jnp.int32, (1, LANES), 1)
    for h in range(FOX_HEADS):
        pair = k[:, (h // 2) * LANES:(h // 2 + 1) * LANES]
        dims = pair if h % 2 == 0 else pltpu.roll(pair, HEAD_DIM, 1)
        bias = pltpu.roll(kbias, HEAD_DIM - h * BIAS_SLOT, 1)
        k3_ref[:, h * FOX_DEPTH:(h + 1) * FOX_DEPTH] = jnp.where(lane < HEAD_DIM, dims, bias).astype(BF16)


def _in_proj(x2, g_mix, wt, wkf, wk, wg, bf_pad, batch, seq, later_weights):
    t = x2.shape[0]
    tps = seq // TM
    steps = t // TM
    (place_qt, ones_qt), (place_k, ones_k) = _bias_placement()
    n_vt = SWA_KV_HEADS + FOX_HEADS
    vone = np.zeros((n_vt * VT_ROWS, 1), np.float32)
    vone[HEAD_DIM::VT_ROWS] = 1.0
    consts = (jnp.asarray(place_qt, BF16), jnp.asarray(ones_qt), jnp.asarray(place_k, BF16), jnp.asarray(ones_k),
              jnp.asarray(vone))
    kern = functools.partial(_in_proj_kernel, tiles_per_seq=tps, n_cast=len(later_weights))
    row_spec = lambda w: pl.BlockSpec((TM, w), lambda i: (i, 0))
    col_spec = lambda *lead: pl.BlockSpec((None,) + lead + (TM,), lambda i: (i // tps,) + (0,) * len(lead) + (i % tps,))
    for w in later_weights:
        assert w.ndim == 2 and w.shape[0] % (steps * BF16_SUBLANES) == 0, w.shape
    slab_specs = [pl.BlockSpec((w.shape[0] // steps, w.shape[1]), lambda i: (i, 0)) for w in later_weights]
    operands = (x2, g_mix, wt, wkf, wk, wg, bf_pad) + consts
    ka, k3, gates = (jax.ShapeDtypeStruct((t, w), BF16) for w in (SWA_KV, FOX_HEADS * FOX_DEPTH, wg.shape[1]))
    qat = jax.ShapeDtypeStruct((batch, SWA_Q, seq), BF16)
    q3t = jax.ShapeDtypeStruct((batch, FOX_HEADS, FOX_DEPTH, seq), BF16)
    vta, vtb = (jax.ShapeDtypeStruct((batch, heads * VT_ROWS, seq), BF16) for heads in (SWA_KV_HEADS, FOX_HEADS))
    return pl.pallas_call(
        kern,
        grid=(steps,),
        in_specs=[row_spec(D_MODEL)] + [_const_spec(a.shape) for a in operands[1:]] + slab_specs,
        out_specs=[col_spec(SWA_Q), row_spec(SWA_KV), col_spec(SWA_KV_HEADS * VT_ROWS),
                   col_spec(FOX_HEADS, FOX_DEPTH), row_spec(FOX_HEADS * FOX_DEPTH), col_spec(FOX_HEADS * VT_ROWS),
                   row_spec(wg.shape[1])] + slab_specs,
        out_shape=[qat, ka, vta, q3t, k3, vtb, gates] + [jax.ShapeDtypeStruct(w.shape, BF16) for w in later_weights],
        scratch_shapes=[pltpu.VMEM((1, LANES), F32)],
        compiler_params=pltpu.CompilerParams(dimension_semantics=("arbitrary",), vmem_limit_bytes=VMEM_LIMIT),
        name="in_proj",
    )(*operands, *later_weights)


def _swa_bias_table():
    sb = SWA_BLOCK
    qi = np.arange(sb)[None, :] + sb
    si = np.arange(2 * sb)[:, None]
    chunk_diff = qi // CHUNK - si // CHUNK
    band_ok = (chunk_diff >= 0) & (chunk_diff <= WINDOW // CHUNK)
    slopes = np.array([2.0 ** (-8.0 * (h + 1) / SWA_HEADS) for h in range(SWA_HEADS)], dtype=np.float32)
    alibi = -slopes[:, None, None] * np.abs(qi - si).astype(np.float32)[None] * np.float32(LOG2E)
    first = band_ok & (si >= sb)
    table = np.stack([np.where(first[None], alibi, NEG_BIG), np.where(band_ok[None], alibi, NEG_BIG)])
    table = table.reshape(2, SWA_KV_HEADS, SWA_GROUP, 2 * sb, sb).transpose(0, 1, 3, 2, 4)
    return np.ascontiguousarray(table.reshape(2, SWA_KV_HEADS, 2 * sb, SWA_GROUP * sb)).astype(np.float32)


def _swa_kernel(sink_ref, qt_ref, kp_ref, kc_ref, vtp_ref, vtc_ref, bias_first_ref, bias_rest_ref, o_ref,
                st_sc, mb_sc, ot_sc):
    sb = SWA_BLOCK
    per_unit = MXU_COLS // sb
    half = jnp.zeros((HEAD_DIM, MXU_COLS), BF16)
    units = [(j, kh, c) for j in range(SWA_TILE // sb) for kh in range(SWA_KV_HEADS)
             for c in range(SWA_GROUP // per_unit)]

    def band(j):
        if j == 0:
            kb = jnp.concatenate([kp_ref[...], kc_ref[0:sb, :]], axis=0)
            vtb = jnp.concatenate([vtp_ref[...], vtc_ref[:, 0:sb]], axis=1)
            return kb, vtb, bias_first_ref
        return kc_ref[(j - 1) * sb:(j + 1) * sb, :], vtc_ref[:, (j - 1) * sb:(j + 1) * sb], bias_rest_ref

    def unit_heads(kh, c):
        return [kh * SWA_GROUP + c * per_unit + g for g in range(per_unit)]

    def scores(i, slot):
        j, kh, c = units[i]
        kb, _, bias_ref = band(j)
        q2 = jnp.concatenate([qt_ref[h * HEAD_DIM:(h + 1) * HEAD_DIM, j * sb:(j + 1) * sb] for h in unit_heads(kh, c)],
                             axis=1)
        q2t = jnp.concatenate([q2, half] if kh == 0 else [half, q2], axis=0)
        st = jnp.dot(kb, q2t, preferred_element_type=F32) + bias_ref[kh, :, c * MXU_COLS:(c + 1) * MXU_COLS]
        st_sc[slot] = st
        mb_sc[slot] = jnp.max(st, axis=0, keepdims=True)

    def consume(unit, slot):
        j, kh, c = unit
        _, vtb, _ = band(j)
        heads = unit_heads(kh, c)
        sink = jnp.concatenate([jnp.full((1, sb), sink_ref[0, h] * LOG2E, F32) for h in heads], axis=1)
        m = jnp.maximum(mb_sc[slot], sink)
        p = jnp.exp2(st_sc[slot] - m).astype(BF16)
        pv = jnp.dot(vtb[kh * VT_ROWS:(kh + 1) * VT_ROWS, :], p, preferred_element_type=F32)
        denom = pv[HEAD_DIM:HEAD_DIM + 1, :] + jnp.exp2(sink - m)
        o = pv[:HEAD_DIM, :] / denom
        for g, h in enumerate(heads):
            ot_sc[h * HEAD_DIM:(h + 1) * HEAD_DIM, j * sb:(j + 1) * sb] = o[:, g * sb:(g + 1) * sb]

    slots = st_sc.shape[0]
    for i in range(slots - 1):
        scores(i, i)
    for i, unit in enumerate(units):
        ahead = i + slots - 1
        if ahead < len(units):
            scores(ahead, ahead % slots)
        consume(unit, i % slots)
    o_ref[...] = ot_sc[...].T.astype(BF16)


def _swa(qt3, k3, vt3, sinks, bias):
    b, s, _ = k3.shape
    sb, ts = SWA_BLOCK, SWA_TILE
    per = ts // sb
    prev = lambda n: jnp.maximum(n * per - 1, 0)
    bias_block = (None,) + bias.shape[1:]
    return pl.pallas_call(
        _swa_kernel,
        grid=(b, s // ts),
        in_specs=[pl.BlockSpec(memory_space=pltpu.SMEM),
                  pl.BlockSpec((None, SWA_Q, ts), lambda bi, n: (bi, 0, n)),
                  pl.BlockSpec((None, sb, SWA_KV), lambda bi, n: (bi, prev(n), 0)),
                  pl.BlockSpec((None, ts, SWA_KV), lambda bi, n: (bi, n, 0)),
                  pl.BlockSpec((None, SWA_KV_HEADS * VT_ROWS, sb), lambda bi, n: (bi, 0, prev(n))),
                  pl.BlockSpec((None, SWA_KV_HEADS * VT_ROWS, ts), lambda bi, n: (bi, 0, n)),
                  pl.BlockSpec(bias_block, lambda bi, n: (jnp.minimum(n, 1), 0, 0, 0)),
                  pl.BlockSpec(bias_block, lambda bi, n: (1, 0, 0, 0))],
        out_specs=pl.BlockSpec((None, ts, SWA_Q), lambda bi, n: (bi, n, 0)),
        out_shape=jax.ShapeDtypeStruct((b, s, SWA_Q), BF16),
        scratch_shapes=[pltpu.VMEM((SWA_SLOTS, 2 * sb, MXU_COLS), F32),
                        pltpu.VMEM((SWA_SLOTS, 1, MXU_COLS), F32),
                        pltpu.VMEM((SWA_Q, ts), F32)],
        compiler_params=pltpu.CompilerParams(dimension_semantics=("parallel", "parallel"),
                                             vmem_limit_bytes=VMEM_LIMIT),
        name="swa",
    )(sinks, qt3, k3, k3, vt3, vt3, bias, bias)


def _fox_kernel(qt_ref, k_ref, vt_ref, o_ref, st_sc, mb_sc, m_sc, acc_sc):
    tq = qt_ref.shape[2]
    qi = pl.program_id(2)
    m_sc[...] = jnp.full_like(m_sc, NEG_BIG)
    acc_sc[...] = jnp.zeros_like(acc_sc)

    units = [(hh, slice(c * MXU_COLS, (c + 1) * MXU_COLS)) for hh in range(2) for c in range(tq // MXU_COLS)]

    def scores(block, slot, unit, diagonal):
        hh, cols = unit
        keys = cols.stop if diagonal else tq
        start = pl.multiple_of(block * tq, tq)
        k3 = k_ref[pl.ds(start, keys), hh * FOX_DEPTH:(hh + 1) * FOX_DEPTH]
        st = jnp.dot(k3, qt_ref[hh, :, cols], preferred_element_type=F32)
        if diagonal:
            key = lax.broadcasted_iota(jnp.int32, st.shape, 0)
            qry = lax.broadcasted_iota(jnp.int32, st.shape, 1) + cols.start
            st = jnp.where(key <= qry, st, NEG_BIG)
        st_sc[slot, hh, :keys, cols] = st
        mb_sc[slot, hh, :, cols] = jnp.max(st, axis=0, keepdims=True)

    def consume(block, slot, unit, diagonal=False):
        hh, cols = unit
        keys = cols.stop if diagonal else tq
        start = pl.multiple_of(block * tq, tq)
        m_prev = m_sc[hh, :, cols]
        m_new = jnp.maximum(m_prev, mb_sc[slot, hh, :, cols])
        p = jnp.exp2(st_sc[slot, hh, :keys, cols] - m_new).astype(BF16)
        alpha = jnp.exp2(m_prev - m_new)
        pv = jnp.dot(vt_ref[hh, :, pl.ds(start, keys)], p, preferred_element_type=F32)
        acc_sc[hh, :, cols] = alpha * acc_sc[hh, :, cols] + pv
        m_sc[hh, :, cols] = m_new

    def stage(next_block, next_slot, block, slot, diagonal=False):
        for unit in units:
            scores(next_block, next_slot, unit, False)
            consume(block, slot, unit, diagonal)

    for unit in units:
        scores(qi, 0, unit, True)
    stage(0, 1, qi, 0, diagonal=True)

    def block_pair(b0):
        stage(b0 + 1, 0, b0, 1)
        stage(jnp.minimum(b0 + 2, qi), 1, b0 + 1, 0)

    def block_oct(t, carry):
        for pair in range(4):
            block_pair(8 * t + 2 * pair)
        return carry

    lax.fori_loop(0, qi // 8, block_oct, 0)

    @pl.when(qi % 8 >= 4)
    def _():
        block_pair(qi // 8 * 8)
        block_pair(qi // 8 * 8 + 2)

    @pl.when(qi % 4 >= 2)
    def _():
        block_pair(qi // 4 * 4)

    @pl.when(qi % 2 == 1)
    def _():
        for unit in units:
            consume(qi - 1, 1, unit)

    outs = [acc_sc[hh, :HEAD_DIM, :] / acc_sc[hh, HEAD_DIM:HEAD_DIM + 1, :] for hh in range(2)]
    o_ref[...] = jnp.concatenate(outs, axis=0).T.astype(BF16)


def _fox(q3t, k3, vt4):
    b, s, _ = k3.shape
    tq = FOX_TQ
    return pl.pallas_call(
        _fox_kernel,
        grid=(b, FOX_PAIRS, s // tq),
        in_specs=[pl.BlockSpec((None, 2, FOX_DEPTH, tq), lambda bi, hp, qi: (bi, hp, 0, qi)),
                  pl.BlockSpec((None, s, 2 * FOX_DEPTH), lambda bi, hp, qi: (bi, 0, hp)),
                  pl.BlockSpec((None, 2, VT_ROWS, s), lambda bi, hp, qi: (bi, hp, 0, 0))],
        out_specs=pl.BlockSpec((None, tq, LANES), lambda bi, hp, qi: (bi, qi, hp)),
        out_shape=jax.ShapeDtypeStruct((b, s, FOX_W), BF16),
        scratch_shapes=[pltpu.VMEM((2, 2, tq, tq), F32),
                        pltpu.VMEM((2, 2, 1, tq), F32),
                        pltpu.VMEM((2, 1, tq), F32),
                        pltpu.VMEM((2, VT_ROWS, tq), F32)],
        compiler_params=pltpu.CompilerParams(dimension_semantics=("parallel", "parallel", "parallel"),
                                             vmem_limit_bytes=VMEM_LIMIT),
        name="fox",
    )(q3t, k3, vt4)


def _mix_kernel(x_ref, ya_ref, yb_ref, gates_ref, wa_ref, wb_ref, wo_ref, h_ref):
    slabs = [slice(r, r + TM) for r in range(0, MIX_TM, TM)]

    def branches(rows):
        y_a = jnp.dot(ya_ref[rows, :], wa_ref[...], preferred_element_type=F32)
        y_b = jnp.dot(yb_ref[rows, :], wb_ref[...], preferred_element_type=F32)
        return y_a, y_b

    pending = branches(slabs[0])
    for n, rows in enumerate(slabs):
        y_a, y_b = pending
        if n + 1 < len(slabs):
            pending = branches(slabs[n + 1])
        gates = gates_ref[rows, :].astype(F32)
        mixed = gates[:, :D_MODEL] * y_a + gates[:, D_MODEL:] * y_b
        h_ref[rows, :] = x_ref[rows, :] + jnp.dot(mixed.astype(BF16), wo_ref[...], preferred_element_type=F32)


def _mix(x2, att_a, att_b, gates, w_a, w_b, w_o):
    t = x2.shape[0]
    assert t % MIX_TM == 0 and MIX_TM % TM == 0
    row_spec = lambda w: pl.BlockSpec((MIX_TM, w), lambda i: (i, 0))
    return pl.pallas_call(
        _mix_kernel,
        grid=(t // MIX_TM,),
        in_specs=[row_spec(D_MODEL), row_spec(SWA_Q), row_spec(FOX_W), row_spec(2 * D_MODEL),
                  _const_spec(w_a.shape), _const_spec(w_b.shape), _const_spec(w_o.shape)],
        out_specs=row_spec(D_MODEL),
        out_shape=jax.ShapeDtypeStruct((t, D_MODEL), F32),
        compiler_params=pltpu.CompilerParams(dimension_semantics=("parallel",), vmem_limit_bytes=VMEM_LIMIT),
        name="mix",
    )(x2, att_a, att_b, gates, w_a, w_b, w_o)


def _mlp_ple_kernel(h_ref, p_ref, gm_ref, w1_ref, w2_ref, gp_ref, wpg_ref, wpp_ref, gf_ref, o_ref):
    h = h_ref[...]
    u = _rms(h, gm_ref[...]).astype(BF16)
    acc = jnp.zeros_like(h)
    for c in range(D_FF // FF_CHUNK):
        cols = slice(c * FF_CHUNK, (c + 1) * FF_CHUNK)
        a = jnp.dot(u, w1_ref[:, cols], preferred_element_type=F32)
        a = jnp.square(jnp.maximum(a, 0.0)).astype(BF16)
        acc = acc + jnp.dot(a, w2_ref[cols, :], preferred_element_type=F32)
    h = h + acc
    gate = jax.nn.sigmoid(jnp.dot(_rms(h, gp_ref[...]).astype(BF16), wpg_ref[...], preferred_element_type=F32))
    proj = jnp.dot(p_ref[...].astype(BF16), wpp_ref[...], preferred_element_type=F32)
    h = h + gate * proj
    o_ref[...] = _rms(h, gf_ref[...])


def _mlp_ple(h2, p2, g_mlp, w1, w2, g_ple, w_pg, w_pp, g_final):
    t = h2.shape[0]
    row_spec = lambda w: pl.BlockSpec((TM, w), lambda i: (i, 0))
    vec = _const_spec((1, D_MODEL))
    return pl.pallas_call(
        _mlp_ple_kernel,
        grid=(t // TM,),
        in_specs=[row_spec(D_MODEL), row_spec(PLE_DIM), vec, _const_spec(w1.shape), _const_spec(w2.shape),
                  vec, _const_spec(w_pg.shape), _const_spec(w_pp.shape), vec],
        out_specs=row_spec(D_MODEL),
        out_shape=jax.ShapeDtypeStruct((t, D_MODEL), F32),
        compiler_params=pltpu.CompilerParams(dimension_semantics=("parallel",), vmem_limit_bytes=VMEM_LIMIT),
        name="mlp_ple",
    )(h2, p2, g_mlp, w1, w2, g_ple, w_pg, w_pp, g_final)


def kernel(x, p, g_mix, w_in, b_forget, swa_sinks, w_br_swa, w_br_fox, w_mix_out,
           g_mlp, w_ff1, w_ff2, g_ple, w_ple_gate, w_ple_proj, g_final):
    b, s, d = x.shape
    assert d == D_MODEL and w_in.shape[0] == 1, "single-layer trunk with D_MODEL channels only"
    assert s % TM == 0 and s % FOX_TQ == 0 and s % SWA_TILE == 0 and SWA_TILE % SWA_BLOCK == 0
    t = b * s
    x2 = x.reshape(t, d)

    ka_end = SWA_Q + SWA_KV
    a_end = ka_end + SWA_KV
    q_end = a_end + FOX_W
    k_end = q_end + FOX_W
    v_end = k_end + FOX_W
    f_end = v_end + FOX_HEADS
    w = w_in[0]
    wv = jnp.concatenate([w[:, ka_end:a_end], w[:, k_end:v_end]], axis=1).T.reshape(-1, HEAD_DIM, d)
    wvt = jnp.pad(wv, ((0, 0), (0, VT_ROWS - HEAD_DIM), (0, 0))).reshape(-1, d)
    wt = jnp.concatenate([w[:, :SWA_Q].T, w[:, a_end:q_end].T, wvt], axis=0).astype(BF16)
    wf = jnp.pad(w[:, v_end:f_end], ((0, 0), (0, LANES - FOX_HEADS)))
    wkf = jnp.concatenate([w[:, SWA_Q:ka_end], wf], axis=1).astype(BF16)
    wk = w[:, q_end:k_end].astype(BF16)
    wg = w[:, f_end:].astype(BF16)
    bf_pad = jnp.pad(b_forget[0], (0, LANES - FOX_HEADS)).reshape(1, LANES)

    later = (w_br_swa[0], w_br_fox[0], w_mix_out[0], w_ff1[0], w_ff2[0], w_ple_gate[0])
    qat, ka, vta, q3t, k3, vtb, gates, *later_bf16 = _in_proj(x2, g_mix[0].reshape(1, d), wt, wkf, wk, wg, bf_pad, b, s, later)
    wa_b, wb_b, wo_b, w1_b, w2_b, wpg_b = later_bf16

    att_a = _swa(qat, ka.reshape(b, s, SWA_KV), vta, swa_sinks[0].reshape(1, SWA_HEADS),
                 jnp.asarray(_swa_bias_table()))
    att_b = _fox(q3t, k3.reshape(b, s, FOX_HEADS * FOX_DEPTH), vtb.reshape(b, FOX_HEADS, VT_ROWS, s))

    h = _mix(x2, att_a.reshape(t, SWA_Q), att_b.reshape(t, FOX_W), gates, wa_b, wb_b, wo_b)
    out = _mlp_ple(h, p[0].reshape(t, PLE_DIM), g_mlp[0].reshape(1, d), w1_b, w2_b,
                   g_ple[0].reshape(1, d), wpg_b, w_ple_proj[0].astype(BF16), g_final.reshape(1, d))
    return out.reshape(b, s, d)
```

```python
import functools

import numpy as np
import jax
import jax.numpy as jnp
from jax import lax
from jax.experimental import pallas as pl
from jax.experimental.pallas import tpu as pltpu

D_MODEL = 1024
CHUNK = 64
PLE_DIM = 256
HEAD_DIM = 64
SWA_HEADS = 8
SWA_KV_HEADS = 2
SWA_GROUP = SWA_HEADS // SWA_KV_HEADS
WINDOW = 128
SWA_BLOCK = WINDOW
FOX_HEADS = 8
D_FF = 4 * D_MODEL
RMS_EPS = 1e-6
SWA_Q = SWA_HEADS * HEAD_DIM
SWA_KV = SWA_KV_HEADS * HEAD_DIM
FOX_W = FOX_HEADS * HEAD_DIM
COL_KA = SWA_Q
COL_VA = COL_KA + SWA_KV
COL_QB = COL_VA + SWA_KV
COL_KB = COL_QB + FOX_W
COL_VB = COL_KB + FOX_W
COL_F = COL_VB + FOX_W
COL_G = COL_F + FOX_HEADS
SCALE = HEAD_DIM ** -0.5
LOG2E = float(np.log2(np.e))

LANES = 128
BF16_SUBLANES = 16
MXU_COLS = 256
NEG_BIG = -1e30
VMEM_LIMIT = 52 * 1024 * 1024

TM = 512
MIX_TM = 1024
SWA_TILE = 512
SWA_SLOTS = 6
FOX_TQ = 512
FF_CHUNK = 1024

FOX_PAIRS = FOX_HEADS // 2
FOX_DEPTH = LANES
BIAS_SLOT = 8
C_PARTS = 3
VT_ROWS = 80

F32 = jnp.float32
BF16 = jnp.bfloat16
NT_DIMS = (((1,), (1,)), ((), ()))


def _rms(x, g):
    return x * lax.rsqrt(jnp.mean(x * x, axis=-1, keepdims=True) + RMS_EPS) * g


def _const_spec(shape):
    return pl.BlockSpec(shape, lambda *_: (0,) * len(shape), pipeline_mode=pl.Buffered(1))


def _split3(v):
    hi = v.astype(BF16)
    r1 = v - hi.astype(F32)
    mid = r1.astype(BF16)
    lo = (r1 - mid.astype(F32)).astype(BF16)
    return hi, mid, lo


def _bias_placement():
    place_qt = np.zeros((LANES, C_PARTS * LANES), np.float32)
    ones_qt = np.zeros((LANES, 1), np.float32)
    place_k = np.zeros((C_PARTS * LANES, LANES), np.float32)
    ones_k = np.zeros((1, LANES), np.float32)
    for h in range(FOX_HEADS):
        for part in range(C_PARTS):
            place_qt[h * BIAS_SLOT + part, part * LANES + h] = 1.0
            ones_qt[h * BIAS_SLOT + C_PARTS + part, 0] = 1.0
            ones_k[0, h * BIAS_SLOT + part] = 1.0
            place_k[part * LANES + h, h * BIAS_SLOT + C_PARTS + part] = -1.0
    return (place_qt, ones_qt), (place_k, ones_k)


def _in_proj_kernel(*refs, tiles_per_seq, n_cast):
    (x_ref, g_ref, wmain_ref, wg_ref, bf_ref, pqt_ref, oqt_ref, pk_ref, ok_ref, vone_ref) = refs[:10]
    cast_in = refs[10:10 + n_cast]
    qat_ref, ka_ref, vta_ref, q3t_ref, k3_ref, vtb_ref, gates_ref = refs[10 + n_cast:17 + n_cast]
    cast_out = refs[17 + n_cast:17 + 2 * n_cast]
    carry_ref, wt_ref, wkf_ref, wk_ref = refs[17 + 2 * n_cast:]
    i = pl.program_id(0)
    for src, dst in zip(cast_in, cast_out):
        dst[...] = src[...].astype(BF16)

    @pl.when(i == 0)
    def _():
        wk_ref[...] = wmain_ref[:, COL_KB:COL_KB + FOX_W].astype(BF16)
        wkf_ref[:, :SWA_KV] = wmain_ref[:, COL_KA:COL_KA + SWA_KV].astype(BF16)
        lane = lax.broadcasted_iota(jnp.int32, (1, LANES), 1)
        wkf_ref[:, SWA_KV:] = jnp.where(lane < FOX_HEADS, wmain_ref[:, COL_F:COL_F + LANES], 0.0).astype(BF16)
        wt_ref[:SWA_Q] = wmain_ref[:, :SWA_Q].T.astype(BF16)
        wt_ref[SWA_Q:SWA_Q + FOX_W] = wmain_ref[:, COL_QB:COL_QB + FOX_W].T.astype(BF16)
        v_t = jnp.concatenate([wmain_ref[:, COL_VA:COL_VA + SWA_KV], wmain_ref[:, COL_VB:COL_VB + FOX_W]], axis=1).T
        pad_rows = jnp.zeros((VT_ROWS - HEAD_DIM, D_MODEL), BF16)
        for head in range(SWA_KV_HEADS + FOX_HEADS):
            base = SWA_Q + FOX_W + head * VT_ROWS
            wt_ref[base:base + HEAD_DIM] = v_t[head * HEAD_DIM:(head + 1) * HEAD_DIM].astype(BF16)
            wt_ref[base + HEAD_DIM:base + VT_ROWS] = pad_rows

    @pl.when(i % tiles_per_seq == 0)
    def _():
        carry_ref[...] = jnp.zeros_like(carry_ref)

    u = _rms(x_ref[...], g_ref[...]).astype(BF16)
    tm = u.shape[0]
    kf = jnp.dot(u, wkf_ref[...], preferred_element_type=F32)
    ka_ref[...] = kf[:, :SWA_KV].astype(BF16)
    f = kf[:, SWA_KV:] + bf_ref[...]
    logf = jnp.minimum(f, 0.0) - jnp.log1p(jnp.exp(-jnp.abs(f)))

    gl = jnp.dot(u, wg_ref[...], preferred_element_type=F32)
    gates_ref[...] = jax.nn.sigmoid(gl).astype(BF16)

    row = lax.broadcasted_iota(jnp.int32, (tm, tm), 0)
    col = lax.broadcasted_iota(jnp.int32, (tm, tm), 1)
    tri = (col <= row).astype(BF16)
    cs = jnp.dot(tri, jnp.concatenate(_split3(logf), axis=1), preferred_element_type=F32)
    c = cs[:, :LANES] + cs[:, LANES:2 * LANES] + cs[:, 2 * LANES:] + carry_ref[...]
    carry_ref[...] = c[tm - 1:tm, :]

    tt = lax.dot_general(wt_ref[...], u, NT_DIMS, preferred_element_type=F32)
    qat_ref[...] = (tt[:SWA_Q] * (SCALE * LOG2E)).astype(BF16)
    qt = (tt[SWA_Q:SWA_Q + FOX_W] * (SCALE * LOG2E)).astype(BF16)
    vt = (tt[SWA_Q + FOX_W:] + vone_ref[...]).astype(BF16)
    vta_ref[...] = vt[:SWA_KV_HEADS * VT_ROWS]
    vtb_ref[...] = vt[SWA_KV_HEADS * VT_ROWS:]

    parts = jnp.concatenate(_split3(c * LOG2E), axis=1)
    kbias = jnp.dot(parts, pk_ref[...], preferred_element_type=F32) + ok_ref[...]
    qbias_t = lax.dot_general(pqt_ref[...], parts, NT_DIMS, preferred_element_type=F32) + oqt_ref[...]
    zero_rows = jnp.zeros((HEAD_DIM - BIAS_SLOT, tm), F32)
    for h in range(FOX_HEADS):
        q3t_ref[h, :HEAD_DIM, :] = qt[h * HEAD_DIM:(h + 1) * HEAD_DIM]
        own = jnp.concatenate([qbias_t[h * BIAS_SLOT:(h + 1) * BIAS_SLOT], zero_rows], axis=0)
        q3t_ref[h, HEAD_DIM:, :] = own.astype(BF16)

    k = jnp.dot(u, wk_ref[...], preferred_element_type=F32)
    lane = lax.broadcasted_iota(jnp.int32, (1, LANES), 1)
    for h in range(FOX_HEADS):
        pair = k[:, (h // 2) * LANES:(h // 2 + 1) * LANES]
        dims = pair if h % 2 == 0 else pltpu.roll(pair, HEAD_DIM, 1)
        bias = pltpu.roll(kbias, HEAD_DIM - h * BIAS_SLOT, 1)
        k3_ref[:, h * FOX_DEPTH:(h + 1) * FOX_DEPTH] = jnp.where(lane < HEAD_DIM, dims, bias).astype(BF16)


def _in_proj(x2, g_mix, w_all, wg, bf_pad, batch, seq, later_weights):
    t = x2.shape[0]
    tps = seq // TM
    steps = t // TM
    (place_qt, ones_qt), (place_k, ones_k) = _bias_placement()
    n_vt = SWA_KV_HEADS + FOX_HEADS
    vone = np.zeros((n_vt * VT_ROWS, 1), np.float32)
    vone[HEAD_DIM::VT_ROWS] = 1.0
    consts = (jnp.asarray(place_qt, BF16), jnp.asarray(ones_qt), jnp.asarray(place_k, BF16), jnp.asarray(ones_k),
              jnp.asarray(vone))
    kern = functools.partial(_in_proj_kernel, tiles_per_seq=tps, n_cast=len(later_weights))
    row_spec = lambda w: pl.BlockSpec((TM, w), lambda i: (i, 0))
    col_spec = lambda *lead: pl.BlockSpec((None,) + lead + (TM,), lambda i: (i // tps,) + (0,) * len(lead) + (i % tps,))
    for w in later_weights:
        assert w.ndim == 2 and w.shape[0] % (steps * BF16_SUBLANES) == 0, w.shape
    slab_specs = [pl.BlockSpec((w.shape[0] // steps, w.shape[1]), lambda i: (i, 0)) for w in later_weights]
    main_cols = -(-COL_G // LANES) * LANES
    main_spec = pl.BlockSpec((D_MODEL, main_cols), lambda i: (0, 0), pipeline_mode=pl.Buffered(1))
    operands = (x2, g_mix, w_all, wg, bf_pad) + consts
    ka, k3, gates = (jax.ShapeDtypeStruct((t, w), BF16) for w in (SWA_KV, FOX_HEADS * FOX_DEPTH, wg.shape[1]))
    qat = jax.ShapeDtypeStruct((batch, SWA_Q, seq), BF16)
    q3t = jax.ShapeDtypeStruct((batch, FOX_HEADS, FOX_DEPTH, seq), BF16)
    vta, vtb = (jax.ShapeDtypeStruct((batch, heads * VT_ROWS, seq), BF16) for heads in (SWA_KV_HEADS, FOX_HEADS))
    return pl.pallas_call(
        kern,
        grid=(steps,),
        in_specs=([row_spec(D_MODEL), _const_spec(g_mix.shape), main_spec]
                  + [_const_spec(a.shape) for a in operands[3:]] + slab_specs),
        out_specs=[col_spec(SWA_Q), row_spec(SWA_KV), col_spec(SWA_KV_HEADS * VT_ROWS),
                   col_spec(FOX_HEADS, FOX_DEPTH), row_spec(FOX_HEADS * FOX_DEPTH), col_spec(FOX_HEADS * VT_ROWS),
                   row_spec(wg.shape[1])] + slab_specs,
        out_shape=[qat, ka, vta, q3t, k3, vtb, gates] + [jax.ShapeDtypeStruct(w.shape, BF16) for w in later_weights],
        scratch_shapes=[pltpu.VMEM((1, LANES), F32),
                        pltpu.VMEM((SWA_Q + FOX_W + n_vt * VT_ROWS, D_MODEL), BF16),
                        pltpu.VMEM((D_MODEL, SWA_KV + LANES), BF16),
                        pltpu.VMEM((D_MODEL, FOX_W), BF16)],
        compiler_params=pltpu.CompilerParams(dimension_semantics=("arbitrary",), vmem_limit_bytes=VMEM_LIMIT),
        name="in_proj",
    )(*operands, *later_weights)


def _swa_bias_table():
    sb = SWA_BLOCK
    qi = np.arange(sb)[None, :] + sb
    si = np.arange(2 * sb)[:, None]
    chunk_diff = qi // CHUNK - si // CHUNK
    band_ok = (chunk_diff >= 0) & (chunk_diff <= WINDOW // CHUNK)
    slopes = np.array([2.0 ** (-8.0 * (h + 1) / SWA_HEADS) for h in range(SWA_HEADS)], dtype=np.float32)
    alibi = -slopes[:, None, None] * np.abs(qi - si).astype(np.float32)[None] * np.float32(LOG2E)
    first = band_ok & (si >= sb)
    table = np.stack([np.where(first[None], alibi, NEG_BIG), np.where(band_ok[None], alibi, NEG_BIG)])
    table = table.reshape(2, SWA_KV_HEADS, SWA_GROUP, 2 * sb, sb).transpose(0, 1, 3, 2, 4)
    return np.ascontiguousarray(table.reshape(2, SWA_KV_HEADS, 2 * sb, SWA_GROUP * sb)).astype(np.float32)


def _swa_kernel(sink_ref, qt_ref, kp_ref, kc_ref, vtp_ref, vtc_ref, bias_first_ref, bias_rest_ref, o_ref,
                st_sc, mb_sc, ot_sc):
    sb = SWA_BLOCK
    per_unit = MXU_COLS // sb
    half = jnp.zeros((HEAD_DIM, MXU_COLS), BF16)
    units = [(j, kh, c) for j in range(SWA_TILE // sb) for kh in range(SWA_KV_HEADS)
             for c in range(SWA_GROUP // per_unit)]

    def band(j):
        if j == 0:
            kb = jnp.concatenate([kp_ref[...], kc_ref[0:sb, :]], axis=0)
            vtb = jnp.concatenate([vtp_ref[...], vtc_ref[:, 0:sb]], axis=1)
            return kb, vtb, bias_first_ref
        return kc_ref[(j - 1) * sb:(j + 1) * sb, :], vtc_ref[:, (j - 1) * sb:(j + 1) * sb], bias_rest_ref

    def unit_heads(kh, c):
        return [kh * SWA_GROUP + c * per_unit + g for g in range(per_unit)]

    def scores(i, slot):
        j, kh, c = units[i]
        kb, _, bias_ref = band(j)
        q2 = jnp.concatenate([qt_ref[h * HEAD_DIM:(h + 1) * HEAD_DIM, j * sb:(j + 1) * sb] for h in unit_heads(kh, c)],
                             axis=1)
        q2t = jnp.concatenate([q2, half] if kh == 0 else [half, q2], axis=0)
        st = jnp.dot(kb, q2t, preferred_element_type=F32) + bias_ref[kh, :, c * MXU_COLS:(c + 1) * MXU_COLS]
        st_sc[slot] = st
        mb_sc[slot] = jnp.max(st, axis=0, keepdims=True)

    def consume(unit, slot):
        j, kh, c = unit
        _, vtb, _ = band(j)
        heads = unit_heads(kh, c)
        sink = jnp.concatenate([jnp.full((1, sb), sink_ref[0, h] * LOG2E, F32) for h in heads], axis=1)
        m = jnp.maximum(mb_sc[slot], sink)
        p = jnp.exp2(st_sc[slot] - m).astype(BF16)
        pv = jnp.dot(vtb[kh * VT_ROWS:(kh + 1) * VT_ROWS, :], p, preferred_element_type=F32)
        denom = pv[HEAD_DIM:HEAD_DIM + 1, :] + jnp.exp2(sink - m)
        o = pv[:HEAD_DIM, :] / denom
        for g, h in enumerate(heads):
            ot_sc[h * HEAD_DIM:(h + 1) * HEAD_DIM, j * sb:(j + 1) * sb] = o[:, g * sb:(g + 1) * sb]

    slots = st_sc.shape[0]
    for i in range(slots - 1):
        scores(i, i)
    for i, unit in enumerate(units):
        ahead = i + slots - 1
        if ahead < len(units):
            scores(ahead, ahead % slots)
        consume(unit, i % slots)
    o_ref[...] = ot_sc[...].T.astype(BF16)


def _swa(qt3, k3, vt3, sinks, bias):
    b, s, _ = k3.shape
    sb, ts = SWA_BLOCK, SWA_TILE
    per = ts // sb
    prev = lambda n: jnp.maximum(n * per - 1, 0)
    bias_block = (None,) + bias.shape[1:]
    return pl.pallas_call(
        _swa_kernel,
        grid=(b, s // ts),
        in_specs=[pl.BlockSpec(memory_space=pltpu.SMEM),
                  pl.BlockSpec((None, SWA_Q, ts), lambda bi, n: (bi, 0, n)),
                  pl.BlockSpec((None, sb, SWA_KV), lambda bi, n: (bi, prev(n), 0)),
                  pl.BlockSpec((None, ts, SWA_KV), lambda bi, n: (bi, n, 0)),
                  pl.BlockSpec((None, SWA_KV_HEADS * VT_ROWS, sb), lambda bi, n: (bi, 0, prev(n))),
                  pl.BlockSpec((None, SWA_KV_HEADS * VT_ROWS, ts), lambda bi, n: (bi, 0, n)),
                  pl.BlockSpec(bias_block, lambda bi, n: (jnp.minimum(n, 1), 0, 0, 0)),
                  pl.BlockSpec(bias_block, lambda bi, n: (1, 0, 0, 0))],
        out_specs=pl.BlockSpec((None, ts, SWA_Q), lambda bi, n: (bi, n, 0)),
        out_shape=jax.ShapeDtypeStruct((b, s, SWA_Q), BF16),
        scratch_shapes=[pltpu.VMEM((SWA_SLOTS, 2 * sb, MXU_COLS), F32),
                        pltpu.VMEM((SWA_SLOTS, 1, MXU_COLS), F32),
                        pltpu.VMEM((SWA_Q, ts), F32)],
        compiler_params=pltpu.CompilerParams(dimension_semantics=("parallel", "parallel"),
                                             vmem_limit_bytes=VMEM_LIMIT),
        name="swa",
    )(sinks, qt3, k3, k3, vt3, vt3, bias, bias)


def _fox_kernel(qt_ref, k_ref, vt_ref, o_ref, st_sc, mb_sc, m_sc, acc_sc):
    tq = qt_ref.shape[2]
    qi = pl.program_id(2)
    m_sc[...] = jnp.full_like(m_sc, NEG_BIG)
    acc_sc[...] = jnp.zeros_like(acc_sc)

    units = [(hh, slice(c * MXU_COLS, (c + 1) * MXU_COLS)) for hh in range(2) for c in range(tq // MXU_COLS)]

    def scores(block, slot, unit, diagonal):
        hh, cols = unit
        keys = cols.stop if diagonal else tq
        start = pl.multiple_of(block * tq, tq)
        k3 = k_ref[pl.ds(start, keys), hh * FOX_DEPTH:(hh + 1) * FOX_DEPTH]
        st = jnp.dot(k3, qt_ref[hh, :, cols], preferred_element_type=F32)
        if diagonal:
            key = lax.broadcasted_iota(jnp.int32, st.shape, 0)
            qry = lax.broadcasted_iota(jnp.int32, st.shape, 1) + cols.start
            st = jnp.where(key <= qry, st, NEG_BIG)
        st_sc[slot, hh, :keys, cols] = st
        mb_sc[slot, hh, :, cols] = jnp.max(st, axis=0, keepdims=True)

    def consume(block, slot, unit, diagonal=False):
        hh, cols = unit
        keys = cols.stop if diagonal else tq
        start = pl.multiple_of(block * tq, tq)
        m_prev = m_sc[hh, :, cols]
        m_new = jnp.maximum(m_prev, mb_sc[slot, hh, :, cols])
        p = jnp.exp2(st_sc[slot, hh, :keys, cols] - m_new).astype(BF16)
        alpha = jnp.exp2(m_prev - m_new)
        pv = jnp.dot(vt_ref[hh, :, pl.ds(start, keys)], p, preferred_element_type=F32)
        acc_sc[hh, :, cols] = alpha * acc_sc[hh, :, cols] + pv
        m_sc[hh, :, cols] = m_new

    def stage(next_block, next_slot, block, slot, diagonal=False):
        for unit in units:
            scores(next_block, next_slot, unit, False)
            consume(block, slot, unit, diagonal)

    for unit in units:
        scores(qi, 0, unit, True)
    stage(0, 1, qi, 0, diagonal=True)

    def block_pair(b0):
        stage(b0 + 1, 0, b0, 1)
        stage(jnp.minimum(b0 + 2, qi), 1, b0 + 1, 0)

    def block_oct(t, carry):
        for pair in range(4):
            block_pair(8 * t + 2 * pair)
        return carry

    lax.fori_loop(0, qi // 8, block_oct, 0)

    @pl.when(qi % 8 >= 4)
    def _():
        block_pair(qi // 8 * 8)
        block_pair(qi // 8 * 8 + 2)

    @pl.when(qi % 4 >= 2)
    def _():
        block_pair(qi // 4 * 4)

    @pl.when(qi % 2 == 1)
    def _():
        for unit in units:
            consume(qi - 1, 1, unit)

    outs = [acc_sc[hh, :HEAD_DIM, :] / acc_sc[hh, HEAD_DIM:HEAD_DIM + 1, :] for hh in range(2)]
    o_ref[...] = jnp.concatenate(outs, axis=0).T.astype(BF16)


def _fox(q3t, k3, vt4):
    b, s, _ = k3.shape
    tq = FOX_TQ
    return pl.pallas_call(
        _fox_kernel,
        grid=(b, FOX_PAIRS, s // tq),
        in_specs=[pl.BlockSpec((None, 2, FOX_DEPTH, tq), lambda bi, hp, qi: (bi, hp, 0, qi)),
                  pl.BlockSpec((None, s, 2 * FOX_DEPTH), lambda bi, hp, qi: (bi, 0, hp)),
                  pl.BlockSpec((None, 2, VT_ROWS, s), lambda bi, hp, qi: (bi, hp, 0, 0))],
        out_specs=pl.BlockSpec((None, tq, LANES), lambda bi, hp, qi: (bi, qi, hp)),
        out_shape=jax.ShapeDtypeStruct((b, s, FOX_W), BF16),
        scratch_shapes=[pltpu.VMEM((2, 2, tq, tq), F32),
                        pltpu.VMEM((2, 2, 1, tq), F32),
                        pltpu.VMEM((2, 1, tq), F32),
                        pltpu.VMEM((2, VT_ROWS, tq), F32)],
        compiler_params=pltpu.CompilerParams(dimension_semantics=("parallel", "parallel", "parallel"),
                                             vmem_limit_bytes=VMEM_LIMIT),
        name="fox",
    )(q3t, k3, vt4)


def _mix_kernel(x_ref, ya_ref, yb_ref, gates_ref, wa_ref, wb_ref, wo_ref, h_ref):
    slabs = [slice(r, r + TM) for r in range(0, MIX_TM, TM)]

    def branches(rows):
        y_a = jnp.dot(ya_ref[rows, :], wa_ref[...], preferred_element_type=F32)
        y_b = jnp.dot(yb_ref[rows, :], wb_ref[...], preferred_element_type=F32)
        return y_a, y_b

    pending = branches(slabs[0])
    for n, rows in enumerate(slabs):
        y_a, y_b = pending
        if n + 1 < len(slabs):
            pending = branches(slabs[n + 1])
        gates = gates_ref[rows, :].astype(F32)
        mixed = gates[:, :D_MODEL] * y_a + gates[:, D_MODEL:] * y_b
        h_ref[rows, :] = x_ref[rows, :] + jnp.dot(mixed.astype(BF16), wo_ref[...], preferred_element_type=F32)


def _mix(x2, att_a, att_b, gates, w_a, w_b, w_o):
    t = x2.shape[0]
    assert t % MIX_TM == 0 and MIX_TM % TM == 0
    row_spec = lambda w: pl.BlockSpec((MIX_TM, w), lambda i: (i, 0))
    return pl.pallas_call(
        _mix_kernel,
        grid=(t // MIX_TM,),
        in_specs=[row_spec(D_MODEL), row_spec(SWA_Q), row_spec(FOX_W), row_spec(2 * D_MODEL),
                  _const_spec(w_a.shape), _const_spec(w_b.shape), _const_spec(w_o.shape)],
        out_specs=row_spec(D_MODEL),
        out_shape=jax.ShapeDtypeStruct((t, D_MODEL), F32),
        compiler_params=pltpu.CompilerParams(dimension_semantics=("parallel",), vmem_limit_bytes=VMEM_LIMIT),
        name="mix",
    )(x2, att_a, att_b, gates, w_a, w_b, w_o)


def _mlp_ple_kernel(h_ref, p_ref, gm_ref, w1_ref, w2_ref, gp_ref, wpg_ref, wpp_ref, gf_ref, o_ref):
    h = h_ref[...]
    u = _rms(h, gm_ref[...]).astype(BF16)
    acc = jnp.zeros_like(h)
    for c in range(D_FF // FF_CHUNK):
        cols = slice(c * FF_CHUNK, (c + 1) * FF_CHUNK)
        a = jnp.dot(u, w1_ref[:, cols], preferred_element_type=F32)
        a = jnp.square(jnp.maximum(a, 0.0)).astype(BF16)
        acc = acc + jnp.dot(a, w2_ref[cols, :], preferred_element_type=F32)
    h = h + acc
    gate = jax.nn.sigmoid(jnp.dot(_rms(h, gp_ref[...]).astype(BF16), wpg_ref[...], preferred_element_type=F32))
    proj = jnp.dot(p_ref[...].astype(BF16), wpp_ref[...], preferred_element_type=F32)
    h = h + gate * proj
    o_ref[...] = _rms(h, gf_ref[...])


def _mlp_ple(h2, p2, g_mlp, w1, w2, g_ple, w_pg, w_pp, g_final):
    t = h2.shape[0]
    row_spec = lambda w: pl.BlockSpec((TM, w), lambda i: (i, 0))
    vec = _const_spec((1, D_MODEL))
    return pl.pallas_call(
        _mlp_ple_kernel,
        grid=(t // TM,),
        in_specs=[row_spec(D_MODEL), row_spec(PLE_DIM), vec, _const_spec(w1.shape), _const_spec(w2.shape),
                  vec, _const_spec(w_pg.shape), _const_spec(w_pp.shape), vec],
        out_specs=row_spec(D_MODEL),
        out_shape=jax.ShapeDtypeStruct((t, D_MODEL), F32),
        compiler_params=pltpu.CompilerParams(dimension_semantics=("parallel",), vmem_limit_bytes=VMEM_LIMIT),
        name="mlp_ple",
    )(h2, p2, g_mlp, w1, w2, g_ple, w_pg, w_pp, g_final)


def kernel(x, p, g_mix, w_in, b_forget, swa_sinks, w_br_swa, w_br_fox, w_mix_out,
           g_mlp, w_ff1, w_ff2, g_ple, w_ple_gate, w_ple_proj, g_final):
    b, s, d = x.shape
    assert d == D_MODEL and w_in.shape[0] == 1, "single-layer trunk with D_MODEL channels only"
    assert s % TM == 0 and s % FOX_TQ == 0 and s % SWA_TILE == 0 and SWA_TILE % SWA_BLOCK == 0
    t = b * s
    x2 = x.reshape(t, d)

    w = w_in[0]
    assert w.shape[1] == COL_G + 2 * D_MODEL
    wg = w[:, COL_G:].astype(BF16)
    bf_pad = jnp.pad(b_forget[0], (0, LANES - FOX_HEADS)).reshape(1, LANES)

    later = (w_br_swa[0], w_br_fox[0], w_mix_out[0], w_ff1[0], w_ff2[0], w_ple_gate[0])
    qat, ka, vta, q3t, k3, vtb, gates, *later_bf16 = _in_proj(x2, g_mix[0].reshape(1, d), w, wg, bf_pad, b, s, later)
    wa_b, wb_b, wo_b, w1_b, w2_b, wpg_b = later_bf16

    att_a = _swa(qat, ka.reshape(b, s, SWA_KV), vta, swa_sinks[0].reshape(1, SWA_HEADS),
                 jnp.asarray(_swa_bias_table()))
    att_b = _fox(q3t, k3.reshape(b, s, FOX_HEADS * FOX_DEPTH), vtb.reshape(b, FOX_HEADS, VT_ROWS, s))

    h = _mix(x2, att_a.reshape(t, SWA_Q), att_b.reshape(t, FOX_W), gates, wa_b, wb_b, wo_b)
    out = _mlp_ple(h, p[0].reshape(t, PLE_DIM), g_mlp[0].reshape(1, d), w1_b, w2_b,
                   g_ple[0].reshape(1, d), wpg_b, w_ple_proj[0].astype(BF16), g_final.reshape(1, d))
    return out.reshape(b, s, d)
```

```python
import functools

import numpy as np
import jax
import jax.numpy as jnp
from jax import lax
from jax.experimental import pallas as pl
from jax.experimental.pallas import tpu as pltpu

D_MODEL = 1024
CHUNK = 64
PLE_DIM = 256
HEAD_DIM = 64
SWA_HEADS = 8
SWA_KV_HEADS = 2
SWA_GROUP = SWA_HEADS // SWA_KV_HEADS
WINDOW = 128
SWA_BLOCK = WINDOW
FOX_HEADS = 8
D_FF = 4 * D_MODEL
RMS_EPS = 1e-6
SWA_Q = SWA_HEADS * HEAD_DIM
SWA_KV = SWA_KV_HEADS * HEAD_DIM
FOX_W = FOX_HEADS * HEAD_DIM
COL_KA = SWA_Q
COL_VA = COL_KA + SWA_KV
COL_QB = COL_VA + SWA_KV
COL_KB = COL_QB + FOX_W
COL_VB = COL_KB + FOX_W
COL_F = COL_VB + FOX_W
COL_G = COL_F + FOX_HEADS
SCALE = HEAD_DIM ** -0.5
LOG2E = float(np.log2(np.e))

LANES = 128
BF16_SUBLANES = 16
MXU_COLS = 256
NEG_BIG = -1e30
VMEM_LIMIT = 52 * 1024 * 1024

TM = 512
MIX_TM = 1024
SWA_TILE = 512
SWA_SLOTS = 6
FOX_TQ = 512
FF_CHUNK = 1024

FOX_PAIRS = FOX_HEADS // 2
FOX_DEPTH = LANES
BIAS_SLOT = 8
C_PARTS = 3
VT_ROWS = 80

F32 = jnp.float32
BF16 = jnp.bfloat16
NT_DIMS = (((1,), (1,)), ((), ()))


def _rms(x, g):
    return x * lax.rsqrt(jnp.mean(x * x, axis=-1, keepdims=True) + RMS_EPS) * g


def _const_spec(shape):
    return pl.BlockSpec(shape, lambda *_: (0,) * len(shape), pipeline_mode=pl.Buffered(1))


def _split3(v):
    hi = v.astype(BF16)
    r1 = v - hi.astype(F32)
    mid = r1.astype(BF16)
    lo = (r1 - mid.astype(F32)).astype(BF16)
    return hi, mid, lo


def _bias_placement():
    place_qt = np.zeros((LANES, C_PARTS * LANES), np.float32)
    ones_qt = np.zeros((LANES, 1), np.float32)
    place_k = np.zeros((C_PARTS * LANES, LANES), np.float32)
    ones_k = np.zeros((1, LANES), np.float32)
    for h in range(FOX_HEADS):
        for part in range(C_PARTS):
            place_qt[h * BIAS_SLOT + part, part * LANES + h] = 1.0
            ones_qt[h * BIAS_SLOT + C_PARTS + part, 0] = 1.0
            ones_k[0, h * BIAS_SLOT + part] = 1.0
            place_k[part * LANES + h, h * BIAS_SLOT + C_PARTS + part] = -1.0
    return (place_qt, ones_qt), (place_k, ones_k)


def _in_proj_kernel(*refs, tiles_per_seq, n_cast):
    (x_ref, g_ref, wmain_ref, wg_ref, bf_ref, pqt_ref, oqt_ref, pk_ref, ok_ref, vone_ref) = refs[:10]
    cast_in = refs[10:10 + n_cast]
    qat_ref, ka_ref, vta_ref, q3t_ref, k3_ref, vtb_ref, gates_ref = refs[10 + n_cast:17 + n_cast]
    cast_out = refs[17 + n_cast:17 + 2 * n_cast]
    carry_ref, wt_ref, wkf_ref, wk_ref = refs[17 + 2 * n_cast:]
    i = pl.program_id(0)
    for src, dst in zip(cast_in, cast_out):
        dst[...] = src[...].astype(BF16)

    @pl.when(i == 0)
    def _():
        wk_ref[...] = wmain_ref[:, COL_KB:COL_KB + FOX_W].astype(BF16)
        wkf_ref[:, :SWA_KV] = wmain_ref[:, COL_KA:COL_KA + SWA_KV].astype(BF16)
        lane = lax.broadcasted_iota(jnp.int32, (1, LANES), 1)
        wkf_ref[:, SWA_KV:] = jnp.where(lane < FOX_HEADS, wmain_ref[:, COL_F:COL_F + LANES], 0.0).astype(BF16)
        wt_ref[:SWA_Q] = wmain_ref[:, :SWA_Q].T.astype(BF16)
        wt_ref[SWA_Q:SWA_Q + FOX_W] = wmain_ref[:, COL_QB:COL_QB + FOX_W].T.astype(BF16)
        v_t = jnp.concatenate([wmain_ref[:, COL_VA:COL_VA + SWA_KV], wmain_ref[:, COL_VB:COL_VB + FOX_W]], axis=1).T
        pad_rows = jnp.zeros((VT_ROWS - HEAD_DIM, D_MODEL), BF16)
        for head in range(SWA_KV_HEADS + FOX_HEADS):
            base = SWA_Q + FOX_W + head * VT_ROWS
            wt_ref[base:base + HEAD_DIM] = v_t[head * HEAD_DIM:(head + 1) * HEAD_DIM].astype(BF16)
            wt_ref[base + HEAD_DIM:base + VT_ROWS] = pad_rows

    @pl.when(i % tiles_per_seq == 0)
    def _():
        carry_ref[...] = jnp.zeros_like(carry_ref)

    u = _rms(x_ref[...], g_ref[...]).astype(BF16)
    tm = u.shape[0]
    kf = jnp.dot(u, wkf_ref[...], preferred_element_type=F32)
    ka_ref[...] = kf[:, :SWA_KV].astype(BF16)
    f = kf[:, SWA_KV:] + bf_ref[...]
    logf = jnp.minimum(f, 0.0) - jnp.log1p(jnp.exp(-jnp.abs(f)))

    gl = jnp.dot(u, wg_ref[...], preferred_element_type=F32)
    gates_ref[...] = jax.nn.sigmoid(gl).astype(BF16)

    row = lax.broadcasted_iota(jnp.int32, (tm, tm), 0)
    col = lax.broadcasted_iota(jnp.int32, (tm, tm), 1)
    tri = (col <= row).astype(BF16)
    cs = jnp.dot(tri, jnp.concatenate(_split3(logf), axis=1), preferred_element_type=F32)
    c = cs[:, :LANES] + cs[:, LANES:2 * LANES] + cs[:, 2 * LANES:] + carry_ref[...]
    carry_ref[...] = c[tm - 1:tm, :]

    tt = lax.dot_general(wt_ref[...], u, NT_DIMS, preferred_element_type=F32)
    qat_ref[...] = (tt[:SWA_Q] * (SCALE * LOG2E)).astype(BF16)
    qt = (tt[SWA_Q:SWA_Q + FOX_W] * (SCALE * LOG2E)).astype(BF16)
    vt = (tt[SWA_Q + FOX_W:] + vone_ref[...]).astype(BF16)
    vta_ref[...] = vt[:SWA_KV_HEADS * VT_ROWS]
    vtb_ref[...] = vt[SWA_KV_HEADS * VT_ROWS:]

    parts = jnp.concatenate(_split3(c * LOG2E), axis=1)
    kbias = jnp.dot(parts, pk_ref[...], preferred_element_type=F32) + ok_ref[...]
    qbias_t = lax.dot_general(pqt_ref[...], parts, NT_DIMS, preferred_element_type=F32) + oqt_ref[...]
    zero_rows = jnp.zeros((HEAD_DIM - BIAS_SLOT, tm), F32)
    for h in range(FOX_HEADS):
        q3t_ref[h, :HEAD_DIM, :] = qt[h * HEAD_DIM:(h + 1) * HEAD_DIM]
        own = jnp.concatenate([qbias_t[h * BIAS_SLOT:(h + 1) * BIAS_SLOT], zero_rows], axis=0)
        q3t_ref[h, HEAD_DIM:, :] = own.astype(BF16)

    k = jnp.dot(u, wk_ref[...], preferred_element_type=F32)
    lane = lax.broadcasted_iota(jnp.int32, (1, LANES), 1)
    for h in range(FOX_HEADS):
        pair = k[:, (h // 2) * LANES:(h // 2 + 1) * LANES]
        dims = pair if h % 2 == 0 else pltpu.roll(pair, HEAD_DIM, 1)
        bias = pltpu.roll(kbias, HEAD_DIM - h * BIAS_SLOT, 1)
        k3_ref[:, h * FOX_DEPTH:(h + 1) * FOX_DEPTH] = jnp.where(lane < HEAD_DIM, dims, bias).astype(BF16)


def _in_proj(x2, g_mix, w_all, wg, bf_pad, batch, seq, later_weights):
    t = x2.shape[0]
    tps = seq // TM
    steps = t // TM
    (place_qt, ones_qt), (place_k, ones_k) = _bias_placement()
    n_vt = SWA_KV_HEADS + FOX_HEADS
    vone = np.zeros((n_vt * VT_ROWS, 1), np.float32)
    vone[HEAD_DIM::VT_ROWS] = 1.0
    consts = (jnp.asarray(place_qt, BF16), jnp.asarray(ones_qt), jnp.asarray(place_k, BF16), jnp.asarray(ones_k),
              jnp.asarray(vone))
    kern = functools.partial(_in_proj_kernel, tiles_per_seq=tps, n_cast=len(later_weights))
    row_spec = lambda w: pl.BlockSpec((TM, w), lambda i: (i, 0))
    col_spec = lambda *lead: pl.BlockSpec((None,) + lead + (TM,), lambda i: (i // tps,) + (0,) * len(lead) + (i % tps,))
    for w in later_weights:
        assert w.ndim == 2 and w.shape[0] % (steps * BF16_SUBLANES) == 0, w.shape
    slab_specs = [pl.BlockSpec((w.shape[0] // steps, w.shape[1]), lambda i: (i, 0)) for w in later_weights]
    main_cols = -(-COL_G // LANES) * LANES
    main_spec = pl.BlockSpec((D_MODEL, main_cols), lambda i: (0, 0), pipeline_mode=pl.Buffered(1))
    operands = (x2, g_mix, w_all, wg, bf_pad) + consts
    ka, k3, gates = (jax.ShapeDtypeStruct((t, w), BF16) for w in (SWA_KV, FOX_HEADS * FOX_DEPTH, wg.shape[1]))
    qat = jax.ShapeDtypeStruct((batch, SWA_Q, seq), BF16)
    q3t = jax.ShapeDtypeStruct((batch, FOX_HEADS, FOX_DEPTH, seq), BF16)
    vta, vtb = (jax.ShapeDtypeStruct((batch, heads * VT_ROWS, seq), BF16) for heads in (SWA_KV_HEADS, FOX_HEADS))
    return pl.pallas_call(
        kern,
        grid=(steps,),
        in_specs=([row_spec(D_MODEL), _const_spec(g_mix.shape), main_spec]
                  + [_const_spec(a.shape) for a in operands[3:]] + slab_specs),
        out_specs=[col_spec(SWA_Q), row_spec(SWA_KV), col_spec(SWA_KV_HEADS * VT_ROWS),
                   col_spec(FOX_HEADS, FOX_DEPTH), row_spec(FOX_HEADS * FOX_DEPTH), col_spec(FOX_HEADS * VT_ROWS),
                   row_spec(wg.shape[1])] + slab_specs,
        out_shape=[qat, ka, vta, q3t, k3, vtb, gates] + [jax.ShapeDtypeStruct(w.shape, BF16) for w in later_weights],
        scratch_shapes=[pltpu.VMEM((1, LANES), F32),
                        pltpu.VMEM((SWA_Q + FOX_W + n_vt * VT_ROWS, D_MODEL), BF16),
                        pltpu.VMEM((D_MODEL, SWA_KV + LANES), BF16),
                        pltpu.VMEM((D_MODEL, FOX_W), BF16)],
        compiler_params=pltpu.CompilerParams(dimension_semantics=("arbitrary",), vmem_limit_bytes=VMEM_LIMIT),
        name="in_proj",
    )(*operands, *later_weights)


def _swa_bias_table():
    sb = SWA_BLOCK
    qi = np.arange(sb)[None, :] + sb
    si = np.arange(2 * sb)[:, None]
    chunk_diff = qi // CHUNK - si // CHUNK
    band_ok = (chunk_diff >= 0) & (chunk_diff <= WINDOW // CHUNK)
    slopes = np.array([2.0 ** (-8.0 * (h + 1) / SWA_HEADS) for h in range(SWA_HEADS)], dtype=np.float32)
    alibi = -slopes[:, None, None] * np.abs(qi - si).astype(np.float32)[None] * np.float32(LOG2E)
    first = band_ok & (si >= sb)
    table = np.stack([np.where(first[None], alibi, NEG_BIG), np.where(band_ok[None], alibi, NEG_BIG)])
    table = table.reshape(2, SWA_KV_HEADS, SWA_GROUP, 2 * sb, sb).transpose(0, 1, 3, 2, 4)
    return np.ascontiguousarray(table.reshape(2, SWA_KV_HEADS, 2 * sb, SWA_GROUP * sb)).astype(np.float32)


def _swa_kernel(sink_ref, qt_ref, kp_ref, kc_ref, vtp_ref, vtc_ref, bias_first_ref, bias_rest_ref, o_ref,
                st_sc, mb_sc, ot_sc):
    sb = SWA_BLOCK
    per_unit = MXU_COLS // sb
    half = jnp.zeros((HEAD_DIM, MXU_COLS), BF16)
    units = [(j, kh, c) for j in range(SWA_TILE // sb) for kh in range(SWA_KV_HEADS)
             for c in range(SWA_GROUP // per_unit)]

    def band(j):
        if j == 0:
            kb = jnp.concatenate([kp_ref[...], kc_ref[0:sb, :]], axis=0)
            vtb = jnp.concatenate([vtp_ref[...], vtc_ref[:, 0:sb]], axis=1)
            return kb, vtb, bias_first_ref
        return kc_ref[(j - 1) * sb:(j + 1) * sb, :], vtc_ref[:, (j - 1) * sb:(j + 1) * sb], bias_rest_ref

    def unit_heads(kh, c):
        return [kh * SWA_GROUP + c * per_unit + g for g in range(per_unit)]

    def scores(i, slot):
        j, kh, c = units[i]
        kb, _, bias_ref = band(j)
        q2 = jnp.concatenate([qt_ref[h * HEAD_DIM:(h + 1) * HEAD_DIM, j * sb:(j + 1) * sb] for h in unit_heads(kh, c)],
                             axis=1)
        q2t = jnp.concatenate([q2, half] if kh == 0 else [half, q2], axis=0)
        st = jnp.dot(kb, q2t, preferred_element_type=F32) + bias_ref[kh, :, c * MXU_COLS:(c + 1) * MXU_COLS]
        st_sc[slot] = st
        mb_sc[slot] = jnp.max(st, axis=0, keepdims=True)

    def consume(unit, slot):
        j, kh, c = unit
        _, vtb, _ = band(j)
        heads = unit_heads(kh, c)
        sink = jnp.concatenate([jnp.full((1, sb), sink_ref[0, h] * LOG2E, F32) for h in heads], axis=1)
        m = jnp.maximum(mb_sc[slot], sink)
        p = jnp.exp2(st_sc[slot] - m).astype(BF16)
        pv = jnp.dot(vtb[kh * VT_ROWS:(kh + 1) * VT_ROWS, :], p, preferred_element_type=F32)
        denom = pv[HEAD_DIM:HEAD_DIM + 1, :] + jnp.exp2(sink - m)
        o = pv[:HEAD_DIM, :] / denom
        for g, h in enumerate(heads):
            ot_sc[h * HEAD_DIM:(h + 1) * HEAD_DIM, j * sb:(j + 1) * sb] = o[:, g * sb:(g + 1) * sb]

    slots = st_sc.shape[0]
    for i in range(slots - 1):
        scores(i, i)
    for i, unit in enumerate(units):
        ahead = i + slots - 1
        if ahead < len(units):
            scores(ahead, ahead % slots)
        consume(unit, i % slots)
    o_ref[...] = ot_sc[...].T.astype(BF16)


def _swa(qt3, k3, vt3, sinks, bias):
    b, s, _ = k3.shape
    sb, ts = SWA_BLOCK, SWA_TILE
    per = ts // sb
    prev = lambda n: jnp.maximum(n * per - 1, 0)
    bias_block = (None,) + bias.shape[1:]
    return pl.pallas_call(
        _swa_kernel,
        grid=(b, s // ts),
        in_specs=[pl.BlockSpec(memory_space=pltpu.SMEM),
                  pl.BlockSpec((None, SWA_Q, ts), lambda bi, n: (bi, 0, n)),
                  pl.BlockSpec((None, sb, SWA_KV), lambda bi, n: (bi, prev(n), 0)),
                  pl.BlockSpec((None, ts, SWA_KV), lambda bi, n: (bi, n, 0)),
                  pl.BlockSpec((None, SWA_KV_HEADS * VT_ROWS, sb), lambda bi, n: (bi, 0, prev(n))),
                  pl.BlockSpec((None, SWA_KV_HEADS * VT_ROWS, ts), lambda bi, n: (bi, 0, n)),
                  pl.BlockSpec(bias_block, lambda bi, n: (jnp.minimum(n, 1), 0, 0, 0)),
                  pl.BlockSpec(bias_block, lambda bi, n: (1, 0, 0, 0))],
        out_specs=pl.BlockSpec((None, ts, SWA_Q), lambda bi, n: (bi, n, 0)),
        out_shape=jax.ShapeDtypeStruct((b, s, SWA_Q), BF16),
        scratch_shapes=[pltpu.VMEM((SWA_SLOTS, 2 * sb, MXU_COLS), F32),
                        pltpu.VMEM((SWA_SLOTS, 1, MXU_COLS), F32),
                        pltpu.VMEM((SWA_Q, ts), F32)],
        compiler_params=pltpu.CompilerParams(dimension_semantics=("parallel", "parallel"),
                                             vmem_limit_bytes=VMEM_LIMIT),
        name="swa",
    )(sinks, qt3, k3, k3, vt3, vt3, bias, bias)


def _fox_kernel(qt_ref, k_ref, vt_ref, o_ref, st_sc, mb_sc, m_sc, acc_sc):
    tq = qt_ref.shape[2]
    qi = pl.program_id(2)
    m_sc[...] = jnp.full_like(m_sc, NEG_BIG)
    acc_sc[...] = jnp.zeros_like(acc_sc)

    units = [(hh, slice(c * MXU_COLS, (c + 1) * MXU_COLS)) for hh in range(2) for c in range(tq // MXU_COLS)]

    def scores(block, slot, unit, diagonal):
        hh, cols = unit
        keys = cols.stop if diagonal else tq
        start = pl.multiple_of(block * tq, tq)
        k3 = k_ref[pl.ds(start, keys), hh * FOX_DEPTH:(hh + 1) * FOX_DEPTH]
        st = jnp.dot(k3, qt_ref[hh, :, cols], preferred_element_type=F32)
        if diagonal:
            key = lax.broadcasted_iota(jnp.int32, st.shape, 0)
            qry = lax.broadcasted_iota(jnp.int32, st.shape, 1) + cols.start
            st = jnp.where(key <= qry, st, NEG_BIG)
        st_sc[slot, hh, :keys, cols] = st
        mb_sc[slot, hh, :, cols] = jnp.max(st, axis=0, keepdims=True)

    def consume(block, slot, unit, diagonal=False):
        hh, cols = unit
        keys = cols.stop if diagonal else tq
        start = pl.multiple_of(block * tq, tq)
        m_prev = m_sc[hh, :, cols]
        m_new = jnp.maximum(m_prev, mb_sc[slot, hh, :, cols])
        p = jnp.exp2(st_sc[slot, hh, :keys, cols] - m_new).astype(BF16)
        alpha = jnp.exp2(m_prev - m_new)
        pv = jnp.dot(vt_ref[hh, :, pl.ds(start, keys)], p, preferred_element_type=F32)
        acc_sc[hh, :, cols] = alpha * acc_sc[hh, :, cols] + pv
        m_sc[hh, :, cols] = m_new

    def stage(next_block, next_slot, block, slot, diagonal=False):
        for unit in units:
            scores(next_block, next_slot, unit, False)
            consume(block, slot, unit, diagonal)

    for unit in units:
        scores(qi, 0, unit, True)
    stage(0, 1, qi, 0, diagonal=True)

    def block_pair(b0):
        stage(b0 + 1, 0, b0, 1)
        stage(jnp.minimum(b0 + 2, qi), 1, b0 + 1, 0)

    def block_oct(t, carry):
        for pair in range(4):
            block_pair(8 * t + 2 * pair)
        return carry

    lax.fori_loop(0, qi // 8, block_oct, 0)

    @pl.when(qi % 8 >= 4)
    def _():
        block_pair(qi // 8 * 8)
        block_pair(qi // 8 * 8 + 2)

    @pl.when(qi % 4 >= 2)
    def _():
        block_pair(qi // 4 * 4)

    @pl.when(qi % 2 == 1)
    def _():
        for unit in units:
            consume(qi - 1, 1, unit)

    outs = [acc_sc[hh, :HEAD_DIM, :] / acc_sc[hh, HEAD_DIM:HEAD_DIM + 1, :] for hh in range(2)]
    o_ref[...] = jnp.concatenate(outs, axis=0).T.astype(BF16)


def _fox(q3t, k3, vt4):
    b, s, _ = k3.shape
    tq = FOX_TQ
    return pl.pallas_call(
        _fox_kernel,
        grid=(b, FOX_PAIRS, s // tq),
        in_specs=[pl.BlockSpec((None, 2, FOX_DEPTH, tq), lambda bi, hp, qi: (bi, hp, 0, qi)),
                  pl.BlockSpec((None, s, 2 * FOX_DEPTH), lambda bi, hp, qi: (bi, 0, hp)),
                  pl.BlockSpec((None, 2, VT_ROWS, s), lambda bi, hp, qi: (bi, hp, 0, 0))],
        out_specs=pl.BlockSpec((None, tq, LANES), lambda bi, hp, qi: (bi, qi, hp)),
        out_shape=jax.ShapeDtypeStruct((b, s, FOX_W), BF16),
        scratch_shapes=[pltpu.VMEM((2, 2, tq, tq), F32),
                        pltpu.VMEM((2, 2, 1, tq), F32),
                        pltpu.VMEM((2, 1, tq), F32),
                        pltpu.VMEM((2, VT_ROWS, tq), F32)],
        compiler_params=pltpu.CompilerParams(dimension_semantics=("parallel", "parallel", "parallel"),
                                             vmem_limit_bytes=VMEM_LIMIT),
        name="fox",
    )(q3t, k3, vt4)


def _mix_kernel(x_ref, ya_ref, yb_ref, gates_ref, wa_ref, wb_ref, wo_ref, h_ref):
    slabs = [slice(r, r + TM) for r in range(0, MIX_TM, TM)]

    def branches(rows):
        y_a = jnp.dot(ya_ref[rows, :], wa_ref[...], preferred_element_type=F32)
        y_b = jnp.dot(yb_ref[rows, :], wb_ref[...], preferred_element_type=F32)
        return y_a, y_b

    pending = branches(slabs[0])
    for n, rows in enumerate(slabs):
        y_a, y_b = pending
        if n + 1 < len(slabs):
            pending = branches(slabs[n + 1])
        gates = gates_ref[rows, :].astype(F32)
        mixed = gates[:, :D_MODEL] * y_a + gates[:, D_MODEL:] * y_b
        h_ref[rows, :] = x_ref[rows, :] + jnp.dot(mixed.astype(BF16), wo_ref[...], preferred_element_type=F32)


def _mix(x2, att_a, att_b, gates, w_a, w_b, w_o):
    t = x2.shape[0]
    assert t % MIX_TM == 0 and MIX_TM % TM == 0
    row_spec = lambda w: pl.BlockSpec((MIX_TM, w), lambda i: (i, 0))
    return pl.pallas_call(
        _mix_kernel,
        grid=(t // MIX_TM,),
        in_specs=[row_spec(D_MODEL), row_spec(SWA_Q), row_spec(FOX_W), row_spec(2 * D_MODEL),
                  _const_spec(w_a.shape), _const_spec(w_b.shape), _const_spec(w_o.shape)],
        out_specs=row_spec(D_MODEL),
        out_shape=jax.ShapeDtypeStruct((t, D_MODEL), F32),
        compiler_params=pltpu.CompilerParams(dimension_semantics=("parallel",), vmem_limit_bytes=VMEM_LIMIT),
        name="mix",
    )(x2, att_a, att_b, gates, w_a, w_b, w_o)


def _mlp_ple_kernel(h_ref, p_ref, gm_ref, w1_ref, w2_ref, gp_ref, wpg_ref, wpp_ref, gf_ref, o_ref):
    h = h_ref[...]
    u = _rms(h, gm_ref[...]).astype(BF16)
    acc = jnp.zeros_like(h)
    for c in range(D_FF // FF_CHUNK):
        cols = slice(c * FF_CHUNK, (c + 1) * FF_CHUNK)
        a = jnp.dot(u, w1_ref[:, cols], preferred_element_type=F32)
        a = jnp.square(jnp.maximum(a, 0.0)).astype(BF16)
        acc = acc + jnp.dot(a, w2_ref[cols, :], preferred_element_type=F32)
    h = h + acc
    gate = jax.nn.sigmoid(jnp.dot(_rms(h, gp_ref[...]).astype(BF16), wpg_ref[...], preferred_element_type=F32))
    proj = jnp.dot(p_ref[...].astype(BF16), wpp_ref[...], preferred_element_type=F32)
    h = h + gate * proj
    o_ref[...] = _rms(h, gf_ref[...])


def _mlp_ple(h2, p2, g_mlp, w1, w2, g_ple, w_pg, w_pp, g_final):
    t = h2.shape[0]
    row_spec = lambda w: pl.BlockSpec((TM, w), lambda i: (i, 0))
    vec = _const_spec((1, D_MODEL))
    return pl.pallas_call(
        _mlp_ple_kernel,
        grid=(t // TM,),
        in_specs=[row_spec(D_MODEL), row_spec(PLE_DIM), vec, _const_spec(w1.shape), _const_spec(w2.shape),
                  vec, _const_spec(w_pg.shape), _const_spec(w_pp.shape), vec],
        out_specs=row_spec(D_MODEL),
        out_shape=jax.ShapeDtypeStruct((t, D_MODEL), F32),
        compiler_params=pltpu.CompilerParams(dimension_semantics=("parallel",), vmem_limit_bytes=VMEM_LIMIT),
        name="mlp_ple",
    )(h2, p2, g_mlp, w1, w2, g_ple, w_pg, w_pp, g_final)


def kernel(x, p, g_mix, w_in, b_forget, swa_sinks, w_br_swa, w_br_fox, w_mix_out,
           g_mlp, w_ff1, w_ff2, g_ple, w_ple_gate, w_ple_proj, g_final):
    b, s, d = x.shape
    assert d == D_MODEL and w_in.shape[0] == 1, "single-layer trunk with D_MODEL channels only"
    assert s % TM == 0 and s % FOX_TQ == 0 and s % SWA_TILE == 0 and SWA_TILE % SWA_BLOCK == 0
    t = b * s
    x2 = x.reshape(t, d)

    w = w_in[0]
    assert w.shape[1] == COL_G + 2 * D_MODEL
    wg = lax.optimization_barrier(w[:, COL_G:]).astype(BF16)
    bf_pad = jnp.pad(b_forget[0], (0, LANES - FOX_HEADS)).reshape(1, LANES)

    later = (w_br_swa[0], w_br_fox[0], w_mix_out[0], w_ff1[0], w_ff2[0], w_ple_gate[0])
    qat, ka, vta, q3t, k3, vtb, gates, *later_bf16 = _in_proj(x2, g_mix[0].reshape(1, d), w, wg, bf_pad, b, s, later)
    wa_b, wb_b, wo_b, w1_b, w2_b, wpg_b = later_bf16

    att_a = _swa(qat, ka.reshape(b, s, SWA_KV), vta, swa_sinks[0].reshape(1, SWA_HEADS),
                 jnp.asarray(_swa_bias_table()))
    att_b = _fox(q3t, k3.reshape(b, s, FOX_HEADS * FOX_DEPTH), vtb.reshape(b, FOX_HEADS, VT_ROWS, s))

    h = _mix(x2, att_a.reshape(t, SWA_Q), att_b.reshape(t, FOX_W), gates, wa_b, wb_b, wo_b)
    out = _mlp_ple(h, p[0].reshape(t, PLE_DIM), g_mlp[0].reshape(1, d), w1_b, w2_b,
                   g_ple[0].reshape(1, d), wpg_b, w_ple_proj[0].astype(BF16), g_final.reshape(1, d))
    return out.reshape(b, s, d)
```

```python
import functools

import numpy as np
import jax
import jax.numpy as jnp
from jax import lax
from jax.experimental import pallas as pl
from jax.experimental.pallas import tpu as pltpu

D_MODEL = 1024
CHUNK = 64
PLE_DIM = 256
HEAD_DIM = 64
SWA_HEADS = 8
SWA_KV_HEADS = 2
SWA_GROUP = SWA_HEADS // SWA_KV_HEADS
WINDOW = 128
SWA_BLOCK = WINDOW
FOX_HEADS = 8
D_FF = 4 * D_MODEL
RMS_EPS = 1e-6
SWA_Q = SWA_HEADS * HEAD_DIM
SWA_KV = SWA_KV_HEADS * HEAD_DIM
FOX_W = FOX_HEADS * HEAD_DIM
COL_KA = SWA_Q
COL_VA = COL_KA + SWA_KV
COL_QB = COL_VA + SWA_KV
COL_KB = COL_QB + FOX_W
COL_VB = COL_KB + FOX_W
COL_F = COL_VB + FOX_W
COL_G = COL_F + FOX_HEADS
SCALE = HEAD_DIM ** -0.5
LOG2E = float(np.log2(np.e))

LANES = 128
BF16_SUBLANES = 16
MXU_COLS = 256
NEG_BIG = -1e30
VMEM_LIMIT = 52 * 1024 * 1024
IN_PROJ_VMEM_LIMIT = 58 * 1024 * 1024

TM = 512
MIX_TM = 1024
SWA_TILE = 512
SWA_SLOTS = 6
FOX_TQ = 512
FF_CHUNK = 1024

FOX_PAIRS = FOX_HEADS // 2
FOX_DEPTH = LANES
BIAS_SLOT = 8
C_PARTS = 3
VT_ROWS = 80

F32 = jnp.float32
BF16 = jnp.bfloat16
NT_DIMS = (((1,), (1,)), ((), ()))


def _rms(x, g):
    return x * lax.rsqrt(jnp.mean(x * x, axis=-1, keepdims=True) + RMS_EPS) * g


def _const_spec(shape):
    return pl.BlockSpec(shape, lambda *_: (0,) * len(shape), pipeline_mode=pl.Buffered(1))


def _split3(v):
    hi = v.astype(BF16)
    r1 = v - hi.astype(F32)
    mid = r1.astype(BF16)
    lo = (r1 - mid.astype(F32)).astype(BF16)
    return hi, mid, lo


def _bias_placement():
    place_qt = np.zeros((LANES, C_PARTS * LANES), np.float32)
    ones_qt = np.zeros((LANES, 1), np.float32)
    place_k = np.zeros((C_PARTS * LANES, LANES), np.float32)
    ones_k = np.zeros((1, LANES), np.float32)
    for h in range(FOX_HEADS):
        for part in range(C_PARTS):
            place_qt[h * BIAS_SLOT + part, part * LANES + h] = 1.0
            ones_qt[h * BIAS_SLOT + C_PARTS + part, 0] = 1.0
            ones_k[0, h * BIAS_SLOT + part] = 1.0
            place_k[part * LANES + h, h * BIAS_SLOT + C_PARTS + part] = -1.0
    return (place_qt, ones_qt), (place_k, ones_k)


def _in_proj_kernel(*refs, tiles_per_seq, n_cast):
    (x_ref, g_ref, wlo_ref, whi_ref, bf_ref, pqt_ref, oqt_ref, pk_ref, ok_ref, vone_ref) = refs[:10]
    cast_in = refs[10:10 + n_cast]
    qat_ref, ka_ref, vta_ref, q3t_ref, k3_ref, vtb_ref, gates_ref = refs[10 + n_cast:17 + n_cast]
    cast_out = refs[17 + n_cast:17 + 2 * n_cast]
    carry_ref, wt_ref, wkf_ref, wk_ref, wg_ref = refs[17 + 2 * n_cast:]
    i = pl.program_id(0)
    lane = lax.broadcasted_iota(jnp.int32, (1, LANES), 1)
    gate_shift = COL_G % LANES
    for src, dst in zip(cast_in, cast_out):
        dst[...] = src[...].astype(BF16)

    @pl.when(i == 0)
    def _():
        wk_ref[...] = wlo_ref[:, COL_KB:COL_KB + FOX_W].astype(BF16)
        wkf_ref[:, :SWA_KV] = wlo_ref[:, COL_KA:COL_KA + SWA_KV].astype(BF16)
        leftover = pltpu.roll(whi_ref[:, 2 * D_MODEL:2 * D_MODEL + LANES], gate_shift, 1)
        second = jnp.where(lane < gate_shift, whi_ref[:, :LANES], jnp.where(lane < 2 * gate_shift, leftover, 0.0))
        wkf_ref[:, SWA_KV:] = second.astype(BF16)
        wg_ref[...] = whi_ref[:, :2 * D_MODEL].astype(BF16)
        wt_ref[:SWA_Q] = wlo_ref[:, :SWA_Q].T.astype(BF16)
        wt_ref[SWA_Q:SWA_Q + FOX_W] = wlo_ref[:, COL_QB:COL_QB + FOX_W].T.astype(BF16)
        v_t = jnp.concatenate([wlo_ref[:, COL_VA:COL_VA + SWA_KV], wlo_ref[:, COL_VB:COL_VB + FOX_W]], axis=1).T
        pad_rows = jnp.zeros((VT_ROWS - HEAD_DIM, D_MODEL), BF16)
        for head in range(SWA_KV_HEADS + FOX_HEADS):
            base = SWA_Q + FOX_W + head * VT_ROWS
            wt_ref[base:base + HEAD_DIM] = v_t[head * HEAD_DIM:(head + 1) * HEAD_DIM].astype(BF16)
            wt_ref[base + HEAD_DIM:base + VT_ROWS] = pad_rows

    @pl.when(i % tiles_per_seq == 0)
    def _():
        carry_ref[...] = jnp.zeros_like(carry_ref)

    u = _rms(x_ref[...], g_ref[...]).astype(BF16)
    tm = u.shape[0]
    kf = jnp.dot(u, wkf_ref[...], preferred_element_type=F32)
    ka_ref[...] = kf[:, :SWA_KV].astype(BF16)
    f = kf[:, SWA_KV:] + bf_ref[...]
    logf = jnp.minimum(f, 0.0) - jnp.log1p(jnp.exp(-jnp.abs(f)))

    gl = jnp.dot(u, wg_ref[...], preferred_element_type=F32)
    n_tiles = gl.shape[1] // LANES
    for c in range(n_tiles):
        cur = pltpu.roll(gl[:, c * LANES:(c + 1) * LANES], LANES - gate_shift, 1)
        if c + 1 < n_tiles:
            nxt = pltpu.roll(gl[:, (c + 1) * LANES:(c + 2) * LANES], LANES - gate_shift, 1)
        else:
            nxt = pltpu.roll(kf[:, SWA_KV:], LANES - 2 * gate_shift, 1)
        tile = jnp.where(lane < LANES - gate_shift, cur, nxt)
        gates_ref[:, c * LANES:(c + 1) * LANES] = jax.nn.sigmoid(tile).astype(BF16)

    row = lax.broadcasted_iota(jnp.int32, (tm, tm), 0)
    col = lax.broadcasted_iota(jnp.int32, (tm, tm), 1)
    tri = (col <= row).astype(BF16)
    cs = jnp.dot(tri, jnp.concatenate(_split3(logf), axis=1), preferred_element_type=F32)
    c = cs[:, :LANES] + cs[:, LANES:2 * LANES] + cs[:, 2 * LANES:] + carry_ref[...]
    carry_ref[...] = c[tm - 1:tm, :]

    tt = lax.dot_general(wt_ref[...], u, NT_DIMS, preferred_element_type=F32)
    qat_ref[...] = (tt[:SWA_Q] * (SCALE * LOG2E)).astype(BF16)
    qt = (tt[SWA_Q:SWA_Q + FOX_W] * (SCALE * LOG2E)).astype(BF16)
    vt = (tt[SWA_Q + FOX_W:] + vone_ref[...]).astype(BF16)
    vta_ref[...] = vt[:SWA_KV_HEADS * VT_ROWS]
    vtb_ref[...] = vt[SWA_KV_HEADS * VT_ROWS:]

    parts = jnp.concatenate(_split3(c * LOG2E), axis=1)
    kbias = jnp.dot(parts, pk_ref[...], preferred_element_type=F32) + ok_ref[...]
    qbias_t = lax.dot_general(pqt_ref[...], parts, NT_DIMS, preferred_element_type=F32) + oqt_ref[...]
    zero_rows = jnp.zeros((HEAD_DIM - BIAS_SLOT, tm), F32)
    for h in range(FOX_HEADS):
        q3t_ref[h, :HEAD_DIM, :] = qt[h * HEAD_DIM:(h + 1) * HEAD_DIM]
        own = jnp.concatenate([qbias_t[h * BIAS_SLOT:(h + 1) * BIAS_SLOT], zero_rows], axis=0)
        q3t_ref[h, HEAD_DIM:, :] = own.astype(BF16)

    k = jnp.dot(u, wk_ref[...], preferred_element_type=F32)
    lane = lax.broadcasted_iota(jnp.int32, (1, LANES), 1)
    for h in range(FOX_HEADS):
        pair = k[:, (h // 2) * LANES:(h // 2 + 1) * LANES]
        dims = pair if h % 2 == 0 else pltpu.roll(pair, HEAD_DIM, 1)
        bias = pltpu.roll(kbias, HEAD_DIM - h * BIAS_SLOT, 1)
        k3_ref[:, h * FOX_DEPTH:(h + 1) * FOX_DEPTH] = jnp.where(lane < HEAD_DIM, dims, bias).astype(BF16)


def _in_proj(x2, g_mix, w_all, bf_pad, batch, seq, later_weights):
    t = x2.shape[0]
    tps = seq // TM
    steps = t // TM
    (place_qt, ones_qt), (place_k, ones_k) = _bias_placement()
    n_vt = SWA_KV_HEADS + FOX_HEADS
    vone = np.zeros((n_vt * VT_ROWS, 1), np.float32)
    vone[HEAD_DIM::VT_ROWS] = 1.0
    consts = (jnp.asarray(place_qt, BF16), jnp.asarray(ones_qt), jnp.asarray(place_k, BF16), jnp.asarray(ones_k),
              jnp.asarray(vone))
    kern = functools.partial(_in_proj_kernel, tiles_per_seq=tps, n_cast=len(later_weights))
    row_spec = lambda w: pl.BlockSpec((TM, w), lambda i: (i, 0))
    col_spec = lambda *lead: pl.BlockSpec((None,) + lead + (TM,), lambda i: (i // tps,) + (0,) * len(lead) + (i % tps,))
    for w in later_weights:
        assert w.ndim == 2 and w.shape[0] % (steps * BF16_SUBLANES) == 0, w.shape
    slab_specs = [pl.BlockSpec((w.shape[0] // steps, w.shape[1]), lambda i: (i, 0)) for w in later_weights]
    assert COL_F % LANES == 0 and w_all.shape == (D_MODEL, COL_G + 2 * D_MODEL) and COL_F >= 2 * D_MODEL + 2 * LANES
    half_specs = [pl.BlockSpec((D_MODEL, COL_F), lambda i, j=j: (0, j), pipeline_mode=pl.Buffered(1)) for j in range(2)]
    tail = (bf_pad,) + consts
    ka, k3, gates = (jax.ShapeDtypeStruct((t, w), BF16) for w in (SWA_KV, FOX_HEADS * FOX_DEPTH, 2 * D_MODEL))
    qat = jax.ShapeDtypeStruct((batch, SWA_Q, seq), BF16)
    q3t = jax.ShapeDtypeStruct((batch, FOX_HEADS, FOX_DEPTH, seq), BF16)
    vta, vtb = (jax.ShapeDtypeStruct((batch, heads * VT_ROWS, seq), BF16) for heads in (SWA_KV_HEADS, FOX_HEADS))
    return pl.pallas_call(
        kern,
        grid=(steps,),
        in_specs=([row_spec(D_MODEL), _const_spec(g_mix.shape)] + half_specs
                  + [_const_spec(a.shape) for a in tail] + slab_specs),
        out_specs=[col_spec(SWA_Q), row_spec(SWA_KV), col_spec(SWA_KV_HEADS * VT_ROWS),
                   col_spec(FOX_HEADS, FOX_DEPTH), row_spec(FOX_HEADS * FOX_DEPTH), col_spec(FOX_HEADS * VT_ROWS),
                   row_spec(2 * D_MODEL)] + slab_specs,
        out_shape=[qat, ka, vta, q3t, k3, vtb, gates] + [jax.ShapeDtypeStruct(w.shape, BF16) for w in later_weights],
        scratch_shapes=[pltpu.VMEM((1, LANES), F32),
                        pltpu.VMEM((SWA_Q + FOX_W + n_vt * VT_ROWS, D_MODEL), BF16),
                        pltpu.VMEM((D_MODEL, SWA_KV + LANES), BF16),
                        pltpu.VMEM((D_MODEL, FOX_W), BF16),
                        pltpu.VMEM((D_MODEL, 2 * D_MODEL), BF16)],
        compiler_params=pltpu.CompilerParams(dimension_semantics=("arbitrary",), vmem_limit_bytes=IN_PROJ_VMEM_LIMIT),
        name="in_proj",
    )(x2, g_mix, w_all, w_all, *tail, *later_weights)


def _swa_bias_table():
    sb = SWA_BLOCK
    qi = np.arange(sb)[None, :] + sb
    si = np.arange(2 * sb)[:, None]
    chunk_diff = qi // CHUNK - si // CHUNK
    band_ok = (chunk_diff >= 0) & (chunk_diff <= WINDOW // CHUNK)
    slopes = np.array([2.0 ** (-8.0 * (h + 1) / SWA_HEADS) for h in range(SWA_HEADS)], dtype=np.float32)
    alibi = -slopes[:, None, None] * np.abs(qi - si).astype(np.float32)[None] * np.float32(LOG2E)
    first = band_ok & (si >= sb)
    table = np.stack([np.where(first[None], alibi, NEG_BIG), np.where(band_ok[None], alibi, NEG_BIG)])
    table = table.reshape(2, SWA_KV_HEADS, SWA_GROUP, 2 * sb, sb).transpose(0, 1, 3, 2, 4)
    return np.ascontiguousarray(table.reshape(2, SWA_KV_HEADS, 2 * sb, SWA_GROUP * sb)).astype(np.float32)


def _swa_kernel(sink_ref, qt_ref, kp_ref, kc_ref, vtp_ref, vtc_ref, bias_first_ref, bias_rest_ref, o_ref,
                st_sc, mb_sc, ot_sc):
    sb = SWA_BLOCK
    per_unit = MXU_COLS // sb
    half = jnp.zeros((HEAD_DIM, MXU_COLS), BF16)
    units = [(j, kh, c) for j in range(SWA_TILE // sb) for kh in range(SWA_KV_HEADS)
             for c in range(SWA_GROUP // per_unit)]

    def band(j):
        if j == 0:
            kb = jnp.concatenate([kp_ref[...], kc_ref[0:sb, :]], axis=0)
            vtb = jnp.concatenate([vtp_ref[...], vtc_ref[:, 0:sb]], axis=1)
            return kb, vtb, bias_first_ref
        return kc_ref[(j - 1) * sb:(j + 1) * sb, :], vtc_ref[:, (j - 1) * sb:(j + 1) * sb], bias_rest_ref

    def unit_heads(kh, c):
        return [kh * SWA_GROUP + c * per_unit + g for g in range(per_unit)]

    def scores(i, slot):
        j, kh, c = units[i]
        kb, _, bias_ref = band(j)
        q2 = jnp.concatenate([qt_ref[h * HEAD_DIM:(h + 1) * HEAD_DIM, j * sb:(j + 1) * sb] for h in unit_heads(kh, c)],
                             axis=1)
        q2t = jnp.concatenate([q2, half] if kh == 0 else [half, q2], axis=0)
        st = jnp.dot(kb, q2t, preferred_element_type=F32) + bias_ref[kh, :, c * MXU_COLS:(c + 1) * MXU_COLS]
        st_sc[slot] = st
        mb_sc[slot] = jnp.max(st, axis=0, keepdims=True)

    def consume(unit, slot):
        j, kh, c = unit
        _, vtb, _ = band(j)
        heads = unit_heads(kh, c)
        sink = jnp.concatenate([jnp.full((1, sb), sink_ref[0, h] * LOG2E, F32) for h in heads], axis=1)
        m = jnp.maximum(mb_sc[slot], sink)
        p = jnp.exp2(st_sc[slot] - m).astype(BF16)
        pv = jnp.dot(vtb[kh * VT_ROWS:(kh + 1) * VT_ROWS, :], p, preferred_element_type=F32)
        denom = pv[HEAD_DIM:HEAD_DIM + 1, :] + jnp.exp2(sink - m)
        o = pv[:HEAD_DIM, :] / denom
        for g, h in enumerate(heads):
            ot_sc[h * HEAD_DIM:(h + 1) * HEAD_DIM, j * sb:(j + 1) * sb] = o[:, g * sb:(g + 1) * sb]

    slots = st_sc.shape[0]
    for i in range(slots - 1):
        scores(i, i)
    for i, unit in enumerate(units):
        ahead = i + slots - 1
        if ahead < len(units):
            scores(ahead, ahead % slots)
        consume(unit, i % slots)
    o_ref[...] = ot_sc[...].T.astype(BF16)


def _swa(qt3, k3, vt3, sinks, bias):
    b, s, _ = k3.shape
    sb, ts = SWA_BLOCK, SWA_TILE
    per = ts // sb
    prev = lambda n: jnp.maximum(n * per - 1, 0)
    bias_block = (None,) + bias.shape[1:]
    return pl.pallas_call(
        _swa_kernel,
        grid=(b, s // ts),
        in_specs=[pl.BlockSpec(memory_space=pltpu.SMEM),
                  pl.BlockSpec((None, SWA_Q, ts), lambda bi, n: (bi, 0, n)),
                  pl.BlockSpec((None, sb, SWA_KV), lambda bi, n: (bi, prev(n), 0)),
                  pl.BlockSpec((None, ts, SWA_KV), lambda bi, n: (bi, n, 0)),
                  pl.BlockSpec((None, SWA_KV_HEADS * VT_ROWS, sb), lambda bi, n: (bi, 0, prev(n))),
                  pl.BlockSpec((None, SWA_KV_HEADS * VT_ROWS, ts), lambda bi, n: (bi, 0, n)),
                  pl.BlockSpec(bias_block, lambda bi, n: (jnp.minimum(n, 1), 0, 0, 0)),
                  pl.BlockSpec(bias_block, lambda bi, n: (1, 0, 0, 0))],
        out_specs=pl.BlockSpec((None, ts, SWA_Q), lambda bi, n: (bi, n, 0)),
        out_shape=jax.ShapeDtypeStruct((b, s, SWA_Q), BF16),
        scratch_shapes=[pltpu.VMEM((SWA_SLOTS, 2 * sb, MXU_COLS), F32),
                        pltpu.VMEM((SWA_SLOTS, 1, MXU_COLS), F32),
                        pltpu.VMEM((SWA_Q, ts), F32)],
        compiler_params=pltpu.CompilerParams(dimension_semantics=("parallel", "parallel"),
                                             vmem_limit_bytes=VMEM_LIMIT),
        name="swa",
    )(sinks, qt3, k3, k3, vt3, vt3, bias, bias)


def _fox_kernel(qt_ref, k_ref, vt_ref, o_ref, st_sc, mb_sc, m_sc, acc_sc):
    tq = qt_ref.shape[2]
    qi = pl.program_id(2)
    m_sc[...] = jnp.full_like(m_sc, NEG_BIG)
    acc_sc[...] = jnp.zeros_like(acc_sc)

    units = [(hh, slice(c * MXU_COLS, (c + 1) * MXU_COLS)) for hh in range(2) for c in range(tq // MXU_COLS)]

    def scores(block, slot, unit, diagonal):
        hh, cols = unit
        keys = cols.stop if diagonal else tq
        start = pl.multiple_of(block * tq, tq)
        k3 = k_ref[pl.ds(start, keys), hh * FOX_DEPTH:(hh + 1) * FOX_DEPTH]
        st = jnp.dot(k3, qt_ref[hh, :, cols], preferred_element_type=F32)
        if diagonal:
            key = lax.broadcasted_iota(jnp.int32, st.shape, 0)
            qry = lax.broadcasted_iota(jnp.int32, st.shape, 1) + cols.start
            st = jnp.where(key <= qry, st, NEG_BIG)
        st_sc[slot, hh, :keys, cols] = st
        mb_sc[slot, hh, :, cols] = jnp.max(st, axis=0, keepdims=True)

    def consume(block, slot, unit, diagonal=False):
        hh, cols = unit
        keys = cols.stop if diagonal else tq
        start = pl.multiple_of(block * tq, tq)
        m_prev = m_sc[hh, :, cols]
        m_new = jnp.maximum(m_prev, mb_sc[slot, hh, :, cols])
        p = jnp.exp2(st_sc[slot, hh, :keys, cols] - m_new).astype(BF16)
        alpha = jnp.exp2(m_prev - m_new)
        pv = jnp.dot(vt_ref[hh, :, pl.ds(start, keys)], p, preferred_element_type=F32)
        acc_sc[hh, :, cols] = alpha * acc_sc[hh, :, cols] + pv
        m_sc[hh, :, cols] = m_new

    def stage(next_block, next_slot, block, slot, diagonal=False):
        for unit in units:
            scores(next_block, next_slot, unit, False)
            consume(block, slot, unit, diagonal)

    for unit in units:
        scores(qi, 0, unit, True)
    stage(0, 1, qi, 0, diagonal=True)

    def block_pair(b0):
        stage(b0 + 1, 0, b0, 1)
        stage(jnp.minimum(b0 + 2, qi), 1, b0 + 1, 0)

    def block_oct(t, carry):
        for pair in range(4):
            block_pair(8 * t + 2 * pair)
        return carry

    lax.fori_loop(0, qi // 8, block_oct, 0)

    @pl.when(qi % 8 >= 4)
    def _():
        block_pair(qi // 8 * 8)
        block_pair(qi // 8 * 8 + 2)

    @pl.when(qi % 4 >= 2)
    def _():
        block_pair(qi // 4 * 4)

    @pl.when(qi % 2 == 1)
    def _():
        for unit in units:
            consume(qi - 1, 1, unit)

    outs = [acc_sc[hh, :HEAD_DIM, :] / acc_sc[hh, HEAD_DIM:HEAD_DIM + 1, :] for hh in range(2)]
    o_ref[...] = jnp.concatenate(outs, axis=0).T.astype(BF16)


def _fox(q3t, k3, vt4):
    b, s, _ = k3.shape
    tq = FOX_TQ
    return pl.pallas_call(
        _fox_kernel,
        grid=(b, FOX_PAIRS, s // tq),
        in_specs=[pl.BlockSpec((None, 2, FOX_DEPTH, tq), lambda bi, hp, qi: (bi, hp, 0, qi)),
                  pl.BlockSpec((None, s, 2 * FOX_DEPTH), lambda bi, hp, qi: (bi, 0, hp)),
                  pl.BlockSpec((None, 2, VT_ROWS, s), lambda bi, hp, qi: (bi, hp, 0, 0))],
        out_specs=pl.BlockSpec((None, tq, LANES), lambda bi, hp, qi: (bi, qi, hp)),
        out_shape=jax.ShapeDtypeStruct((b, s, FOX_W), BF16),
        scratch_shapes=[pltpu.VMEM((2, 2, tq, tq), F32),
                        pltpu.VMEM((2, 2, 1, tq), F32),
                        pltpu.VMEM((2, 1, tq), F32),
                        pltpu.VMEM((2, VT_ROWS, tq), F32)],
        compiler_params=pltpu.CompilerParams(dimension_semantics=("parallel", "parallel", "parallel"),
                                             vmem_limit_bytes=VMEM_LIMIT),
        name="fox",
    )(q3t, k3, vt4)


def _mix_kernel(x_ref, ya_ref, yb_ref, gates_ref, wa_ref, wb_ref, wo_ref, h_ref):
    slabs = [slice(r, r + TM) for r in range(0, MIX_TM, TM)]

    def branches(rows):
        y_a = jnp.dot(ya_ref[rows, :], wa_ref[...], preferred_element_type=F32)
        y_b = jnp.dot(yb_ref[rows, :], wb_ref[...], preferred_element_type=F32)
        return y_a, y_b

    pending = branches(slabs[0])
    for n, rows in enumerate(slabs):
        y_a, y_b = pending
        if n + 1 < len(slabs):
            pending = branches(slabs[n + 1])
        gates = gates_ref[rows, :].astype(F32)
        mixed = gates[:, :D_MODEL] * y_a + gates[:, D_MODEL:] * y_b
        h_ref[rows, :] = x_ref[rows, :] + jnp.dot(mixed.astype(BF16), wo_ref[...], preferred_element_type=F32)


def _mix(x2, att_a, att_b, gates, w_a, w_b, w_o):
    t = x2.shape[0]
    assert t % MIX_TM == 0 and MIX_TM % TM == 0
    row_spec = lambda w: pl.BlockSpec((MIX_TM, w), lambda i: (i, 0))
    return pl.pallas_call(
        _mix_kernel,
        grid=(t // MIX_TM,),
        in_specs=[row_spec(D_MODEL), row_spec(SWA_Q), row_spec(FOX_W), row_spec(2 * D_MODEL),
                  _const_spec(w_a.shape), _const_spec(w_b.shape), _const_spec(w_o.shape)],
        out_specs=row_spec(D_MODEL),
        out_shape=jax.ShapeDtypeStruct((t, D_MODEL), F32),
        compiler_params=pltpu.CompilerParams(dimension_semantics=("parallel",), vmem_limit_bytes=VMEM_LIMIT),
        name="mix",
    )(x2, att_a, att_b, gates, w_a, w_b, w_o)


def _mlp_ple_kernel(h_ref, p_ref, gm_ref, w1_ref, w2_ref, gp_ref, wpg_ref, wpp_ref, gf_ref, o_ref):
    h = h_ref[...]
    u = _rms(h, gm_ref[...]).astype(BF16)
    acc = jnp.zeros_like(h)
    for c in range(D_FF // FF_CHUNK):
        cols = slice(c * FF_CHUNK, (c + 1) * FF_CHUNK)
        a = jnp.dot(u, w1_ref[:, cols], preferred_element_type=F32)
        a = jnp.square(jnp.maximum(a, 0.0)).astype(BF16)
        acc = acc + jnp.dot(a, w2_ref[cols, :], preferred_element_type=F32)
    h = h + acc
    gate = jax.nn.sigmoid(jnp.dot(_rms(h, gp_ref[...]).astype(BF16), wpg_ref[...], preferred_element_type=F32))
    proj = jnp.dot(p_ref[...].astype(BF16), wpp_ref[...], preferred_element_type=F32)
    h = h + gate * proj
    o_ref[...] = _rms(h, gf_ref[...])


def _mlp_ple(h2, p2, g_mlp, w1, w2, g_ple, w_pg, w_pp, g_final):
    t = h2.shape[0]
    row_spec = lambda w: pl.BlockSpec((TM, w), lambda i: (i, 0))
    vec = _const_spec((1, D_MODEL))
    return pl.pallas_call(
        _mlp_ple_kernel,
        grid=(t // TM,),
        in_specs=[row_spec(D_MODEL), row_spec(PLE_DIM), vec, _const_spec(w1.shape), _const_spec(w2.shape),
                  vec, _const_spec(w_pg.shape), _const_spec(w_pp.shape), vec],
        out_specs=row_spec(D_MODEL),
        out_shape=jax.ShapeDtypeStruct((t, D_MODEL), F32),
        compiler_params=pltpu.CompilerParams(dimension_semantics=("parallel",), vmem_limit_bytes=VMEM_LIMIT),
        name="mlp_ple",
    )(h2, p2, g_mlp, w1, w2, g_ple, w_pg, w_pp, g_final)


def kernel(x, p, g_mix, w_in, b_forget, swa_sinks, w_br_swa, w_br_fox, w_mix_out,
           g_mlp, w_ff1, w_ff2, g_ple, w_ple_gate, w_ple_proj, g_final):
    b, s, d = x.shape
    assert d == D_MODEL and w_in.shape[0] == 1, "single-layer trunk with D_MODEL channels only"
    assert s % TM == 0 and s % FOX_TQ == 0 and s % SWA_TILE == 0 and SWA_TILE % SWA_BLOCK == 0
    t = b * s
    x2 = x.reshape(t, d)

    w = w_in[0]
    bf_pad = jnp.pad(b_forget[0], (0, LANES - FOX_HEADS)).reshape(1, LANES)

    later = (w_br_swa[0], w_br_fox[0], w_mix_out[0], w_ff1[0], w_ff2[0], w_ple_gate[0])
    qat, ka, vta, q3t, k3, vtb, gates, *later_bf16 = _in_proj(x2, g_mix[0].reshape(1, d), w, bf_pad, b, s, later)
    wa_b, wb_b, wo_b, w1_b, w2_b, wpg_b = later_bf16

    att_a = _swa(qat, ka.reshape(b, s, SWA_KV), vta, swa_sinks[0].reshape(1, SWA_HEADS),
                 jnp.asarray(_swa_bias_table()))
    att_b = _fox(q3t, k3.reshape(b, s, FOX_HEADS * FOX_DEPTH), vtb.reshape(b, FOX_HEADS, VT_ROWS, s))

    h = _mix(x2, att_a.reshape(t, SWA_Q), att_b.reshape(t, FOX_W), gates, wa_b, wb_b, wo_b)
    out = _mlp_ple(h, p[0].reshape(t, PLE_DIM), g_mlp[0].reshape(1, d), w1_b, w2_b,
                   g_ple[0].reshape(1, d), wpg_b, w_ple_proj[0].astype(BF16), g_final.reshape(1, d))
    return out.reshape(b, s, d)
```

```python
import functools

import numpy as np
import jax
import jax.numpy as jnp
from jax import lax
from jax.experimental import pallas as pl
from jax.experimental.pallas import tpu as pltpu

D_MODEL = 1024
CHUNK = 64
PLE_DIM = 256
HEAD_DIM = 64
SWA_HEADS = 8
SWA_KV_HEADS = 2
SWA_GROUP = SWA_HEADS // SWA_KV_HEADS
WINDOW = 128
SWA_BLOCK = WINDOW
FOX_HEADS = 8
D_FF = 4 * D_MODEL
RMS_EPS = 1e-6
SWA_Q = SWA_HEADS * HEAD_DIM
SWA_KV = SWA_KV_HEADS * HEAD_DIM
FOX_W = FOX_HEADS * HEAD_DIM
COL_KA = SWA_Q
COL_VA = COL_KA + SWA_KV
COL_QB = COL_VA + SWA_KV
COL_KB = COL_QB + FOX_W
COL_VB = COL_KB + FOX_W
COL_F = COL_VB + FOX_W
COL_G = COL_F + FOX_HEADS
SCALE = HEAD_DIM ** -0.5
LOG2E = float(np.log2(np.e))

LANES = 128
BF16_SUBLANES = 16
MXU_COLS = 256
NEG_BIG = -1e30
VMEM_LIMIT = 52 * 1024 * 1024
IN_PROJ_VMEM_LIMIT = 58 * 1024 * 1024

TM = 512
MIX_TM = 1024
SWA_TILE = 512
SWA_SLOTS = 6
FOX_TQ = 512
FF_CHUNK = 1024

FOX_PAIRS = FOX_HEADS // 2
FOX_DEPTH = LANES
BIAS_SLOT = 8
C_PARTS = 3
VT_ROWS = 80

F32 = jnp.float32
BF16 = jnp.bfloat16
NT_DIMS = (((1,), (1,)), ((), ()))


def _rms(x, g):
    return x * lax.rsqrt(jnp.mean(x * x, axis=-1, keepdims=True) + RMS_EPS) * g


def _const_spec(shape):
    return pl.BlockSpec(shape, lambda *_: (0,) * len(shape), pipeline_mode=pl.Buffered(1))


def _split3(v):
    hi = v.astype(BF16)
    r1 = v - hi.astype(F32)
    mid = r1.astype(BF16)
    lo = (r1 - mid.astype(F32)).astype(BF16)
    return hi, mid, lo


def _bias_placement():
    place_qt = np.zeros((LANES, C_PARTS * LANES), np.float32)
    ones_qt = np.zeros((LANES, 1), np.float32)
    place_k = np.zeros((C_PARTS * LANES, LANES), np.float32)
    ones_k = np.zeros((1, LANES), np.float32)
    for h in range(FOX_HEADS):
        for part in range(C_PARTS):
            place_qt[h * BIAS_SLOT + part, part * LANES + h] = 1.0
            ones_qt[h * BIAS_SLOT + C_PARTS + part, 0] = 1.0
            ones_k[0, h * BIAS_SLOT + part] = 1.0
            place_k[part * LANES + h, h * BIAS_SLOT + C_PARTS + part] = -1.0
    return (place_qt, ones_qt), (place_k, ones_k)


def _in_proj_kernel(*refs, tiles_per_seq, n_cast):
    (x_ref, g_ref, wlo_ref, whi_ref, bf_ref, pqt_ref, oqt_ref, pk_ref, ok_ref, vone_ref) = refs[:10]
    cast_in = refs[10:10 + n_cast]
    qat_ref, ka_ref, vta_ref, q3t_ref, k3_ref, vtb_ref, gates_ref = refs[10 + n_cast:17 + n_cast]
    cast_out = refs[17 + n_cast:17 + 2 * n_cast]
    carry_ref, wt_ref, wkf_ref, wk_ref, wg_ref = refs[17 + 2 * n_cast:]
    i = pl.program_id(0)
    lane = lax.broadcasted_iota(jnp.int32, (1, LANES), 1)
    gate_shift = COL_G % LANES
    for src, dst in zip(cast_in, cast_out):
        dst[...] = src[...].astype(BF16)

    @pl.when(i == 0)
    def _():
        wk_ref[...] = wlo_ref[COL_KB:COL_KB + FOX_W].T.astype(BF16)
        wkf_ref[:, :SWA_KV] = wlo_ref[COL_KA:COL_KA + SWA_KV].T.astype(BF16)
        second = jnp.concatenate([whi_ref[:gate_shift], whi_ref[2 * D_MODEL:2 * D_MODEL + gate_shift],
                                  jnp.zeros((LANES - 2 * gate_shift, D_MODEL), F32)], axis=0)
        wkf_ref[:, SWA_KV:] = second.T.astype(BF16)
        wg_ref[...] = whi_ref[:2 * D_MODEL].T.astype(BF16)
        wt_ref[:SWA_Q] = wlo_ref[:SWA_Q].astype(BF16)
        wt_ref[SWA_Q:SWA_Q + FOX_W] = wlo_ref[COL_QB:COL_QB + FOX_W].astype(BF16)
        pad_rows = jnp.zeros((VT_ROWS - HEAD_DIM, D_MODEL), BF16)
        v_rows = [COL_VA + g * HEAD_DIM for g in range(SWA_KV_HEADS)] + [COL_VB + h * HEAD_DIM for h in range(FOX_HEADS)]
        for head, row in enumerate(v_rows):
            base = SWA_Q + FOX_W + head * VT_ROWS
            wt_ref[base:base + HEAD_DIM] = wlo_ref[row:row + HEAD_DIM].astype(BF16)
            wt_ref[base + HEAD_DIM:base + VT_ROWS] = pad_rows

    @pl.when(i % tiles_per_seq == 0)
    def _():
        carry_ref[...] = jnp.zeros_like(carry_ref)

    u = _rms(x_ref[...], g_ref[...]).astype(BF16)
    tm = u.shape[0]
    kf = jnp.dot(u, wkf_ref[...], preferred_element_type=F32)
    ka_ref[...] = kf[:, :SWA_KV].astype(BF16)
    f = kf[:, SWA_KV:] + bf_ref[...]
    logf = jnp.minimum(f, 0.0) - jnp.log1p(jnp.exp(-jnp.abs(f)))

    gl = jnp.dot(u, wg_ref[...], preferred_element_type=F32)
    n_tiles = gl.shape[1] // LANES
    for c in range(n_tiles):
        cur = pltpu.roll(gl[:, c * LANES:(c + 1) * LANES], LANES - gate_shift, 1)
        if c + 1 < n_tiles:
            nxt = pltpu.roll(gl[:, (c + 1) * LANES:(c + 2) * LANES], LANES - gate_shift, 1)
        else:
            nxt = pltpu.roll(kf[:, SWA_KV:], LANES - 2 * gate_shift, 1)
        tile = jnp.where(lane < LANES - gate_shift, cur, nxt)
        gates_ref[:, c * LANES:(c + 1) * LANES] = jax.nn.sigmoid(tile).astype(BF16)

    row = lax.broadcasted_iota(jnp.int32, (tm, tm), 0)
    col = lax.broadcasted_iota(jnp.int32, (tm, tm), 1)
    tri = (col <= row).astype(BF16)
    cs = jnp.dot(tri, jnp.concatenate(_split3(logf), axis=1), preferred_element_type=F32)
    c = cs[:, :LANES] + cs[:, LANES:2 * LANES] + cs[:, 2 * LANES:] + carry_ref[...]
    carry_ref[...] = c[tm - 1:tm, :]

    tt = lax.dot_general(wt_ref[...], u, NT_DIMS, preferred_element_type=F32)
    qat_ref[...] = (tt[:SWA_Q] * (SCALE * LOG2E)).astype(BF16)
    qt = (tt[SWA_Q:SWA_Q + FOX_W] * (SCALE * LOG2E)).astype(BF16)
    vt = (tt[SWA_Q + FOX_W:] + vone_ref[...]).astype(BF16)
    vta_ref[...] = vt[:SWA_KV_HEADS * VT_ROWS]
    vtb_ref[...] = vt[SWA_KV_HEADS * VT_ROWS:]

    parts = jnp.concatenate(_split3(c * LOG2E), axis=1)
    kbias = jnp.dot(parts, pk_ref[...], preferred_element_type=F32) + ok_ref[...]
    qbias_t = lax.dot_general(pqt_ref[...], parts, NT_DIMS, preferred_element_type=F32) + oqt_ref[...]
    zero_rows = jnp.zeros((HEAD_DIM - BIAS_SLOT, tm), F32)
    for h in range(FOX_HEADS):
        q3t_ref[h, :HEAD_DIM, :] = qt[h * HEAD_DIM:(h + 1) * HEAD_DIM]
        own = jnp.concatenate([qbias_t[h * BIAS_SLOT:(h + 1) * BIAS_SLOT], zero_rows], axis=0)
        q3t_ref[h, HEAD_DIM:, :] = own.astype(BF16)

    k = jnp.dot(u, wk_ref[...], preferred_element_type=F32)
    lane = lax.broadcasted_iota(jnp.int32, (1, LANES), 1)
    for h in range(FOX_HEADS):
        pair = k[:, (h // 2) * LANES:(h // 2 + 1) * LANES]
        dims = pair if h % 2 == 0 else pltpu.roll(pair, HEAD_DIM, 1)
        bias = pltpu.roll(kbias, HEAD_DIM - h * BIAS_SLOT, 1)
        k3_ref[:, h * FOX_DEPTH:(h + 1) * FOX_DEPTH] = jnp.where(lane < HEAD_DIM, dims, bias).astype(BF16)


def _in_proj(x2, g_mix, w_all, bf_pad, batch, seq, later_weights):
    t = x2.shape[0]
    tps = seq // TM
    steps = t // TM
    (place_qt, ones_qt), (place_k, ones_k) = _bias_placement()
    n_vt = SWA_KV_HEADS + FOX_HEADS
    vone = np.zeros((n_vt * VT_ROWS, 1), np.float32)
    vone[HEAD_DIM::VT_ROWS] = 1.0
    consts = (jnp.asarray(place_qt, BF16), jnp.asarray(ones_qt), jnp.asarray(place_k, BF16), jnp.asarray(ones_k),
              jnp.asarray(vone))
    kern = functools.partial(_in_proj_kernel, tiles_per_seq=tps, n_cast=len(later_weights))
    row_spec = lambda w: pl.BlockSpec((TM, w), lambda i: (i, 0))
    col_spec = lambda *lead: pl.BlockSpec((None,) + lead + (TM,), lambda i: (i // tps,) + (0,) * len(lead) + (i % tps,))
    for w in later_weights:
        assert w.ndim == 2 and w.shape[0] % (steps * BF16_SUBLANES) == 0, w.shape
    slab_specs = [pl.BlockSpec((w.shape[0] // steps, w.shape[1]), lambda i: (i, 0)) for w in later_weights]
    assert w_all.shape == (COL_G + 2 * D_MODEL, D_MODEL) and COL_F >= 2 * D_MODEL + LANES and COL_F % BF16_SUBLANES == 0
    half_specs = [pl.BlockSpec((COL_F, D_MODEL), lambda i, j=j: (j, 0), pipeline_mode=pl.Buffered(1)) for j in range(2)]
    tail = (bf_pad,) + consts
    ka, k3, gates = (jax.ShapeDtypeStruct((t, w), BF16) for w in (SWA_KV, FOX_HEADS * FOX_DEPTH, 2 * D_MODEL))
    qat = jax.ShapeDtypeStruct((batch, SWA_Q, seq), BF16)
    q3t = jax.ShapeDtypeStruct((batch, FOX_HEADS, FOX_DEPTH, seq), BF16)
    vta, vtb = (jax.ShapeDtypeStruct((batch, heads * VT_ROWS, seq), BF16) for heads in (SWA_KV_HEADS, FOX_HEADS))
    return pl.pallas_call(
        kern,
        grid=(steps,),
        in_specs=([row_spec(D_MODEL), _const_spec(g_mix.shape)] + half_specs
                  + [_const_spec(a.shape) for a in tail] + slab_specs),
        out_specs=[col_spec(SWA_Q), row_spec(SWA_KV), col_spec(SWA_KV_HEADS * VT_ROWS),
                   col_spec(FOX_HEADS, FOX_DEPTH), row_spec(FOX_HEADS * FOX_DEPTH), col_spec(FOX_HEADS * VT_ROWS),
                   row_spec(2 * D_MODEL)] + slab_specs,
        out_shape=[qat, ka, vta, q3t, k3, vtb, gates] + [jax.ShapeDtypeStruct(w.shape, BF16) for w in later_weights],
        scratch_shapes=[pltpu.VMEM((1, LANES), F32),
                        pltpu.VMEM((SWA_Q + FOX_W + n_vt * VT_ROWS, D_MODEL), BF16),
                        pltpu.VMEM((D_MODEL, SWA_KV + LANES), BF16),
                        pltpu.VMEM((D_MODEL, FOX_W), BF16),
                        pltpu.VMEM((D_MODEL, 2 * D_MODEL), BF16)],
        compiler_params=pltpu.CompilerParams(dimension_semantics=("arbitrary",), vmem_limit_bytes=IN_PROJ_VMEM_LIMIT),
        name="in_proj",
    )(x2, g_mix, w_all, w_all, *tail, *later_weights)


def _swa_bias_table():
    sb = SWA_BLOCK
    qi = np.arange(sb)[None, :] + sb
    si = np.arange(2 * sb)[:, None]
    chunk_diff = qi // CHUNK - si // CHUNK
    band_ok = (chunk_diff >= 0) & (chunk_diff <= WINDOW // CHUNK)
    slopes = np.array([2.0 ** (-8.0 * (h + 1) / SWA_HEADS) for h in range(SWA_HEADS)], dtype=np.float32)
    alibi = -slopes[:, None, None] * np.abs(qi - si).astype(np.float32)[None] * np.float32(LOG2E)
    first = band_ok & (si >= sb)
    table = np.stack([np.where(first[None], alibi, NEG_BIG), np.where(band_ok[None], alibi, NEG_BIG)])
    table = table.reshape(2, SWA_KV_HEADS, SWA_GROUP, 2 * sb, sb).transpose(0, 1, 3, 2, 4)
    return np.ascontiguousarray(table.reshape(2, SWA_KV_HEADS, 2 * sb, SWA_GROUP * sb)).astype(np.float32)


def _swa_kernel(sink_ref, qt_ref, kp_ref, kc_ref, vtp_ref, vtc_ref, bias_first_ref, bias_rest_ref, o_ref,
                st_sc, mb_sc, ot_sc):
    sb = SWA_BLOCK
    per_unit = MXU_COLS // sb
    half = jnp.zeros((HEAD_DIM, MXU_COLS), BF16)
    units = [(j, kh, c) for j in range(SWA_TILE // sb) for kh in range(SWA_KV_HEADS)
             for c in range(SWA_GROUP // per_unit)]

    def band(j):
        if j == 0:
            kb = jnp.concatenate([kp_ref[...], kc_ref[0:sb, :]], axis=0)
            vtb = jnp.concatenate([vtp_ref[...], vtc_ref[:, 0:sb]], axis=1)
            return kb, vtb, bias_first_ref
        return kc_ref[(j - 1) * sb:(j + 1) * sb, :], vtc_ref[:, (j - 1) * sb:(j + 1) * sb], bias_rest_ref

    def unit_heads(kh, c):
        return [kh * SWA_GROUP + c * per_unit + g for g in range(per_unit)]

    def scores(i, slot):
        j, kh, c = units[i]
        kb, _, bias_ref = band(j)
        q2 = jnp.concatenate([qt_ref[h * HEAD_DIM:(h + 1) * HEAD_DIM, j * sb:(j + 1) * sb] for h in unit_heads(kh, c)],
                             axis=1)
        q2t = jnp.concatenate([q2, half] if kh == 0 else [half, q2], axis=0)
        st = jnp.dot(kb, q2t, preferred_element_type=F32) + bias_ref[kh, :, c * MXU_COLS:(c + 1) * MXU_COLS]
        st_sc[slot] = st
        mb_sc[slot] = jnp.max(st, axis=0, keepdims=True)

    def consume(unit, slot):
        j, kh, c = unit
        _, vtb, _ = band(j)
        heads = unit_heads(kh, c)
        sink = jnp.concatenate([jnp.full((1, sb), sink_ref[0, h] * LOG2E, F32) for h in heads], axis=1)
        m = jnp.maximum(mb_sc[slot], sink)
        p = jnp.exp2(st_sc[slot] - m).astype(BF16)
        pv = jnp.dot(vtb[kh * VT_ROWS:(kh + 1) * VT_ROWS, :], p, preferred_element_type=F32)
        denom = pv[HEAD_DIM:HEAD_DIM + 1, :] + jnp.exp2(sink - m)
        o = pv[:HEAD_DIM, :] / denom
        for g, h in enumerate(heads):
            ot_sc[h * HEAD_DIM:(h + 1) * HEAD_DIM, j * sb:(j + 1) * sb] = o[:, g * sb:(g + 1) * sb]

    slots = st_sc.shape[0]
    for i in range(slots - 1):
        scores(i, i)
    for i, unit in enumerate(units):
        ahead = i + slots - 1
        if ahead < len(units):
            scores(ahead, ahead % slots)
        consume(unit, i % slots)
    o_ref[...] = ot_sc[...].T.astype(BF16)


def _swa(qt3, k3, vt3, sinks, bias):
    b, s, _ = k3.shape
    sb, ts = SWA_BLOCK, SWA_TILE
    per = ts // sb
    prev = lambda n: jnp.maximum(n * per - 1, 0)
    bias_block = (None,) + bias.shape[1:]
    return pl.pallas_call(
        _swa_kernel,
        grid=(b, s // ts),
        in_specs=[pl.BlockSpec(memory_space=pltpu.SMEM),
                  pl.BlockSpec((None, SWA_Q, ts), lambda bi, n: (bi, 0, n)),
                  pl.BlockSpec((None, sb, SWA_KV), lambda bi, n: (bi, prev(n), 0)),
                  pl.BlockSpec((None, ts, SWA_KV), lambda bi, n: (bi, n, 0)),
                  pl.BlockSpec((None, SWA_KV_HEADS * VT_ROWS, sb), lambda bi, n: (bi, 0, prev(n))),
                  pl.BlockSpec((None, SWA_KV_HEADS * VT_ROWS, ts), lambda bi, n: (bi, 0, n)),
                  pl.BlockSpec(bias_block, lambda bi, n: (jnp.minimum(n, 1), 0, 0, 0)),
                  pl.BlockSpec(bias_block, lambda bi, n: (1, 0, 0, 0))],
        out_specs=pl.BlockSpec((None, ts, SWA_Q), lambda bi, n: (bi, n, 0)),
        out_shape=jax.ShapeDtypeStruct((b, s, SWA_Q), BF16),
        scratch_shapes=[pltpu.VMEM((SWA_SLOTS, 2 * sb, MXU_COLS), F32),
                        pltpu.VMEM((SWA_SLOTS, 1, MXU_COLS), F32),
                        pltpu.VMEM((SWA_Q, ts), F32)],
        compiler_params=pltpu.CompilerParams(dimension_semantics=("parallel", "parallel"),
                                             vmem_limit_bytes=VMEM_LIMIT),
        name="swa",
    )(sinks, qt3, k3, k3, vt3, vt3, bias, bias)


def _fox_kernel(qt_ref, k_ref, vt_ref, o_ref, st_sc, mb_sc, m_sc, acc_sc):
    tq = qt_ref.shape[2]
    qi = pl.program_id(2)
    m_sc[...] = jnp.full_like(m_sc, NEG_BIG)
    acc_sc[...] = jnp.zeros_like(acc_sc)

    units = [(hh, slice(c * MXU_COLS, (c + 1) * MXU_COLS)) for hh in range(2) for c in range(tq // MXU_COLS)]

    def scores(block, slot, unit, diagonal):
        hh, cols = unit
        keys = cols.stop if diagonal else tq
        start = pl.multiple_of(block * tq, tq)
        k3 = k_ref[pl.ds(start, keys), hh * FOX_DEPTH:(hh + 1) * FOX_DEPTH]
        st = jnp.dot(k3, qt_ref[hh, :, cols], preferred_element_type=F32)
        if diagonal:
            key = lax.broadcasted_iota(jnp.int32, st.shape, 0)
            qry = lax.broadcasted_iota(jnp.int32, st.shape, 1) + cols.start
            st = jnp.where(key <= qry, st, NEG_BIG)
        st_sc[slot, hh, :keys, cols] = st
        mb_sc[slot, hh, :, cols] = jnp.max(st, axis=0, keepdims=True)

    def consume(block, slot, unit, diagonal=False):
        hh, cols = unit
        keys = cols.stop if diagonal else tq
        start = pl.multiple_of(block * tq, tq)
        m_prev = m_sc[hh, :, cols]
        m_new = jnp.maximum(m_prev, mb_sc[slot, hh, :, cols])
        p = jnp.exp2(st_sc[slot, hh, :keys, cols] - m_new).astype(BF16)
        alpha = jnp.exp2(m_prev - m_new)
        pv = jnp.dot(vt_ref[hh, :, pl.ds(start, keys)], p, preferred_element_type=F32)
        acc_sc[hh, :, cols] = alpha * acc_sc[hh, :, cols] + pv
        m_sc[hh, :, cols] = m_new

    def stage(next_block, next_slot, block, slot, diagonal=False):
        for unit in units:
            scores(next_block, next_slot, unit, False)
            consume(block, slot, unit, diagonal)

    for unit in units:
        scores(qi, 0, unit, True)
    stage(0, 1, qi, 0, diagonal=True)

    def block_pair(b0):
        stage(b0 + 1, 0, b0, 1)
        stage(jnp.minimum(b0 + 2, qi), 1, b0 + 1, 0)

    def block_oct(t, carry):
        for pair in range(4):
            block_pair(8 * t + 2 * pair)
        return carry

    lax.fori_loop(0, qi // 8, block_oct, 0)

    @pl.when(qi % 8 >= 4)
    def _():
        block_pair(qi // 8 * 8)
        block_pair(qi // 8 * 8 + 2)

    @pl.when(qi % 4 >= 2)
    def _():
        block_pair(qi // 4 * 4)

    @pl.when(qi % 2 == 1)
    def _():
        for unit in units:
            consume(qi - 1, 1, unit)

    outs = [acc_sc[hh, :HEAD_DIM, :] / acc_sc[hh, HEAD_DIM:HEAD_DIM + 1, :] for hh in range(2)]
    o_ref[...] = jnp.concatenate(outs, axis=0).T.astype(BF16)


def _fox(q3t, k3, vt4):
    b, s, _ = k3.shape
    tq = FOX_TQ
    return pl.pallas_call(
        _fox_kernel,
        grid=(b, FOX_PAIRS, s // tq),
        in_specs=[pl.BlockSpec((None, 2, FOX_DEPTH, tq), lambda bi, hp, qi: (bi, hp, 0, qi)),
                  pl.BlockSpec((None, s, 2 * FOX_DEPTH), lambda bi, hp, qi: (bi, 0, hp)),
                  pl.BlockSpec((None, 2, VT_ROWS, s), lambda bi, hp, qi: (bi, hp, 0, 0))],
        out_specs=pl.BlockSpec((None, tq, LANES), lambda bi, hp, qi: (bi, qi, hp)),
        out_shape=jax.ShapeDtypeStruct((b, s, FOX_W), BF16),
        scratch_shapes=[pltpu.VMEM((2, 2, tq, tq), F32),
                        pltpu.VMEM((2, 2, 1, tq), F32),
                        pltpu.VMEM((2, 1, tq), F32),
                        pltpu.VMEM((2, VT_ROWS, tq), F32)],
        compiler_params=pltpu.CompilerParams(dimension_semantics=("parallel", "parallel", "parallel"),
                                             vmem_limit_bytes=VMEM_LIMIT),
        name="fox",
    )(q3t, k3, vt4)


def _mix_kernel(x_ref, ya_ref, yb_ref, gates_ref, wa_ref, wb_ref, wo_ref, h_ref):
    slabs = [slice(r, r + TM) for r in range(0, MIX_TM, TM)]

    def branches(rows):
        y_a = jnp.dot(ya_ref[rows, :], wa_ref[...], preferred_element_type=F32)
        y_b = jnp.dot(yb_ref[rows, :], wb_ref[...], preferred_element_type=F32)
        return y_a, y_b

    pending = branches(slabs[0])
    for n, rows in enumerate(slabs):
        y_a, y_b = pending
        if n + 1 < len(slabs):
            pending = branches(slabs[n + 1])
        gates = gates_ref[rows, :].astype(F32)
        mixed = gates[:, :D_MODEL] * y_a + gates[:, D_MODEL:] * y_b
        h_ref[rows, :] = x_ref[rows, :] + jnp.dot(mixed.astype(BF16), wo_ref[...], preferred_element_type=F32)


def _mix(x2, att_a, att_b, gates, w_a, w_b, w_o):
    t = x2.shape[0]
    assert t % MIX_TM == 0 and MIX_TM % TM == 0
    row_spec = lambda w: pl.BlockSpec((MIX_TM, w), lambda i: (i, 0))
    return pl.pallas_call(
        _mix_kernel,
        grid=(t // MIX_TM,),
        in_specs=[row_spec(D_MODEL), row_spec(SWA_Q), row_spec(FOX_W), row_spec(2 * D_MODEL),
                  _const_spec(w_a.shape), _const_spec(w_b.shape), _const_spec(w_o.shape)],
        out_specs=row_spec(D_MODEL),
        out_shape=jax.ShapeDtypeStruct((t, D_MODEL), F32),
        compiler_params=pltpu.CompilerParams(dimension_semantics=("parallel",), vmem_limit_bytes=VMEM_LIMIT),
        name="mix",
    )(x2, att_a, att_b, gates, w_a, w_b, w_o)


def _mlp_ple_kernel(h_ref, p_ref, gm_ref, w1_ref, w2_ref, gp_ref, wpg_ref, wpp_ref, gf_ref, o_ref):
    h = h_ref[...]
    u = _rms(h, gm_ref[...]).astype(BF16)
    acc = jnp.zeros_like(h)
    for c in range(D_FF // FF_CHUNK):
        cols = slice(c * FF_CHUNK, (c + 1) * FF_CHUNK)
        a = jnp.dot(u, w1_ref[:, cols], preferred_element_type=F32)
        a = jnp.square(jnp.maximum(a, 0.0)).astype(BF16)
        acc = acc + jnp.dot(a, w2_ref[cols, :], preferred_element_type=F32)
    h = h + acc
    gate = jax.nn.sigmoid(jnp.dot(_rms(h, gp_ref[...]).astype(BF16), wpg_ref[...], preferred_element_type=F32))
    proj = jnp.dot(p_ref[...].astype(BF16), wpp_ref[...], preferred_element_type=F32)
    h = h + gate * proj
    o_ref[...] = _rms(h, gf_ref[...])


def _mlp_ple(h2, p2, g_mlp, w1, w2, g_ple, w_pg, w_pp, g_final):
    t = h2.shape[0]
    row_spec = lambda w: pl.BlockSpec((TM, w), lambda i: (i, 0))
    vec = _const_spec((1, D_MODEL))
    return pl.pallas_call(
        _mlp_ple_kernel,
        grid=(t // TM,),
        in_specs=[row_spec(D_MODEL), row_spec(PLE_DIM), vec, _const_spec(w1.shape), _const_spec(w2.shape),
                  vec, _const_spec(w_pg.shape), _const_spec(w_pp.shape), vec],
        out_specs=row_spec(D_MODEL),
        out_shape=jax.ShapeDtypeStruct((t, D_MODEL), F32),
        compiler_params=pltpu.CompilerParams(dimension_semantics=("parallel",), vmem_limit_bytes=VMEM_LIMIT),
        name="mlp_ple",
    )(h2, p2, g_mlp, w1, w2, g_ple, w_pg, w_pp, g_final)


def kernel(x, p, g_mix, w_in, b_forget, swa_sinks, w_br_swa, w_br_fox, w_mix_out,
           g_mlp, w_ff1, w_ff2, g_ple, w_ple_gate, w_ple_proj, g_final):
    b, s, d = x.shape
    assert d == D_MODEL and w_in.shape[0] == 1, "single-layer trunk with D_MODEL channels only"
    assert s % TM == 0 and s % FOX_TQ == 0 and s % SWA_TILE == 0 and SWA_TILE % SWA_BLOCK == 0
    t = b * s
    x2 = x.reshape(t, d)

    w = jnp.swapaxes(w_in, 1, 2)[0]
    bf_pad = jnp.pad(b_forget[0], (0, LANES - FOX_HEADS)).reshape(1, LANES)

    later = (w_br_swa[0], w_br_fox[0], w_mix_out[0], w_ff1[0], w_ff2[0], w_ple_gate[0])
    qat, ka, vta, q3t, k3, vtb, gates, *later_bf16 = _in_proj(x2, g_mix[0].reshape(1, d), w, bf_pad, b, s, later)
    wa_b, wb_b, wo_b, w1_b, w2_b, wpg_b = later_bf16

    att_a = _swa(qat, ka.reshape(b, s, SWA_KV), vta, swa_sinks[0].reshape(1, SWA_HEADS),
                 jnp.asarray(_swa_bias_table()))
    att_b = _fox(q3t, k3.reshape(b, s, FOX_HEADS * FOX_DEPTH), vtb.reshape(b, FOX_HEADS, VT_ROWS, s))

    h = _mix(x2, att_a.reshape(t, SWA_Q), att_b.reshape(t, FOX_W), gates, wa_b, wb_b, wo_b)
    out = _mlp_ple(h, p[0].reshape(t, PLE_DIM), g_mlp[0].reshape(1, d), w1_b, w2_b,
                   g_ple[0].reshape(1, d), wpg_b, w_ple_proj[0].astype(BF16), g_final.reshape(1, d))
    return out.reshape(b, s, d)
```

```python
import functools

import numpy as np
import jax
import jax.numpy as jnp
from jax import lax
from jax.experimental import pallas as pl
from jax.experimental.pallas import tpu as pltpu

D_MODEL = 1024
CHUNK = 64
PLE_DIM = 256
HEAD_DIM = 64
SWA_HEADS = 8
SWA_KV_HEADS = 2
SWA_GROUP = SWA_HEADS // SWA_KV_HEADS
WINDOW = 128
SWA_BLOCK = WINDOW
FOX_HEADS = 8
D_FF = 4 * D_MODEL
RMS_EPS = 1e-6
SWA_Q = SWA_HEADS * HEAD_DIM
SWA_KV = SWA_KV_HEADS * HEAD_DIM
FOX_W = FOX_HEADS * HEAD_DIM
COL_KA = SWA_Q
COL_VA = COL_KA + SWA_KV
COL_QB = COL_VA + SWA_KV
COL_KB = COL_QB + FOX_W
COL_VB = COL_KB + FOX_W
COL_F = COL_VB + FOX_W
COL_G = COL_F + FOX_HEADS
SCALE = HEAD_DIM ** -0.5
LOG2E = float(np.log2(np.e))

LANES = 128
BF16_SUBLANES = 16
MXU_COLS = 256
NEG_BIG = -1e30
VMEM_LIMIT = 52 * 1024 * 1024
IN_PROJ_VMEM_LIMIT = 58 * 1024 * 1024

TM = 512
MIX_TM = 1024
MLP_TM = 1024
SWA_TILE = 2048
SWA_SLOTS = 6
FOX_TQ = 512
FF_CHUNK = 1024

FOX_PAIRS = FOX_HEADS // 2
FOX_DEPTH = LANES
BIAS_SLOT = 8
C_PARTS = 3
VT_ROWS = 80

F32 = jnp.float32
BF16 = jnp.bfloat16
NT_DIMS = (((1,), (1,)), ((), ()))


def _rms(x, g):
    return x * lax.rsqrt(jnp.mean(x * x, axis=-1, keepdims=True) + RMS_EPS) * g


def _const_spec(shape):
    return pl.BlockSpec(shape, lambda *_: (0,) * len(shape), pipeline_mode=pl.Buffered(1))


def _split3(v):
    hi = v.astype(BF16)
    r1 = v - hi.astype(F32)
    mid = r1.astype(BF16)
    lo = (r1 - mid.astype(F32)).astype(BF16)
    return hi, mid, lo


def _bias_placement():
    place_qt = np.zeros((LANES, C_PARTS * LANES), np.float32)
    ones_qt = np.zeros((LANES, 1), np.float32)
    place_k = np.zeros((C_PARTS * LANES, LANES), np.float32)
    ones_k = np.zeros((1, LANES), np.float32)
    for h in range(FOX_HEADS):
        for part in range(C_PARTS):
            place_qt[h * BIAS_SLOT + part, part * LANES + h] = 1.0
            ones_qt[h * BIAS_SLOT + C_PARTS + part, 0] = 1.0
            ones_k[0, h * BIAS_SLOT + part] = 1.0
            place_k[part * LANES + h, h * BIAS_SLOT + C_PARTS + part] = -1.0
    return (place_qt, ones_qt), (place_k, ones_k)


def _in_proj_kernel(*refs, tiles_per_seq, n_cast):
    (x_ref, g_ref, wlo_ref, whi_ref, bf_ref, pqt_ref, oqt_ref, pk_ref, ok_ref, vone_ref) = refs[:10]
    cast_in = refs[10:10 + n_cast]
    qat_ref, ka_ref, vta_ref, q3t_ref, k3_ref, vtb_ref, gates_ref = refs[10 + n_cast:17 + n_cast]
    cast_out = refs[17 + n_cast:17 + 2 * n_cast]
    carry_ref, wt_ref, wkf_ref, wk_ref, wg_ref = refs[17 + 2 * n_cast:]
    i = pl.program_id(0)
    lane = lax.broadcasted_iota(jnp.int32, (1, LANES), 1)
    gate_shift = COL_G % LANES
    for src, dst in zip(cast_in, cast_out):
        dst[...] = src[...].astype(BF16)

    @pl.when(i == 0)
    def _():
        wk_ref[...] = wlo_ref[COL_KB:COL_KB + FOX_W].T.astype(BF16)
        wkf_ref[:, :SWA_KV] = wlo_ref[COL_KA:COL_KA + SWA_KV].T.astype(BF16)
        second = jnp.concatenate([whi_ref[:gate_shift], whi_ref[2 * D_MODEL:2 * D_MODEL + gate_shift],
                                  jnp.zeros((LANES - 2 * gate_shift, D_MODEL), F32)], axis=0)
        wkf_ref[:, SWA_KV:] = second.T.astype(BF16)
        wg_ref[...] = whi_ref[:2 * D_MODEL].T.astype(BF16)
        wt_ref[:SWA_Q] = wlo_ref[:SWA_Q].astype(BF16)
        wt_ref[SWA_Q:SWA_Q + FOX_W] = wlo_ref[COL_QB:COL_QB + FOX_W].astype(BF16)
        pad_rows = jnp.zeros((VT_ROWS - HEAD_DIM, D_MODEL), BF16)
        v_rows = [COL_VA + g * HEAD_DIM for g in range(SWA_KV_HEADS)] + [COL_VB + h * HEAD_DIM for h in range(FOX_HEADS)]
        for head, row in enumerate(v_rows):
            base = SWA_Q + FOX_W + head * VT_ROWS
            wt_ref[base:base + HEAD_DIM] = wlo_ref[row:row + HEAD_DIM].astype(BF16)
            wt_ref[base + HEAD_DIM:base + VT_ROWS] = pad_rows

    @pl.when(i % tiles_per_seq == 0)
    def _():
        carry_ref[...] = jnp.zeros_like(carry_ref)

    u = _rms(x_ref[...], g_ref[...]).astype(BF16)
    tm = u.shape[0]
    kf = jnp.dot(u, wkf_ref[...], preferred_element_type=F32)
    ka_ref[...] = kf[:, :SWA_KV].astype(BF16)
    f = kf[:, SWA_KV:] + bf_ref[...]
    logf = jnp.minimum(f, 0.0) - jnp.log1p(jnp.exp(-jnp.abs(f)))

    gl = jnp.dot(u, wg_ref[...], preferred_element_type=F32)
    n_tiles = gl.shape[1] // LANES
    for c in range(n_tiles):
        cur = pltpu.roll(gl[:, c * LANES:(c + 1) * LANES], LANES - gate_shift, 1)
        if c + 1 < n_tiles:
            nxt = pltpu.roll(gl[:, (c + 1) * LANES:(c + 2) * LANES], LANES - gate_shift, 1)
        else:
            nxt = pltpu.roll(kf[:, SWA_KV:], LANES - 2 * gate_shift, 1)
        tile = jnp.where(lane < LANES - gate_shift, cur, nxt)
        gates_ref[:, c * LANES:(c + 1) * LANES] = jax.nn.sigmoid(tile).astype(BF16)

    row = lax.broadcasted_iota(jnp.int32, (tm, tm), 0)
    col = lax.broadcasted_iota(jnp.int32, (tm, tm), 1)
    tri = (col <= row).astype(BF16)
    cs = jnp.dot(tri, jnp.concatenate(_split3(logf), axis=1), preferred_element_type=F32)
    c = cs[:, :LANES] + cs[:, LANES:2 * LANES] + cs[:, 2 * LANES:] + carry_ref[...]
    carry_ref[...] = c[tm - 1:tm, :]

    tt = lax.dot_general(wt_ref[...], u, NT_DIMS, preferred_element_type=F32)
    qat_ref[...] = (tt[:SWA_Q] * (SCALE * LOG2E)).astype(BF16)
    qt = (tt[SWA_Q:SWA_Q + FOX_W] * (SCALE * LOG2E)).astype(BF16)
    vt = (tt[SWA_Q + FOX_W:] + vone_ref[...]).astype(BF16)
    vta_ref[...] = vt[:SWA_KV_HEADS * VT_ROWS]
    vtb_ref[...] = vt[SWA_KV_HEADS * VT_ROWS:]

    parts = jnp.concatenate(_split3(c * LOG2E), axis=1)
    kbias = jnp.dot(parts, pk_ref[...], preferred_element_type=F32) + ok_ref[...]
    qbias_t = lax.dot_general(pqt_ref[...], parts, NT_DIMS, preferred_element_type=F32) + oqt_ref[...]
    zero_rows = jnp.zeros((HEAD_DIM - BIAS_SLOT, tm), F32)
    for h in range(FOX_HEADS):
        q3t_ref[h, :HEAD_DIM, :] = qt[h * HEAD_DIM:(h + 1) * HEAD_DIM]
        own = jnp.concatenate([qbias_t[h * BIAS_SLOT:(h + 1) * BIAS_SLOT], zero_rows], axis=0)
        q3t_ref[h, HEAD_DIM:, :] = own.astype(BF16)

    k = jnp.dot(u, wk_ref[...], preferred_element_type=F32)
    lane = lax.broadcasted_iota(jnp.int32, (1, LANES), 1)
    for h in range(FOX_HEADS):
        pair = k[:, (h // 2) * LANES:(h // 2 + 1) * LANES]
        dims = pair if h % 2 == 0 else pltpu.roll(pair, HEAD_DIM, 1)
        bias = pltpu.roll(kbias, HEAD_DIM - h * BIAS_SLOT, 1)
        k3_ref[:, h * FOX_DEPTH:(h + 1) * FOX_DEPTH] = jnp.where(lane < HEAD_DIM, dims, bias).astype(BF16)


def _in_proj(x2, g_mix, w_all, bf_pad, batch, seq, later_weights):
    t = x2.shape[0]
    tps = seq // TM
    steps = t // TM
    (place_qt, ones_qt), (place_k, ones_k) = _bias_placement()
    n_vt = SWA_KV_HEADS + FOX_HEADS
    vone = np.zeros((n_vt * VT_ROWS, 1), np.float32)
    vone[HEAD_DIM::VT_ROWS] = 1.0
    consts = (jnp.asarray(place_qt, BF16), jnp.asarray(ones_qt), jnp.asarray(place_k, BF16), jnp.asarray(ones_k),
              jnp.asarray(vone))
    kern = functools.partial(_in_proj_kernel, tiles_per_seq=tps, n_cast=len(later_weights))
    row_spec = lambda w: pl.BlockSpec((TM, w), lambda i: (i, 0))
    col_spec = lambda *lead: pl.BlockSpec((None,) + lead + (TM,), lambda i: (i // tps,) + (0,) * len(lead) + (i % tps,))
    for w in later_weights:
        assert w.ndim == 2 and w.shape[0] % (steps * BF16_SUBLANES) == 0, w.shape
    slab_specs = [pl.BlockSpec((w.shape[0] // steps, w.shape[1]), lambda i: (i, 0)) for w in later_weights]
    assert w_all.shape == (COL_G + 2 * D_MODEL, D_MODEL) and COL_F >= 2 * D_MODEL + LANES and COL_F % BF16_SUBLANES == 0
    half_specs = [pl.BlockSpec((COL_F, D_MODEL), lambda i, j=j: (j, 0), pipeline_mode=pl.Buffered(1)) for j in range(2)]
    tail = (bf_pad,) + consts
    ka, k3, gates = (jax.ShapeDtypeStruct((t, w), BF16) for w in (SWA_KV, FOX_HEADS * FOX_DEPTH, 2 * D_MODEL))
    qat = jax.ShapeDtypeStruct((batch, SWA_Q, seq), BF16)
    q3t = jax.ShapeDtypeStruct((batch, FOX_HEADS, FOX_DEPTH, seq), BF16)
    vta, vtb = (jax.ShapeDtypeStruct((batch, heads * VT_ROWS, seq), BF16) for heads in (SWA_KV_HEADS, FOX_HEADS))
    return pl.pallas_call(
        kern,
        grid=(steps,),
        in_specs=([row_spec(D_MODEL), _const_spec(g_mix.shape)] + half_specs
                  + [_const_spec(a.shape) for a in tail] + slab_specs),
        out_specs=[col_spec(SWA_Q), row_spec(SWA_KV), col_spec(SWA_KV_HEADS * VT_ROWS),
                   col_spec(FOX_HEADS, FOX_DEPTH), row_spec(FOX_HEADS * FOX_DEPTH), col_spec(FOX_HEADS * VT_ROWS),
                   row_spec(2 * D_MODEL)] + slab_specs,
        out_shape=[qat, ka, vta, q3t, k3, vtb, gates] + [jax.ShapeDtypeStruct(w.shape, BF16) for w in later_weights],
        scratch_shapes=[pltpu.VMEM((1, LANES), F32),
                        pltpu.VMEM((SWA_Q + FOX_W + n_vt * VT_ROWS, D_MODEL), BF16),
                        pltpu.VMEM((D_MODEL, SWA_KV + LANES), BF16),
                        pltpu.VMEM((D_MODEL, FOX_W), BF16),
                        pltpu.VMEM((D_MODEL, 2 * D_MODEL), BF16)],
        compiler_params=pltpu.CompilerParams(dimension_semantics=("arbitrary",), vmem_limit_bytes=IN_PROJ_VMEM_LIMIT),
        name="in_proj",
    )(x2, g_mix, w_all, w_all, *tail, *later_weights)


def _swa_bias_table():
    sb = SWA_BLOCK
    qi = np.arange(sb)[None, :] + sb
    si = np.arange(2 * sb)[:, None]
    chunk_diff = qi // CHUNK - si // CHUNK
    band_ok = (chunk_diff >= 0) & (chunk_diff <= WINDOW // CHUNK)
    slopes = np.array([2.0 ** (-8.0 * (h + 1) / SWA_HEADS) for h in range(SWA_HEADS)], dtype=np.float32)
    alibi = -slopes[:, None, None] * np.abs(qi - si).astype(np.float32)[None] * np.float32(LOG2E)
    first = band_ok & (si >= sb)
    table = np.stack([np.where(first[None], alibi, NEG_BIG), np.where(band_ok[None], alibi, NEG_BIG)])
    table = table.reshape(2, SWA_KV_HEADS, SWA_GROUP, 2 * sb, sb).transpose(0, 1, 3, 2, 4)
    return np.ascontiguousarray(table.reshape(2, SWA_KV_HEADS, 2 * sb, SWA_GROUP * sb)).astype(np.float32)


def _swa_kernel(sink_ref, qt_ref, kp_ref, kc_ref, vtp_ref, vtc_ref, bias_first_ref, bias_rest_ref, o_ref,
                st_sc, mb_sc, ot_sc):
    sb = SWA_BLOCK
    per_unit = MXU_COLS // sb
    half = jnp.zeros((HEAD_DIM, MXU_COLS), BF16)
    units = [(j, kh, c) for j in range(SWA_TILE // sb) for kh in range(SWA_KV_HEADS)
             for c in range(SWA_GROUP // per_unit)]

    def band(j):
        if j == 0:
            kb = jnp.concatenate([kp_ref[...], kc_ref[0:sb, :]], axis=0)
            vtb = jnp.concatenate([vtp_ref[...], vtc_ref[:, 0:sb]], axis=1)
            return kb, vtb, bias_first_ref
        return kc_ref[(j - 1) * sb:(j + 1) * sb, :], vtc_ref[:, (j - 1) * sb:(j + 1) * sb], bias_rest_ref

    def unit_heads(kh, c):
        return [kh * SWA_GROUP + c * per_unit + g for g in range(per_unit)]

    def scores(i, slot):
        j, kh, c = units[i]
        kb, _, bias_ref = band(j)
        q2 = jnp.concatenate([qt_ref[h * HEAD_DIM:(h + 1) * HEAD_DIM, j * sb:(j + 1) * sb] for h in unit_heads(kh, c)],
                             axis=1)
        q2t = jnp.concatenate([q2, half] if kh == 0 else [half, q2], axis=0)
        st = jnp.dot(kb, q2t, preferred_element_type=F32) + bias_ref[kh, :, c * MXU_COLS:(c + 1) * MXU_COLS]
        st_sc[slot] = st
        mb_sc[slot] = jnp.max(st, axis=0, keepdims=True)

    def consume(unit, slot):
        j, kh, c = unit
        _, vtb, _ = band(j)
        heads = unit_heads(kh, c)
        sink = jnp.concatenate([jnp.full((1, sb), sink_ref[0, h] * LOG2E, F32) for h in heads], axis=1)
        m = jnp.maximum(mb_sc[slot], sink)
        p = jnp.exp2(st_sc[slot] - m).astype(BF16)
        pv = jnp.dot(vtb[kh * VT_ROWS:(kh + 1) * VT_ROWS, :], p, preferred_element_type=F32)
        denom = pv[HEAD_DIM:HEAD_DIM + 1, :] + jnp.exp2(sink - m)
        o = pv[:HEAD_DIM, :] / denom
        for g, h in enumerate(heads):
            ot_sc[h * HEAD_DIM:(h + 1) * HEAD_DIM, j * sb:(j + 1) * sb] = o[:, g * sb:(g + 1) * sb]

    slots = st_sc.shape[0]
    for i in range(slots - 1):
        scores(i, i)
    for i, unit in enumerate(units):
        ahead = i + slots - 1
        if ahead < len(units):
            scores(ahead, ahead % slots)
        consume(unit, i % slots)
    o_ref[...] = ot_sc[...].T.astype(BF16)


def _swa(qt3, k3, vt3, sinks, bias):
    b, s, _ = k3.shape
    sb, ts = SWA_BLOCK, SWA_TILE
    per = ts // sb
    prev = lambda n: jnp.maximum(n * per - 1, 0)
    bias_block = (None,) + bias.shape[1:]
    return pl.pallas_call(
        _swa_kernel,
        grid=(b, s // ts),
        in_specs=[pl.BlockSpec(memory_space=pltpu.SMEM),
                  pl.BlockSpec((None, SWA_Q, ts), lambda bi, n: (bi, 0, n)),
                  pl.BlockSpec((None, sb, SWA_KV), lambda bi, n: (bi, prev(n), 0)),
                  pl.BlockSpec((None, ts, SWA_KV), lambda bi, n: (bi, n, 0)),
                  pl.BlockSpec((None, SWA_KV_HEADS * VT_ROWS, sb), lambda bi, n: (bi, 0, prev(n))),
                  pl.BlockSpec((None, SWA_KV_HEADS * VT_ROWS, ts), lambda bi, n: (bi, 0, n)),
                  pl.BlockSpec(bias_block, lambda bi, n: (jnp.minimum(n, 1), 0, 0, 0)),
                  pl.BlockSpec(bias_block, lambda bi, n: (1, 0, 0, 0))],
        out_specs=pl.BlockSpec((None, ts, SWA_Q), lambda bi, n: (bi, n, 0)),
        out_shape=jax.ShapeDtypeStruct((b, s, SWA_Q), BF16),
        scratch_shapes=[pltpu.VMEM((SWA_SLOTS, 2 * sb, MXU_COLS), F32),
                        pltpu.VMEM((SWA_SLOTS, 1, MXU_COLS), F32),
                        pltpu.VMEM((SWA_Q, ts), F32)],
        compiler_params=pltpu.CompilerParams(dimension_semantics=("parallel", "parallel"),
                                             vmem_limit_bytes=VMEM_LIMIT),
        name="swa",
    )(sinks, qt3, k3, k3, vt3, vt3, bias, bias)


def _fox_kernel(qt_ref, k_ref, vt_ref, o_ref, st_sc, mb_sc, m_sc, acc_sc):
    tq = qt_ref.shape[2]
    qi = pl.program_id(2)
    m_sc[...] = jnp.full_like(m_sc, NEG_BIG)
    acc_sc[...] = jnp.zeros_like(acc_sc)

    units = [(hh, slice(c * MXU_COLS, (c + 1) * MXU_COLS)) for hh in range(2) for c in range(tq // MXU_COLS)]

    def scores(block, slot, unit, diagonal):
        hh, cols = unit
        keys = cols.stop if diagonal else tq
        start = pl.multiple_of(block * tq, tq)
        k3 = k_ref[pl.ds(start, keys), hh * FOX_DEPTH:(hh + 1) * FOX_DEPTH]
        st = jnp.dot(k3, qt_ref[hh, :, cols], preferred_element_type=F32)
        if diagonal:
            key = lax.broadcasted_iota(jnp.int32, st.shape, 0)
            qry = lax.broadcasted_iota(jnp.int32, st.shape, 1) + cols.start
            st = jnp.where(key <= qry, st, NEG_BIG)
        st_sc[slot, hh, :keys, cols] = st
        mb_sc[slot, hh, :, cols] = jnp.max(st, axis=0, keepdims=True)

    def consume(block, slot, unit, diagonal=False):
        hh, cols = unit
        keys = cols.stop if diagonal else tq
        start = pl.multiple_of(block * tq, tq)
        m_prev = m_sc[hh, :, cols]
        m_new = jnp.maximum(m_prev, mb_sc[slot, hh, :, cols])
        p = jnp.exp2(st_sc[slot, hh, :keys, cols] - m_new).astype(BF16)
        alpha = jnp.exp2(m_prev - m_new)
        pv = jnp.dot(vt_ref[hh, :, pl.ds(start, keys)], p, preferred_element_type=F32)
        acc_sc[hh, :, cols] = alpha * acc_sc[hh, :, cols] + pv
        m_sc[hh, :, cols] = m_new

    def stage(next_block, next_slot, block, slot, diagonal=False):
        for unit in units:
            scores(next_block, next_slot, unit, False)
            consume(block, slot, unit, diagonal)

    for unit in units:
        scores(qi, 0, unit, True)
    stage(0, 1, qi, 0, diagonal=True)

    def block_pair(b0):
        stage(b0 + 1, 0, b0, 1)
        stage(jnp.minimum(b0 + 2, qi), 1, b0 + 1, 0)

    def block_oct(t, carry):
        for pair in range(4):
            block_pair(8 * t + 2 * pair)
        return carry

    lax.fori_loop(0, qi // 8, block_oct, 0)

    @pl.when(qi % 8 >= 4)
    def _():
        block_pair(qi // 8 * 8)
        block_pair(qi // 8 * 8 + 2)

    @pl.when(qi % 4 >= 2)
    def _():
        block_pair(qi // 4 * 4)

    @pl.when(qi % 2 == 1)
    def _():
        for unit in units:
            consume(qi - 1, 1, unit)

    outs = [acc_sc[hh, :HEAD_DIM, :] / acc_sc[hh, HEAD_DIM:HEAD_DIM + 1, :] for hh in range(2)]
    o_ref[...] = jnp.concatenate(outs, axis=0).T.astype(BF16)


def _fox(q3t, k3, vt4):
    b, s, _ = k3.shape
    tq = FOX_TQ
    return pl.pallas_call(
        _fox_kernel,
        grid=(b, FOX_PAIRS, s // tq),
        in_specs=[pl.BlockSpec((None, 2, FOX_DEPTH, tq), lambda bi, hp, qi: (bi, hp, 0, qi)),
                  pl.BlockSpec((None, s, 2 * FOX_DEPTH), lambda bi, hp, qi: (bi, 0, hp)),
                  pl.BlockSpec((None, 2, VT_ROWS, s), lambda bi, hp, qi: (bi, hp, 0, 0))],
        out_specs=pl.BlockSpec((None, tq, LANES), lambda bi, hp, qi: (bi, qi, hp)),
        out_shape=jax.ShapeDtypeStruct((b, s, FOX_W), BF16),
        scratch_shapes=[pltpu.VMEM((2, 2, tq, tq), F32),
                        pltpu.VMEM((2, 2, 1, tq), F32),
                        pltpu.VMEM((2, 1, tq), F32),
                        pltpu.VMEM((2, VT_ROWS, tq), F32)],
        compiler_params=pltpu.CompilerParams(dimension_semantics=("parallel", "parallel", "parallel"),
                                             vmem_limit_bytes=VMEM_LIMIT),
        name="fox",
    )(q3t, k3, vt4)


def _mix_kernel(x_ref, ya_ref, yb_ref, gates_ref, wa_ref, wb_ref, wo_ref, h_ref):
    slabs = [slice(r, r + TM) for r in range(0, MIX_TM, TM)]

    def branches(rows):
        y_a = jnp.dot(ya_ref[rows, :], wa_ref[...], preferred_element_type=F32)
        y_b = jnp.dot(yb_ref[rows, :], wb_ref[...], preferred_element_type=F32)
        return y_a, y_b

    pending = branches(slabs[0])
    for n, rows in enumerate(slabs):
        y_a, y_b = pending
        if n + 1 < len(slabs):
            pending = branches(slabs[n + 1])
        gates = gates_ref[rows, :].astype(F32)
        mixed = gates[:, :D_MODEL] * y_a + gates[:, D_MODEL:] * y_b
        h_ref[rows, :] = x_ref[rows, :] + jnp.dot(mixed.astype(BF16), wo_ref[...], preferred_element_type=F32)


def _mix(x2, att_a, att_b, gates, w_a, w_b, w_o):
    t = x2.shape[0]
    assert t % MIX_TM == 0 and MIX_TM % TM == 0
    row_spec = lambda w: pl.BlockSpec((MIX_TM, w), lambda i: (i, 0))
    return pl.pallas_call(
        _mix_kernel,
        grid=(t // MIX_TM,),
        in_specs=[row_spec(D_MODEL), row_spec(SWA_Q), row_spec(FOX_W), row_spec(2 * D_MODEL),
                  _const_spec(w_a.shape), _const_spec(w_b.shape), _const_spec(w_o.shape)],
        out_specs=row_spec(D_MODEL),
        out_shape=jax.ShapeDtypeStruct((t, D_MODEL), F32),
        compiler_params=pltpu.CompilerParams(dimension_semantics=("parallel",), vmem_limit_bytes=VMEM_LIMIT),
        name="mix",
    )(x2, att_a, att_b, gates, w_a, w_b, w_o)


def _mlp_ple_kernel(h_ref, p_ref, gm_ref, w1_ref, w2_ref, gp_ref, wpg_ref, wpp_ref, gf_ref, o_ref):
    slabs = [slice(r, r + TM) for r in range(0, MLP_TM, TM)]

    def normed(rows):
        return _rms(h_ref[rows, :], gm_ref[...]).astype(BF16)

    def mlp_chunk(u, c):
        cols = slice(c * FF_CHUNK, (c + 1) * FF_CHUNK)
        a = jnp.dot(u, w1_ref[:, cols], preferred_element_type=F32)
        a = jnp.square(jnp.maximum(a, 0.0)).astype(BF16)
        return jnp.dot(a, w2_ref[cols, :], preferred_element_type=F32)

    def embed(rows, h):
        gate = jax.nn.sigmoid(jnp.dot(_rms(h, gp_ref[...]).astype(BF16), wpg_ref[...], preferred_element_type=F32))
        proj = jnp.dot(p_ref[rows, :].astype(BF16), wpp_ref[...], preferred_element_type=F32)
        h = h + gate * proj
        o_ref[rows, :] = _rms(h, gf_ref[...])

    n_chunks = D_FF // FF_CHUNK
    finished = None
    u = normed(slabs[0])
    for n, rows in enumerate(slabs):
        acc = mlp_chunk(u, 0)
        if finished is not None:
            embed(*finished)
        for c in range(1, n_chunks):
            acc = acc + mlp_chunk(u, c)
        finished = (rows, h_ref[rows, :] + acc)
        if n + 1 < len(slabs):
            u = normed(slabs[n + 1])
    embed(*finished)


def _mlp_ple(h2, p2, g_mlp, w1, w2, g_ple, w_pg, w_pp, g_final):
    t = h2.shape[0]
    assert t % MLP_TM == 0 and MLP_TM % TM == 0
    row_spec = lambda w: pl.BlockSpec((MLP_TM, w), lambda i: (i, 0))
    vec = _const_spec((1, D_MODEL))
    return pl.pallas_call(
        _mlp_ple_kernel,
        grid=(t // MLP_TM,),
        in_specs=[row_spec(D_MODEL), row_spec(PLE_DIM), vec, _const_spec(w1.shape), _const_spec(w2.shape),
                  vec, _const_spec(w_pg.shape), _const_spec(w_pp.shape), vec],
        out_specs=row_spec(D_MODEL),
        out_shape=jax.ShapeDtypeStruct((t, D_MODEL), F32),
        compiler_params=pltpu.CompilerParams(dimension_semantics=("parallel",), vmem_limit_bytes=VMEM_LIMIT),
        name="mlp_ple",
    )(h2, p2, g_mlp, w1, w2, g_ple, w_pg, w_pp, g_final)


def kernel(x, p, g_mix, w_in, b_forget, swa_sinks, w_br_swa, w_br_fox, w_mix_out,
           g_mlp, w_ff1, w_ff2, g_ple, w_ple_gate, w_ple_proj, g_final):
    b, s, d = x.shape
    assert d == D_MODEL and w_in.shape[0] == 1, "single-layer trunk with D_MODEL channels only"
    assert s % TM == 0 and s % FOX_TQ == 0 and s % SWA_TILE == 0 and SWA_TILE % SWA_BLOCK == 0
    t = b * s
    x2 = x.reshape(t, d)

    w = jnp.swapaxes(w_in, 1, 2)[0]
    bf_pad = jnp.pad(b_forget[0], (0, LANES - FOX_HEADS)).reshape(1, LANES)

    later = (w_br_swa[0], w_br_fox[0], w_mix_out[0], w_ff1[0], w_ff2[0], w_ple_gate[0])
    qat, ka, vta, q3t, k3, vtb, gates, *later_bf16 = _in_proj(x2, g_mix[0].reshape(1, d), w, bf_pad, b, s, later)
    wa_b, wb_b, wo_b, w1_b, w2_b, wpg_b = later_bf16

    att_a = _swa(qat, ka.reshape(b, s, SWA_KV), vta, swa_sinks[0].reshape(1, SWA_HEADS),
                 jnp.asarray(_swa_bias_table()))
    att_b = _fox(q3t, k3.reshape(b, s, FOX_HEADS * FOX_DEPTH), vtb.reshape(b, FOX_HEADS, VT_ROWS, s))

    h = _mix(x2, att_a.reshape(t, SWA_Q), att_b.reshape(t, FOX_W), gates, wa_b, wb_b, wo_b)
    out = _mlp_ple(h, p[0].reshape(t, PLE_DIM), g_mlp[0].reshape(1, d), w1_b, w2_b,
                   g_ple[0].reshape(1, d), wpg_b, w_ple_proj[0].astype(BF16), g_final.reshape(1, d))
    return out.reshape(b, s, d)
```

```python
import functools

import numpy as np
import jax
import jax.numpy as jnp
from jax import lax
from jax.experimental import pallas as pl
from jax.experimental.pallas import tpu as pltpu

D_MODEL = 1024
CHUNK = 64
PLE_DIM = 256
HEAD_DIM = 64
SWA_HEADS = 8
SWA_KV_HEADS = 2
SWA_GROUP = SWA_HEADS // SWA_KV_HEADS
WINDOW = 128
SWA_BLOCK = WINDOW
FOX_HEADS = 8
D_FF = 4 * D_MODEL
RMS_EPS = 1e-6
SWA_Q = SWA_HEADS * HEAD_DIM
SWA_KV = SWA_KV_HEADS * HEAD_DIM
FOX_W = FOX_HEADS * HEAD_DIM
COL_KA = SWA_Q
COL_VA = COL_KA + SWA_KV
COL_QB = COL_VA + SWA_KV
COL_KB = COL_QB + FOX_W
COL_VB = COL_KB + FOX_W
COL_F = COL_VB + FOX_W
COL_G = COL_F + FOX_HEADS
SCALE = HEAD_DIM ** -0.5
LOG2E = float(np.log2(np.e))

LANES = 128
BF16_SUBLANES = 16
MXU_COLS = 256
NEG_BIG = -1e30
VMEM_LIMIT = 52 * 1024 * 1024
IN_PROJ_VMEM_LIMIT = 58 * 1024 * 1024

TM = 512
MIX_TM = 1024
SWA_TILE = 2048
SWA_SLOTS = 6
FOX_TQ = 512
FF_CHUNK = 1024

FOX_PAIRS = FOX_HEADS // 2
FOX_DEPTH = LANES
BIAS_SLOT = 8
C_PARTS = 3
VT_ROWS = 80

F32 = jnp.float32
BF16 = jnp.bfloat16
NT_DIMS = (((1,), (1,)), ((), ()))


def _rms(x, g):
    return x * lax.rsqrt(jnp.mean(x * x, axis=-1, keepdims=True) + RMS_EPS) * g


def _const_spec(shape):
    return pl.BlockSpec(shape, lambda *_: (0,) * len(shape), pipeline_mode=pl.Buffered(1))


def _split3(v):
    hi = v.astype(BF16)
    r1 = v - hi.astype(F32)
    mid = r1.astype(BF16)
    lo = (r1 - mid.astype(F32)).astype(BF16)
    return hi, mid, lo


def _bias_placement():
    place_qt = np.zeros((LANES, C_PARTS * LANES), np.float32)
    ones_qt = np.zeros((LANES, 1), np.float32)
    place_k = np.zeros((C_PARTS * LANES, LANES), np.float32)
    ones_k = np.zeros((1, LANES), np.float32)
    for h in range(FOX_HEADS):
        for part in range(C_PARTS):
            place_qt[h * BIAS_SLOT + part, part * LANES + h] = 1.0
            ones_qt[h * BIAS_SLOT + C_PARTS + part, 0] = 1.0
            ones_k[0, h * BIAS_SLOT + part] = 1.0
            place_k[part * LANES + h, h * BIAS_SLOT + C_PARTS + part] = -1.0
    return (place_qt, ones_qt), (place_k, ones_k)


def _in_proj_kernel(*refs, tiles_per_seq, n_cast):
    (x_ref, g_ref, wlo_ref, whi_ref, bf_ref, pqt_ref, oqt_ref, pk_ref, ok_ref, vone_ref) = refs[:10]
    cast_in = refs[10:10 + n_cast]
    qat_ref, ka_ref, vta_ref, q3t_ref, k3_ref, vtb_ref, gates_ref = refs[10 + n_cast:17 + n_cast]
    cast_out = refs[17 + n_cast:17 + 2 * n_cast]
    carry_ref, wt_ref, wkf_ref, wk_ref, wg_ref = refs[17 + 2 * n_cast:]
    i = pl.program_id(0)
    lane = lax.broadcasted_iota(jnp.int32, (1, LANES), 1)
    gate_shift = COL_G % LANES
    for src, dst in zip(cast_in, cast_out):
        dst[...] = src[...].astype(BF16)

    @pl.when(i == 0)
    def _():
        wk_ref[...] = wlo_ref[COL_KB:COL_KB + FOX_W].T.astype(BF16)
        wkf_ref[:, :SWA_KV] = wlo_ref[COL_KA:COL_KA + SWA_KV].T.astype(BF16)
        second = jnp.concatenate([whi_ref[:gate_shift], whi_ref[2 * D_MODEL:2 * D_MODEL + gate_shift],
                                  jnp.zeros((LANES - 2 * gate_shift, D_MODEL), F32)], axis=0)
        wkf_ref[:, SWA_KV:] = second.T.astype(BF16)
        wg_ref[...] = whi_ref[:2 * D_MODEL].T.astype(BF16)
        wt_ref[:SWA_Q] = wlo_ref[:SWA_Q].astype(BF16)
        wt_ref[SWA_Q:SWA_Q + FOX_W] = wlo_ref[COL_QB:COL_QB + FOX_W].astype(BF16)
        pad_rows = jnp.zeros((VT_ROWS - HEAD_DIM, D_MODEL), BF16)
        v_rows = [COL_VA + g * HEAD_DIM for g in range(SWA_KV_HEADS)] + [COL_VB + h * HEAD_DIM for h in range(FOX_HEADS)]
        for head, row in enumerate(v_rows):
            base = SWA_Q + FOX_W + head * VT_ROWS
            wt_ref[base:base + HEAD_DIM] = wlo_ref[row:row + HEAD_DIM].astype(BF16)
            wt_ref[base + HEAD_DIM:base + VT_ROWS] = pad_rows

    @pl.when(i % tiles_per_seq == 0)
    def _():
        carry_ref[...] = jnp.zeros_like(carry_ref)

    u = _rms(x_ref[...], g_ref[...]).astype(BF16)
    tm = u.shape[0]
    kf = jnp.dot(u, wkf_ref[...], preferred_element_type=F32)
    ka_ref[...] = kf[:, :SWA_KV].astype(BF16)
    f = kf[:, SWA_KV:] + bf_ref[...]
    logf = jnp.minimum(f, 0.0) - jnp.log1p(jnp.exp(-jnp.abs(f)))

    gl = jnp.dot(u, wg_ref[...], preferred_element_type=F32)
    n_tiles = gl.shape[1] // LANES
    for c in range(n_tiles):
        cur = pltpu.roll(gl[:, c * LANES:(c + 1) * LANES], LANES - gate_shift, 1)
        if c + 1 < n_tiles:
            nxt = pltpu.roll(gl[:, (c + 1) * LANES:(c + 2) * LANES], LANES - gate_shift, 1)
        else:
            nxt = pltpu.roll(kf[:, SWA_KV:], LANES - 2 * gate_shift, 1)
        tile = jnp.where(lane < LANES - gate_shift, cur, nxt)
        gates_ref[:, c * LANES:(c + 1) * LANES] = jax.nn.sigmoid(tile).astype(BF16)

    row = lax.broadcasted_iota(jnp.int32, (tm, tm), 0)
    col = lax.broadcasted_iota(jnp.int32, (tm, tm), 1)
    tri = (col <= row).astype(BF16)
    cs = jnp.dot(tri, jnp.concatenate(_split3(logf), axis=1), preferred_element_type=F32)
    c = cs[:, :LANES] + cs[:, LANES:2 * LANES] + cs[:, 2 * LANES:] + carry_ref[...]
    carry_ref[...] = c[tm - 1:tm, :]

    tt = lax.dot_general(wt_ref[...], u, NT_DIMS, preferred_element_type=F32)
    qat_ref[...] = (tt[:SWA_Q] * (SCALE * LOG2E)).astype(BF16)
    qt = (tt[SWA_Q:SWA_Q + FOX_W] * (SCALE * LOG2E)).astype(BF16)
    vt = (tt[SWA_Q + FOX_W:] + vone_ref[...]).astype(BF16)
    vta_ref[...] = vt[:SWA_KV_HEADS * VT_ROWS]
    vtb_ref[...] = vt[SWA_KV_HEADS * VT_ROWS:]

    parts = jnp.concatenate(_split3(c * LOG2E), axis=1)
    kbias = jnp.dot(parts, pk_ref[...], preferred_element_type=F32) + ok_ref[...]
    qbias_t = lax.dot_general(pqt_ref[...], parts, NT_DIMS, preferred_element_type=F32) + oqt_ref[...]
    zero_rows = jnp.zeros((HEAD_DIM - BIAS_SLOT, tm), F32)
    for h in range(FOX_HEADS):
        q3t_ref[h, :HEAD_DIM, :] = qt[h * HEAD_DIM:(h + 1) * HEAD_DIM]
        own = jnp.concatenate([qbias_t[h * BIAS_SLOT:(h + 1) * BIAS_SLOT], zero_rows], axis=0)
        q3t_ref[h, HEAD_DIM:, :] = own.astype(BF16)

    k = jnp.dot(u, wk_ref[...], preferred_element_type=F32)
    lane = lax.broadcasted_iota(jnp.int32, (1, LANES), 1)
    for h in range(FOX_HEADS):
        pair = k[:, (h // 2) * LANES:(h // 2 + 1) * LANES]
        dims = pair if h % 2 == 0 else pltpu.roll(pair, HEAD_DIM, 1)
        bias = pltpu.roll(kbias, HEAD_DIM - h * BIAS_SLOT, 1)
        k3_ref[:, h * FOX_DEPTH:(h + 1) * FOX_DEPTH] = jnp.where(lane < HEAD_DIM, dims, bias).astype(BF16)


def _in_proj(x2, g_mix, w_all, bf_pad, batch, seq, later_weights):
    t = x2.shape[0]
    tps = seq // TM
    steps = t // TM
    (place_qt, ones_qt), (place_k, ones_k) = _bias_placement()
    n_vt = SWA_KV_HEADS + FOX_HEADS
    vone = np.zeros((n_vt * VT_ROWS, 1), np.float32)
    vone[HEAD_DIM::VT_ROWS] = 1.0
    consts = (jnp.asarray(place_qt, BF16), jnp.asarray(ones_qt), jnp.asarray(place_k, BF16), jnp.asarray(ones_k),
              jnp.asarray(vone))
    kern = functools.partial(_in_proj_kernel, tiles_per_seq=tps, n_cast=len(later_weights))
    row_spec = lambda w: pl.BlockSpec((TM, w), lambda i: (i, 0))
    col_spec = lambda *lead: pl.BlockSpec((None,) + lead + (TM,), lambda i: (i // tps,) + (0,) * len(lead) + (i % tps,))
    for w in later_weights:
        assert w.ndim == 2 and w.shape[0] % (steps * BF16_SUBLANES) == 0, w.shape
    slab_specs = [pl.BlockSpec((w.shape[0] // steps, w.shape[1]), lambda i: (i, 0)) for w in later_weights]
    assert w_all.shape == (COL_G + 2 * D_MODEL, D_MODEL) and COL_F >= 2 * D_MODEL + LANES and COL_F % BF16_SUBLANES == 0
    half_specs = [pl.BlockSpec((COL_F, D_MODEL), lambda i, j=j: (j, 0), pipeline_mode=pl.Buffered(1)) for j in range(2)]
    tail = (bf_pad,) + consts
    ka, k3, gates = (jax.ShapeDtypeStruct((t, w), BF16) for w in (SWA_KV, FOX_HEADS * FOX_DEPTH, 2 * D_MODEL))
    qat = jax.ShapeDtypeStruct((batch, SWA_Q, seq), BF16)
    q3t = jax.ShapeDtypeStruct((batch, FOX_HEADS, FOX_DEPTH, seq), BF16)
    vta, vtb = (jax.ShapeDtypeStruct((batch, heads * VT_ROWS, seq), BF16) for heads in (SWA_KV_HEADS, FOX_HEADS))
    return pl.pallas_call(
        kern,
        grid=(steps,),
        in_specs=([row_spec(D_MODEL), _const_spec(g_mix.shape)] + half_specs
                  + [_const_spec(a.shape) for a in tail] + slab_specs),
        out_specs=[col_spec(SWA_Q), row_spec(SWA_KV), col_spec(SWA_KV_HEADS * VT_ROWS),
                   col_spec(FOX_HEADS, FOX_DEPTH), row_spec(FOX_HEADS * FOX_DEPTH), col_spec(FOX_HEADS * VT_ROWS),
                   row_spec(2 * D_MODEL)] + slab_specs,
        out_shape=[qat, ka, vta, q3t, k3, vtb, gates] + [jax.ShapeDtypeStruct(w.shape, BF16) for w in later_weights],
        scratch_shapes=[pltpu.VMEM((1, LANES), F32),
                        pltpu.VMEM((SWA_Q + FOX_W + n_vt * VT_ROWS, D_MODEL), BF16),
                        pltpu.VMEM((D_MODEL, SWA_KV + LANES), BF16),
                        pltpu.VMEM((D_MODEL, FOX_W), BF16),
                        pltpu.VMEM((D_MODEL, 2 * D_MODEL), BF16)],
        compiler_params=pltpu.CompilerParams(dimension_semantics=("arbitrary",), vmem_limit_bytes=IN_PROJ_VMEM_LIMIT),
        name="in_proj",
    )(x2, g_mix, w_all, w_all, *tail, *later_weights)


def _swa_bias_table():
    sb = SWA_BLOCK
    qi = np.arange(sb)[None, :] + sb
    si = np.arange(2 * sb)[:, None]
    chunk_diff = qi // CHUNK - si // CHUNK
    band_ok = (chunk_diff >= 0) & (chunk_diff <= WINDOW // CHUNK)
    slopes = np.array([2.0 ** (-8.0 * (h + 1) / SWA_HEADS) for h in range(SWA_HEADS)], dtype=np.float32)
    alibi = -slopes[:, None, None] * np.abs(qi - si).astype(np.float32)[None] * np.float32(LOG2E)
    first = band_ok & (si >= sb)
    table = np.stack([np.where(first[None], alibi, NEG_BIG), np.where(band_ok[None], alibi, NEG_BIG)])
    table = table.reshape(2, SWA_KV_HEADS, SWA_GROUP, 2 * sb, sb).transpose(0, 1, 3, 2, 4)
    return np.ascontiguousarray(table.reshape(2, SWA_KV_HEADS, 2 * sb, SWA_GROUP * sb)).astype(np.float32)


def _swa_kernel(sink_ref, qt_ref, kp_ref, kc_ref, vtp_ref, vtc_ref, bias_first_ref, bias_rest_ref, o_ref,
                st_sc, mb_sc, ot_sc):
    sb = SWA_BLOCK
    per_unit = MXU_COLS // sb
    half = jnp.zeros((HEAD_DIM, MXU_COLS), BF16)
    units = [(j, kh, c) for j in range(SWA_TILE // sb) for kh in range(SWA_KV_HEADS)
             for c in range(SWA_GROUP // per_unit)]

    def band(j):
        if j == 0:
            kb = jnp.concatenate([kp_ref[...], kc_ref[0:sb, :]], axis=0)
            vtb = jnp.concatenate([vtp_ref[...], vtc_ref[:, 0:sb]], axis=1)
            return kb, vtb, bias_first_ref
        return kc_ref[(j - 1) * sb:(j + 1) * sb, :], vtc_ref[:, (j - 1) * sb:(j + 1) * sb], bias_rest_ref

    def unit_heads(kh, c):
        return [kh * SWA_GROUP + c * per_unit + g for g in range(per_unit)]

    def scores(i, slot):
        j, kh, c = units[i]
        kb, _, bias_ref = band(j)
        q2 = jnp.concatenate([qt_ref[h * HEAD_DIM:(h + 1) * HEAD_DIM, j * sb:(j + 1) * sb] for h in unit_heads(kh, c)],
                             axis=1)
        q2t = jnp.concatenate([q2, half] if kh == 0 else [half, q2], axis=0)
        st = jnp.dot(kb, q2t, preferred_element_type=F32) + bias_ref[kh, :, c * MXU_COLS:(c + 1) * MXU_COLS]
        st_sc[slot] = st
        mb_sc[slot] = jnp.max(st, axis=0, keepdims=True)

    def consume(unit, slot):
        j, kh, c = unit
        _, vtb, _ = band(j)
        heads = unit_heads(kh, c)
        sink = jnp.concatenate([jnp.full((1, sb), sink_ref[0, h] * LOG2E, F32) for h in heads], axis=1)
        m = jnp.maximum(mb_sc[slot], sink)
        p = jnp.exp2(st_sc[slot] - m).astype(BF16)
        pv = jnp.dot(vtb[kh * VT_ROWS:(kh + 1) * VT_ROWS, :], p, preferred_element_type=F32)
        denom = pv[HEAD_DIM:HEAD_DIM + 1, :] + jnp.exp2(sink - m)
        o = pv[:HEAD_DIM, :] / denom
        for g, h in enumerate(heads):
            ot_sc[h * HEAD_DIM:(h + 1) * HEAD_DIM, j * sb:(j + 1) * sb] = o[:, g * sb:(g + 1) * sb]

    slots = st_sc.shape[0]
    for i in range(slots - 1):
        scores(i, i)
    for i, unit in enumerate(units):
        ahead = i + slots - 1
        if ahead < len(units):
            scores(ahead, ahead % slots)
        consume(unit, i % slots)
    o_ref[...] = ot_sc[...].T.astype(BF16)


def _swa(qt3, k3, vt3, sinks, bias):
    b, s, _ = k3.shape
    sb, ts = SWA_BLOCK, SWA_TILE
    per = ts // sb
    prev = lambda n: jnp.maximum(n * per - 1, 0)
    bias_block = (None,) + bias.shape[1:]
    return pl.pallas_call(
        _swa_kernel,
        grid=(b, s // ts),
        in_specs=[pl.BlockSpec(memory_space=pltpu.SMEM),
                  pl.BlockSpec((None, SWA_Q, ts), lambda bi, n: (bi, 0, n)),
                  pl.BlockSpec((None, sb, SWA_KV), lambda bi, n: (bi, prev(n), 0)),
                  pl.BlockSpec((None, ts, SWA_KV), lambda bi, n: (bi, n, 0)),
                  pl.BlockSpec((None, SWA_KV_HEADS * VT_ROWS, sb), lambda bi, n: (bi, 0, prev(n))),
                  pl.BlockSpec((None, SWA_KV_HEADS * VT_ROWS, ts), lambda bi, n: (bi, 0, n)),
                  pl.BlockSpec(bias_block, lambda bi, n: (jnp.minimum(n, 1), 0, 0, 0)),
                  pl.BlockSpec(bias_block, lambda bi, n: (1, 0, 0, 0))],
        out_specs=pl.BlockSpec((None, ts, SWA_Q), lambda bi, n: (bi, n, 0)),
        out_shape=jax.ShapeDtypeStruct((b, s, SWA_Q), BF16),
        scratch_shapes=[pltpu.VMEM((SWA_SLOTS, 2 * sb, MXU_COLS), F32),
                        pltpu.VMEM((SWA_SLOTS, 1, MXU_COLS), F32),
                        pltpu.VMEM((SWA_Q, ts), F32)],
        compiler_params=pltpu.CompilerParams(dimension_semantics=("parallel", "parallel"),
                                             vmem_limit_bytes=VMEM_LIMIT),
        name="swa",
    )(sinks, qt3, k3, k3, vt3, vt3, bias, bias)


def _fox_kernel(qt_ref, k_ref, vt_ref, o_ref, st_sc, mb_sc, m_sc, acc_sc):
    tq = qt_ref.shape[2]
    qi = pl.program_id(2)
    m_sc[...] = jnp.full_like(m_sc, NEG_BIG)
    acc_sc[...] = jnp.zeros_like(acc_sc)

    units = [(hh, slice(c * MXU_COLS, (c + 1) * MXU_COLS)) for hh in range(2) for c in range(tq // MXU_COLS)]

    def scores(block, slot, unit, diagonal):
        hh, cols = unit
        keys = cols.stop if diagonal else tq
        start = pl.multiple_of(block * tq, tq)
        k3 = k_ref[pl.ds(start, keys), hh * FOX_DEPTH:(hh + 1) * FOX_DEPTH]
        st = jnp.dot(k3, qt_ref[hh, :, cols], preferred_element_type=F32)
        if diagonal:
            key = lax.broadcasted_iota(jnp.int32, st.shape, 0)
            qry = lax.broadcasted_iota(jnp.int32, st.shape, 1) + cols.start
            st = jnp.where(key <= qry, st, NEG_BIG)
        st_sc[slot, hh, :keys, cols] = st
        mb_sc[slot, hh, :, cols] = jnp.max(st, axis=0, keepdims=True)

    def consume(block, slot, unit, diagonal=False):
        hh, cols = unit
        keys = cols.stop if diagonal else tq
        start = pl.multiple_of(block * tq, tq)
        m_prev = m_sc[hh, :, cols]
        m_new = jnp.maximum(m_prev, mb_sc[slot, hh, :, cols])
        p = jnp.exp2(st_sc[slot, hh, :keys, cols] - m_new).astype(BF16)
        alpha = jnp.exp2(m_prev - m_new)
        pv = jnp.dot(vt_ref[hh, :, pl.ds(start, keys)], p, preferred_element_type=F32)
        acc_sc[hh, :, cols] = alpha * acc_sc[hh, :, cols] + pv
        m_sc[hh, :, cols] = m_new

    def stage(next_block, next_slot, block, slot, diagonal=False):
        for unit in units:
            scores(next_block, next_slot, unit, False)
            consume(block, slot, unit, diagonal)

    for unit in units:
        scores(qi, 0, unit, True)
    stage(0, 1, qi, 0, diagonal=True)

    def block_pair(b0):
        stage(b0 + 1, 0, b0, 1)
        stage(jnp.minimum(b0 + 2, qi), 1, b0 + 1, 0)

    def block_oct(t, carry):
        for pair in range(4):
            block_pair(8 * t + 2 * pair)
        return carry

    lax.fori_loop(0, qi // 8, block_oct, 0)

    @pl.when(qi % 8 >= 4)
    def _():
        block_pair(qi // 8 * 8)
        block_pair(qi // 8 * 8 + 2)

    @pl.when(qi % 4 >= 2)
    def _():
        block_pair(qi // 4 * 4)

    @pl.when(qi % 2 == 1)
    def _():
        for unit in units:
            consume(qi - 1, 1, unit)

    outs = [acc_sc[hh, :HEAD_DIM, :] / acc_sc[hh, HEAD_DIM:HEAD_DIM + 1, :] for hh in range(2)]
    o_ref[...] = jnp.concatenate(outs, axis=0).T.astype(BF16)


def _fox(q3t, k3, vt4):
    b, s, _ = k3.shape
    tq = FOX_TQ
    return pl.pallas_call(
        _fox_kernel,
        grid=(b, FOX_PAIRS, s // tq),
        in_specs=[pl.BlockSpec((None, 2, FOX_DEPTH, tq), lambda bi, hp, qi: (bi, hp, 0, qi)),
                  pl.BlockSpec((None, s, 2 * FOX_DEPTH), lambda bi, hp, qi: (bi, 0, hp)),
                  pl.BlockSpec((None, 2, VT_ROWS, s), lambda bi, hp, qi: (bi, hp, 0, 0))],
        out_specs=pl.BlockSpec((None, tq, LANES), lambda bi, hp, qi: (bi, qi, hp)),
        out_shape=jax.ShapeDtypeStruct((b, s, FOX_W), BF16),
        scratch_shapes=[pltpu.VMEM((2, 2, tq, tq), F32),
                        pltpu.VMEM((2, 2, 1, tq), F32),
                        pltpu.VMEM((2, 1, tq), F32),
                        pltpu.VMEM((2, VT_ROWS, tq), F32)],
        compiler_params=pltpu.CompilerParams(dimension_semantics=("parallel", "parallel", "parallel"),
                                             vmem_limit_bytes=VMEM_LIMIT),
        name="fox",
    )(q3t, k3, vt4)


def _mix_kernel(x_ref, ya_ref, yb_ref, gates_ref, wa_ref, wb_ref, wo_ref, h_ref):
    slabs = [slice(r, r + TM) for r in range(0, MIX_TM, TM)]

    def branches(rows):
        y_a = jnp.dot(ya_ref[rows, :], wa_ref[...], preferred_element_type=F32)
        y_b = jnp.dot(yb_ref[rows, :], wb_ref[...], preferred_element_type=F32)
        return y_a, y_b

    pending = branches(slabs[0])
    for n, rows in enumerate(slabs):
        y_a, y_b = pending
        if n + 1 < len(slabs):
            pending = branches(slabs[n + 1])
        gates = gates_ref[rows, :].astype(F32)
        mixed = gates[:, :D_MODEL] * y_a + gates[:, D_MODEL:] * y_b
        h_ref[rows, :] = x_ref[rows, :] + jnp.dot(mixed.astype(BF16), wo_ref[...], preferred_element_type=F32)


def _mix(x2, att_a, att_b, gates, w_a, w_b, w_o):
    t = x2.shape[0]
    assert t % MIX_TM == 0 and MIX_TM % TM == 0
    row_spec = lambda w: pl.BlockSpec((MIX_TM, w), lambda i: (i, 0))
    return pl.pallas_call(
        _mix_kernel,
        grid=(t // MIX_TM,),
        in_specs=[row_spec(D_MODEL), row_spec(SWA_Q), row_spec(FOX_W), row_spec(2 * D_MODEL),
                  _const_spec(w_a.shape), _const_spec(w_b.shape), _const_spec(w_o.shape)],
        out_specs=row_spec(D_MODEL),
        out_shape=jax.ShapeDtypeStruct((t, D_MODEL), F32),
        compiler_params=pltpu.CompilerParams(dimension_semantics=("parallel",), vmem_limit_bytes=VMEM_LIMIT),
        name="mix",
    )(x2, att_a, att_b, gates, w_a, w_b, w_o)


def _mlp_ple_kernel(h_ref, p_ref, gm_ref, w1_ref, w2_ref, gp_ref, wpg_ref, wpp_ref, gf_ref, o_ref):
    h = h_ref[...]
    u = _rms(h, gm_ref[...]).astype(BF16)
    acc = jnp.zeros_like(h)
    for c in range(D_FF // FF_CHUNK):
        cols = slice(c * FF_CHUNK, (c + 1) * FF_CHUNK)
        a = jnp.dot(u, w1_ref[:, cols], preferred_element_type=F32)
        a = jnp.square(jnp.maximum(a, 0.0)).astype(BF16)
        acc = acc + jnp.dot(a, w2_ref[cols, :], preferred_element_type=F32)
    h = h + acc
    gate = jax.nn.sigmoid(jnp.dot(_rms(h, gp_ref[...]).astype(BF16), wpg_ref[...], preferred_element_type=F32))
    proj = jnp.dot(p_ref[...].astype(BF16), wpp_ref[...], preferred_element_type=F32)
    h = h + gate * proj
    o_ref[...] = _rms(h, gf_ref[...])


def _mlp_ple(h2, p2, g_mlp, w1, w2, g_ple, w_pg, w_pp, g_final):
    t = h2.shape[0]
    row_spec = lambda w: pl.BlockSpec((TM, w), lambda i: (i, 0))
    vec = _const_spec((1, D_MODEL))
    return pl.pallas_call(
        _mlp_ple_kernel,
        grid=(t // TM,),
        in_specs=[row_spec(D_MODEL), row_spec(PLE_DIM), vec, _const_spec(w1.shape), _const_spec(w2.shape),
                  vec, _const_spec(w_pg.shape), _const_spec(w_pp.shape), vec],
        out_specs=row_spec(D_MODEL),
        out_shape=jax.ShapeDtypeStruct((t, D_MODEL), F32),
        compiler_params=pltpu.CompilerParams(dimension_semantics=("parallel",), vmem_limit_bytes=VMEM_LIMIT),
        name="mlp_ple",
    )(h2, p2, g_mlp, w1, w2, g_ple, w_pg, w_pp, g_final)


def kernel(x, p, g_mix, w_in, b_forget, swa_sinks, w_br_swa, w_br_fox, w_mix_out,
           g_mlp, w_ff1, w_ff2, g_ple, w_ple_gate, w_ple_proj, g_final):
    b, s, d = x.shape
    assert d == D_MODEL and w_in.shape[0] == 1, "single-layer trunk with D_MODEL channels only"
    assert s % TM == 0 and s % FOX_TQ == 0 and s % SWA_TILE == 0 and SWA_TILE % SWA_BLOCK == 0
    t = b * s
    x2 = x.reshape(t, d)

    w = jnp.swapaxes(w_in, 1, 2)[0]
    bf_pad = jnp.pad(b_forget[0], (0, LANES - FOX_HEADS)).reshape(1, LANES)

    later = (w_br_swa[0], w_br_fox[0], w_mix_out[0], w_ff1[0], w_ff2[0], w_ple_gate[0])
    qat, ka, vta, q3t, k3, vtb, gates, *later_bf16 = _in_proj(x2, g_mix[0].reshape(1, d), w, bf_pad, b, s, later)
    wa_b, wb_b, wo_b, w1_b, w2_b, wpg_b = later_bf16

    att_a = _swa(qat, ka.reshape(b, s, SWA_KV), vta, swa_sinks[0].reshape(1, SWA_HEADS),
                 jnp.asarray(_swa_bias_table()))
    att_b = _fox(q3t, k3.reshape(b, s, FOX_HEADS * FOX_DEPTH), vtb.reshape(b, FOX_HEADS, VT_ROWS, s))

    h = _mix(x2, att_a.reshape(t, SWA_Q), att_b.reshape(t, FOX_W), gates, wa_b, wb_b, wo_b)
    out = _mlp_ple(h, p[0].reshape(t, PLE_DIM), g_mlp[0].reshape(1, d), w1_b, w2_b,
                   g_ple[0].reshape(1, d), wpg_b, w_ple_proj[0].astype(BF16), g_final.reshape(1, d))
    return out.reshape(b, s, d)
```

```python
import functools

import numpy as np
import jax
import jax.numpy as jnp
from jax import lax
from jax.experimental import pallas as pl
from jax.experimental.pallas import tpu as pltpu

D_MODEL = 1024
CHUNK = 64
PLE_DIM = 256
HEAD_DIM = 64
SWA_HEADS = 8
SWA_KV_HEADS = 2
SWA_GROUP = SWA_HEADS // SWA_KV_HEADS
WINDOW = 128
SWA_BLOCK = WINDOW
FOX_HEADS = 8
D_FF = 4 * D_MODEL
RMS_EPS = 1e-6
SWA_Q = SWA_HEADS * HEAD_DIM
SWA_KV = SWA_KV_HEADS * HEAD_DIM
FOX_W = FOX_HEADS * HEAD_DIM
COL_KA = SWA_Q
COL_VA = COL_KA + SWA_KV
COL_QB = COL_VA + SWA_KV
COL_KB = COL_QB + FOX_W
COL_VB = COL_KB + FOX_W
COL_F = COL_VB + FOX_W
COL_G = COL_F + FOX_HEADS
SCALE = HEAD_DIM ** -0.5
LOG2E = float(np.log2(np.e))

LANES = 128
BF16_SUBLANES = 16
MXU_COLS = 256
NEG_BIG = -1e30
NORM_SLACK = 1.03
FOX_SKIP_MARGIN = 192.0
VMEM_LIMIT = 52 * 1024 * 1024
IN_PROJ_VMEM_LIMIT = 58 * 1024 * 1024

TM = 512
MIX_TM = 1024
SWA_TILE = 2048
SWA_SLOTS = 6
FOX_TQ = 512
FF_CHUNK = 1024

FOX_PAIRS = FOX_HEADS // 2
FOX_DEPTH = LANES
BIAS_SLOT = 8
C_PARTS = 3
VT_ROWS = 80

F32 = jnp.float32
BF16 = jnp.bfloat16
NT_DIMS = (((1,), (1,)), ((), ()))


def _rms(x, g):
    return x * lax.rsqrt(jnp.mean(x * x, axis=-1, keepdims=True) + RMS_EPS) * g


def _const_spec(shape):
    return pl.BlockSpec(shape, lambda *_: (0,) * len(shape), pipeline_mode=pl.Buffered(1))


def _split3(v):
    hi = v.astype(BF16)
    r1 = v - hi.astype(F32)
    mid = r1.astype(BF16)
    lo = (r1 - mid.astype(F32)).astype(BF16)
    return hi, mid, lo


def _bias_placement():
    place_qt = np.zeros((LANES, C_PARTS * LANES), np.float32)
    ones_qt = np.zeros((LANES, 1), np.float32)
    place_k = np.zeros((C_PARTS * LANES, LANES), np.float32)
    ones_k = np.zeros((1, LANES), np.float32)
    for h in range(FOX_HEADS):
        for part in range(C_PARTS):
            place_qt[h * BIAS_SLOT + part, part * LANES + h] = 1.0
            ones_qt[h * BIAS_SLOT + C_PARTS + part, 0] = 1.0
            ones_k[0, h * BIAS_SLOT + part] = 1.0
            place_k[part * LANES + h, h * BIAS_SLOT + C_PARTS + part] = -1.0
    return (place_qt, ones_qt), (place_k, ones_k)


def _in_proj_kernel(*refs, tiles_per_seq, n_cast):
    (x_ref, g_ref, wlo_ref, whi_ref, bf_ref, pqt_ref, oqt_ref, pk_ref, ok_ref, vone_ref, sel_ref) = refs[:11]
    cast_in = refs[11:11 + n_cast]
    (qat_ref, ka_ref, vta_ref, q3t_ref, k3_ref, vtb_ref, gates_ref, knorm_ref, cmin_ref) = refs[11 + n_cast:20 + n_cast]
    cast_out = refs[20 + n_cast:20 + 2 * n_cast]
    carry_ref, wt_ref, wkf_ref, wk_ref, wg_ref = refs[20 + 2 * n_cast:]
    i = pl.program_id(0)
    lane = lax.broadcasted_iota(jnp.int32, (1, LANES), 1)
    gate_shift = COL_G % LANES
    for src, dst in zip(cast_in, cast_out):
        dst[...] = src[...].astype(BF16)

    @pl.when(i == 0)
    def _():
        wk_ref[...] = wlo_ref[COL_KB:COL_KB + FOX_W].T.astype(BF16)
        wkf_ref[:, :SWA_KV] = wlo_ref[COL_KA:COL_KA + SWA_KV].T.astype(BF16)
        second = jnp.concatenate([whi_ref[:gate_shift], whi_ref[2 * D_MODEL:2 * D_MODEL + gate_shift],
                                  jnp.zeros((LANES - 2 * gate_shift, D_MODEL), F32)], axis=0)
        wkf_ref[:, SWA_KV:] = second.T.astype(BF16)
        wg_ref[...] = whi_ref[:2 * D_MODEL].T.astype(BF16)
        wt_ref[:SWA_Q] = wlo_ref[:SWA_Q].astype(BF16)
        wt_ref[SWA_Q:SWA_Q + FOX_W] = wlo_ref[COL_QB:COL_QB + FOX_W].astype(BF16)
        pad_rows = jnp.zeros((VT_ROWS - HEAD_DIM, D_MODEL), BF16)
        v_rows = [COL_VA + g * HEAD_DIM for g in range(SWA_KV_HEADS)] + [COL_VB + h * HEAD_DIM for h in range(FOX_HEADS)]
        for head, row in enumerate(v_rows):
            base = SWA_Q + FOX_W + head * VT_ROWS
            wt_ref[base:base + HEAD_DIM] = wlo_ref[row:row + HEAD_DIM].astype(BF16)
            wt_ref[base + HEAD_DIM:base + VT_ROWS] = pad_rows

    @pl.when(i % tiles_per_seq == 0)
    def _():
        carry_ref[...] = jnp.zeros_like(carry_ref)

    u = _rms(x_ref[...], g_ref[...]).astype(BF16)
    tm = u.shape[0]
    kf = jnp.dot(u, wkf_ref[...], preferred_element_type=F32)
    ka_ref[...] = kf[:, :SWA_KV].astype(BF16)
    f = kf[:, SWA_KV:] + bf_ref[...]
    logf = jnp.minimum(f, 0.0) - jnp.log1p(jnp.exp(-jnp.abs(f)))

    gl = jnp.dot(u, wg_ref[...], preferred_element_type=F32)
    n_tiles = gl.shape[1] // LANES
    for c in range(n_tiles):
        cur = pltpu.roll(gl[:, c * LANES:(c + 1) * LANES], LANES - gate_shift, 1)
        if c + 1 < n_tiles:
            nxt = pltpu.roll(gl[:, (c + 1) * LANES:(c + 2) * LANES], LANES - gate_shift, 1)
        else:
            nxt = pltpu.roll(kf[:, SWA_KV:], LANES - 2 * gate_shift, 1)
        tile = jnp.where(lane < LANES - gate_shift, cur, nxt)
        gates_ref[:, c * LANES:(c + 1) * LANES] = jax.nn.sigmoid(tile).astype(BF16)

    row = lax.broadcasted_iota(jnp.int32, (tm, tm), 0)
    col = lax.broadcasted_iota(jnp.int32, (tm, tm), 1)
    tri = (col <= row).astype(BF16)
    cs = jnp.dot(tri, jnp.concatenate(_split3(logf), axis=1), preferred_element_type=F32)
    c = cs[:, :LANES] + cs[:, LANES:2 * LANES] + cs[:, 2 * LANES:] + carry_ref[...]
    carry_ref[...] = c[tm - 1:tm, :]
    cmin_ref[...] = jnp.min(c * LOG2E, axis=0, keepdims=True)

    tt = lax.dot_general(wt_ref[...], u, NT_DIMS, preferred_element_type=F32)
    qat_ref[...] = (tt[:SWA_Q] * (SCALE * LOG2E)).astype(BF16)
    qt = (tt[SWA_Q:SWA_Q + FOX_W] * (SCALE * LOG2E)).astype(BF16)
    vt = (tt[SWA_Q + FOX_W:] + vone_ref[...]).astype(BF16)
    vta_ref[...] = vt[:SWA_KV_HEADS * VT_ROWS]
    vtb_ref[...] = vt[SWA_KV_HEADS * VT_ROWS:]

    parts = jnp.concatenate(_split3(c * LOG2E), axis=1)
    kbias = jnp.dot(parts, pk_ref[...], preferred_element_type=F32) + ok_ref[...]
    qbias_t = lax.dot_general(pqt_ref[...], parts, NT_DIMS, preferred_element_type=F32) + oqt_ref[...]
    zero_rows = jnp.zeros((HEAD_DIM - BIAS_SLOT, tm), F32)
    for h in range(FOX_HEADS):
        q3t_ref[h, :HEAD_DIM, :] = qt[h * HEAD_DIM:(h + 1) * HEAD_DIM]
        own = jnp.concatenate([qbias_t[h * BIAS_SLOT:(h + 1) * BIAS_SLOT], zero_rows], axis=0)
        q3t_ref[h, HEAD_DIM:, :] = own.astype(BF16)

    k = jnp.dot(u, wk_ref[...], preferred_element_type=F32)
    k_sq = jnp.square(k.astype(BF16).astype(F32)).astype(BF16)
    knorm_ref[...] = NORM_SLACK * jnp.max(jnp.dot(k_sq, sel_ref[...], preferred_element_type=F32), axis=0, keepdims=True)
    lane = lax.broadcasted_iota(jnp.int32, (1, LANES), 1)
    for h in range(FOX_HEADS):
        pair = k[:, (h // 2) * LANES:(h // 2 + 1) * LANES]
        dims = pair if h % 2 == 0 else pltpu.roll(pair, HEAD_DIM, 1)
        bias = pltpu.roll(kbias, HEAD_DIM - h * BIAS_SLOT, 1)
        k3_ref[:, h * FOX_DEPTH:(h + 1) * FOX_DEPTH] = jnp.where(lane < HEAD_DIM, dims, bias).astype(BF16)


def _in_proj(x2, g_mix, w_all, bf_pad, batch, seq, later_weights):
    t = x2.shape[0]
    tps = seq // TM
    steps = t // TM
    (place_qt, ones_qt), (place_k, ones_k) = _bias_placement()
    n_vt = SWA_KV_HEADS + FOX_HEADS
    vone = np.zeros((n_vt * VT_ROWS, 1), np.float32)
    vone[HEAD_DIM::VT_ROWS] = 1.0
    head_of_lane = np.zeros((FOX_W, LANES), np.float32)
    head_of_lane[np.arange(FOX_W), np.arange(FOX_W) // HEAD_DIM] = 1.0
    consts = (jnp.asarray(place_qt, BF16), jnp.asarray(ones_qt), jnp.asarray(place_k, BF16), jnp.asarray(ones_k),
              jnp.asarray(vone), jnp.asarray(head_of_lane, BF16))
    kern = functools.partial(_in_proj_kernel, tiles_per_seq=tps, n_cast=len(later_weights))
    row_spec = lambda w: pl.BlockSpec((TM, w), lambda i: (i, 0))
    col_spec = lambda *lead: pl.BlockSpec((None,) + lead + (TM,), lambda i: (i // tps,) + (0,) * len(lead) + (i % tps,))
    for w in later_weights:
        assert w.ndim == 2 and w.shape[0] % (steps * BF16_SUBLANES) == 0, w.shape
    slab_specs = [pl.BlockSpec((w.shape[0] // steps, w.shape[1]), lambda i: (i, 0)) for w in later_weights]
    assert w_all.shape == (COL_G + 2 * D_MODEL, D_MODEL) and COL_F >= 2 * D_MODEL + LANES and COL_F % BF16_SUBLANES == 0
    half_specs = [pl.BlockSpec((COL_F, D_MODEL), lambda i, j=j: (j, 0), pipeline_mode=pl.Buffered(1)) for j in range(2)]
    tail = (bf_pad,) + consts
    ka, k3, gates = (jax.ShapeDtypeStruct((t, w), BF16) for w in (SWA_KV, FOX_HEADS * FOX_DEPTH, 2 * D_MODEL))
    qat = jax.ShapeDtypeStruct((batch, SWA_Q, seq), BF16)
    stat = jax.ShapeDtypeStruct((steps, 1, LANES), F32)
    stat_spec = pl.BlockSpec((None, 1, LANES), lambda i: (i, 0, 0))
    q3t = jax.ShapeDtypeStruct((batch, FOX_HEADS, FOX_DEPTH, seq), BF16)
    vta, vtb = (jax.ShapeDtypeStruct((batch, heads * VT_ROWS, seq), BF16) for heads in (SWA_KV_HEADS, FOX_HEADS))
    return pl.pallas_call(
        kern,
        grid=(steps,),
        in_specs=([row_spec(D_MODEL), _const_spec(g_mix.shape)] + half_specs
                  + [_const_spec(a.shape) for a in tail] + slab_specs),
        out_specs=[col_spec(SWA_Q), row_spec(SWA_KV), col_spec(SWA_KV_HEADS * VT_ROWS),
                   col_spec(FOX_HEADS, FOX_DEPTH), row_spec(FOX_HEADS * FOX_DEPTH), col_spec(FOX_HEADS * VT_ROWS),
                   row_spec(2 * D_MODEL), stat_spec, stat_spec] + slab_specs,
        out_shape=([qat, ka, vta, q3t, k3, vtb, gates, stat, stat]
                   + [jax.ShapeDtypeStruct(w.shape, BF16) for w in later_weights]),
        scratch_shapes=[pltpu.VMEM((1, LANES), F32),
                        pltpu.VMEM((SWA_Q + FOX_W + n_vt * VT_ROWS, D_MODEL), BF16),
                        pltpu.VMEM((D_MODEL, SWA_KV + LANES), BF16),
                        pltpu.VMEM((D_MODEL, FOX_W), BF16),
                        pltpu.VMEM((D_MODEL, 2 * D_MODEL), BF16)],
        compiler_params=pltpu.CompilerParams(dimension_semantics=("arbitrary",), vmem_limit_bytes=IN_PROJ_VMEM_LIMIT),
        name="in_proj",
    )(x2, g_mix, w_all, w_all, *tail, *later_weights)


def _swa_bias_table():
    sb = SWA_BLOCK
    qi = np.arange(sb)[None, :] + sb
    si = np.arange(2 * sb)[:, None]
    chunk_diff = qi // CHUNK - si // CHUNK
    band_ok = (chunk_diff >= 0) & (chunk_diff <= WINDOW // CHUNK)
    slopes = np.array([2.0 ** (-8.0 * (h + 1) / SWA_HEADS) for h in range(SWA_HEADS)], dtype=np.float32)
    alibi = -slopes[:, None, None] * np.abs(qi - si).astype(np.float32)[None] * np.float32(LOG2E)
    first = band_ok & (si >= sb)
    table = np.stack([np.where(first[None], alibi, NEG_BIG), np.where(band_ok[None], alibi, NEG_BIG)])
    table = table.reshape(2, SWA_KV_HEADS, SWA_GROUP, 2 * sb, sb).transpose(0, 1, 3, 2, 4)
    return np.ascontiguousarray(table.reshape(2, SWA_KV_HEADS, 2 * sb, SWA_GROUP * sb)).astype(np.float32)


def _swa_kernel(sink_ref, qt_ref, kp_ref, kc_ref, vtp_ref, vtc_ref, bias_first_ref, bias_rest_ref, o_ref,
                st_sc, mb_sc, ot_sc):
    sb = SWA_BLOCK
    per_unit = MXU_COLS // sb
    half = jnp.zeros((HEAD_DIM, MXU_COLS), BF16)
    units = [(j, kh, c) for j in range(SWA_TILE // sb) for kh in range(SWA_KV_HEADS)
             for c in range(SWA_GROUP // per_unit)]

    def band(j):
        if j == 0:
            kb = jnp.concatenate([kp_ref[...], kc_ref[0:sb, :]], axis=0)
            vtb = jnp.concatenate([vtp_ref[...], vtc_ref[:, 0:sb]], axis=1)
            return kb, vtb, bias_first_ref
        return kc_ref[(j - 1) * sb:(j + 1) * sb, :], vtc_ref[:, (j - 1) * sb:(j + 1) * sb], bias_rest_ref

    def unit_heads(kh, c):
        return [kh * SWA_GROUP + c * per_unit + g for g in range(per_unit)]

    def scores(i, slot):
        j, kh, c = units[i]
        kb, _, bias_ref = band(j)
        q2 = jnp.concatenate([qt_ref[h * HEAD_DIM:(h + 1) * HEAD_DIM, j * sb:(j + 1) * sb] for h in unit_heads(kh, c)],
                             axis=1)
        q2t = jnp.concatenate([q2, half] if kh == 0 else [half, q2], axis=0)
        st = jnp.dot(kb, q2t, preferred_element_type=F32) + bias_ref[kh, :, c * MXU_COLS:(c + 1) * MXU_COLS]
        st_sc[slot] = st
        mb_sc[slot] = jnp.max(st, axis=0, keepdims=True)

    def consume(unit, slot):
        j, kh, c = unit
        _, vtb, _ = band(j)
        heads = unit_heads(kh, c)
        sink = jnp.concatenate([jnp.full((1, sb), sink_ref[0, h] * LOG2E, F32) for h in heads], axis=1)
        m = jnp.maximum(mb_sc[slot], sink)
        p = jnp.exp2(st_sc[slot] - m).astype(BF16)
        pv = jnp.dot(vtb[kh * VT_ROWS:(kh + 1) * VT_ROWS, :], p, preferred_element_type=F32)
        denom = pv[HEAD_DIM:HEAD_DIM + 1, :] + jnp.exp2(sink - m)
        o = pv[:HEAD_DIM, :] / denom
        for g, h in enumerate(heads):
            ot_sc[h * HEAD_DIM:(h + 1) * HEAD_DIM, j * sb:(j + 1) * sb] = o[:, g * sb:(g + 1) * sb]

    slots = st_sc.shape[0]
    for i in range(slots - 1):
        scores(i, i)
    for i, unit in enumerate(units):
        ahead = i + slots - 1
        if ahead < len(units):
            scores(ahead, ahead % slots)
        consume(unit, i % slots)
    o_ref[...] = ot_sc[...].T.astype(BF16)


def _swa(qt3, k3, vt3, sinks, bias):
    b, s, _ = k3.shape
    sb, ts = SWA_BLOCK, SWA_TILE
    per = ts // sb
    prev = lambda n: jnp.maximum(n * per - 1, 0)
    bias_block = (None,) + bias.shape[1:]
    return pl.pallas_call(
        _swa_kernel,
        grid=(b, s // ts),
        in_specs=[pl.BlockSpec(memory_space=pltpu.SMEM),
                  pl.BlockSpec((None, SWA_Q, ts), lambda bi, n: (bi, 0, n)),
                  pl.BlockSpec((None, sb, SWA_KV), lambda bi, n: (bi, prev(n), 0)),
                  pl.BlockSpec((None, ts, SWA_KV), lambda bi, n: (bi, n, 0)),
                  pl.BlockSpec((None, SWA_KV_HEADS * VT_ROWS, sb), lambda bi, n: (bi, 0, prev(n))),
                  pl.BlockSpec((None, SWA_KV_HEADS * VT_ROWS, ts), lambda bi, n: (bi, 0, n)),
                  pl.BlockSpec(bias_block, lambda bi, n: (jnp.minimum(n, 1), 0, 0, 0)),
                  pl.BlockSpec(bias_block, lambda bi, n: (1, 0, 0, 0))],
        out_specs=pl.BlockSpec((None, ts, SWA_Q), lambda bi, n: (bi, n, 0)),
        out_shape=jax.ShapeDtypeStruct((b, s, SWA_Q), BF16),
        scratch_shapes=[pltpu.VMEM((SWA_SLOTS, 2 * sb, MXU_COLS), F32),
                        pltpu.VMEM((SWA_SLOTS, 1, MXU_COLS), F32),
                        pltpu.VMEM((SWA_Q, ts), F32)],
        compiler_params=pltpu.CompilerParams(dimension_semantics=("parallel", "parallel"),
                                             vmem_limit_bytes=VMEM_LIMIT),
        name="swa",
    )(sinks, qt3, k3, k3, vt3, vt3, bias, bias)


def _fox_kernel(qt_ref, k_ref, vt_ref, knorm_ref, cmin_ref, o_ref, st_sc, mb_sc, m_sc, acc_sc):
    tq = qt_ref.shape[2]
    hp = pl.program_id(1)
    qi = pl.program_id(2)
    m_sc[...] = jnp.full_like(m_sc, NEG_BIG)
    acc_sc[...] = jnp.zeros_like(acc_sc)

    units = [(hh, slice(c * MXU_COLS, (c + 1) * MXU_COLS)) for hh in range(2) for c in range(tq // MXU_COLS)]

    def scores(block, slot, unit, diagonal):
        hh, cols = unit
        keys = cols.stop if diagonal else tq
        start = pl.multiple_of(block * tq, tq)
        k3 = k_ref[pl.ds(start, keys), hh * FOX_DEPTH:(hh + 1) * FOX_DEPTH]
        st = jnp.dot(k3, qt_ref[hh, :, cols], preferred_element_type=F32)
        if diagonal:
            key = lax.broadcasted_iota(jnp.int32, st.shape, 0)
            qry = lax.broadcasted_iota(jnp.int32, st.shape, 1) + cols.start
            st = jnp.where(key <= qry, st, NEG_BIG)
        st_sc[slot, hh, :keys, cols] = st
        mb_sc[slot, hh, :, cols] = jnp.max(st, axis=0, keepdims=True)

    def consume(block, slot, unit, diagonal=False):
        hh, cols = unit
        keys = cols.stop if diagonal else tq
        start = pl.multiple_of(block * tq, tq)
        m_prev = m_sc[hh, :, cols]
        m_new = jnp.maximum(m_prev, mb_sc[slot, hh, :, cols])
        p = jnp.exp2(st_sc[slot, hh, :keys, cols] - m_new).astype(BF16)
        alpha = jnp.exp2(m_prev - m_new)
        pv = jnp.dot(vt_ref[hh, :, pl.ds(start, keys)], p, preferred_element_type=F32)
        acc_sc[hh, :, cols] = alpha * acc_sc[hh, :, cols] + pv
        m_sc[hh, :, cols] = m_new

    def stage(next_block, next_slot, block, slot, diagonal=False):
        for unit in units:
            scores(next_block, next_slot, unit, False)
            consume(block, slot, unit, diagonal)

    def skippable_blocks(hh):
        lane = lax.broadcasted_iota(jnp.int32, (1, LANES), 1)
        q32 = qt_ref[hh, :HEAD_DIM, :].astype(F32)
        q_sq = jnp.max(jnp.sum(q32 * q32, axis=0, keepdims=True), axis=1, keepdims=True)
        c_q = jnp.sum(qt_ref[hh, HEAD_DIM:HEAD_DIM + C_PARTS, :].astype(F32), axis=0, keepdims=True)
        c_first = jnp.max(c_q, axis=1, keepdims=True)
        m_min = jnp.min(mb_sc[0, hh], axis=1, keepdims=True)
        k_sq = jnp.zeros((1, LANES), F32)
        c_last = jnp.full((1, LANES), -NEG_BIG, F32)
        count = jnp.zeros((1, LANES), jnp.int32)
        for j in range(knorm_ref.shape[0]):
            k_sq = jnp.maximum(k_sq, knorm_ref[j])
            c_last = jnp.minimum(c_last, cmin_ref[j])
            room = (m_min - FOX_SKIP_MARGIN) - (c_first - c_last)
            zero = (room > 0.0) & (q_sq * k_sq < room * room) & (j < qi)
            count = count + zero.astype(jnp.int32)
        return jnp.sum(jnp.where(lane == 2 * hp + hh, count, 0))

    for unit in units:
        scores(qi, 0, unit, True)
    first = jnp.minimum(skippable_blocks(0), skippable_blocks(1))
    count = qi - first
    stage(jnp.minimum(first, qi), 1, qi, 0, diagonal=True)

    def block_pair(b0):
        stage(b0 + 1, 0, b0, 1)
        stage(jnp.minimum(b0 + 2, qi), 1, b0 + 1, 0)

    def block_oct(t, carry):
        for pair in range(4):
            block_pair(first + 8 * t + 2 * pair)
        return carry

    lax.fori_loop(0, count // 8, block_oct, 0)

    @pl.when(count % 8 >= 4)
    def _():
        block_pair(first + count // 8 * 8)
        block_pair(first + count // 8 * 8 + 2)

    @pl.when(count % 4 >= 2)
    def _():
        block_pair(first + count // 4 * 4)

    @pl.when(count % 2 == 1)
    def _():
        for unit in units:
            consume(qi - 1, 1, unit)

    outs = [acc_sc[hh, :HEAD_DIM, :] / acc_sc[hh, HEAD_DIM:HEAD_DIM + 1, :] for hh in range(2)]
    o_ref[...] = jnp.concatenate(outs, axis=0).T.astype(BF16)


def _fox(q3t, k3, vt4, knorm, cmin):
    b, s, _ = k3.shape
    tq = FOX_TQ
    assert knorm.shape == cmin.shape == (b, s // tq, 1, LANES)
    stat_spec = pl.BlockSpec((None, s // tq, 1, LANES), lambda bi, hp, qi: (bi, 0, 0, 0))
    return pl.pallas_call(
        _fox_kernel,
        grid=(b, FOX_PAIRS, s // tq),
        in_specs=[pl.BlockSpec((None, 2, FOX_DEPTH, tq), lambda bi, hp, qi: (bi, hp, 0, qi)),
                  pl.BlockSpec((None, s, 2 * FOX_DEPTH), lambda bi, hp, qi: (bi, 0, hp)),
                  pl.BlockSpec((None, 2, VT_ROWS, s), lambda bi, hp, qi: (bi, hp, 0, 0)),
                  stat_spec, stat_spec],
        out_specs=pl.BlockSpec((None, tq, LANES), lambda bi, hp, qi: (bi, qi, hp)),
        out_shape=jax.ShapeDtypeStruct((b, s, FOX_W), BF16),
        scratch_shapes=[pltpu.VMEM((2, 2, tq, tq), F32),
                        pltpu.VMEM((2, 2, 1, tq), F32),
                        pltpu.VMEM((2, 1, tq), F32),
                        pltpu.VMEM((2, VT_ROWS, tq), F32)],
        compiler_params=pltpu.CompilerParams(dimension_semantics=("parallel", "parallel", "parallel"),
                                             vmem_limit_bytes=VMEM_LIMIT),
        name="fox",
    )(q3t, k3, vt4, knorm, cmin)


def _mix_kernel(x_ref, ya_ref, yb_ref, gates_ref, wa_ref, wb_ref, wo_ref, h_ref):
    slabs = [slice(r, r + TM) for r in range(0, MIX_TM, TM)]

    def branches(rows):
        y_a = jnp.dot(ya_ref[rows, :], wa_ref[...], preferred_element_type=F32)
        y_b = jnp.dot(yb_ref[rows, :], wb_ref[...], preferred_element_type=F32)
        return y_a, y_b

    pending = branches(slabs[0])
    for n, rows in enumerate(slabs):
        y_a, y_b = pending
        if n + 1 < len(slabs):
            pending = branches(slabs[n + 1])
        gates = gates_ref[rows, :].astype(F32)
        mixed = gates[:, :D_MODEL] * y_a + gates[:, D_MODEL:] * y_b
        h_ref[rows, :] = x_ref[rows, :] + jnp.dot(mixed.astype(BF16), wo_ref[...], preferred_element_type=F32)


def _mix(x2, att_a, att_b, gates, w_a, w_b, w_o):
    t = x2.shape[0]
    assert t % MIX_TM == 0 and MIX_TM % TM == 0
    row_spec = lambda w: pl.BlockSpec((MIX_TM, w), lambda i: (i, 0))
    return pl.pallas_call(
        _mix_kernel,
        grid=(t // MIX_TM,),
        in_specs=[row_spec(D_MODEL), row_spec(SWA_Q), row_spec(FOX_W), row_spec(2 * D_MODEL),
                  _const_spec(w_a.shape), _const_spec(w_b.shape), _const_spec(w_o.shape)],
        out_specs=row_spec(D_MODEL),
        out_shape=jax.ShapeDtypeStruct((t, D_MODEL), F32),
        compiler_params=pltpu.CompilerParams(dimension_semantics=("parallel",), vmem_limit_bytes=VMEM_LIMIT),
        name="mix",
    )(x2, att_a, att_b, gates, w_a, w_b, w_o)


def _mlp_ple_kernel(h_ref, p_ref, gm_ref, w1_ref, w2_ref, gp_ref, wpg_ref, wpp_ref, gf_ref, o_ref):
    h = h_ref[...]
    u = _rms(h, gm_ref[...]).astype(BF16)
    acc = jnp.zeros_like(h)
    for c in range(D_FF // FF_CHUNK):
        cols = slice(c * FF_CHUNK, (c + 1) * FF_CHUNK)
        a = jnp.dot(u, w1_ref[:, cols], preferred_element_type=F32)
        a = jnp.square(jnp.maximum(a, 0.0)).astype(BF16)
        acc = acc + jnp.dot(a, w2_ref[cols, :], preferred_element_type=F32)
    h = h + acc
    gate = jax.nn.sigmoid(jnp.dot(_rms(h, gp_ref[...]).astype(BF16), wpg_ref[...], preferred_element_type=F32))
    proj = jnp.dot(p_ref[...].astype(BF16), wpp_ref[...], preferred_element_type=F32)
    h = h + gate * proj
    o_ref[...] = _rms(h, gf_ref[...])


def _mlp_ple(h2, p2, g_mlp, w1, w2, g_ple, w_pg, w_pp, g_final):
    t = h2.shape[0]
    row_spec = lambda w: pl.BlockSpec((TM, w), lambda i: (i, 0))
    vec = _const_spec((1, D_MODEL))
    return pl.pallas_call(
        _mlp_ple_kernel,
        grid=(t // TM,),
        in_specs=[row_spec(D_MODEL), row_spec(PLE_DIM), vec, _const_spec(w1.shape), _const_spec(w2.shape),
                  vec, _const_spec(w_pg.shape), _const_spec(w_pp.shape), vec],
        out_specs=row_spec(D_MODEL),
        out_shape=jax.ShapeDtypeStruct((t, D_MODEL), F32),
        compiler_params=pltpu.CompilerParams(dimension_semantics=("parallel",), vmem_limit_bytes=VMEM_LIMIT),
        name="mlp_ple",
    )(h2, p2, g_mlp, w1, w2, g_ple, w_pg, w_pp, g_final)


def kernel(x, p, g_mix, w_in, b_forget, swa_sinks, w_br_swa, w_br_fox, w_mix_out,
           g_mlp, w_ff1, w_ff2, g_ple, w_ple_gate, w_ple_proj, g_final):
    b, s, d = x.shape
    assert d == D_MODEL and w_in.shape[0] == 1, "single-layer trunk with D_MODEL channels only"
    assert s % TM == 0 and FOX_TQ == TM and s % SWA_TILE == 0 and SWA_TILE % SWA_BLOCK == 0
    t = b * s
    x2 = x.reshape(t, d)

    w = jnp.swapaxes(w_in, 1, 2)[0]
    bf_pad = jnp.pad(b_forget[0], (0, LANES - FOX_HEADS)).reshape(1, LANES)

    later = (w_br_swa[0], w_br_fox[0], w_mix_out[0], w_ff1[0], w_ff2[0], w_ple_gate[0])
    qat, ka, vta, q3t, k3, vtb, gates, knorm, cmin, *later_bf16 = _in_proj(x2, g_mix[0].reshape(1, d), w, bf_pad, b, s, later)
    wa_b, wb_b, wo_b, w1_b, w2_b, wpg_b = later_bf16

    att_a = _swa(qat, ka.reshape(b, s, SWA_KV), vta, swa_sinks[0].reshape(1, SWA_HEADS),
                 jnp.asarray(_swa_bias_table()))
    blocks = (b, s // FOX_TQ, 1, LANES)
    att_b = _fox(q3t, k3.reshape(b, s, FOX_HEADS * FOX_DEPTH), vtb.reshape(b, FOX_HEADS, VT_ROWS, s),
                 knorm.reshape(blocks), cmin.reshape(blocks))

    h = _mix(x2, att_a.reshape(t, SWA_Q), att_b.reshape(t, FOX_W), gates, wa_b, wb_b, wo_b)
    out = _mlp_ple(h, p[0].reshape(t, PLE_DIM), g_mlp[0].reshape(1, d), w1_b, w2_b,
                   g_ple[0].reshape(1, d), wpg_b, w_ple_proj[0].astype(BF16), g_final.reshape(1, d))
    return out.reshape(b, s, d)
```

```python
import functools

import numpy as np
import jax
import jax.numpy as jnp
from jax import lax
from jax.experimental import pallas as pl
from jax.experimental.pallas import tpu as pltpu

D_MODEL = 1024
CHUNK = 64
PLE_DIM = 256
HEAD_DIM = 64
SWA_HEADS = 8
SWA_KV_HEADS = 2
SWA_GROUP = SWA_HEADS // SWA_KV_HEADS
WINDOW = 128
SWA_BLOCK = WINDOW
FOX_HEADS = 8
D_FF = 4 * D_MODEL
RMS_EPS = 1e-6
SWA_Q = SWA_HEADS * HEAD_DIM
SWA_KV = SWA_KV_HEADS * HEAD_DIM
FOX_W = FOX_HEADS * HEAD_DIM
COL_KA = SWA_Q
COL_VA = COL_KA + SWA_KV
COL_QB = COL_VA + SWA_KV
COL_KB = COL_QB + FOX_W
COL_VB = COL_KB + FOX_W
COL_F = COL_VB + FOX_W
COL_G = COL_F + FOX_HEADS
SCALE = HEAD_DIM ** -0.5
LOG2E = float(np.log2(np.e))

LANES = 128
BF16_SUBLANES = 16
MXU_COLS = 256
NEG_BIG = -1e30
NORM_SLACK = 1.03
FOX_SKIP_MARGIN = 192.0
VMEM_LIMIT = 52 * 1024 * 1024
IN_PROJ_VMEM_LIMIT = 58 * 1024 * 1024

TM = 512
MIX_TM = 1024
SWA_TILE = 2048
SWA_SLOTS = 6
FOX_TQ = 512
FF_CHUNK = 1024

FOX_PAIRS = FOX_HEADS // 2
FOX_DEPTH = LANES
BIAS_SLOT = 8
C_PARTS = 3
VT_ROWS = 80

F32 = jnp.float32
BF16 = jnp.bfloat16
NT_DIMS = (((1,), (1,)), ((), ()))


def _rms(x, g):
    return x * lax.rsqrt(jnp.mean(x * x, axis=-1, keepdims=True) + RMS_EPS) * g


def _const_spec(shape):
    return pl.BlockSpec(shape, lambda *_: (0,) * len(shape), pipeline_mode=pl.Buffered(1))


def _split3(v):
    hi = v.astype(BF16)
    r1 = v - hi.astype(F32)
    mid = r1.astype(BF16)
    lo = (r1 - mid.astype(F32)).astype(BF16)
    return hi, mid, lo


def _bias_placement():
    place_qt = np.zeros((LANES, C_PARTS * LANES), np.float32)
    ones_qt = np.zeros((LANES, 1), np.float32)
    place_k = np.zeros((C_PARTS * LANES, LANES), np.float32)
    ones_k = np.zeros((1, LANES), np.float32)
    for h in range(FOX_HEADS):
        for part in range(C_PARTS):
            place_qt[h * BIAS_SLOT + part, part * LANES + h] = 1.0
            ones_qt[h * BIAS_SLOT + C_PARTS + part, 0] = 1.0
            ones_k[0, h * BIAS_SLOT + part] = 1.0
            place_k[part * LANES + h, h * BIAS_SLOT + C_PARTS + part] = -1.0
    return (place_qt, ones_qt), (place_k, ones_k)


def _in_proj_kernel(*refs, tiles_per_seq, n_cast):
    (x_ref, g_ref, wlo_ref, whi_ref, bf_ref, pqt_ref, oqt_ref, pk_ref, ok_ref, vone_ref, sel_ref) = refs[:11]
    cast_in = refs[11:11 + n_cast]
    (qat_ref, ka_ref, vta_ref, q3t_ref, k3_ref, vtb_ref, gates_ref, knorm_ref, cmin_ref) = refs[11 + n_cast:20 + n_cast]
    cast_out = refs[20 + n_cast:20 + 2 * n_cast]
    carry_ref, wt_ref, wkf_ref, wk_ref, wg_ref = refs[20 + 2 * n_cast:]
    i = pl.program_id(0)
    lane = lax.broadcasted_iota(jnp.int32, (1, LANES), 1)
    gate_shift = COL_G % LANES
    for src, dst in zip(cast_in, cast_out):
        dst[...] = src[...].astype(BF16)

    @pl.when(i == 0)
    def _():
        wk_ref[...] = wlo_ref[COL_KB:COL_KB + FOX_W].T.astype(BF16)
        wkf_ref[:, :SWA_KV] = wlo_ref[COL_KA:COL_KA + SWA_KV].T.astype(BF16)
        second = jnp.concatenate([whi_ref[:gate_shift], whi_ref[2 * D_MODEL:2 * D_MODEL + gate_shift],
                                  jnp.zeros((LANES - 2 * gate_shift, D_MODEL), F32)], axis=0)
        wkf_ref[:, SWA_KV:] = second.T.astype(BF16)
        wg_ref[...] = whi_ref[:2 * D_MODEL].T.astype(BF16)
        wt_ref[:SWA_Q] = wlo_ref[:SWA_Q].astype(BF16)
        wt_ref[SWA_Q:SWA_Q + FOX_W] = wlo_ref[COL_QB:COL_QB + FOX_W].astype(BF16)
        pad_rows = jnp.zeros((VT_ROWS - HEAD_DIM, D_MODEL), BF16)
        v_rows = [COL_VA + g * HEAD_DIM for g in range(SWA_KV_HEADS)] + [COL_VB + h * HEAD_DIM for h in range(FOX_HEADS)]
        for head, row in enumerate(v_rows):
            base = SWA_Q + FOX_W + head * VT_ROWS
            wt_ref[base:base + HEAD_DIM] = wlo_ref[row:row + HEAD_DIM].astype(BF16)
            wt_ref[base + HEAD_DIM:base + VT_ROWS] = pad_rows

    @pl.when(i % tiles_per_seq == 0)
    def _():
        carry_ref[...] = jnp.zeros_like(carry_ref)

    u = _rms(x_ref[...], g_ref[...]).astype(BF16)
    tm = u.shape[0]
    kf = jnp.dot(u, wkf_ref[...], preferred_element_type=F32)
    ka_ref[...] = kf[:, :SWA_KV].astype(BF16)
    f = kf[:, SWA_KV:] + bf_ref[...]
    logf = jnp.minimum(f, 0.0) - jnp.log1p(jnp.exp(-jnp.abs(f)))

    gl = jnp.dot(u, wg_ref[...], preferred_element_type=F32)
    n_tiles = gl.shape[1] // LANES
    for c in range(n_tiles):
        cur = pltpu.roll(gl[:, c * LANES:(c + 1) * LANES], LANES - gate_shift, 1)
        if c + 1 < n_tiles:
            nxt = pltpu.roll(gl[:, (c + 1) * LANES:(c + 2) * LANES], LANES - gate_shift, 1)
        else:
            nxt = pltpu.roll(kf[:, SWA_KV:], LANES - 2 * gate_shift, 1)
        tile = jnp.where(lane < LANES - gate_shift, cur, nxt)
        gates_ref[:, c * LANES:(c + 1) * LANES] = jax.nn.sigmoid(tile).astype(BF16)

    row = lax.broadcasted_iota(jnp.int32, (tm, tm), 0)
    col = lax.broadcasted_iota(jnp.int32, (tm, tm), 1)
    tri = (col <= row).astype(BF16)
    cs = jnp.dot(tri, jnp.concatenate(_split3(logf), axis=1), preferred_element_type=F32)
    c = cs[:, :LANES] + cs[:, LANES:2 * LANES] + cs[:, 2 * LANES:] + carry_ref[...]
    carry_ref[...] = c[tm - 1:tm, :]
    cmin_ref[...] = jnp.min(c * LOG2E, axis=0, keepdims=True)

    tt = lax.dot_general(wt_ref[...], u, NT_DIMS, preferred_element_type=F32)
    qat_ref[...] = (tt[:SWA_Q] * (SCALE * LOG2E)).astype(BF16)
    qt = (tt[SWA_Q:SWA_Q + FOX_W] * (SCALE * LOG2E)).astype(BF16)
    vt = (tt[SWA_Q + FOX_W:] + vone_ref[...]).astype(BF16)
    vta_ref[...] = vt[:SWA_KV_HEADS * VT_ROWS]
    vtb_ref[...] = vt[SWA_KV_HEADS * VT_ROWS:]

    parts = jnp.concatenate(_split3(c * LOG2E), axis=1)
    kbias = jnp.dot(parts, pk_ref[...], preferred_element_type=F32) + ok_ref[...]
    qbias_t = lax.dot_general(pqt_ref[...], parts, NT_DIMS, preferred_element_type=F32) + oqt_ref[...]
    zero_rows = jnp.zeros((HEAD_DIM - BIAS_SLOT, tm), F32)
    for h in range(FOX_HEADS):
        q3t_ref[h, :HEAD_DIM, :] = qt[h * HEAD_DIM:(h + 1) * HEAD_DIM]
        own = jnp.concatenate([qbias_t[h * BIAS_SLOT:(h + 1) * BIAS_SLOT], zero_rows], axis=0)
        q3t_ref[h, HEAD_DIM:, :] = own.astype(BF16)

    k = jnp.dot(u, wk_ref[...], preferred_element_type=F32)
    k_sq = jnp.square(k.astype(BF16).astype(F32)).astype(BF16)
    knorm_ref[...] = NORM_SLACK * jnp.max(jnp.dot(k_sq, sel_ref[...], preferred_element_type=F32), axis=0, keepdims=True)
    lane = lax.broadcasted_iota(jnp.int32, (1, LANES), 1)
    for h in range(FOX_HEADS):
        pair = k[:, (h // 2) * LANES:(h // 2 + 1) * LANES]
        dims = pair if h % 2 == 0 else pltpu.roll(pair, HEAD_DIM, 1)
        bias = pltpu.roll(kbias, HEAD_DIM - h * BIAS_SLOT, 1)
        k3_ref[:, h * FOX_DEPTH:(h + 1) * FOX_DEPTH] = jnp.where(lane < HEAD_DIM, dims, bias).astype(BF16)


def _in_proj(x2, g_mix, w_all, bf_pad, batch, seq, later_weights):
    t = x2.shape[0]
    tps = seq // TM
    steps = t // TM
    (place_qt, ones_qt), (place_k, ones_k) = _bias_placement()
    n_vt = SWA_KV_HEADS + FOX_HEADS
    vone = np.zeros((n_vt * VT_ROWS, 1), np.float32)
    vone[HEAD_DIM::VT_ROWS] = 1.0
    head_of_lane = np.zeros((FOX_W, LANES), np.float32)
    head_of_lane[np.arange(FOX_W), np.arange(FOX_W) // HEAD_DIM] = 1.0
    consts = (jnp.asarray(place_qt, BF16), jnp.asarray(ones_qt), jnp.asarray(place_k, BF16), jnp.asarray(ones_k),
              jnp.asarray(vone), jnp.asarray(head_of_lane, BF16))
    kern = functools.partial(_in_proj_kernel, tiles_per_seq=tps, n_cast=len(later_weights))
    row_spec = lambda w: pl.BlockSpec((TM, w), lambda i: (i, 0))
    col_spec = lambda *lead: pl.BlockSpec((None,) + lead + (TM,), lambda i: (i // tps,) + (0,) * len(lead) + (i % tps,))
    for w in later_weights:
        assert w.ndim == 2 and w.shape[0] % (steps * BF16_SUBLANES) == 0, w.shape
    slab_specs = [pl.BlockSpec((w.shape[0] // steps, w.shape[1]), lambda i: (i, 0)) for w in later_weights]
    assert w_all.shape == (COL_G + 2 * D_MODEL, D_MODEL) and COL_F >= 2 * D_MODEL + LANES and COL_F % BF16_SUBLANES == 0
    half_specs = [pl.BlockSpec((COL_F, D_MODEL), lambda i, j=j: (j, 0), pipeline_mode=pl.Buffered(1)) for j in range(2)]
    tail = (bf_pad,) + consts
    ka, k3, gates = (jax.ShapeDtypeStruct((t, w), BF16) for w in (SWA_KV, FOX_HEADS * FOX_DEPTH, 2 * D_MODEL))
    qat = jax.ShapeDtypeStruct((batch, SWA_Q, seq), BF16)
    stat = jax.ShapeDtypeStruct((steps, 1, LANES), F32)
    stat_spec = pl.BlockSpec((None, 1, LANES), lambda i: (i, 0, 0))
    q3t = jax.ShapeDtypeStruct((batch, FOX_HEADS, FOX_DEPTH, seq), BF16)
    vta, vtb = (jax.ShapeDtypeStruct((batch, heads * VT_ROWS, seq), BF16) for heads in (SWA_KV_HEADS, FOX_HEADS))
    return pl.pallas_call(
        kern,
        grid=(steps,),
        in_specs=([row_spec(D_MODEL), _const_spec(g_mix.shape)] + half_specs
                  + [_const_spec(a.shape) for a in tail] + slab_specs),
        out_specs=[col_spec(SWA_Q), row_spec(SWA_KV), col_spec(SWA_KV_HEADS * VT_ROWS),
                   col_spec(FOX_HEADS, FOX_DEPTH), row_spec(FOX_HEADS * FOX_DEPTH), col_spec(FOX_HEADS * VT_ROWS),
                   row_spec(2 * D_MODEL), stat_spec, stat_spec] + slab_specs,
        out_shape=([qat, ka, vta, q3t, k3, vtb, gates, stat, stat]
                   + [jax.ShapeDtypeStruct(w.shape, BF16) for w in later_weights]),
        scratch_shapes=[pltpu.VMEM((1, LANES), F32),
                        pltpu.VMEM((SWA_Q + FOX_W + n_vt * VT_ROWS, D_MODEL), BF16),
                        pltpu.VMEM((D_MODEL, SWA_KV + LANES), BF16),
                        pltpu.VMEM((D_MODEL, FOX_W), BF16),
                        pltpu.VMEM((D_MODEL, 2 * D_MODEL), BF16)],
        compiler_params=pltpu.CompilerParams(dimension_semantics=("arbitrary",), vmem_limit_bytes=IN_PROJ_VMEM_LIMIT),
        name="in_proj",
    )(x2, g_mix, w_all, w_all, *tail, *later_weights)


def _swa_bias_table():
    sb = SWA_BLOCK
    qi = np.arange(sb)[None, :] + sb
    si = np.arange(2 * sb)[:, None]
    chunk_diff = qi // CHUNK - si // CHUNK
    band_ok = (chunk_diff >= 0) & (chunk_diff <= WINDOW // CHUNK)
    slopes = np.array([2.0 ** (-8.0 * (h + 1) / SWA_HEADS) for h in range(SWA_HEADS)], dtype=np.float32)
    alibi = -slopes[:, None, None] * np.abs(qi - si).astype(np.float32)[None] * np.float32(LOG2E)
    first = band_ok & (si >= sb)
    table = np.stack([np.where(first[None], alibi, NEG_BIG), np.where(band_ok[None], alibi, NEG_BIG)])
    table = table.reshape(2, SWA_KV_HEADS, SWA_GROUP, 2 * sb, sb).transpose(0, 1, 3, 2, 4)
    return np.ascontiguousarray(table.reshape(2, SWA_KV_HEADS, 2 * sb, SWA_GROUP * sb)).astype(np.float32)


def _swa_kernel(sink_ref, qt_ref, kp_ref, kc_ref, vtp_ref, vtc_ref, bias_first_ref, bias_rest_ref, o_ref,
                st_sc, mb_sc, ot_sc):
    sb = SWA_BLOCK
    per_unit = MXU_COLS // sb
    half = jnp.zeros((HEAD_DIM, MXU_COLS), BF16)
    units = [(j, kh, c) for j in range(SWA_TILE // sb) for kh in range(SWA_KV_HEADS)
             for c in range(SWA_GROUP // per_unit)]

    def band(j):
        if j == 0:
            kb = jnp.concatenate([kp_ref[...], kc_ref[0:sb, :]], axis=0)
            vtb = jnp.concatenate([vtp_ref[...], vtc_ref[:, 0:sb]], axis=1)
            return kb, vtb, bias_first_ref
        return kc_ref[(j - 1) * sb:(j + 1) * sb, :], vtc_ref[:, (j - 1) * sb:(j + 1) * sb], bias_rest_ref

    def unit_heads(kh, c):
        return [kh * SWA_GROUP + c * per_unit + g for g in range(per_unit)]

    def scores(i, slot):
        j, kh, c = units[i]
        kb, _, bias_ref = band(j)
        q2 = jnp.concatenate([qt_ref[h * HEAD_DIM:(h + 1) * HEAD_DIM, j * sb:(j + 1) * sb] for h in unit_heads(kh, c)],
                             axis=1)
        q2t = jnp.concatenate([q2, half] if kh == 0 else [half, q2], axis=0)
        st = jnp.dot(kb, q2t, preferred_element_type=F32) + bias_ref[kh, :, c * MXU_COLS:(c + 1) * MXU_COLS]
        st_sc[slot] = st
        mb_sc[slot] = jnp.max(st, axis=0, keepdims=True)

    def consume(unit, slot):
        j, kh, c = unit
        _, vtb, _ = band(j)
        heads = unit_heads(kh, c)
        sink = jnp.concatenate([jnp.full((1, sb), sink_ref[0, h] * LOG2E, F32) for h in heads], axis=1)
        m = jnp.maximum(mb_sc[slot], sink)
        p = jnp.exp2(st_sc[slot] - m).astype(BF16)
        pv = jnp.dot(vtb[kh * VT_ROWS:(kh + 1) * VT_ROWS, :], p, preferred_element_type=F32)
        denom = pv[HEAD_DIM:HEAD_DIM + 1, :] + jnp.exp2(sink - m)
        o = pv[:HEAD_DIM, :] / denom
        for g, h in enumerate(heads):
            ot_sc[h * HEAD_DIM:(h + 1) * HEAD_DIM, j * sb:(j + 1) * sb] = o[:, g * sb:(g + 1) * sb]

    slots = st_sc.shape[0]
    for i in range(slots - 1):
        scores(i, i)
    for i, unit in enumerate(units):
        ahead = i + slots - 1
        if ahead < len(units):
            scores(ahead, ahead % slots)
        consume(unit, i % slots)
    o_ref[...] = ot_sc[...].T.astype(BF16)


def _swa(qt3, k3, vt3, sinks, bias):
    b, s, _ = k3.shape
    sb, ts = SWA_BLOCK, SWA_TILE
    per = ts // sb
    prev = lambda n: jnp.maximum(n * per - 1, 0)
    bias_block = (None,) + bias.shape[1:]
    return pl.pallas_call(
        _swa_kernel,
        grid=(b, s // ts),
        in_specs=[pl.BlockSpec(memory_space=pltpu.SMEM),
                  pl.BlockSpec((None, SWA_Q, ts), lambda bi, n: (bi, 0, n)),
                  pl.BlockSpec((None, sb, SWA_KV), lambda bi, n: (bi, prev(n), 0)),
                  pl.BlockSpec((None, ts, SWA_KV), lambda bi, n: (bi, n, 0)),
                  pl.BlockSpec((None, SWA_KV_HEADS * VT_ROWS, sb), lambda bi, n: (bi, 0, prev(n))),
                  pl.BlockSpec((None, SWA_KV_HEADS * VT_ROWS, ts), lambda bi, n: (bi, 0, n)),
                  pl.BlockSpec(bias_block, lambda bi, n: (jnp.minimum(n, 1), 0, 0, 0)),
                  pl.BlockSpec(bias_block, lambda bi, n: (1, 0, 0, 0))],
        out_specs=pl.BlockSpec((None, ts, SWA_Q), lambda bi, n: (bi, n, 0)),
        out_shape=jax.ShapeDtypeStruct((b, s, SWA_Q), BF16),
        scratch_shapes=[pltpu.VMEM((SWA_SLOTS, 2 * sb, MXU_COLS), F32),
                        pltpu.VMEM((SWA_SLOTS, 1, MXU_COLS), F32),
                        pltpu.VMEM((SWA_Q, ts), F32)],
        compiler_params=pltpu.CompilerParams(dimension_semantics=("parallel", "parallel"),
                                             vmem_limit_bytes=VMEM_LIMIT),
        name="swa",
    )(sinks, qt3, k3, k3, vt3, vt3, bias, bias)


def _fox_kernel(qt_ref, k_ref, vt_ref, knorm_ref, cmin_ref, o_ref, st_sc, mb_sc, m_sc, acc_sc):
    tq = qt_ref.shape[2]
    hp = pl.program_id(1)
    qi = pl.program_id(2)
    m_sc[...] = jnp.full_like(m_sc, NEG_BIG)
    acc_sc[...] = jnp.zeros_like(acc_sc)

    units = [(hh, slice(c * MXU_COLS, (c + 1) * MXU_COLS)) for hh in range(2) for c in range(tq // MXU_COLS)]

    def scores(block, slot, unit, diagonal):
        hh, cols = unit
        keys = cols.stop if diagonal else tq
        start = pl.multiple_of(block * tq, tq)
        k3 = k_ref[pl.ds(start, keys), hh * FOX_DEPTH:(hh + 1) * FOX_DEPTH]
        st = jnp.dot(k3, qt_ref[hh, :, cols], preferred_element_type=F32)
        if diagonal:
            key = lax.broadcasted_iota(jnp.int32, st.shape, 0)
            qry = lax.broadcasted_iota(jnp.int32, st.shape, 1) + cols.start
            st = jnp.where(key <= qry, st, NEG_BIG)
        st_sc[slot, hh, :keys, cols] = st
        mb_sc[slot, hh, :, cols] = jnp.max(st, axis=0, keepdims=True)

    def consume(block, slot, unit, diagonal=False):
        hh, cols = unit
        keys = cols.stop if diagonal else tq
        start = pl.multiple_of(block * tq, tq)
        m_prev = m_sc[hh, :, cols]
        m_new = jnp.maximum(m_prev, mb_sc[slot, hh, :, cols])
        p = jnp.exp2(st_sc[slot, hh, :keys, cols] - m_new).astype(BF16)
        alpha = jnp.exp2(m_prev - m_new)
        pv = jnp.dot(vt_ref[hh, :, pl.ds(start, keys)], p, preferred_element_type=F32)
        acc_sc[hh, :, cols] = alpha * acc_sc[hh, :, cols] + pv
        m_sc[hh, :, cols] = m_new

    def stage(next_block, next_slot, block, slot, diagonal=False):
        for unit in units:
            scores(next_block, next_slot, unit, False)
            consume(block, slot, unit, diagonal)

    def skippable_blocks(hh):
        lane = lax.broadcasted_iota(jnp.int32, (1, LANES), 1)
        q32 = qt_ref[hh, :HEAD_DIM, :].astype(F32)
        q_sq = jnp.max(jnp.sum(q32 * q32, axis=0, keepdims=True), axis=1, keepdims=True)
        c_q = jnp.sum(qt_ref[hh, HEAD_DIM:HEAD_DIM + C_PARTS, :].astype(F32), axis=0, keepdims=True)
        c_first = jnp.max(c_q, axis=1, keepdims=True)
        m_min = jnp.min(mb_sc[0, hh], axis=1, keepdims=True)
        k_sq = jnp.zeros((1, LANES), F32)
        c_last = jnp.full((1, LANES), -NEG_BIG, F32)
        count = jnp.zeros((1, LANES), jnp.int32)
        for j in range(knorm_ref.shape[0]):
            k_sq = jnp.maximum(k_sq, knorm_ref[j])
            c_last = jnp.minimum(c_last, cmin_ref[j])
            room = (m_min - FOX_SKIP_MARGIN) - (c_first - c_last)
            zero = (room > 0.0) & (q_sq * k_sq < room * room) & (j < qi)
            count = count + zero.astype(jnp.int32)
        return jnp.sum(jnp.where(lane == 2 * hp + hh, count, 0))

    for unit in units:
        scores(qi, 0, unit, True)
    first = jnp.minimum(skippable_blocks(0), skippable_blocks(1))
    nearest = jnp.maximum(qi - 1, 0)
    stage(nearest, 1, qi, 0, diagonal=True)
    rest = jnp.maximum(qi - 1 - first, 0)

    @pl.when((rest == 0) & (qi >= 1))
    def _():
        for unit in units:
            consume(nearest, 1, unit)

    @pl.when(rest >= 1)
    def _():
        stage(first, 0, nearest, 1)
        stage(jnp.minimum(first + 1, qi), 1, first, 0)

    start = first + 1
    count = jnp.maximum(rest - 1, 0)

    def block_pair(b0):
        stage(b0 + 1, 0, b0, 1)
        stage(jnp.minimum(b0 + 2, qi), 1, b0 + 1, 0)

    def block_oct(t, carry):
        for pair in range(4):
            block_pair(start + 8 * t + 2 * pair)
        return carry

    lax.fori_loop(0, count // 8, block_oct, 0)

    @pl.when(count % 8 >= 4)
    def _():
        block_pair(start + count // 8 * 8)
        block_pair(start + count // 8 * 8 + 2)

    @pl.when(count % 4 >= 2)
    def _():
        block_pair(start + count // 4 * 4)

    @pl.when(count % 2 == 1)
    def _():
        for unit in units:
            consume(qi - 2, 1, unit)

    outs = [acc_sc[hh, :HEAD_DIM, :] / acc_sc[hh, HEAD_DIM:HEAD_DIM + 1, :] for hh in range(2)]
    o_ref[...] = jnp.concatenate(outs, axis=0).T.astype(BF16)


def _fox(q3t, k3, vt4, knorm, cmin):
    b, s, _ = k3.shape
    tq = FOX_TQ
    assert knorm.shape == cmin.shape == (b, s // tq, 1, LANES)
    stat_spec = pl.BlockSpec((None, s // tq, 1, LANES), lambda bi, hp, qi: (bi, 0, 0, 0))
    return pl.pallas_call(
        _fox_kernel,
        grid=(b, FOX_PAIRS, s // tq),
        in_specs=[pl.BlockSpec((None, 2, FOX_DEPTH, tq), lambda bi, hp, qi: (bi, hp, 0, qi)),
                  pl.BlockSpec((None, s, 2 * FOX_DEPTH), lambda bi, hp, qi: (bi, 0, hp)),
                  pl.BlockSpec((None, 2, VT_ROWS, s), lambda bi, hp, qi: (bi, hp, 0, 0)),
                  stat_spec, stat_spec],
        out_specs=pl.BlockSpec((None, tq, LANES), lambda bi, hp, qi: (bi, qi, hp)),
        out_shape=jax.ShapeDtypeStruct((b, s, FOX_W), BF16),
        scratch_shapes=[pltpu.VMEM((2, 2, tq, tq), F32),
                        pltpu.VMEM((2, 2, 1, tq), F32),
                        pltpu.VMEM((2, 1, tq), F32),
                        pltpu.VMEM((2, VT_ROWS, tq), F32)],
        compiler_params=pltpu.CompilerParams(dimension_semantics=("parallel", "parallel", "parallel"),
                                             vmem_limit_bytes=VMEM_LIMIT),
        name="fox",
    )(q3t, k3, vt4, knorm, cmin)


def _mix_kernel(x_ref, ya_ref, yb_ref, gates_ref, wa_ref, wb_ref, wo_ref, h_ref):
    slabs = [slice(r, r + TM) for r in range(0, MIX_TM, TM)]

    def branches(rows):
        y_a = jnp.dot(ya_ref[rows, :], wa_ref[...], preferred_element_type=F32)
        y_b = jnp.dot(yb_ref[rows, :], wb_ref[...], preferred_element_type=F32)
        return y_a, y_b

    pending = branches(slabs[0])
    for n, rows in enumerate(slabs):
        y_a, y_b = pending
        if n + 1 < len(slabs):
            pending = branches(slabs[n + 1])
        gates = gates_ref[rows, :].astype(F32)
        mixed = gates[:, :D_MODEL] * y_a + gates[:, D_MODEL:] * y_b
        h_ref[rows, :] = x_ref[rows, :] + jnp.dot(mixed.astype(BF16), wo_ref[...], preferred_element_type=F32)


def _mix(x2, att_a, att_b, gates, w_a, w_b, w_o):
    t = x2.shape[0]
    assert t % MIX_TM == 0 and MIX_TM % TM == 0
    row_spec = lambda w: pl.BlockSpec((MIX_TM, w), lambda i: (i, 0))
    return pl.pallas_call(
        _mix_kernel,
        grid=(t // MIX_TM,),
        in_specs=[row_spec(D_MODEL), row_spec(SWA_Q), row_spec(FOX_W), row_spec(2 * D_MODEL),
                  _const_spec(w_a.shape), _const_spec(w_b.shape), _const_spec(w_o.shape)],
        out_specs=row_spec(D_MODEL),
        out_shape=jax.ShapeDtypeStruct((t, D_MODEL), F32),
        compiler_params=pltpu.CompilerParams(dimension_semantics=("parallel",), vmem_limit_bytes=VMEM_LIMIT),
        name="mix",
    )(x2, att_a, att_b, gates, w_a, w_b, w_o)


def _mlp_ple_kernel(h_ref, p_ref, gm_ref, w1_ref, w2_ref, gp_ref, wpg_ref, wpp_ref, gf_ref, o_ref):
    h = h_ref[...]
    u = _rms(h, gm_ref[...]).astype(BF16)
    acc = jnp.zeros_like(h)
    for c in range(D_FF // FF_CHUNK):
        cols = slice(c * FF_CHUNK, (c + 1) * FF_CHUNK)
        a = jnp.dot(u, w1_ref[:, cols], preferred_element_type=F32)
        a = jnp.square(jnp.maximum(a, 0.0)).astype(BF16)
        acc = acc + jnp.dot(a, w2_ref[cols, :], preferred_element_type=F32)
    h = h + acc
    gate = jax.nn.sigmoid(jnp.dot(_rms(h, gp_ref[...]).astype(BF16), wpg_ref[...], preferred_element_type=F32))
    proj = jnp.dot(p_ref[...].astype(BF16), wpp_ref[...], preferred_element_type=F32)
    h = h + gate * proj
    o_ref[...] = _rms(h, gf_ref[...])


def _mlp_ple(h2, p2, g_mlp, w1, w2, g_ple, w_pg, w_pp, g_final):
    t = h2.shape[0]
    row_spec = lambda w: pl.BlockSpec((TM, w), lambda i: (i, 0))
    vec = _const_spec((1, D_MODEL))
    return pl.pallas_call(
        _mlp_ple_kernel,
        grid=(t // TM,),
        in_specs=[row_spec(D_MODEL), row_spec(PLE_DIM), vec, _const_spec(w1.shape), _const_spec(w2.shape),
                  vec, _const_spec(w_pg.shape), _const_spec(w_pp.shape), vec],
        out_specs=row_spec(D_MODEL),
        out_shape=jax.ShapeDtypeStruct((t, D_MODEL), F32),
        compiler_params=pltpu.CompilerParams(dimension_semantics=("parallel",), vmem_limit_bytes=VMEM_LIMIT),
        name="mlp_ple",
    )(h2, p2, g_mlp, w1, w2, g_ple, w_pg, w_pp, g_final)


def kernel(x, p, g_mix, w_in, b_forget, swa_sinks, w_br_swa, w_br_fox, w_mix_out,
           g_mlp, w_ff1, w_ff2, g_ple, w_ple_gate, w_ple_proj, g_final):
    b, s, d = x.shape
    assert d == D_MODEL and w_in.shape[0] == 1, "single-layer trunk with D_MODEL channels only"
    assert s % TM == 0 and FOX_TQ == TM and s % SWA_TILE == 0 and SWA_TILE % SWA_BLOCK == 0
    t = b * s
    x2 = x.reshape(t, d)

    w = jnp.swapaxes(w_in, 1, 2)[0]
    bf_pad = jnp.pad(b_forget[0], (0, LANES - FOX_HEADS)).reshape(1, LANES)

    later = (w_br_swa[0], w_br_fox[0], w_mix_out[0], w_ff1[0], w_ff2[0], w_ple_gate[0])
    qat, ka, vta, q3t, k3, vtb, gates, knorm, cmin, *later_bf16 = _in_proj(x2, g_mix[0].reshape(1, d), w, bf_pad, b, s, later)
    wa_b, wb_b, wo_b, w1_b, w2_b, wpg_b = later_bf16

    att_a = _swa(qat, ka.reshape(b, s, SWA_KV), vta, swa_sinks[0].reshape(1, SWA_HEADS),
                 jnp.asarray(_swa_bias_table()))
    blocks = (b, s // FOX_TQ, 1, LANES)
    att_b = _fox(q3t, k3.reshape(b, s, FOX_HEADS * FOX_DEPTH), vtb.reshape(b, FOX_HEADS, VT_ROWS, s),
                 knorm.reshape(blocks), cmin.reshape(blocks))

    h = _mix(x2, att_a.reshape(t, SWA_Q), att_b.reshape(t, FOX_W), gates, wa_b, wb_b, wo_b)
    out = _mlp_ple(h, p[0].reshape(t, PLE_DIM), g_mlp[0].reshape(1, d), w1_b, w2_b,
                   g_ple[0].reshape(1, d), wpg_b, w_ple_proj[0].astype(BF16), g_final.reshape(1, d))
    return out.reshape(b, s, d)
```

```python
import functools

import numpy as np
import jax
import jax.numpy as jnp
from jax import lax
from jax.experimental import pallas as pl
from jax.experimental.pallas import tpu as pltpu

D_MODEL = 1024
CHUNK = 64
PLE_DIM = 256
HEAD_DIM = 64
SWA_HEADS = 8
SWA_KV_HEADS = 2
SWA_GROUP = SWA_HEADS // SWA_KV_HEADS
WINDOW = 128
SWA_BLOCK = WINDOW
FOX_HEADS = 8
D_FF = 4 * D_MODEL
RMS_EPS = 1e-6
SWA_Q = SWA_HEADS * HEAD_DIM
SWA_KV = SWA_KV_HEADS * HEAD_DIM
FOX_W = FOX_HEADS * HEAD_DIM
COL_KA = SWA_Q
COL_VA = COL_KA + SWA_KV
COL_QB = COL_VA + SWA_KV
COL_KB = COL_QB + FOX_W
COL_VB = COL_KB + FOX_W
COL_F = COL_VB + FOX_W
COL_G = COL_F + FOX_HEADS
SCALE = HEAD_DIM ** -0.5
LOG2E = float(np.log2(np.e))

LANES = 128
BF16_SUBLANES = 16
MXU_COLS = 256
NEG_BIG = -1e30
NORM_SLACK = 1.03
FOX_SKIP_MARGIN = 160.0
VMEM_LIMIT = 52 * 1024 * 1024
IN_PROJ_VMEM_LIMIT = 58 * 1024 * 1024

TM = 512
MIX_TM = 1024
SWA_TILE = 2048
SWA_SLOTS = 6
FOX_TQ = 512
FF_CHUNK = 1024

FOX_PAIRS = FOX_HEADS // 2
FOX_DEPTH = LANES
BIAS_SLOT = 8
C_PARTS = 3
VT_ROWS = 80

F32 = jnp.float32
BF16 = jnp.bfloat16
NT_DIMS = (((1,), (1,)), ((), ()))


def _rms(x, g):
    return x * lax.rsqrt(jnp.mean(x * x, axis=-1, keepdims=True) + RMS_EPS) * g


def _const_spec(shape):
    return pl.BlockSpec(shape, lambda *_: (0,) * len(shape), pipeline_mode=pl.Buffered(1))


def _split3(v):
    hi = v.astype(BF16)
    r1 = v - hi.astype(F32)
    mid = r1.astype(BF16)
    lo = (r1 - mid.astype(F32)).astype(BF16)
    return hi, mid, lo


def _bias_placement():
    place_qt = np.zeros((LANES, C_PARTS * LANES), np.float32)
    ones_qt = np.zeros((LANES, 1), np.float32)
    place_k = np.zeros((C_PARTS * LANES, LANES), np.float32)
    ones_k = np.zeros((1, LANES), np.float32)
    for h in range(FOX_HEADS):
        for part in range(C_PARTS):
            place_qt[h * BIAS_SLOT + part, part * LANES + h] = 1.0
            ones_qt[h * BIAS_SLOT + C_PARTS + part, 0] = 1.0
            ones_k[0, h * BIAS_SLOT + part] = 1.0
            place_k[part * LANES + h, h * BIAS_SLOT + C_PARTS + part] = -1.0
    return (place_qt, ones_qt), (place_k, ones_k)


def _in_proj_kernel(*refs, tiles_per_seq, n_cast):
    (x_ref, g_ref, wlo_ref, whi_ref, bf_ref, pqt_ref, oqt_ref, pk_ref, ok_ref, vone_ref, sel_ref) = refs[:11]
    cast_in = refs[11:11 + n_cast]
    (qat_ref, ka_ref, vta_ref, q3t_ref, k3_ref, vtb_ref, gates_ref, knorm_ref, cmin_ref) = refs[11 + n_cast:20 + n_cast]
    cast_out = refs[20 + n_cast:20 + 2 * n_cast]
    carry_ref, wt_ref, wkf_ref, wk_ref, wg_ref = refs[20 + 2 * n_cast:]
    i = pl.program_id(0)
    lane = lax.broadcasted_iota(jnp.int32, (1, LANES), 1)
    gate_shift = COL_G % LANES
    for src, dst in zip(cast_in, cast_out):
        dst[...] = src[...].astype(BF16)

    @pl.when(i == 0)
    def _():
        wk_ref[...] = wlo_ref[COL_KB:COL_KB + FOX_W].T.astype(BF16)
        wkf_ref[:, :SWA_KV] = wlo_ref[COL_KA:COL_KA + SWA_KV].T.astype(BF16)
        second = jnp.concatenate([whi_ref[:gate_shift], whi_ref[2 * D_MODEL:2 * D_MODEL + gate_shift],
                                  jnp.zeros((LANES - 2 * gate_shift, D_MODEL), F32)], axis=0)
        wkf_ref[:, SWA_KV:] = second.T.astype(BF16)
        wg_ref[...] = whi_ref[:2 * D_MODEL].T.astype(BF16)
        wt_ref[:SWA_Q] = wlo_ref[:SWA_Q].astype(BF16)
        wt_ref[SWA_Q:SWA_Q + FOX_W] = wlo_ref[COL_QB:COL_QB + FOX_W].astype(BF16)
        pad_rows = jnp.zeros((VT_ROWS - HEAD_DIM, D_MODEL), BF16)
        v_rows = [COL_VA + g * HEAD_DIM for g in range(SWA_KV_HEADS)] + [COL_VB + h * HEAD_DIM for h in range(FOX_HEADS)]
        for head, row in enumerate(v_rows):
            base = SWA_Q + FOX_W + head * VT_ROWS
            wt_ref[base:base + HEAD_DIM] = wlo_ref[row:row + HEAD_DIM].astype(BF16)
            wt_ref[base + HEAD_DIM:base + VT_ROWS] = pad_rows

    @pl.when(i % tiles_per_seq == 0)
    def _():
        carry_ref[...] = jnp.zeros_like(carry_ref)

    u = _rms(x_ref[...], g_ref[...]).astype(BF16)
    tm = u.shape[0]
    kf = jnp.dot(u, wkf_ref[...], preferred_element_type=F32)
    ka_ref[...] = kf[:, :SWA_KV].astype(BF16)
    f = kf[:, SWA_KV:] + bf_ref[...]
    logf = jnp.minimum(f, 0.0) - jnp.log1p(jnp.exp(-jnp.abs(f)))

    gl = jnp.dot(u, wg_ref[...], preferred_element_type=F32)
    n_tiles = gl.shape[1] // LANES
    for c in range(n_tiles):
        cur = pltpu.roll(gl[:, c * LANES:(c + 1) * LANES], LANES - gate_shift, 1)
        if c + 1 < n_tiles:
            nxt = pltpu.roll(gl[:, (c + 1) * LANES:(c + 2) * LANES], LANES - gate_shift, 1)
        else:
            nxt = pltpu.roll(kf[:, SWA_KV:], LANES - 2 * gate_shift, 1)
        tile = jnp.where(lane < LANES - gate_shift, cur, nxt)
        gates_ref[:, c * LANES:(c + 1) * LANES] = jax.nn.sigmoid(tile).astype(BF16)

    row = lax.broadcasted_iota(jnp.int32, (tm, tm), 0)
    col = lax.broadcasted_iota(jnp.int32, (tm, tm), 1)
    tri = (col <= row).astype(BF16)
    cs = jnp.dot(tri, jnp.concatenate(_split3(logf), axis=1), preferred_element_type=F32)
    c = cs[:, :LANES] + cs[:, LANES:2 * LANES] + cs[:, 2 * LANES:] + carry_ref[...]
    carry_ref[...] = c[tm - 1:tm, :]
    cmin_ref[...] = jnp.min(c * LOG2E, axis=0, keepdims=True)

    tt = lax.dot_general(wt_ref[...], u, NT_DIMS, preferred_element_type=F32)
    qat_ref[...] = (tt[:SWA_Q] * (SCALE * LOG2E)).astype(BF16)
    qt = (tt[SWA_Q:SWA_Q + FOX_W] * (SCALE * LOG2E)).astype(BF16)
    vt = (tt[SWA_Q + FOX_W:] + vone_ref[...]).astype(BF16)
    vta_ref[...] = vt[:SWA_KV_HEADS * VT_ROWS]
    vtb_ref[...] = vt[SWA_KV_HEADS * VT_ROWS:]

    parts = jnp.concatenate(_split3(c * LOG2E), axis=1)
    kbias = jnp.dot(parts, pk_ref[...], preferred_element_type=F32) + ok_ref[...]
    qbias_t = lax.dot_general(pqt_ref[...], parts, NT_DIMS, preferred_element_type=F32) + oqt_ref[...]
    zero_rows = jnp.zeros((HEAD_DIM - BIAS_SLOT, tm), F32)
    for h in range(FOX_HEADS):
        q3t_ref[h, :HEAD_DIM, :] = qt[h * HEAD_DIM:(h + 1) * HEAD_DIM]
        own = jnp.concatenate([qbias_t[h * BIAS_SLOT:(h + 1) * BIAS_SLOT], zero_rows], axis=0)
        q3t_ref[h, HEAD_DIM:, :] = own.astype(BF16)

    k = jnp.dot(u, wk_ref[...], preferred_element_type=F32)
    k_sq = jnp.square(k.astype(BF16).astype(F32)).astype(BF16)
    knorm_ref[...] = NORM_SLACK * jnp.max(jnp.dot(k_sq, sel_ref[...], preferred_element_type=F32), axis=0, keepdims=True)
    lane = lax.broadcasted_iota(jnp.int32, (1, LANES), 1)
    for h in range(FOX_HEADS):
        pair = k[:, (h // 2) * LANES:(h // 2 + 1) * LANES]
        dims = pair if h % 2 == 0 else pltpu.roll(pair, HEAD_DIM, 1)
        bias = pltpu.roll(kbias, HEAD_DIM - h * BIAS_SLOT, 1)
        k3_ref[:, h * FOX_DEPTH:(h + 1) * FOX_DEPTH] = jnp.where(lane < HEAD_DIM, dims, bias).astype(BF16)


def _in_proj(x2, g_mix, w_all, bf_pad, batch, seq, later_weights):
    t = x2.shape[0]
    tps = seq // TM
    steps = t // TM
    (place_qt, ones_qt), (place_k, ones_k) = _bias_placement()
    n_vt = SWA_KV_HEADS + FOX_HEADS
    vone = np.zeros((n_vt * VT_ROWS, 1), np.float32)
    vone[HEAD_DIM::VT_ROWS] = 1.0
    head_of_lane = np.zeros((FOX_W, LANES), np.float32)
    head_of_lane[np.arange(FOX_W), np.arange(FOX_W) // HEAD_DIM] = 1.0
    consts = (jnp.asarray(place_qt, BF16), jnp.asarray(ones_qt), jnp.asarray(place_k, BF16), jnp.asarray(ones_k),
              jnp.asarray(vone), jnp.asarray(head_of_lane, BF16))
    kern = functools.partial(_in_proj_kernel, tiles_per_seq=tps, n_cast=len(later_weights))
    row_spec = lambda w: pl.BlockSpec((TM, w), lambda i: (i, 0))
    col_spec = lambda *lead: pl.BlockSpec((None,) + lead + (TM,), lambda i: (i // tps,) + (0,) * len(lead) + (i % tps,))
    for w in later_weights:
        assert w.ndim == 2 and w.shape[0] % (steps * BF16_SUBLANES) == 0, w.shape
    slab_specs = [pl.BlockSpec((w.shape[0] // steps, w.shape[1]), lambda i: (i, 0)) for w in later_weights]
    assert w_all.shape == (COL_G + 2 * D_MODEL, D_MODEL) and COL_F >= 2 * D_MODEL + LANES and COL_F % BF16_SUBLANES == 0
    half_specs = [pl.BlockSpec((COL_F, D_MODEL), lambda i, j=j: (j, 0), pipeline_mode=pl.Buffered(1)) for j in range(2)]
    tail = (bf_pad,) + consts
    ka, k3, gates = (jax.ShapeDtypeStruct((t, w), BF16) for w in (SWA_KV, FOX_HEADS * FOX_DEPTH, 2 * D_MODEL))
    qat = jax.ShapeDtypeStruct((batch, SWA_Q, seq), BF16)
    stat = jax.ShapeDtypeStruct((steps, 1, LANES), F32)
    stat_spec = pl.BlockSpec((None, 1, LANES), lambda i: (i, 0, 0))
    q3t = jax.ShapeDtypeStruct((batch, FOX_HEADS, FOX_DEPTH, seq), BF16)
    vta, vtb = (jax.ShapeDtypeStruct((batch, heads * VT_ROWS, seq), BF16) for heads in (SWA_KV_HEADS, FOX_HEADS))
    return pl.pallas_call(
        kern,
        grid=(steps,),
        in_specs=([row_spec(D_MODEL), _const_spec(g_mix.shape)] + half_specs
                  + [_const_spec(a.shape) for a in tail] + slab_specs),
        out_specs=[col_spec(SWA_Q), row_spec(SWA_KV), col_spec(SWA_KV_HEADS * VT_ROWS),
                   col_spec(FOX_HEADS, FOX_DEPTH), row_spec(FOX_HEADS * FOX_DEPTH), col_spec(FOX_HEADS * VT_ROWS),
                   row_spec(2 * D_MODEL), stat_spec, stat_spec] + slab_specs,
        out_shape=([qat, ka, vta, q3t, k3, vtb, gates, stat, stat]
                   + [jax.ShapeDtypeStruct(w.shape, BF16) for w in later_weights]),
        scratch_shapes=[pltpu.VMEM((1, LANES), F32),
                        pltpu.VMEM((SWA_Q + FOX_W + n_vt * VT_ROWS, D_MODEL), BF16),
                        pltpu.VMEM((D_MODEL, SWA_KV + LANES), BF16),
                        pltpu.VMEM((D_MODEL, FOX_W), BF16),
                        pltpu.VMEM((D_MODEL, 2 * D_MODEL), BF16)],
        compiler_params=pltpu.CompilerParams(dimension_semantics=("arbitrary",), vmem_limit_bytes=IN_PROJ_VMEM_LIMIT),
        name="in_proj",
    )(x2, g_mix, w_all, w_all, *tail, *later_weights)


def _swa_bias_table():
    sb = SWA_BLOCK
    qi = np.arange(sb)[None, :] + sb
    si = np.arange(2 * sb)[:, None]
    chunk_diff = qi // CHUNK - si // CHUNK
    band_ok = (chunk_diff >= 0) & (chunk_diff <= WINDOW // CHUNK)
    slopes = np.array([2.0 ** (-8.0 * (h + 1) / SWA_HEADS) for h in range(SWA_HEADS)], dtype=np.float32)
    alibi = -slopes[:, None, None] * np.abs(qi - si).astype(np.float32)[None] * np.float32(LOG2E)
    first = band_ok & (si >= sb)
    table = np.stack([np.where(first[None], alibi, NEG_BIG), np.where(band_ok[None], alibi, NEG_BIG)])
    table = table.reshape(2, SWA_KV_HEADS, SWA_GROUP, 2 * sb, sb).transpose(0, 1, 3, 2, 4)
    return np.ascontiguousarray(table.reshape(2, SWA_KV_HEADS, 2 * sb, SWA_GROUP * sb)).astype(np.float32)


def _swa_kernel(sink_ref, qt_ref, kp_ref, kc_ref, vtp_ref, vtc_ref, bias_first_ref, bias_rest_ref, o_ref,
                st_sc, mb_sc, ot_sc):
    sb = SWA_BLOCK
    per_unit = MXU_COLS // sb
    half = jnp.zeros((HEAD_DIM, MXU_COLS), BF16)
    units = [(j, kh, c) for j in range(SWA_TILE // sb) for kh in range(SWA_KV_HEADS)
             for c in range(SWA_GROUP // per_unit)]

    def band(j):
        if j == 0:
            kb = jnp.concatenate([kp_ref[...], kc_ref[0:sb, :]], axis=0)
            vtb = jnp.concatenate([vtp_ref[...], vtc_ref[:, 0:sb]], axis=1)
            return kb, vtb, bias_first_ref
        return kc_ref[(j - 1) * sb:(j + 1) * sb, :], vtc_ref[:, (j - 1) * sb:(j + 1) * sb], bias_rest_ref

    def unit_heads(kh, c):
        return [kh * SWA_GROUP + c * per_unit + g for g in range(per_unit)]

    def scores(i, slot):
        j, kh, c = units[i]
        kb, _, bias_ref = band(j)
        q2 = jnp.concatenate([qt_ref[h * HEAD_DIM:(h + 1) * HEAD_DIM, j * sb:(j + 1) * sb] for h in unit_heads(kh, c)],
                             axis=1)
        q2t = jnp.concatenate([q2, half] if kh == 0 else [half, q2], axis=0)
        st = jnp.dot(kb, q2t, preferred_element_type=F32) + bias_ref[kh, :, c * MXU_COLS:(c + 1) * MXU_COLS]
        st_sc[slot] = st
        mb_sc[slot] = jnp.max(st, axis=0, keepdims=True)

    def consume(unit, slot):
        j, kh, c = unit
        _, vtb, _ = band(j)
        heads = unit_heads(kh, c)
        sink = jnp.concatenate([jnp.full((1, sb), sink_ref[0, h] * LOG2E, F32) for h in heads], axis=1)
        m = jnp.maximum(mb_sc[slot], sink)
        p = jnp.exp2(st_sc[slot] - m).astype(BF16)
        pv = jnp.dot(vtb[kh * VT_ROWS:(kh + 1) * VT_ROWS, :], p, preferred_element_type=F32)
        denom = pv[HEAD_DIM:HEAD_DIM + 1, :] + jnp.exp2(sink - m)
        o = pv[:HEAD_DIM, :] / denom
        for g, h in enumerate(heads):
            ot_sc[h * HEAD_DIM:(h + 1) * HEAD_DIM, j * sb:(j + 1) * sb] = o[:, g * sb:(g + 1) * sb]

    slots = st_sc.shape[0]
    for i in range(slots - 1):
        scores(i, i)
    for i, unit in enumerate(units):
        ahead = i + slots - 1
        if ahead < len(units):
            scores(ahead, ahead % slots)
        consume(unit, i % slots)
    o_ref[...] = ot_sc[...].T.astype(BF16)


def _swa(qt3, k3, vt3, sinks, bias):
    b, s, _ = k3.shape
    sb, ts = SWA_BLOCK, SWA_TILE
    per = ts // sb
    prev = lambda n: jnp.maximum(n * per - 1, 0)
    bias_block = (None,) + bias.shape[1:]
    return pl.pallas_call(
        _swa_kernel,
        grid=(b, s // ts),
        in_specs=[pl.BlockSpec(memory_space=pltpu.SMEM),
                  pl.BlockSpec((None, SWA_Q, ts), lambda bi, n: (bi, 0, n)),
                  pl.BlockSpec((None, sb, SWA_KV), lambda bi, n: (bi, prev(n), 0)),
                  pl.BlockSpec((None, ts, SWA_KV), lambda bi, n: (bi, n, 0)),
                  pl.BlockSpec((None, SWA_KV_HEADS * VT_ROWS, sb), lambda bi, n: (bi, 0, prev(n))),
                  pl.BlockSpec((None, SWA_KV_HEADS * VT_ROWS, ts), lambda bi, n: (bi, 0, n)),
                  pl.BlockSpec(bias_block, lambda bi, n: (jnp.minimum(n, 1), 0, 0, 0)),
                  pl.BlockSpec(bias_block, lambda bi, n: (1, 0, 0, 0))],
        out_specs=pl.BlockSpec((None, ts, SWA_Q), lambda bi, n: (bi, n, 0)),
        out_shape=jax.ShapeDtypeStruct((b, s, SWA_Q), BF16),
        scratch_shapes=[pltpu.VMEM((SWA_SLOTS, 2 * sb, MXU_COLS), F32),
                        pltpu.VMEM((SWA_SLOTS, 1, MXU_COLS), F32),
                        pltpu.VMEM((SWA_Q, ts), F32)],
        compiler_params=pltpu.CompilerParams(dimension_semantics=("parallel", "parallel"),
                                             vmem_limit_bytes=VMEM_LIMIT),
        name="swa",
    )(sinks, qt3, k3, k3, vt3, vt3, bias, bias)


def _fox_kernel(qt_ref, k_ref, vt_ref, knorm_ref, cmin_ref, o_ref, st_sc, mb_sc, m_sc, acc_sc):
    tq = qt_ref.shape[2]
    hp = pl.program_id(1)
    qi = pl.program_id(2)
    m_sc[...] = jnp.full_like(m_sc, NEG_BIG)
    acc_sc[...] = jnp.zeros_like(acc_sc)

    units = [(hh, slice(c * MXU_COLS, (c + 1) * MXU_COLS)) for hh in range(2) for c in range(tq // MXU_COLS)]

    def scores(block, slot, unit, diagonal):
        hh, cols = unit
        keys = cols.stop if diagonal else tq
        start = pl.multiple_of(block * tq, tq)
        k3 = k_ref[pl.ds(start, keys), hh * FOX_DEPTH:(hh + 1) * FOX_DEPTH]
        st = jnp.dot(k3, qt_ref[hh, :, cols], preferred_element_type=F32)
        if diagonal:
            key = lax.broadcasted_iota(jnp.int32, st.shape, 0)
            qry = lax.broadcasted_iota(jnp.int32, st.shape, 1) + cols.start
            st = jnp.where(key <= qry, st, NEG_BIG)
        st_sc[slot, hh, :keys, cols] = st
        mb_sc[slot, hh, :, cols] = jnp.max(st, axis=0, keepdims=True)

    def consume(block, slot, unit, diagonal=False):
        hh, cols = unit
        keys = cols.stop if diagonal else tq
        start = pl.multiple_of(block * tq, tq)
        m_prev = m_sc[hh, :, cols]
        m_new = jnp.maximum(m_prev, mb_sc[slot, hh, :, cols])
        p = jnp.exp2(st_sc[slot, hh, :keys, cols] - m_new).astype(BF16)
        alpha = jnp.exp2(m_prev - m_new)
        pv = jnp.dot(vt_ref[hh, :, pl.ds(start, keys)], p, preferred_element_type=F32)
        acc_sc[hh, :, cols] = alpha * acc_sc[hh, :, cols] + pv
        m_sc[hh, :, cols] = m_new

    def stage(next_block, next_slot, block, slot, diagonal=False):
        for unit in units:
            scores(next_block, next_slot, unit, False)
            consume(block, slot, unit, diagonal)

    def skippable_blocks(hh):
        lane = lax.broadcasted_iota(jnp.int32, (1, LANES), 1)
        q32 = qt_ref[hh, :HEAD_DIM, :].astype(F32)
        q_sq = jnp.max(jnp.sum(q32 * q32, axis=0, keepdims=True), axis=1, keepdims=True)
        c_q = jnp.sum(qt_ref[hh, HEAD_DIM:HEAD_DIM + C_PARTS, :].astype(F32), axis=0, keepdims=True)
        c_first = jnp.max(c_q, axis=1, keepdims=True)
        m_min = jnp.min(mb_sc[0, hh], axis=1, keepdims=True)
        k_sq = jnp.zeros((1, LANES), F32)
        c_last = jnp.full((1, LANES), -NEG_BIG, F32)
        count = jnp.zeros((1, LANES), jnp.int32)
        for j in range(knorm_ref.shape[0]):
            k_sq = jnp.maximum(k_sq, knorm_ref[j])
            c_last = jnp.minimum(c_last, cmin_ref[j])
            room = (m_min - FOX_SKIP_MARGIN) - (c_first - c_last)
            zero = (room > 0.0) & (q_sq * k_sq < room * room) & (j < qi)
            count = count + zero.astype(jnp.int32)
        return jnp.sum(jnp.where(lane == 2 * hp + hh, count, 0))

    for unit in units:
        scores(qi, 0, unit, True)
    first = jnp.minimum(skippable_blocks(0), skippable_blocks(1))
    nearest = jnp.maximum(qi - 1, 0)
    stage(nearest, 1, qi, 0, diagonal=True)
    rest = jnp.maximum(qi - 1 - first, 0)

    @pl.when((rest == 0) & (qi >= 1))
    def _():
        for unit in units:
            consume(nearest, 1, unit)

    @pl.when(rest >= 1)
    def _():
        stage(first, 0, nearest, 1)
        stage(jnp.minimum(first + 1, qi), 1, first, 0)

    start = first + 1
    count = jnp.maximum(rest - 1, 0)

    def block_pair(b0):
        stage(b0 + 1, 0, b0, 1)
        stage(jnp.minimum(b0 + 2, qi), 1, b0 + 1, 0)

    def block_oct(t, carry):
        for pair in range(4):
            block_pair(start + 8 * t + 2 * pair)
        return carry

    lax.fori_loop(0, count // 8, block_oct, 0)

    @pl.when(count % 8 >= 4)
    def _():
        block_pair(start + count // 8 * 8)
        block_pair(start + count // 8 * 8 + 2)

    @pl.when(count % 4 >= 2)
    def _():
        block_pair(start + count // 4 * 4)

    @pl.when(count % 2 == 1)
    def _():
        for unit in units:
            consume(qi - 2, 1, unit)

    outs = [acc_sc[hh, :HEAD_DIM, :] / acc_sc[hh, HEAD_DIM:HEAD_DIM + 1, :] for hh in range(2)]
    o_ref[...] = jnp.concatenate(outs, axis=0).T.astype(BF16)


def _fox(q3t, k3, vt4, knorm, cmin):
    b, s, _ = k3.shape
    tq = FOX_TQ
    assert knorm.shape == cmin.shape == (b, s // tq, 1, LANES)
    stat_spec = pl.BlockSpec((None, s // tq, 1, LANES), lambda bi, hp, qi: (bi, 0, 0, 0))
    return pl.pallas_call(
        _fox_kernel,
        grid=(b, FOX_PAIRS, s // tq),
        in_specs=[pl.BlockSpec((None, 2, FOX_DEPTH, tq), lambda bi, hp, qi: (bi, hp, 0, qi)),
                  pl.BlockSpec((None, s, 2 * FOX_DEPTH), lambda bi, hp, qi: (bi, 0, hp)),
                  pl.BlockSpec((None, 2, VT_ROWS, s), lambda bi, hp, qi: (bi, hp, 0, 0)),
                  stat_spec, stat_spec],
        out_specs=pl.BlockSpec((None, tq, LANES), lambda bi, hp, qi: (bi, qi, hp)),
        out_shape=jax.ShapeDtypeStruct((b, s, FOX_W), BF16),
        scratch_shapes=[pltpu.VMEM((2, 2, tq, tq), F32),
                        pltpu.VMEM((2, 2, 1, tq), F32),
                        pltpu.VMEM((2, 1, tq), F32),
                        pltpu.VMEM((2, VT_ROWS, tq), F32)],
        compiler_params=pltpu.CompilerParams(dimension_semantics=("parallel", "parallel", "parallel"),
                                             vmem_limit_bytes=VMEM_LIMIT),
        name="fox",
    )(q3t, k3, vt4, knorm, cmin)


def _mix_kernel(x_ref, ya_ref, yb_ref, gates_ref, wa_ref, wb_ref, wo_ref, h_ref):
    slabs = [slice(r, r + TM) for r in range(0, MIX_TM, TM)]

    def branches(rows):
        y_a = jnp.dot(ya_ref[rows, :], wa_ref[...], preferred_element_type=F32)
        y_b = jnp.dot(yb_ref[rows, :], wb_ref[...], preferred_element_type=F32)
        return y_a, y_b

    pending = branches(slabs[0])
    for n, rows in enumerate(slabs):
        y_a, y_b = pending
        if n + 1 < len(slabs):
            pending = branches(slabs[n + 1])
        gates = gates_ref[rows, :].astype(F32)
        mixed = gates[:, :D_MODEL] * y_a + gates[:, D_MODEL:] * y_b
        h_ref[rows, :] = x_ref[rows, :] + jnp.dot(mixed.astype(BF16), wo_ref[...], preferred_element_type=F32)


def _mix(x2, att_a, att_b, gates, w_a, w_b, w_o):
    t = x2.shape[0]
    assert t % MIX_TM == 0 and MIX_TM % TM == 0
    row_spec = lambda w: pl.BlockSpec((MIX_TM, w), lambda i: (i, 0))
    return pl.pallas_call(
        _mix_kernel,
        grid=(t // MIX_TM,),
        in_specs=[row_spec(D_MODEL), row_spec(SWA_Q), row_spec(FOX_W), row_spec(2 * D_MODEL),
                  _const_spec(w_a.shape), _const_spec(w_b.shape), _const_spec(w_o.shape)],
        out_specs=row_spec(D_MODEL),
        out_shape=jax.ShapeDtypeStruct((t, D_MODEL), F32),
        compiler_params=pltpu.CompilerParams(dimension_semantics=("parallel",), vmem_limit_bytes=VMEM_LIMIT),
        name="mix",
    )(x2, att_a, att_b, gates, w_a, w_b, w_o)


def _mlp_ple_kernel(h_ref, p_ref, gm_ref, w1_ref, w2_ref, gp_ref, wpg_ref, wpp_ref, gf_ref, o_ref):
    h = h_ref[...]
    u = _rms(h, gm_ref[...]).astype(BF16)
    acc = jnp.zeros_like(h)
    for c in range(D_FF // FF_CHUNK):
        cols = slice(c * FF_CHUNK, (c + 1) * FF_CHUNK)
        a = jnp.dot(u, w1_ref[:, cols], preferred_element_type=F32)
        a = jnp.square(jnp.maximum(a, 0.0)).astype(BF16)
        acc = acc + jnp.dot(a, w2_ref[cols, :], preferred_element_type=F32)
    h = h + acc
    gate = jax.nn.sigmoid(jnp.dot(_rms(h, gp_ref[...]).astype(BF16), wpg_ref[...], preferred_element_type=F32))
    proj = jnp.dot(p_ref[...].astype(BF16), wpp_ref[...], preferred_element_type=F32)
    h = h + gate * proj
    o_ref[...] = _rms(h, gf_ref[...])


def _mlp_ple(h2, p2, g_mlp, w1, w2, g_ple, w_pg, w_pp, g_final):
    t = h2.shape[0]
    row_spec = lambda w: pl.BlockSpec((TM, w), lambda i: (i, 0))
    vec = _const_spec((1, D_MODEL))
    return pl.pallas_call(
        _mlp_ple_kernel,
        grid=(t // TM,),
        in_specs=[row_spec(D_MODEL), row_spec(PLE_DIM), vec, _const_spec(w1.shape), _const_spec(w2.shape),
                  vec, _const_spec(w_pg.shape), _const_spec(w_pp.shape), vec],
        out_specs=row_spec(D_MODEL),
        out_shape=jax.ShapeDtypeStruct((t, D_MODEL), F32),
        compiler_params=pltpu.CompilerParams(dimension_semantics=("parallel",), vmem_limit_bytes=VMEM_LIMIT),
        name="mlp_ple",
    )(h2, p2, g_mlp, w1, w2, g_ple, w_pg, w_pp, g_final)


def kernel(x, p, g_mix, w_in, b_forget, swa_sinks, w_br_swa, w_br_fox, w_mix_out,
           g_mlp, w_ff1, w_ff2, g_ple, w_ple_gate, w_ple_proj, g_final):
    b, s, d = x.shape
    assert d == D_MODEL and w_in.shape[0] == 1, "single-layer trunk with D_MODEL channels only"
    assert s % TM == 0 and FOX_TQ == TM and s % SWA_TILE == 0 and SWA_TILE % SWA_BLOCK == 0
    t = b * s
    x2 = x.reshape(t, d)

    w = jnp.swapaxes(w_in, 1, 2)[0]
    bf_pad = jnp.pad(b_forget[0], (0, LANES - FOX_HEADS)).reshape(1, LANES)

    later = (w_br_swa[0], w_br_fox[0], w_mix_out[0], w_ff1[0], w_ff2[0], w_ple_gate[0])
    qat, ka, vta, q3t, k3, vtb, gates, knorm, cmin, *later_bf16 = _in_proj(x2, g_mix[0].reshape(1, d), w, bf_pad, b, s, later)
    wa_b, wb_b, wo_b, w1_b, w2_b, wpg_b = later_bf16

    att_a = _swa(qat, ka.reshape(b, s, SWA_KV), vta, swa_sinks[0].reshape(1, SWA_HEADS),
                 jnp.asarray(_swa_bias_table()))
    blocks = (b, s // FOX_TQ, 1, LANES)
    att_b = _fox(q3t, k3.reshape(b, s, FOX_HEADS * FOX_DEPTH), vtb.reshape(b, FOX_HEADS, VT_ROWS, s),
                 knorm.reshape(blocks), cmin.reshape(blocks))

    h = _mix(x2, att_a.reshape(t, SWA_Q), att_b.reshape(t, FOX_W), gates, wa_b, wb_b, wo_b)
    out = _mlp_ple(h, p[0].reshape(t, PLE_DIM), g_mlp[0].reshape(1, d), w1_b, w2_b,
                   g_ple[0].reshape(1, d), wpg_b, w_ple_proj[0].astype(BF16), g_final.reshape(1, d))
    return out.reshape(b, s, d)
```

```python
import functools

import numpy as np
import jax
import jax.numpy as jnp
from jax import lax
from jax.experimental import pallas as pl
from jax.experimental.pallas import tpu as pltpu

D_MODEL = 1024
CHUNK = 64
PLE_DIM = 256
HEAD_DIM = 64
SWA_HEADS = 8
SWA_KV_HEADS = 2
SWA_GROUP = SWA_HEADS // SWA_KV_HEADS
WINDOW = 128
SWA_BLOCK = WINDOW
FOX_HEADS = 8
D_FF = 4 * D_MODEL
RMS_EPS = 1e-6
SWA_Q = SWA_HEADS * HEAD_DIM
SWA_KV = SWA_KV_HEADS * HEAD_DIM
FOX_W = FOX_HEADS * HEAD_DIM
COL_KA = SWA_Q
COL_VA = COL_KA + SWA_KV
COL_QB = COL_VA + SWA_KV
COL_KB = COL_QB + FOX_W
COL_VB = COL_KB + FOX_W
COL_F = COL_VB + FOX_W
COL_G = COL_F + FOX_HEADS
SCALE = HEAD_DIM ** -0.5
LOG2E = float(np.log2(np.e))

LANES = 128
BF16_SUBLANES = 16
MXU_COLS = 256
NEG_BIG = -1e30
NORM_SLACK = 1.03
FOX_SKIP_MARGIN = 160.0
VMEM_LIMIT = 52 * 1024 * 1024
IN_PROJ_VMEM_LIMIT = 58 * 1024 * 1024
POST_VMEM_LIMIT = 58 * 1024 * 1024

TM = 512
SWA_TILE = 2048
SWA_SLOTS = 6
FOX_TQ = 512
FF_CHUNK = 1024

FOX_PAIRS = FOX_HEADS // 2
FOX_DEPTH = LANES
BIAS_SLOT = 8
C_PARTS = 3
VT_ROWS = 80

F32 = jnp.float32
BF16 = jnp.bfloat16
NT_DIMS = (((1,), (1,)), ((), ()))


def _rms(x, g):
    return x * lax.rsqrt(jnp.mean(x * x, axis=-1, keepdims=True) + RMS_EPS) * g


def _const_spec(shape):
    return pl.BlockSpec(shape, lambda *_: (0,) * len(shape), pipeline_mode=pl.Buffered(1))


def _split3(v):
    hi = v.astype(BF16)
    r1 = v - hi.astype(F32)
    mid = r1.astype(BF16)
    lo = (r1 - mid.astype(F32)).astype(BF16)
    return hi, mid, lo


def _bias_placement():
    place_qt = np.zeros((LANES, C_PARTS * LANES), np.float32)
    ones_qt = np.zeros((LANES, 1), np.float32)
    place_k = np.zeros((C_PARTS * LANES, LANES), np.float32)
    ones_k = np.zeros((1, LANES), np.float32)
    for h in range(FOX_HEADS):
        for part in range(C_PARTS):
            place_qt[h * BIAS_SLOT + part, part * LANES + h] = 1.0
            ones_qt[h * BIAS_SLOT + C_PARTS + part, 0] = 1.0
            ones_k[0, h * BIAS_SLOT + part] = 1.0
            place_k[part * LANES + h, h * BIAS_SLOT + C_PARTS + part] = -1.0
    return (place_qt, ones_qt), (place_k, ones_k)


def _in_proj_kernel(*refs, tiles_per_seq, n_cast):
    (x_ref, g_ref, wlo_ref, whi_ref, bf_ref, pqt_ref, oqt_ref, pk_ref, ok_ref, vone_ref, sel_ref) = refs[:11]
    cast_in = refs[11:11 + n_cast]
    (qat_ref, ka_ref, vta_ref, q3t_ref, k3_ref, vtb_ref, gates_ref, knorm_ref, cmin_ref) = refs[11 + n_cast:20 + n_cast]
    cast_out = refs[20 + n_cast:20 + 2 * n_cast]
    carry_ref, wt_ref, wkf_ref, wk_ref, wg_ref = refs[20 + 2 * n_cast:]
    i = pl.program_id(0)
    lane = lax.broadcasted_iota(jnp.int32, (1, LANES), 1)
    gate_shift = COL_G % LANES
    for src, dst in zip(cast_in, cast_out):
        dst[...] = src[...].astype(BF16)

    @pl.when(i == 0)
    def _():
        wk_ref[...] = wlo_ref[COL_KB:COL_KB + FOX_W].T.astype(BF16)
        wkf_ref[:, :SWA_KV] = wlo_ref[COL_KA:COL_KA + SWA_KV].T.astype(BF16)
        second = jnp.concatenate([whi_ref[:gate_shift], whi_ref[2 * D_MODEL:2 * D_MODEL + gate_shift],
                                  jnp.zeros((LANES - 2 * gate_shift, D_MODEL), F32)], axis=0)
        wkf_ref[:, SWA_KV:] = second.T.astype(BF16)
        wg_ref[...] = whi_ref[:2 * D_MODEL].T.astype(BF16)
        wt_ref[:SWA_Q] = wlo_ref[:SWA_Q].astype(BF16)
        wt_ref[SWA_Q:SWA_Q + FOX_W] = wlo_ref[COL_QB:COL_QB + FOX_W].astype(BF16)
        pad_rows = jnp.zeros((VT_ROWS - HEAD_DIM, D_MODEL), BF16)
        v_rows = [COL_VA + g * HEAD_DIM for g in range(SWA_KV_HEADS)] + [COL_VB + h * HEAD_DIM for h in range(FOX_HEADS)]
        for head, row in enumerate(v_rows):
            base = SWA_Q + FOX_W + head * VT_ROWS
            wt_ref[base:base + HEAD_DIM] = wlo_ref[row:row + HEAD_DIM].astype(BF16)
            wt_ref[base + HEAD_DIM:base + VT_ROWS] = pad_rows

    @pl.when(i % tiles_per_seq == 0)
    def _():
        carry_ref[...] = jnp.zeros_like(carry_ref)

    u = _rms(x_ref[...], g_ref[...]).astype(BF16)
    tm = u.shape[0]
    kf = jnp.dot(u, wkf_ref[...], preferred_element_type=F32)
    ka_ref[...] = kf[:, :SWA_KV].astype(BF16)
    f = kf[:, SWA_KV:] + bf_ref[...]
    logf = jnp.minimum(f, 0.0) - jnp.log1p(jnp.exp(-jnp.abs(f)))

    gl = jnp.dot(u, wg_ref[...], preferred_element_type=F32)
    n_tiles = gl.shape[1] // LANES
    for c in range(n_tiles):
        cur = pltpu.roll(gl[:, c * LANES:(c + 1) * LANES], LANES - gate_shift, 1)
        if c + 1 < n_tiles:
            nxt = pltpu.roll(gl[:, (c + 1) * LANES:(c + 2) * LANES], LANES - gate_shift, 1)
        else:
            nxt = pltpu.roll(kf[:, SWA_KV:], LANES - 2 * gate_shift, 1)
        tile = jnp.where(lane < LANES - gate_shift, cur, nxt)
        gates_ref[:, c * LANES:(c + 1) * LANES] = jax.nn.sigmoid(tile).astype(BF16)

    row = lax.broadcasted_iota(jnp.int32, (tm, tm), 0)
    col = lax.broadcasted_iota(jnp.int32, (tm, tm), 1)
    tri = (col <= row).astype(BF16)
    cs = jnp.dot(tri, jnp.concatenate(_split3(logf), axis=1), preferred_element_type=F32)
    c = cs[:, :LANES] + cs[:, LANES:2 * LANES] + cs[:, 2 * LANES:] + carry_ref[...]
    carry_ref[...] = c[tm - 1:tm, :]
    cmin_ref[...] = jnp.min(c * LOG2E, axis=0, keepdims=True)

    tt = lax.dot_general(wt_ref[...], u, NT_DIMS, preferred_element_type=F32)
    qat_ref[...] = (tt[:SWA_Q] * (SCALE * LOG2E)).astype(BF16)
    qt = (tt[SWA_Q:SWA_Q + FOX_W] * (SCALE * LOG2E)).astype(BF16)
    vt = (tt[SWA_Q + FOX_W:] + vone_ref[...]).astype(BF16)
    vta_ref[...] = vt[:SWA_KV_HEADS * VT_ROWS]
    vtb_ref[...] = vt[SWA_KV_HEADS * VT_ROWS:]

    parts = jnp.concatenate(_split3(c * LOG2E), axis=1)
    kbias = jnp.dot(parts, pk_ref[...], preferred_element_type=F32) + ok_ref[...]
    qbias_t = lax.dot_general(pqt_ref[...], parts, NT_DIMS, preferred_element_type=F32) + oqt_ref[...]
    zero_rows = jnp.zeros((HEAD_DIM - BIAS_SLOT, tm), F32)
    for h in range(FOX_HEADS):
        q3t_ref[h, :HEAD_DIM, :] = qt[h * HEAD_DIM:(h + 1) * HEAD_DIM]
        own = jnp.concatenate([qbias_t[h * BIAS_SLOT:(h + 1) * BIAS_SLOT], zero_rows], axis=0)
        q3t_ref[h, HEAD_DIM:, :] = own.astype(BF16)

    k = jnp.dot(u, wk_ref[...], preferred_element_type=F32)
    k_sq = jnp.square(k.astype(BF16).astype(F32)).astype(BF16)
    knorm_ref[...] = NORM_SLACK * jnp.max(jnp.dot(k_sq, sel_ref[...], preferred_element_type=F32), axis=0, keepdims=True)
    lane = lax.broadcasted_iota(jnp.int32, (1, LANES), 1)
    for h in range(FOX_HEADS):
        pair = k[:, (h // 2) * LANES:(h // 2 + 1) * LANES]
        dims = pair if h % 2 == 0 else pltpu.roll(pair, HEAD_DIM, 1)
        bias = pltpu.roll(kbias, HEAD_DIM - h * BIAS_SLOT, 1)
        k3_ref[:, h * FOX_DEPTH:(h + 1) * FOX_DEPTH] = jnp.where(lane < HEAD_DIM, dims, bias).astype(BF16)


def _in_proj(x2, g_mix, w_all, bf_pad, batch, seq, later_weights):
    t = x2.shape[0]
    tps = seq // TM
    steps = t // TM
    (place_qt, ones_qt), (place_k, ones_k) = _bias_placement()
    n_vt = SWA_KV_HEADS + FOX_HEADS
    vone = np.zeros((n_vt * VT_ROWS, 1), np.float32)
    vone[HEAD_DIM::VT_ROWS] = 1.0
    head_of_lane = np.zeros((FOX_W, LANES), np.float32)
    head_of_lane[np.arange(FOX_W), np.arange(FOX_W) // HEAD_DIM] = 1.0
    consts = (jnp.asarray(place_qt, BF16), jnp.asarray(ones_qt), jnp.asarray(place_k, BF16), jnp.asarray(ones_k),
              jnp.asarray(vone), jnp.asarray(head_of_lane, BF16))
    kern = functools.partial(_in_proj_kernel, tiles_per_seq=tps, n_cast=len(later_weights))
    row_spec = lambda w: pl.BlockSpec((TM, w), lambda i: (i, 0))
    col_spec = lambda *lead: pl.BlockSpec((None,) + lead + (TM,), lambda i: (i // tps,) + (0,) * len(lead) + (i % tps,))
    for w in later_weights:
        assert w.ndim == 2 and w.shape[0] % (steps * BF16_SUBLANES) == 0, w.shape
    slab_specs = [pl.BlockSpec((w.shape[0] // steps, w.shape[1]), lambda i: (i, 0)) for w in later_weights]
    assert w_all.shape == (COL_G + 2 * D_MODEL, D_MODEL) and COL_F >= 2 * D_MODEL + LANES and COL_F % BF16_SUBLANES == 0
    half_specs = [pl.BlockSpec((COL_F, D_MODEL), lambda i, j=j: (j, 0), pipeline_mode=pl.Buffered(1)) for j in range(2)]
    tail = (bf_pad,) + consts
    ka, k3, gates = (jax.ShapeDtypeStruct((t, w), BF16) for w in (SWA_KV, FOX_HEADS * FOX_DEPTH, 2 * D_MODEL))
    qat = jax.ShapeDtypeStruct((batch, SWA_Q, seq), BF16)
    stat = jax.ShapeDtypeStruct((steps, 1, LANES), F32)
    stat_spec = pl.BlockSpec((None, 1, LANES), lambda i: (i, 0, 0))
    q3t = jax.ShapeDtypeStruct((batch, FOX_HEADS, FOX_DEPTH, seq), BF16)
    vta, vtb = (jax.ShapeDtypeStruct((batch, heads * VT_ROWS, seq), BF16) for heads in (SWA_KV_HEADS, FOX_HEADS))
    return pl.pallas_call(
        kern,
        grid=(steps,),
        in_specs=([row_spec(D_MODEL), _const_spec(g_mix.shape)] + half_specs
                  + [_const_spec(a.shape) for a in tail] + slab_specs),
        out_specs=[col_spec(SWA_Q), row_spec(SWA_KV), col_spec(SWA_KV_HEADS * VT_ROWS),
                   col_spec(FOX_HEADS, FOX_DEPTH), row_spec(FOX_HEADS * FOX_DEPTH), col_spec(FOX_HEADS * VT_ROWS),
                   row_spec(2 * D_MODEL), stat_spec, stat_spec] + slab_specs,
        out_shape=([qat, ka, vta, q3t, k3, vtb, gates, stat, stat]
                   + [jax.ShapeDtypeStruct(w.shape, BF16) for w in later_weights]),
        scratch_shapes=[pltpu.VMEM((1, LANES), F32),
                        pltpu.VMEM((SWA_Q + FOX_W + n_vt * VT_ROWS, D_MODEL), BF16),
                        pltpu.VMEM((D_MODEL, SWA_KV + LANES), BF16),
                        pltpu.VMEM((D_MODEL, FOX_W), BF16),
                        pltpu.VMEM((D_MODEL, 2 * D_MODEL), BF16)],
        compiler_params=pltpu.CompilerParams(dimension_semantics=("arbitrary",), vmem_limit_bytes=IN_PROJ_VMEM_LIMIT),
        name="in_proj",
    )(x2, g_mix, w_all, w_all, *tail, *later_weights)


def _swa_bias_table():
    sb = SWA_BLOCK
    qi = np.arange(sb)[None, :] + sb
    si = np.arange(2 * sb)[:, None]
    chunk_diff = qi // CHUNK - si // CHUNK
    band_ok = (chunk_diff >= 0) & (chunk_diff <= WINDOW // CHUNK)
    slopes = np.array([2.0 ** (-8.0 * (h + 1) / SWA_HEADS) for h in range(SWA_HEADS)], dtype=np.float32)
    alibi = -slopes[:, None, None] * np.abs(qi - si).astype(np.float32)[None] * np.float32(LOG2E)
    first = band_ok & (si >= sb)
    table = np.stack([np.where(first[None], alibi, NEG_BIG), np.where(band_ok[None], alibi, NEG_BIG)])
    table = table.reshape(2, SWA_KV_HEADS, SWA_GROUP, 2 * sb, sb).transpose(0, 1, 3, 2, 4)
    return np.ascontiguousarray(table.reshape(2, SWA_KV_HEADS, 2 * sb, SWA_GROUP * sb)).astype(np.float32)


def _swa_kernel(sink_ref, qt_ref, kp_ref, kc_ref, vtp_ref, vtc_ref, bias_first_ref, bias_rest_ref, o_ref,
                st_sc, mb_sc, ot_sc):
    sb = SWA_BLOCK
    per_unit = MXU_COLS // sb
    half = jnp.zeros((HEAD_DIM, MXU_COLS), BF16)
    units = [(j, kh, c) for j in range(SWA_TILE // sb) for kh in range(SWA_KV_HEADS)
             for c in range(SWA_GROUP // per_unit)]

    def band(j):
        if j == 0:
            kb = jnp.concatenate([kp_ref[...], kc_ref[0:sb, :]], axis=0)
            vtb = jnp.concatenate([vtp_ref[...], vtc_ref[:, 0:sb]], axis=1)
            return kb, vtb, bias_first_ref
        return kc_ref[(j - 1) * sb:(j + 1) * sb, :], vtc_ref[:, (j - 1) * sb:(j + 1) * sb], bias_rest_ref

    def unit_heads(kh, c):
        return [kh * SWA_GROUP + c * per_unit + g for g in range(per_unit)]

    def scores(i, slot):
        j, kh, c = units[i]
        kb, _, bias_ref = band(j)
        q2 = jnp.concatenate([qt_ref[h * HEAD_DIM:(h + 1) * HEAD_DIM, j * sb:(j + 1) * sb] for h in unit_heads(kh, c)],
                             axis=1)
        q2t = jnp.concatenate([q2, half] if kh == 0 else [half, q2], axis=0)
        st = jnp.dot(kb, q2t, preferred_element_type=F32) + bias_ref[kh, :, c * MXU_COLS:(c + 1) * MXU_COLS]
        st_sc[slot] = st
        mb_sc[slot] = jnp.max(st, axis=0, keepdims=True)

    def consume(unit, slot):
        j, kh, c = unit
        _, vtb, _ = band(j)
        heads = unit_heads(kh, c)
        sink = jnp.concatenate([jnp.full((1, sb), sink_ref[0, h] * LOG2E, F32) for h in heads], axis=1)
        m = jnp.maximum(mb_sc[slot], sink)
        p = jnp.exp2(st_sc[slot] - m).astype(BF16)
        pv = jnp.dot(vtb[kh * VT_ROWS:(kh + 1) * VT_ROWS, :], p, preferred_element_type=F32)
        denom = pv[HEAD_DIM:HEAD_DIM + 1, :] + jnp.exp2(sink - m)
        o = pv[:HEAD_DIM, :] / denom
        for g, h in enumerate(heads):
            ot_sc[h * HEAD_DIM:(h + 1) * HEAD_DIM, j * sb:(j + 1) * sb] = o[:, g * sb:(g + 1) * sb]

    slots = st_sc.shape[0]
    for i in range(slots - 1):
        scores(i, i)
    for i, unit in enumerate(units):
        ahead = i + slots - 1
        if ahead < len(units):
            scores(ahead, ahead % slots)
        consume(unit, i % slots)
    o_ref[...] = ot_sc[...].T.astype(BF16)


def _swa(qt3, k3, vt3, sinks, bias):
    b, s, _ = k3.shape
    sb, ts = SWA_BLOCK, SWA_TILE
    per = ts // sb
    prev = lambda n: jnp.maximum(n * per - 1, 0)
    bias_block = (None,) + bias.shape[1:]
    return pl.pallas_call(
        _swa_kernel,
        grid=(b, s // ts),
        in_specs=[pl.BlockSpec(memory_space=pltpu.SMEM),
                  pl.BlockSpec((None, SWA_Q, ts), lambda bi, n: (bi, 0, n)),
                  pl.BlockSpec((None, sb, SWA_KV), lambda bi, n: (bi, prev(n), 0)),
                  pl.BlockSpec((None, ts, SWA_KV), lambda bi, n: (bi, n, 0)),
                  pl.BlockSpec((None, SWA_KV_HEADS * VT_ROWS, sb), lambda bi, n: (bi, 0, prev(n))),
                  pl.BlockSpec((None, SWA_KV_HEADS * VT_ROWS, ts), lambda bi, n: (bi, 0, n)),
                  pl.BlockSpec(bias_block, lambda bi, n: (jnp.minimum(n, 1), 0, 0, 0)),
                  pl.BlockSpec(bias_block, lambda bi, n: (1, 0, 0, 0))],
        out_specs=pl.BlockSpec((None, ts, SWA_Q), lambda bi, n: (bi, n, 0)),
        out_shape=jax.ShapeDtypeStruct((b, s, SWA_Q), BF16),
        scratch_shapes=[pltpu.VMEM((SWA_SLOTS, 2 * sb, MXU_COLS), F32),
                        pltpu.VMEM((SWA_SLOTS, 1, MXU_COLS), F32),
                        pltpu.VMEM((SWA_Q, ts), F32)],
        compiler_params=pltpu.CompilerParams(dimension_semantics=("parallel", "parallel"),
                                             vmem_limit_bytes=VMEM_LIMIT),
        name="swa",
    )(sinks, qt3, k3, k3, vt3, vt3, bias, bias)


def _fox_kernel(qt_ref, k_ref, vt_ref, knorm_ref, cmin_ref, o_ref, st_sc, mb_sc, m_sc, acc_sc):
    tq = qt_ref.shape[2]
    hp = pl.program_id(1)
    qi = pl.program_id(2)
    m_sc[...] = jnp.full_like(m_sc, NEG_BIG)
    acc_sc[...] = jnp.zeros_like(acc_sc)

    units = [(hh, slice(c * MXU_COLS, (c + 1) * MXU_COLS)) for hh in range(2) for c in range(tq // MXU_COLS)]

    def scores(block, slot, unit, diagonal):
        hh, cols = unit
        keys = cols.stop if diagonal else tq
        start = pl.multiple_of(block * tq, tq)
        k3 = k_ref[pl.ds(start, keys), hh * FOX_DEPTH:(hh + 1) * FOX_DEPTH]
        st = jnp.dot(k3, qt_ref[hh, :, cols], preferred_element_type=F32)
        if diagonal:
            key = lax.broadcasted_iota(jnp.int32, st.shape, 0)
            qry = lax.broadcasted_iota(jnp.int32, st.shape, 1) + cols.start
            st = jnp.where(key <= qry, st, NEG_BIG)
        st_sc[slot, hh, :keys, cols] = st
        mb_sc[slot, hh, :, cols] = jnp.max(st, axis=0, keepdims=True)

    def consume(block, slot, unit, diagonal=False):
        hh, cols = unit
        keys = cols.stop if diagonal else tq
        start = pl.multiple_of(block * tq, tq)
        m_prev = m_sc[hh, :, cols]
        m_new = jnp.maximum(m_prev, mb_sc[slot, hh, :, cols])
        p = jnp.exp2(st_sc[slot, hh, :keys, cols] - m_new).astype(BF16)
        alpha = jnp.exp2(m_prev - m_new)
        pv = jnp.dot(vt_ref[hh, :, pl.ds(start, keys)], p, preferred_element_type=F32)
        acc_sc[hh, :, cols] = alpha * acc_sc[hh, :, cols] + pv
        m_sc[hh, :, cols] = m_new

    def stage(next_block, next_slot, block, slot, diagonal=False):
        for unit in units:
            scores(next_block, next_slot, unit, False)
            consume(block, slot, unit, diagonal)

    def skippable_blocks(hh):
        lane = lax.broadcasted_iota(jnp.int32, (1, LANES), 1)
        q32 = qt_ref[hh, :HEAD_DIM, :].astype(F32)
        q_sq = jnp.max(jnp.sum(q32 * q32, axis=0, keepdims=True), axis=1, keepdims=True)
        c_q = jnp.sum(qt_ref[hh, HEAD_DIM:HEAD_DIM + C_PARTS, :].astype(F32), axis=0, keepdims=True)
        c_first = jnp.max(c_q, axis=1, keepdims=True)
        m_min = jnp.min(mb_sc[0, hh], axis=1, keepdims=True)
        k_sq = jnp.zeros((1, LANES), F32)
        c_last = jnp.full((1, LANES), -NEG_BIG, F32)
        count = jnp.zeros((1, LANES), jnp.int32)
        for j in range(knorm_ref.shape[0]):
            k_sq = jnp.maximum(k_sq, knorm_ref[j])
            c_last = jnp.minimum(c_last, cmin_ref[j])
            room = (m_min - FOX_SKIP_MARGIN) - (c_first - c_last)
            zero = (room > 0.0) & (q_sq * k_sq < room * room) & (j < qi)
            count = count + zero.astype(jnp.int32)
        return jnp.sum(jnp.where(lane == 2 * hp + hh, count, 0))

    for unit in units:
        scores(qi, 0, unit, True)
    first = jnp.minimum(skippable_blocks(0), skippable_blocks(1))
    nearest = jnp.maximum(qi - 1, 0)
    stage(nearest, 1, qi, 0, diagonal=True)
    rest = jnp.maximum(qi - 1 - first, 0)

    @pl.when((rest == 0) & (qi >= 1))
    def _():
        for unit in units:
            consume(nearest, 1, unit)

    @pl.when(rest >= 1)
    def _():
        stage(first, 0, nearest, 1)
        stage(jnp.minimum(first + 1, qi), 1, first, 0)

    start = first + 1
    count = jnp.maximum(rest - 1, 0)

    def block_pair(b0):
        stage(b0 + 1, 0, b0, 1)
        stage(jnp.minimum(b0 + 2, qi), 1, b0 + 1, 0)

    def block_oct(t, carry):
        for pair in range(4):
            block_pair(start + 8 * t + 2 * pair)
        return carry

    lax.fori_loop(0, count // 8, block_oct, 0)

    @pl.when(count % 8 >= 4)
    def _():
        block_pair(start + count // 8 * 8)
        block_pair(start + count // 8 * 8 + 2)

    @pl.when(count % 4 >= 2)
    def _():
        block_pair(start + count // 4 * 4)

    @pl.when(count % 2 == 1)
    def _():
        for unit in units:
            consume(qi - 2, 1, unit)

    outs = [acc_sc[hh, :HEAD_DIM, :] / acc_sc[hh, HEAD_DIM:HEAD_DIM + 1, :] for hh in range(2)]
    o_ref[...] = jnp.concatenate(outs, axis=0).T.astype(BF16)


def _fox(q3t, k3, vt4, knorm, cmin):
    b, s, _ = k3.shape
    tq = FOX_TQ
    assert knorm.shape == cmin.shape == (b, s // tq, 1, LANES)
    stat_spec = pl.BlockSpec((None, s // tq, 1, LANES), lambda bi, hp, qi: (bi, 0, 0, 0))
    return pl.pallas_call(
        _fox_kernel,
        grid=(b, FOX_PAIRS, s // tq),
        in_specs=[pl.BlockSpec((None, 2, FOX_DEPTH, tq), lambda bi, hp, qi: (bi, hp, 0, qi)),
                  pl.BlockSpec((None, s, 2 * FOX_DEPTH), lambda bi, hp, qi: (bi, 0, hp)),
                  pl.BlockSpec((None, 2, VT_ROWS, s), lambda bi, hp, qi: (bi, hp, 0, 0)),
                  stat_spec, stat_spec],
        out_specs=pl.BlockSpec((None, tq, LANES), lambda bi, hp, qi: (bi, qi, hp)),
        out_shape=jax.ShapeDtypeStruct((b, s, FOX_W), BF16),
        scratch_shapes=[pltpu.VMEM((2, 2, tq, tq), F32),
                        pltpu.VMEM((2, 2, 1, tq), F32),
                        pltpu.VMEM((2, 1, tq), F32),
                        pltpu.VMEM((2, VT_ROWS, tq), F32)],
        compiler_params=pltpu.CompilerParams(dimension_semantics=("parallel", "parallel", "parallel"),
                                             vmem_limit_bytes=VMEM_LIMIT),
        name="fox",
    )(q3t, k3, vt4, knorm, cmin)


def _post_kernel(x_ref, ya_ref, yb_ref, gates_ref, p_ref, wa_ref, wb_ref, wo_ref, gm_ref, w1_ref, w2_ref,
                 gp_ref, wpg_ref, wpp_ref, gf_ref, o_ref):
    y_a = jnp.dot(ya_ref[...], wa_ref[...], preferred_element_type=F32)
    y_b = jnp.dot(yb_ref[...], wb_ref[...], preferred_element_type=F32)
    gates = gates_ref[...].astype(F32)
    mixed = gates[:, :D_MODEL] * y_a + gates[:, D_MODEL:] * y_b
    h = x_ref[...] + jnp.dot(mixed.astype(BF16), wo_ref[...], preferred_element_type=F32)
    u = _rms(h, gm_ref[...]).astype(BF16)
    acc = jnp.zeros_like(h)
    for c in range(D_FF // FF_CHUNK):
        cols = slice(c * FF_CHUNK, (c + 1) * FF_CHUNK)
        a = jnp.dot(u, w1_ref[:, cols], preferred_element_type=F32)
        a = jnp.square(jnp.maximum(a, 0.0)).astype(BF16)
        acc = acc + jnp.dot(a, w2_ref[cols, :], preferred_element_type=F32)
    h = h + acc
    gate = jax.nn.sigmoid(jnp.dot(_rms(h, gp_ref[...]).astype(BF16), wpg_ref[...], preferred_element_type=F32))
    proj = jnp.dot(p_ref[...].astype(BF16), wpp_ref[...], preferred_element_type=F32)
    h = h + gate * proj
    o_ref[...] = _rms(h, gf_ref[...])


def _post(x2, att_a, att_b, gates, p2, w_a, w_b, w_o, g_mlp, w1, w2, g_ple, w_pg, w_pp, g_final):
    t = x2.shape[0]
    row_spec = lambda w: pl.BlockSpec((TM, w), lambda i: (i, 0))
    weights = (w_a, w_b, w_o, g_mlp, w1, w2, g_ple, w_pg, w_pp, g_final)
    return pl.pallas_call(
        _post_kernel,
        grid=(t // TM,),
        in_specs=[row_spec(D_MODEL), row_spec(SWA_Q), row_spec(FOX_W), row_spec(2 * D_MODEL), row_spec(PLE_DIM)]
                 + [_const_spec(w.shape) for w in weights],
        out_specs=row_spec(D_MODEL),
        out_shape=jax.ShapeDtypeStruct((t, D_MODEL), F32),
        compiler_params=pltpu.CompilerParams(dimension_semantics=("parallel",), vmem_limit_bytes=POST_VMEM_LIMIT),
        name="post",
    )(x2, att_a, att_b, gates, p2, *weights)


def kernel(x, p, g_mix, w_in, b_forget, swa_sinks, w_br_swa, w_br_fox, w_mix_out,
           g_mlp, w_ff1, w_ff2, g_ple, w_ple_gate, w_ple_proj, g_final):
    b, s, d = x.shape
    assert d == D_MODEL and w_in.shape[0] == 1, "single-layer trunk with D_MODEL channels only"
    assert s % TM == 0 and FOX_TQ == TM and s % SWA_TILE == 0 and SWA_TILE % SWA_BLOCK == 0
    t = b * s
    x2 = x.reshape(t, d)

    w = jnp.swapaxes(w_in, 1, 2)[0]
    bf_pad = jnp.pad(b_forget[0], (0, LANES - FOX_HEADS)).reshape(1, LANES)

    later = (w_br_swa[0], w_br_fox[0], w_mix_out[0], w_ff1[0], w_ff2[0], w_ple_gate[0])
    qat, ka, vta, q3t, k3, vtb, gates, knorm, cmin, *later_bf16 = _in_proj(x2, g_mix[0].reshape(1, d), w, bf_pad, b, s, later)
    wa_b, wb_b, wo_b, w1_b, w2_b, wpg_b = later_bf16

    att_a = _swa(qat, ka.reshape(b, s, SWA_KV), vta, swa_sinks[0].reshape(1, SWA_HEADS),
                 jnp.asarray(_swa_bias_table()))
    blocks = (b, s // FOX_TQ, 1, LANES)
    att_b = _fox(q3t, k3.reshape(b, s, FOX_HEADS * FOX_DEPTH), vtb.reshape(b, FOX_HEADS, VT_ROWS, s),
                 knorm.reshape(blocks), cmin.reshape(blocks))

    out = _post(x2, att_a.reshape(t, SWA_Q), att_b.reshape(t, FOX_W), gates, p[0].reshape(t, PLE_DIM),
                wa_b, wb_b, wo_b, g_mlp[0].reshape(1, d), w1_b, w2_b,
                g_ple[0].reshape(1, d), wpg_b, w_ple_proj[0].astype(BF16), g_final.reshape(1, d))
    return out.reshape(b, s, d)
```

```python
import functools

import numpy as np
import jax
import jax.numpy as jnp
from jax import lax
from jax.experimental import pallas as pl
from jax.experimental.pallas import tpu as pltpu

D_MODEL = 1024
CHUNK = 64
PLE_DIM = 256
HEAD_DIM = 64
SWA_HEADS = 8
SWA_KV_HEADS = 2
SWA_GROUP = SWA_HEADS // SWA_KV_HEADS
WINDOW = 128
SWA_BLOCK = WINDOW
FOX_HEADS = 8
D_FF = 4 * D_MODEL
RMS_EPS = 1e-6
SWA_Q = SWA_HEADS * HEAD_DIM
SWA_KV = SWA_KV_HEADS * HEAD_DIM
FOX_W = FOX_HEADS * HEAD_DIM
COL_KA = SWA_Q
COL_VA = COL_KA + SWA_KV
COL_QB = COL_VA + SWA_KV
COL_KB = COL_QB + FOX_W
COL_VB = COL_KB + FOX_W
COL_F = COL_VB + FOX_W
COL_G = COL_F + FOX_HEADS
SCALE = HEAD_DIM ** -0.5
LOG2E = float(np.log2(np.e))

LANES = 128
BF16_SUBLANES = 16
MXU_COLS = 256
NEG_BIG = -1e30
NORM_SLACK = 1.03
FOX_SKIP_MARGIN = 152.0
VMEM_LIMIT = 52 * 1024 * 1024
IN_PROJ_VMEM_LIMIT = 58 * 1024 * 1024
POST_VMEM_LIMIT = 58 * 1024 * 1024

TM = 512
SWA_TILE = 2048
SWA_SLOTS = 6
FOX_TQ = 512
FF_CHUNK = 1024

FOX_PAIRS = FOX_HEADS // 2
FOX_DEPTH = LANES
BIAS_SLOT = 8
C_PARTS = 3
VT_ROWS = 80

F32 = jnp.float32
BF16 = jnp.bfloat16
NT_DIMS = (((1,), (1,)), ((), ()))


def _rms(x, g):
    return x * lax.rsqrt(jnp.mean(x * x, axis=-1, keepdims=True) + RMS_EPS) * g


def _const_spec(shape):
    return pl.BlockSpec(shape, lambda *_: (0,) * len(shape), pipeline_mode=pl.Buffered(1))


def _split3(v):
    hi = v.astype(BF16)
    r1 = v - hi.astype(F32)
    mid = r1.astype(BF16)
    lo = (r1 - mid.astype(F32)).astype(BF16)
    return hi, mid, lo


def _bias_placement():
    place_qt = np.zeros((LANES, C_PARTS * LANES), np.float32)
    ones_qt = np.zeros((LANES, 1), np.float32)
    place_k = np.zeros((C_PARTS * LANES, LANES), np.float32)
    ones_k = np.zeros((1, LANES), np.float32)
    for h in range(FOX_HEADS):
        for part in range(C_PARTS):
            place_qt[h * BIAS_SLOT + part, part * LANES + h] = 1.0
            ones_qt[h * BIAS_SLOT + C_PARTS + part, 0] = 1.0
            ones_k[0, h * BIAS_SLOT + part] = 1.0
            place_k[part * LANES + h, h * BIAS_SLOT + C_PARTS + part] = -1.0
    return (place_qt, ones_qt), (place_k, ones_k)


def _in_proj_kernel(*refs, tiles_per_seq, n_cast):
    (x_ref, g_ref, wlo_ref, whi_ref, bf_ref, pqt_ref, oqt_ref, pk_ref, ok_ref, vone_ref, sel_ref) = refs[:11]
    cast_in = refs[11:11 + n_cast]
    (qat_ref, ka_ref, vta_ref, q3t_ref, k3_ref, vtb_ref, gates_ref, knorm_ref, cmin_ref) = refs[11 + n_cast:20 + n_cast]
    cast_out = refs[20 + n_cast:20 + 2 * n_cast]
    carry_ref, wt_ref, wkf_ref, wk_ref, wg_ref = refs[20 + 2 * n_cast:]
    i = pl.program_id(0)
    lane = lax.broadcasted_iota(jnp.int32, (1, LANES), 1)
    gate_shift = COL_G % LANES
    for src, dst in zip(cast_in, cast_out):
        dst[...] = src[...].astype(BF16)

    @pl.when(i == 0)
    def _():
        wk_ref[...] = wlo_ref[COL_KB:COL_KB + FOX_W].T.astype(BF16)
        wkf_ref[:, :SWA_KV] = wlo_ref[COL_KA:COL_KA + SWA_KV].T.astype(BF16)
        second = jnp.concatenate([whi_ref[:gate_shift], whi_ref[2 * D_MODEL:2 * D_MODEL + gate_shift],
                                  jnp.zeros((LANES - 2 * gate_shift, D_MODEL), F32)], axis=0)
        wkf_ref[:, SWA_KV:] = second.T.astype(BF16)
        wg_ref[...] = whi_ref[:2 * D_MODEL].T.astype(BF16)
        wt_ref[:SWA_Q] = wlo_ref[:SWA_Q].astype(BF16)
        wt_ref[SWA_Q:SWA_Q + FOX_W] = wlo_ref[COL_QB:COL_QB + FOX_W].astype(BF16)
        pad_rows = jnp.zeros((VT_ROWS - HEAD_DIM, D_MODEL), BF16)
        v_rows = [COL_VA + g * HEAD_DIM for g in range(SWA_KV_HEADS)] + [COL_VB + h * HEAD_DIM for h in range(FOX_HEADS)]
        for head, row in enumerate(v_rows):
            base = SWA_Q + FOX_W + head * VT_ROWS
            wt_ref[base:base + HEAD_DIM] = wlo_ref[row:row + HEAD_DIM].astype(BF16)
            wt_ref[base + HEAD_DIM:base + VT_ROWS] = pad_rows

    @pl.when(i % tiles_per_seq == 0)
    def _():
        carry_ref[...] = jnp.zeros_like(carry_ref)

    u = _rms(x_ref[...], g_ref[...]).astype(BF16)
    tm = u.shape[0]
    kf = jnp.dot(u, wkf_ref[...], preferred_element_type=F32)
    ka_ref[...] = kf[:, :SWA_KV].astype(BF16)
    f = kf[:, SWA_KV:] + bf_ref[...]
    logf = jnp.minimum(f, 0.0) - jnp.log1p(jnp.exp(-jnp.abs(f)))

    gl = jnp.dot(u, wg_ref[...], preferred_element_type=F32)
    n_tiles = gl.shape[1] // LANES
    for c in range(n_tiles):
        cur = pltpu.roll(gl[:, c * LANES:(c + 1) * LANES], LANES - gate_shift, 1)
        if c + 1 < n_tiles:
            nxt = pltpu.roll(gl[:, (c + 1) * LANES:(c + 2) * LANES], LANES - gate_shift, 1)
        else:
            nxt = pltpu.roll(kf[:, SWA_KV:], LANES - 2 * gate_shift, 1)
        tile = jnp.where(lane < LANES - gate_shift, cur, nxt)
        gates_ref[:, c * LANES:(c + 1) * LANES] = jax.nn.sigmoid(tile).astype(BF16)

    row = lax.broadcasted_iota(jnp.int32, (tm, tm), 0)
    col = lax.broadcasted_iota(jnp.int32, (tm, tm), 1)
    tri = (col <= row).astype(BF16)
    cs = jnp.dot(tri, jnp.concatenate(_split3(logf), axis=1), preferred_element_type=F32)
    c = cs[:, :LANES] + cs[:, LANES:2 * LANES] + cs[:, 2 * LANES:] + carry_ref[...]
    carry_ref[...] = c[tm - 1:tm, :]
    cmin_ref[...] = jnp.min(c * LOG2E, axis=0, keepdims=True)

    tt = lax.dot_general(wt_ref[...], u, NT_DIMS, preferred_element_type=F32)
    qat_ref[...] = (tt[:SWA_Q] * (SCALE * LOG2E)).astype(BF16)
    qt = (tt[SWA_Q:SWA_Q + FOX_W] * (SCALE * LOG2E)).astype(BF16)
    vt = (tt[SWA_Q + FOX_W:] + vone_ref[...]).astype(BF16)
    vta_ref[...] = vt[:SWA_KV_HEADS * VT_ROWS]
    vtb_ref[...] = vt[SWA_KV_HEADS * VT_ROWS:]

    parts = jnp.concatenate(_split3(c * LOG2E), axis=1)
    kbias = jnp.dot(parts, pk_ref[...], preferred_element_type=F32) + ok_ref[...]
    qbias_t = lax.dot_general(pqt_ref[...], parts, NT_DIMS, preferred_element_type=F32) + oqt_ref[...]
    zero_rows = jnp.zeros((HEAD_DIM - BIAS_SLOT, tm), F32)
    for h in range(FOX_HEADS):
        q3t_ref[h, :HEAD_DIM, :] = qt[h * HEAD_DIM:(h + 1) * HEAD_DIM]
        own = jnp.concatenate([qbias_t[h * BIAS_SLOT:(h + 1) * BIAS_SLOT], zero_rows], axis=0)
        q3t_ref[h, HEAD_DIM:, :] = own.astype(BF16)

    k = jnp.dot(u, wk_ref[...], preferred_element_type=F32)
    k_sq = jnp.square(k.astype(BF16).astype(F32)).astype(BF16)
    knorm_ref[...] = NORM_SLACK * jnp.max(jnp.dot(k_sq, sel_ref[...], preferred_element_type=F32), axis=0, keepdims=True)
    lane = lax.broadcasted_iota(jnp.int32, (1, LANES), 1)
    for h in range(FOX_HEADS):
        pair = k[:, (h // 2) * LANES:(h // 2 + 1) * LANES]
        dims = pair if h % 2 == 0 else pltpu.roll(pair, HEAD_DIM, 1)
        bias = pltpu.roll(kbias, HEAD_DIM - h * BIAS_SLOT, 1)
        k3_ref[:, h * FOX_DEPTH:(h + 1) * FOX_DEPTH] = jnp.where(lane < HEAD_DIM, dims, bias).astype(BF16)


def _in_proj(x2, g_mix, w_all, bf_pad, batch, seq, later_weights):
    t = x2.shape[0]
    tps = seq // TM
    steps = t // TM
    (place_qt, ones_qt), (place_k, ones_k) = _bias_placement()
    n_vt = SWA_KV_HEADS + FOX_HEADS
    vone = np.zeros((n_vt * VT_ROWS, 1), np.float32)
    vone[HEAD_DIM::VT_ROWS] = 1.0
    head_of_lane = np.zeros((FOX_W, LANES), np.float32)
    head_of_lane[np.arange(FOX_W), np.arange(FOX_W) // HEAD_DIM] = 1.0
    consts = (jnp.asarray(place_qt, BF16), jnp.asarray(ones_qt), jnp.asarray(place_k, BF16), jnp.asarray(ones_k),
              jnp.asarray(vone), jnp.asarray(head_of_lane, BF16))
    kern = functools.partial(_in_proj_kernel, tiles_per_seq=tps, n_cast=len(later_weights))
    row_spec = lambda w: pl.BlockSpec((TM, w), lambda i: (i, 0))
    col_spec = lambda *lead: pl.BlockSpec((None,) + lead + (TM,), lambda i: (i // tps,) + (0,) * len(lead) + (i % tps,))
    for w in later_weights:
        assert w.ndim == 2 and w.shape[0] % (steps * BF16_SUBLANES) == 0, w.shape
    slab_specs = [pl.BlockSpec((w.shape[0] // steps, w.shape[1]), lambda i: (i, 0)) for w in later_weights]
    assert w_all.shape == (COL_G + 2 * D_MODEL, D_MODEL) and COL_F >= 2 * D_MODEL + LANES and COL_F % BF16_SUBLANES == 0
    half_specs = [pl.BlockSpec((COL_F, D_MODEL), lambda i, j=j: (j, 0), pipeline_mode=pl.Buffered(1)) for j in range(2)]
    tail = (bf_pad,) + consts
    ka, k3, gates = (jax.ShapeDtypeStruct((t, w), BF16) for w in (SWA_KV, FOX_HEADS * FOX_DEPTH, 2 * D_MODEL))
    qat = jax.ShapeDtypeStruct((batch, SWA_Q, seq), BF16)
    stat = jax.ShapeDtypeStruct((steps, 1, LANES), F32)
    stat_spec = pl.BlockSpec((None, 1, LANES), lambda i: (i, 0, 0))
    q3t = jax.ShapeDtypeStruct((batch, FOX_HEADS, FOX_DEPTH, seq), BF16)
    vta, vtb = (jax.ShapeDtypeStruct((batch, heads * VT_ROWS, seq), BF16) for heads in (SWA_KV_HEADS, FOX_HEADS))
    return pl.pallas_call(
        kern,
        grid=(steps,),
        in_specs=([row_spec(D_MODEL), _const_spec(g_mix.shape)] + half_specs
                  + [_const_spec(a.shape) for a in tail] + slab_specs),
        out_specs=[col_spec(SWA_Q), row_spec(SWA_KV), col_spec(SWA_KV_HEADS * VT_ROWS),
                   col_spec(FOX_HEADS, FOX_DEPTH), row_spec(FOX_HEADS * FOX_DEPTH), col_spec(FOX_HEADS * VT_ROWS),
                   row_spec(2 * D_MODEL), stat_spec, stat_spec] + slab_specs,
        out_shape=([qat, ka, vta, q3t, k3, vtb, gates, stat, stat]
                   + [jax.ShapeDtypeStruct(w.shape, BF16) for w in later_weights]),
        scratch_shapes=[pltpu.VMEM((1, LANES), F32),
                        pltpu.VMEM((SWA_Q + FOX_W + n_vt * VT_ROWS, D_MODEL), BF16),
                        pltpu.VMEM((D_MODEL, SWA_KV + LANES), BF16),
                        pltpu.VMEM((D_MODEL, FOX_W), BF16),
                        pltpu.VMEM((D_MODEL, 2 * D_MODEL), BF16)],
        compiler_params=pltpu.CompilerParams(dimension_semantics=("arbitrary",), vmem_limit_bytes=IN_PROJ_VMEM_LIMIT),
        name="in_proj",
    )(x2, g_mix, w_all, w_all, *tail, *later_weights)


def _swa_bias_table():
    sb = SWA_BLOCK
    qi = np.arange(sb)[None, :] + sb
    si = np.arange(2 * sb)[:, None]
    chunk_diff = qi // CHUNK - si // CHUNK
    band_ok = (chunk_diff >= 0) & (chunk_diff <= WINDOW // CHUNK)
    slopes = np.array([2.0 ** (-8.0 * (h + 1) / SWA_HEADS) for h in range(SWA_HEADS)], dtype=np.float32)
    alibi = -slopes[:, None, None] * np.abs(qi - si).astype(np.float32)[None] * np.float32(LOG2E)
    first = band_ok & (si >= sb)
    table = np.stack([np.where(first[None], alibi, NEG_BIG), np.where(band_ok[None], alibi, NEG_BIG)])
    table = table.reshape(2, SWA_KV_HEADS, SWA_GROUP, 2 * sb, sb).transpose(0, 1, 3, 2, 4)
    return np.ascontiguousarray(table.reshape(2, SWA_KV_HEADS, 2 * sb, SWA_GROUP * sb)).astype(np.float32)


def _swa_kernel(sink_ref, qt_ref, kp_ref, kc_ref, vtp_ref, vtc_ref, bias_first_ref, bias_rest_ref, o_ref,
                st_sc, mb_sc, ot_sc):
    sb = SWA_BLOCK
    per_unit = MXU_COLS // sb
    half = jnp.zeros((HEAD_DIM, MXU_COLS), BF16)
    units = [(j, kh, c) for j in range(SWA_TILE // sb) for kh in range(SWA_KV_HEADS)
             for c in range(SWA_GROUP // per_unit)]

    def band(j):
        if j == 0:
            kb = jnp.concatenate([kp_ref[...], kc_ref[0:sb, :]], axis=0)
            vtb = jnp.concatenate([vtp_ref[...], vtc_ref[:, 0:sb]], axis=1)
            return kb, vtb, bias_first_ref
        return kc_ref[(j - 1) * sb:(j + 1) * sb, :], vtc_ref[:, (j - 1) * sb:(j + 1) * sb], bias_rest_ref

    def unit_heads(kh, c):
        return [kh * SWA_GROUP + c * per_unit + g for g in range(per_unit)]

    def scores(i, slot):
        j, kh, c = units[i]
        kb, _, bias_ref = band(j)
        q2 = jnp.concatenate([qt_ref[h * HEAD_DIM:(h + 1) * HEAD_DIM, j * sb:(j + 1) * sb] for h in unit_heads(kh, c)],
                             axis=1)
        q2t = jnp.concatenate([q2, half] if kh == 0 else [half, q2], axis=0)
        st = jnp.dot(kb, q2t, preferred_element_type=F32) + bias_ref[kh, :, c * MXU_COLS:(c + 1) * MXU_COLS]
        st_sc[slot] = st
        mb_sc[slot] = jnp.max(st, axis=0, keepdims=True)

    def consume(unit, slot):
        j, kh, c = unit
        _, vtb, _ = band(j)
        heads = unit_heads(kh, c)
        sink = jnp.concatenate([jnp.full((1, sb), sink_ref[0, h] * LOG2E, F32) for h in heads], axis=1)
        m = jnp.maximum(mb_sc[slot], sink)
        p = jnp.exp2(st_sc[slot] - m).astype(BF16)
        pv = jnp.dot(vtb[kh * VT_ROWS:(kh + 1) * VT_ROWS, :], p, preferred_element_type=F32)
        denom = pv[HEAD_DIM:HEAD_DIM + 1, :] + jnp.exp2(sink - m)
        o = pv[:HEAD_DIM, :] / denom
        for g, h in enumerate(heads):
            ot_sc[h * HEAD_DIM:(h + 1) * HEAD_DIM, j * sb:(j + 1) * sb] = o[:, g * sb:(g + 1) * sb]

    slots = st_sc.shape[0]
    for i in range(slots - 1):
        scores(i, i)
    for i, unit in enumerate(units):
        ahead = i + slots - 1
        if ahead < len(units):
            scores(ahead, ahead % slots)
        consume(unit, i % slots)
    o_ref[...] = ot_sc[...].T.astype(BF16)


def _swa(qt3, k3, vt3, sinks, bias):
    b, s, _ = k3.shape
    sb, ts = SWA_BLOCK, SWA_TILE
    per = ts // sb
    prev = lambda n: jnp.maximum(n * per - 1, 0)
    bias_block = (None,) + bias.shape[1:]
    return pl.pallas_call(
        _swa_kernel,
        grid=(b, s // ts),
        in_specs=[pl.BlockSpec(memory_space=pltpu.SMEM),
                  pl.BlockSpec((None, SWA_Q, ts), lambda bi, n: (bi, 0, n)),
                  pl.BlockSpec((None, sb, SWA_KV), lambda bi, n: (bi, prev(n), 0)),
                  pl.BlockSpec((None, ts, SWA_KV), lambda bi, n: (bi, n, 0)),
                  pl.BlockSpec((None, SWA_KV_HEADS * VT_ROWS, sb), lambda bi, n: (bi, 0, prev(n))),
                  pl.BlockSpec((None, SWA_KV_HEADS * VT_ROWS, ts), lambda bi, n: (bi, 0, n)),
                  pl.BlockSpec(bias_block, lambda bi, n: (jnp.minimum(n, 1), 0, 0, 0)),
                  pl.BlockSpec(bias_block, lambda bi, n: (1, 0, 0, 0))],
        out_specs=pl.BlockSpec((None, ts, SWA_Q), lambda bi, n: (bi, n, 0)),
        out_shape=jax.ShapeDtypeStruct((b, s, SWA_Q), BF16),
        scratch_shapes=[pltpu.VMEM((SWA_SLOTS, 2 * sb, MXU_COLS), F32),
                        pltpu.VMEM((SWA_SLOTS, 1, MXU_COLS), F32),
                        pltpu.VMEM((SWA_Q, ts), F32)],
        compiler_params=pltpu.CompilerParams(dimension_semantics=("parallel", "parallel"),
                                             vmem_limit_bytes=VMEM_LIMIT),
        name="swa",
    )(sinks, qt3, k3, k3, vt3, vt3, bias, bias)


def _fox_kernel(qt_ref, k_ref, vt_ref, knorm_ref, cmin_ref, o_ref, st_sc, mb_sc, m_sc, acc_sc):
    tq = qt_ref.shape[2]
    hp = pl.program_id(1)
    qi = pl.program_id(2)
    m_sc[...] = jnp.full_like(m_sc, NEG_BIG)
    acc_sc[...] = jnp.zeros_like(acc_sc)

    units = [(hh, slice(c * MXU_COLS, (c + 1) * MXU_COLS)) for hh in range(2) for c in range(tq // MXU_COLS)]

    def scores(block, slot, unit, diagonal):
        hh, cols = unit
        keys = cols.stop if diagonal else tq
        start = pl.multiple_of(block * tq, tq)
        k3 = k_ref[pl.ds(start, keys), hh * FOX_DEPTH:(hh + 1) * FOX_DEPTH]
        st = jnp.dot(k3, qt_ref[hh, :, cols], preferred_element_type=F32)
        if diagonal:
            key = lax.broadcasted_iota(jnp.int32, st.shape, 0)
            qry = lax.broadcasted_iota(jnp.int32, st.shape, 1) + cols.start
            st = jnp.where(key <= qry, st, NEG_BIG)
        st_sc[slot, hh, :keys, cols] = st
        mb_sc[slot, hh, :, cols] = jnp.max(st, axis=0, keepdims=True)

    def consume(block, slot, unit, diagonal=False):
        hh, cols = unit
        keys = cols.stop if diagonal else tq
        start = pl.multiple_of(block * tq, tq)
        m_prev = m_sc[hh, :, cols]
        m_new = jnp.maximum(m_prev, mb_sc[slot, hh, :, cols])
        p = jnp.exp2(st_sc[slot, hh, :keys, cols] - m_new).astype(BF16)
        alpha = jnp.exp2(m_prev - m_new)
        pv = jnp.dot(vt_ref[hh, :, pl.ds(start, keys)], p, preferred_element_type=F32)
        acc_sc[hh, :, cols] = alpha * acc_sc[hh, :, cols] + pv
        m_sc[hh, :, cols] = m_new

    def stage(next_block, next_slot, block, slot, diagonal=False):
        for unit in units:
            scores(next_block, next_slot, unit, False)
            consume(block, slot, unit, diagonal)

    def skippable_blocks(hh):
        lane = lax.broadcasted_iota(jnp.int32, (1, LANES), 1)
        q32 = qt_ref[hh, :HEAD_DIM, :].astype(F32)
        q_sq = jnp.max(jnp.sum(q32 * q32, axis=0, keepdims=True), axis=1, keepdims=True)
        c_q = jnp.sum(qt_ref[hh, HEAD_DIM:HEAD_DIM + C_PARTS, :].astype(F32), axis=0, keepdims=True)
        c_first = jnp.max(c_q, axis=1, keepdims=True)
        m_min = jnp.min(mb_sc[0, hh], axis=1, keepdims=True)
        k_sq = jnp.zeros((1, LANES), F32)
        c_last = jnp.full((1, LANES), -NEG_BIG, F32)
        count = jnp.zeros((1, LANES), jnp.int32)
        for j in range(knorm_ref.shape[0]):
            k_sq = jnp.maximum(k_sq, knorm_ref[j])
            c_last = jnp.minimum(c_last, cmin_ref[j])
            room = (m_min - FOX_SKIP_MARGIN) - (c_first - c_last)
            zero = (room > 0.0) & (q_sq * k_sq < room * room) & (j < qi)
            count = count + zero.astype(jnp.int32)
        return jnp.sum(jnp.where(lane == 2 * hp + hh, count, 0))

    for unit in units:
        scores(qi, 0, unit, True)
    first = jnp.minimum(skippable_blocks(0), skippable_blocks(1))
    nearest = jnp.maximum(qi - 1, 0)
    stage(nearest, 1, qi, 0, diagonal=True)
    rest = jnp.maximum(qi - 1 - first, 0)

    @pl.when((rest == 0) & (qi >= 1))
    def _():
        for unit in units:
            consume(nearest, 1, unit)

    long_bridge = rest >= 3

    @pl.when((rest >= 1) & jnp.logical_not(long_bridge))
    def _():
        stage(first, 0, nearest, 1)
        stage(jnp.minimum(first + 1, qi), 1, first, 0)

    @pl.when(long_bridge)
    def _():
        stage(first, 0, nearest, 1)
        stage(first + 1, 1, first, 0)
        stage(first + 2, 0, first + 1, 1)
        stage(jnp.minimum(first + 3, qi), 1, first + 2, 0)

    bridged = jnp.where(long_bridge, 3, 1)
    start = first + bridged
    count = jnp.maximum(rest - bridged, 0)

    def block_pair(b0):
        stage(b0 + 1, 0, b0, 1)
        stage(jnp.minimum(b0 + 2, qi), 1, b0 + 1, 0)

    def block_oct(t, carry):
        for pair in range(4):
            block_pair(start + 8 * t + 2 * pair)
        return carry

    lax.fori_loop(0, count // 8, block_oct, 0)

    @pl.when(count % 8 >= 4)
    def _():
        block_pair(start + count // 8 * 8)
        block_pair(start + count // 8 * 8 + 2)

    @pl.when(count % 4 >= 2)
    def _():
        block_pair(start + count // 4 * 4)

    @pl.when(count % 2 == 1)
    def _():
        for unit in units:
            consume(qi - 2, 1, unit)

    outs = [acc_sc[hh, :HEAD_DIM, :] / acc_sc[hh, HEAD_DIM:HEAD_DIM + 1, :] for hh in range(2)]
    o_ref[...] = jnp.concatenate(outs, axis=0).T.astype(BF16)


def _fox(q3t, k3, vt4, knorm, cmin):
    b, s, _ = k3.shape
    tq = FOX_TQ
    assert knorm.shape == cmin.shape == (b, s // tq, 1, LANES)
    stat_spec = pl.BlockSpec((None, s // tq, 1, LANES), lambda bi, hp, qi: (bi, 0, 0, 0))
    return pl.pallas_call(
        _fox_kernel,
        grid=(b, FOX_PAIRS, s // tq),
        in_specs=[pl.BlockSpec((None, 2, FOX_DEPTH, tq), lambda bi, hp, qi: (bi, hp, 0, qi)),
                  pl.BlockSpec((None, s, 2 * FOX_DEPTH), lambda bi, hp, qi: (bi, 0, hp)),
                  pl.BlockSpec((None, 2, VT_ROWS, s), lambda bi, hp, qi: (bi, hp, 0, 0)),
                  stat_spec, stat_spec],
        out_specs=pl.BlockSpec((None, tq, LANES), lambda bi, hp, qi: (bi, qi, hp)),
        out_shape=jax.ShapeDtypeStruct((b, s, FOX_W), BF16),
        scratch_shapes=[pltpu.VMEM((2, 2, tq, tq), F32),
                        pltpu.VMEM((2, 2, 1, tq), F32),
                        pltpu.VMEM((2, 1, tq), F32),
                        pltpu.VMEM((2, VT_ROWS, tq), F32)],
        compiler_params=pltpu.CompilerParams(dimension_semantics=("parallel", "parallel", "parallel"),
                                             vmem_limit_bytes=VMEM_LIMIT),
        name="fox",
    )(q3t, k3, vt4, knorm, cmin)


def _post_kernel(x_ref, ya_ref, yb_ref, gates_ref, p_ref, wa_ref, wb_ref, wo_ref, gm_ref, w1_ref, w2_ref,
                 gp_ref, wpg_ref, wpp_ref, gf_ref, o_ref):
    y_a = jnp.dot(ya_ref[...], wa_ref[...], preferred_element_type=F32)
    y_b = jnp.dot(yb_ref[...], wb_ref[...], preferred_element_type=F32)
    gates = gates_ref[...].astype(F32)
    mixed = gates[:, :D_MODEL] * y_a + gates[:, D_MODEL:] * y_b
    h = x_ref[...] + jnp.dot(mixed.astype(BF16), wo_ref[...], preferred_element_type=F32)
    u = _rms(h, gm_ref[...]).astype(BF16)
    acc = jnp.zeros_like(h)
    for c in range(D_FF // FF_CHUNK):
        cols = slice(c * FF_CHUNK, (c + 1) * FF_CHUNK)
        a = jnp.dot(u, w1_ref[:, cols], preferred_element_type=F32)
        a = jnp.square(jnp.maximum(a, 0.0)).astype(BF16)
        acc = acc + jnp.dot(a, w2_ref[cols, :], preferred_element_type=F32)
    h = h + acc
    gate = jax.nn.sigmoid(jnp.dot(_rms(h, gp_ref[...]).astype(BF16), wpg_ref[...], preferred_element_type=F32))
    proj = jnp.dot(p_ref[...].astype(BF16), wpp_ref[...], preferred_element_type=F32)
    h = h + gate * proj
    o_ref[...] = _rms(h, gf_ref[...])


def _post(x2, att_a, att_b, gates, p2, w_a, w_b, w_o, g_mlp, w1, w2, g_ple, w_pg, w_pp, g_final):
    t = x2.shape[0]
    row_spec = lambda w: pl.BlockSpec((TM, w), lambda i: (i, 0))
    weights = (w_a, w_b, w_o, g_mlp, w1, w2, g_ple, w_pg, w_pp, g_final)
    return pl.pallas_call(
        _post_kernel,
        grid=(t // TM,),
        in_specs=[row_spec(D_MODEL), row_spec(SWA_Q), row_spec(FOX_W), row_spec(2 * D_MODEL), row_spec(PLE_DIM)]
                 + [_const_spec(w.shape) for w in weights],
        out_specs=row_spec(D_MODEL),
        out_shape=jax.ShapeDtypeStruct((t, D_MODEL), F32),
        compiler_params=pltpu.CompilerParams(dimension_semantics=("parallel",), vmem_limit_bytes=POST_VMEM_LIMIT),
        name="post",
    )(x2, att_a, att_b, gates, p2, *weights)


def kernel(x, p, g_mix, w_in, b_forget, swa_sinks, w_br_swa, w_br_fox, w_mix_out,
           g_mlp, w_ff1, w_ff2, g_ple, w_ple_gate, w_ple_proj, g_final):
    b, s, d = x.shape
    assert d == D_MODEL and w_in.shape[0] == 1, "single-layer trunk with D_MODEL channels only"
    assert s % TM == 0 and FOX_TQ == TM and s % SWA_TILE == 0 and SWA_TILE % SWA_BLOCK == 0
    t = b * s
    x2 = x.reshape(t, d)

    w = jnp.swapaxes(w_in, 1, 2)[0]
    bf_pad = jnp.pad(b_forget[0], (0, LANES - FOX_HEADS)).reshape(1, LANES)

    later = (w_br_swa[0], w_br_fox[0], w_mix_out[0], w_ff1[0], w_ff2[0], w_ple_gate[0])
    qat, ka, vta, q3t, k3, vtb, gates, knorm, cmin, *later_bf16 = _in_proj(x2, g_mix[0].reshape(1, d), w, bf_pad, b, s, later)
    wa_b, wb_b, wo_b, w1_b, w2_b, wpg_b = later_bf16

    att_a = _swa(qat, ka.reshape(b, s, SWA_KV), vta, swa_sinks[0].reshape(1, SWA_HEADS),
                 jnp.asarray(_swa_bias_table()))
    blocks = (b, s // FOX_TQ, 1, LANES)
    att_b = _fox(q3t, k3.reshape(b, s, FOX_HEADS * FOX_DEPTH), vtb.reshape(b, FOX_HEADS, VT_ROWS, s),
                 knorm.reshape(blocks), cmin.reshape(blocks))

    out = _post(x2, att_a.reshape(t, SWA_Q), att_b.reshape(t, FOX_W), gates, p[0].reshape(t, PLE_DIM),
                wa_b, wb_b, wo_b, g_mlp[0].reshape(1, d), w1_b, w2_b,
                g_ple[0].reshape(1, d), wpg_b, w_ple_proj[0].astype(BF16), g_final.reshape(1, d))
    return out.reshape(b, s, d)
```

```python
import functools

import numpy as np
import jax
import jax.numpy as jnp
from jax import lax
from jax.experimental import pallas as pl
from jax.experimental.pallas import tpu as pltpu

D_MODEL = 1024
CHUNK = 64
PLE_DIM = 256
HEAD_DIM = 64
SWA_HEADS = 8
SWA_KV_HEADS = 2
SWA_GROUP = SWA_HEADS // SWA_KV_HEADS
WINDOW = 128
SWA_BLOCK = WINDOW
FOX_HEADS = 8
D_FF = 4 * D_MODEL
RMS_EPS = 1e-6
SWA_Q = SWA_HEADS * HEAD_DIM
SWA_KV = SWA_KV_HEADS * HEAD_DIM
FOX_W = FOX_HEADS * HEAD_DIM
COL_KA = SWA_Q
COL_VA = COL_KA + SWA_KV
COL_QB = COL_VA + SWA_KV
COL_KB = COL_QB + FOX_W
COL_VB = COL_KB + FOX_W
COL_F = COL_VB + FOX_W
COL_G = COL_F + FOX_HEADS
SCALE = HEAD_DIM ** -0.5
LOG2E = float(np.log2(np.e))

LANES = 128
BF16_SUBLANES = 16
MXU_COLS = 256
NEG_BIG = -1e30
NORM_SLACK = 1.03
FOX_SKIP_MARGIN = 152.0
VMEM_LIMIT = 52 * 1024 * 1024
IN_PROJ_VMEM_LIMIT = 58 * 1024 * 1024
POST_VMEM_LIMIT = 58 * 1024 * 1024

TM = 512
SWA_TILE = 2048
SWA_SLOTS = 6
FOX_TQ = 512
FF_CHUNK = 1024

FOX_PAIRS = FOX_HEADS // 2
FOX_DEPTH = LANES
BIAS_SLOT = 8
C_PARTS = 3
VT_ROWS = 80

F32 = jnp.float32
BF16 = jnp.bfloat16
NT_DIMS = (((1,), (1,)), ((), ()))


def _rms(x, g):
    return x * lax.rsqrt(jnp.mean(x * x, axis=-1, keepdims=True) + RMS_EPS) * g


def _const_spec(shape):
    return pl.BlockSpec(shape, lambda *_: (0,) * len(shape), pipeline_mode=pl.Buffered(1))


def _split3(v):
    hi = v.astype(BF16)
    r1 = v - hi.astype(F32)
    mid = r1.astype(BF16)
    lo = (r1 - mid.astype(F32)).astype(BF16)
    return hi, mid, lo


def _bias_placement():
    place_qt = np.zeros((LANES, C_PARTS * LANES), np.float32)
    ones_qt = np.zeros((LANES, 1), np.float32)
    place_k = np.zeros((C_PARTS * LANES, LANES), np.float32)
    ones_k = np.zeros((1, LANES), np.float32)
    for h in range(FOX_HEADS):
        for part in range(C_PARTS):
            place_qt[h * BIAS_SLOT + part, part * LANES + h] = 1.0
            ones_qt[h * BIAS_SLOT + C_PARTS + part, 0] = 1.0
            ones_k[0, h * BIAS_SLOT + part] = 1.0
            place_k[part * LANES + h, h * BIAS_SLOT + C_PARTS + part] = -1.0
    return (place_qt, ones_qt), (place_k, ones_k)


def _in_proj_kernel(*refs, tiles_per_seq, n_cast):
    (x_ref, g_ref, wlo_ref, whi_ref, bf_ref, pqt_ref, oqt_ref, pk_ref, ok_ref, vone_ref, sel_ref) = refs[:11]
    cast_in = refs[11:11 + n_cast]
    (qat_ref, ka_ref, vta_ref, q3t_ref, k3_ref, vtb_ref, gates_ref, knorm_ref, cmin_ref) = refs[11 + n_cast:20 + n_cast]
    cast_out = refs[20 + n_cast:20 + 2 * n_cast]
    carry_ref, wt_ref, wkf_ref, wk_ref, wg_ref = refs[20 + 2 * n_cast:]
    i = pl.program_id(0)
    lane = lax.broadcasted_iota(jnp.int32, (1, LANES), 1)
    gate_shift = COL_G % LANES
    for src, dst in zip(cast_in, cast_out):
        dst[...] = src[...].astype(BF16)

    @pl.when(i == 0)
    def _():
        wk_ref[...] = wlo_ref[COL_KB:COL_KB + FOX_W].T.astype(BF16)
        wkf_ref[:, :SWA_KV] = wlo_ref[COL_KA:COL_KA + SWA_KV].T.astype(BF16)
        second = jnp.concatenate([whi_ref[:gate_shift], whi_ref[2 * D_MODEL:2 * D_MODEL + gate_shift],
                                  jnp.zeros((LANES - 2 * gate_shift, D_MODEL), F32)], axis=0)
        wkf_ref[:, SWA_KV:] = second.T.astype(BF16)
        wg_ref[...] = whi_ref[:2 * D_MODEL].T.astype(BF16)
        wt_ref[:SWA_Q] = wlo_ref[:SWA_Q].astype(BF16)
        wt_ref[SWA_Q:SWA_Q + FOX_W] = wlo_ref[COL_QB:COL_QB + FOX_W].astype(BF16)
        pad_rows = jnp.zeros((VT_ROWS - HEAD_DIM, D_MODEL), BF16)
        v_rows = [COL_VA + g * HEAD_DIM for g in range(SWA_KV_HEADS)] + [COL_VB + h * HEAD_DIM for h in range(FOX_HEADS)]
        for head, row in enumerate(v_rows):
            base = SWA_Q + FOX_W + head * VT_ROWS
            wt_ref[base:base + HEAD_DIM] = wlo_ref[row:row + HEAD_DIM].astype(BF16)
            wt_ref[base + HEAD_DIM:base + VT_ROWS] = pad_rows

    @pl.when(i % tiles_per_seq == 0)
    def _():
        carry_ref[...] = jnp.zeros_like(carry_ref)

    u = _rms(x_ref[...], g_ref[...]).astype(BF16)
    tm = u.shape[0]
    kf = jnp.dot(u, wkf_ref[...], preferred_element_type=F32)
    ka_ref[...] = kf[:, :SWA_KV].astype(BF16)
    f = kf[:, SWA_KV:] + bf_ref[...]
    logf = jnp.minimum(f, 0.0) - jnp.log1p(jnp.exp(-jnp.abs(f)))

    gl = jnp.dot(u, wg_ref[...], preferred_element_type=F32)
    n_tiles = gl.shape[1] // LANES
    for c in range(n_tiles):
        cur = pltpu.roll(gl[:, c * LANES:(c + 1) * LANES], LANES - gate_shift, 1)
        if c + 1 < n_tiles:
            nxt = pltpu.roll(gl[:, (c + 1) * LANES:(c + 2) * LANES], LANES - gate_shift, 1)
        else:
            nxt = pltpu.roll(kf[:, SWA_KV:], LANES - 2 * gate_shift, 1)
        tile = jnp.where(lane < LANES - gate_shift, cur, nxt)
        gates_ref[:, c * LANES:(c + 1) * LANES] = jax.nn.sigmoid(tile).astype(BF16)

    row = lax.broadcasted_iota(jnp.int32, (tm, tm), 0)
    col = lax.broadcasted_iota(jnp.int32, (tm, tm), 1)
    tri = (col <= row).astype(BF16)
    cs = jnp.dot(tri, jnp.concatenate(_split3(logf), axis=1), preferred_element_type=F32)
    c = cs[:, :LANES] + cs[:, LANES:2 * LANES] + cs[:, 2 * LANES:] + carry_ref[...]
    carry_ref[...] = c[tm - 1:tm, :]
    cmin_ref[...] = jnp.min(c * LOG2E, axis=0, keepdims=True)

    tt = lax.dot_general(wt_ref[...], u, NT_DIMS, preferred_element_type=F32)
    qat_ref[...] = (tt[:SWA_Q] * (SCALE * LOG2E)).astype(BF16)
    qt = (tt[SWA_Q:SWA_Q + FOX_W] * (SCALE * LOG2E)).astype(BF16)
    vt = (tt[SWA_Q + FOX_W:] + vone_ref[...]).astype(BF16)
    vta_ref[...] = vt[:SWA_KV_HEADS * VT_ROWS]
    vtb_ref[...] = vt[SWA_KV_HEADS * VT_ROWS:]

    parts = jnp.concatenate(_split3(c * LOG2E), axis=1)
    kbias = jnp.dot(parts, pk_ref[...], preferred_element_type=F32) + ok_ref[...]
    qbias_t = lax.dot_general(pqt_ref[...], parts, NT_DIMS, preferred_element_type=F32) + oqt_ref[...]
    zero_rows = jnp.zeros((HEAD_DIM - BIAS_SLOT, tm), F32)
    for h in range(FOX_HEADS):
        q3t_ref[h, :HEAD_DIM, :] = qt[h * HEAD_DIM:(h + 1) * HEAD_DIM]
        own = jnp.concatenate([qbias_t[h * BIAS_SLOT:(h + 1) * BIAS_SLOT], zero_rows], axis=0)
        q3t_ref[h, HEAD_DIM:, :] = own.astype(BF16)

    k = jnp.dot(u, wk_ref[...], preferred_element_type=F32)
    k_sq = jnp.square(k.astype(BF16).astype(F32)).astype(BF16)
    knorm_ref[...] = NORM_SLACK * jnp.max(jnp.dot(k_sq, sel_ref[...], preferred_element_type=F32), axis=0, keepdims=True)
    lane = lax.broadcasted_iota(jnp.int32, (1, LANES), 1)
    for h in range(FOX_HEADS):
        pair = k[:, (h // 2) * LANES:(h // 2 + 1) * LANES]
        dims = pair if h % 2 == 0 else pltpu.roll(pair, HEAD_DIM, 1)
        bias = pltpu.roll(kbias, HEAD_DIM - h * BIAS_SLOT, 1)
        k3_ref[:, h * FOX_DEPTH:(h + 1) * FOX_DEPTH] = jnp.where(lane < HEAD_DIM, dims, bias).astype(BF16)


def _in_proj(x2, g_mix, w_all, bf_pad, batch, seq, later_weights):
    t = x2.shape[0]
    tps = seq // TM
    steps = t // TM
    (place_qt, ones_qt), (place_k, ones_k) = _bias_placement()
    n_vt = SWA_KV_HEADS + FOX_HEADS
    vone = np.zeros((n_vt * VT_ROWS, 1), np.float32)
    vone[HEAD_DIM::VT_ROWS] = 1.0
    head_of_lane = np.zeros((FOX_W, LANES), np.float32)
    head_of_lane[np.arange(FOX_W), np.arange(FOX_W) // HEAD_DIM] = 1.0
    consts = (jnp.asarray(place_qt, BF16), jnp.asarray(ones_qt), jnp.asarray(place_k, BF16), jnp.asarray(ones_k),
              jnp.asarray(vone), jnp.asarray(head_of_lane, BF16))
    kern = functools.partial(_in_proj_kernel, tiles_per_seq=tps, n_cast=len(later_weights))
    row_spec = lambda w: pl.BlockSpec((TM, w), lambda i: (i, 0))
    col_spec = lambda *lead: pl.BlockSpec((None,) + lead + (TM,), lambda i: (i // tps,) + (0,) * len(lead) + (i % tps,))
    for w in later_weights:
        assert w.ndim == 2 and w.shape[0] % (steps * BF16_SUBLANES) == 0, w.shape
    slab_specs = [pl.BlockSpec((w.shape[0] // steps, w.shape[1]), lambda i: (i, 0)) for w in later_weights]
    assert w_all.shape == (COL_G + 2 * D_MODEL, D_MODEL) and COL_F >= 2 * D_MODEL + LANES and COL_F % BF16_SUBLANES == 0
    half_specs = [pl.BlockSpec((COL_F, D_MODEL), lambda i, j=j: (j, 0), pipeline_mode=pl.Buffered(1)) for j in range(2)]
    tail = (bf_pad,) + consts
    ka, k3, gates = (jax.ShapeDtypeStruct((t, w), BF16) for w in (SWA_KV, FOX_HEADS * FOX_DEPTH, 2 * D_MODEL))
    qat = jax.ShapeDtypeStruct((batch, SWA_Q, seq), BF16)
    stat = jax.ShapeDtypeStruct((steps, 1, LANES), F32)
    stat_spec = pl.BlockSpec((None, 1, LANES), lambda i: (i, 0, 0))
    q3t = jax.ShapeDtypeStruct((batch, FOX_HEADS, FOX_DEPTH, seq), BF16)
    vta, vtb = (jax.ShapeDtypeStruct((batch, heads * VT_ROWS, seq), BF16) for heads in (SWA_KV_HEADS, FOX_HEADS))
    return pl.pallas_call(
        kern,
        grid=(steps,),
        in_specs=([row_spec(D_MODEL), _const_spec(g_mix.shape)] + half_specs
                  + [_const_spec(a.shape) for a in tail] + slab_specs),
        out_specs=[col_spec(SWA_Q), row_spec(SWA_KV), col_spec(SWA_KV_HEADS * VT_ROWS),
                   col_spec(FOX_HEADS, FOX_DEPTH), row_spec(FOX_HEADS * FOX_DEPTH), col_spec(FOX_HEADS * VT_ROWS),
                   row_spec(2 * D_MODEL), stat_spec, stat_spec] + slab_specs,
        out_shape=([qat, ka, vta, q3t, k3, vtb, gates, stat, stat]
                   + [jax.ShapeDtypeStruct(w.shape, BF16) for w in later_weights]),
        scratch_shapes=[pltpu.VMEM((1, LANES), F32),
                        pltpu.VMEM((SWA_Q + FOX_W + n_vt * VT_ROWS, D_MODEL), BF16),
                        pltpu.VMEM((D_MODEL, SWA_KV + LANES), BF16),
                        pltpu.VMEM((D_MODEL, FOX_W), BF16),
                        pltpu.VMEM((D_MODEL, 2 * D_MODEL), BF16)],
        compiler_params=pltpu.CompilerParams(dimension_semantics=("arbitrary",), vmem_limit_bytes=IN_PROJ_VMEM_LIMIT),
        name="in_proj",
    )(x2, g_mix, w_all, w_all, *tail, *later_weights)


def _swa_bias_table():
    sb = SWA_BLOCK
    qi = np.arange(sb)[None, :] + sb
    si = np.arange(2 * sb)[:, None]
    chunk_diff = qi // CHUNK - si // CHUNK
    band_ok = (chunk_diff >= 0) & (chunk_diff <= WINDOW // CHUNK)
    slopes = np.array([2.0 ** (-8.0 * (h + 1) / SWA_HEADS) for h in range(SWA_HEADS)], dtype=np.float32)
    alibi = -slopes[:, None, None] * np.abs(qi - si).astype(np.float32)[None] * np.float32(LOG2E)
    first = band_ok & (si >= sb)
    table = np.stack([np.where(first[None], alibi, NEG_BIG), np.where(band_ok[None], alibi, NEG_BIG)])
    table = table.reshape(2, SWA_KV_HEADS, SWA_GROUP, 2 * sb, sb).transpose(0, 1, 3, 2, 4)
    return np.ascontiguousarray(table.reshape(2, SWA_KV_HEADS, 2 * sb, SWA_GROUP * sb)).astype(np.float32)


def _swa_kernel(sink_ref, qt_ref, kp_ref, kc_ref, vtp_ref, vtc_ref, bias_first_ref, bias_rest_ref, o_ref,
                st_sc, mb_sc, ot_sc):
    sb = SWA_BLOCK
    per_unit = MXU_COLS // sb
    half = jnp.zeros((HEAD_DIM, MXU_COLS), BF16)
    units = [(j, kh, c) for j in range(SWA_TILE // sb) for kh in range(SWA_KV_HEADS)
             for c in range(SWA_GROUP // per_unit)]

    def band(j):
        if j == 0:
            kb = jnp.concatenate([kp_ref[...], kc_ref[0:sb, :]], axis=0)
            vtb = jnp.concatenate([vtp_ref[...], vtc_ref[:, 0:sb]], axis=1)
            return kb, vtb, bias_first_ref
        return kc_ref[(j - 1) * sb:(j + 1) * sb, :], vtc_ref[:, (j - 1) * sb:(j + 1) * sb], bias_rest_ref

    def unit_heads(kh, c):
        return [kh * SWA_GROUP + c * per_unit + g for g in range(per_unit)]

    def scores(i, slot):
        j, kh, c = units[i]
        kb, _, bias_ref = band(j)
        q2 = jnp.concatenate([qt_ref[h * HEAD_DIM:(h + 1) * HEAD_DIM, j * sb:(j + 1) * sb] for h in unit_heads(kh, c)],
                             axis=1)
        q2t = jnp.concatenate([q2, half] if kh == 0 else [half, q2], axis=0)
        st = jnp.dot(kb, q2t, preferred_element_type=F32) + bias_ref[kh, :, c * MXU_COLS:(c + 1) * MXU_COLS]
        st_sc[slot] = st
        mb_sc[slot] = jnp.max(st, axis=0, keepdims=True)

    def consume(unit, slot):
        j, kh, c = unit
        _, vtb, _ = band(j)
        heads = unit_heads(kh, c)
        sink = jnp.concatenate([jnp.full((1, sb), sink_ref[0, h] * LOG2E, F32) for h in heads], axis=1)
        m = jnp.maximum(mb_sc[slot], sink)
        p = jnp.exp2(st_sc[slot] - m).astype(BF16)
        pv = jnp.dot(vtb[kh * VT_ROWS:(kh + 1) * VT_ROWS, :], p, preferred_element_type=F32)
        denom = pv[HEAD_DIM:HEAD_DIM + 1, :] + jnp.exp2(sink - m)
        o = pv[:HEAD_DIM, :] / denom
        for g, h in enumerate(heads):
            ot_sc[h * HEAD_DIM:(h + 1) * HEAD_DIM, j * sb:(j + 1) * sb] = o[:, g * sb:(g + 1) * sb]

    slots = st_sc.shape[0]
    for i in range(slots - 1):
        scores(i, i)
    for i, unit in enumerate(units):
        ahead = i + slots - 1
        if ahead < len(units):
            scores(ahead, ahead % slots)
        consume(unit, i % slots)
    o_ref[...] = ot_sc[...].T.astype(BF16)


def _swa(qt3, k3, vt3, sinks, bias):
    b, s, _ = k3.shape
    sb, ts = SWA_BLOCK, SWA_TILE
    per = ts // sb
    prev = lambda n: jnp.maximum(n * per - 1, 0)
    bias_block = (None,) + bias.shape[1:]
    return pl.pallas_call(
        _swa_kernel,
        grid=(b, s // ts),
        in_specs=[pl.BlockSpec(memory_space=pltpu.SMEM),
                  pl.BlockSpec((None, SWA_Q, ts), lambda bi, n: (bi, 0, n)),
                  pl.BlockSpec((None, sb, SWA_KV), lambda bi, n: (bi, prev(n), 0)),
                  pl.BlockSpec((None, ts, SWA_KV), lambda bi, n: (bi, n, 0)),
                  pl.BlockSpec((None, SWA_KV_HEADS * VT_ROWS, sb), lambda bi, n: (bi, 0, prev(n))),
                  pl.BlockSpec((None, SWA_KV_HEADS * VT_ROWS, ts), lambda bi, n: (bi, 0, n)),
                  pl.BlockSpec(bias_block, lambda bi, n: (jnp.minimum(n, 1), 0, 0, 0)),
                  pl.BlockSpec(bias_block, lambda bi, n: (1, 0, 0, 0))],
        out_specs=pl.BlockSpec((None, ts, SWA_Q), lambda bi, n: (bi, n, 0)),
        out_shape=jax.ShapeDtypeStruct((b, s, SWA_Q), BF16),
        scratch_shapes=[pltpu.VMEM((SWA_SLOTS, 2 * sb, MXU_COLS), F32),
                        pltpu.VMEM((SWA_SLOTS, 1, MXU_COLS), F32),
                        pltpu.VMEM((SWA_Q, ts), F32)],
        compiler_params=pltpu.CompilerParams(dimension_semantics=("parallel", "parallel"),
                                             vmem_limit_bytes=VMEM_LIMIT),
        name="swa",
    )(sinks, qt3, k3, k3, vt3, vt3, bias, bias)


def _fox_kernel(qt_ref, k_ref, vt_ref, knorm_ref, cmin_ref, o_ref, st_sc, mb_sc, m_sc, acc_sc):
    tq = qt_ref.shape[2]
    hp = pl.program_id(1)
    qi = pl.program_id(2)
    m_sc[...] = jnp.full_like(m_sc, NEG_BIG)
    acc_sc[...] = jnp.zeros_like(acc_sc)

    units = [(hh, slice(c * MXU_COLS, (c + 1) * MXU_COLS)) for hh in range(2) for c in range(tq // MXU_COLS)]

    def scores(block, slot, unit, diagonal):
        hh, cols = unit
        keys = cols.stop if diagonal else tq
        start = pl.multiple_of(block * tq, tq)
        k3 = k_ref[pl.ds(start, keys), hh * FOX_DEPTH:(hh + 1) * FOX_DEPTH]
        st = jnp.dot(k3, qt_ref[hh, :, cols], preferred_element_type=F32)
        if diagonal:
            key = lax.broadcasted_iota(jnp.int32, st.shape, 0)
            qry = lax.broadcasted_iota(jnp.int32, st.shape, 1) + cols.start
            st = jnp.where(key <= qry, st, NEG_BIG)
        st_sc[slot, hh, :keys, cols] = st
        mb_sc[slot, hh, :, cols] = jnp.max(st, axis=0, keepdims=True)

    def consume(block, slot, unit, diagonal=False):
        hh, cols = unit
        keys = cols.stop if diagonal else tq
        start = pl.multiple_of(block * tq, tq)
        m_prev = m_sc[hh, :, cols]
        m_new = jnp.maximum(m_prev, mb_sc[slot, hh, :, cols])
        p = jnp.exp2(st_sc[slot, hh, :keys, cols] - m_new).astype(BF16)
        alpha = jnp.exp2(m_prev - m_new)
        pv = jnp.dot(vt_ref[hh, :, pl.ds(start, keys)], p, preferred_element_type=F32)
        acc_sc[hh, :, cols] = alpha * acc_sc[hh, :, cols] + pv
        m_sc[hh, :, cols] = m_new

    def stage(next_block, next_slot, block, slot, diagonal=False):
        for unit in units:
            scores(next_block, next_slot, unit, False)
            consume(block, slot, unit, diagonal)

    def skippable_blocks(hh):
        lane = lax.broadcasted_iota(jnp.int32, (1, LANES), 1)
        q32 = qt_ref[hh, :HEAD_DIM, :].astype(F32)
        q_sq = jnp.max(jnp.sum(q32 * q32, axis=0, keepdims=True), axis=1, keepdims=True)
        c_q = jnp.sum(qt_ref[hh, HEAD_DIM:HEAD_DIM + C_PARTS, :].astype(F32), axis=0, keepdims=True)
        c_first = jnp.max(c_q, axis=1, keepdims=True)
        m_min = jnp.min(mb_sc[0, hh], axis=1, keepdims=True)
        k_sq = jnp.zeros((1, LANES), F32)
        c_last = jnp.full((1, LANES), -NEG_BIG, F32)
        count = jnp.zeros((1, LANES), jnp.int32)
        for j in range(knorm_ref.shape[0]):
            k_sq = jnp.maximum(k_sq, knorm_ref[j])
            c_last = jnp.minimum(c_last, cmin_ref[j])
            room = (m_min - FOX_SKIP_MARGIN) - (c_first - c_last)
            zero = (room > 0.0) & (q_sq * k_sq < room * room) & (j < qi)
            count = count + zero.astype(jnp.int32)
        return jnp.sum(jnp.where(lane == 2 * hp + hh, count, 0))

    for unit in units:
        scores(qi, 0, unit, True)
    first = jnp.minimum(skippable_blocks(0), skippable_blocks(1))
    nearest = jnp.maximum(qi - 1, 0)
    stage(nearest, 1, qi, 0, diagonal=True)
    rest = jnp.maximum(qi - 1 - first, 0)

    @pl.when((rest == 0) & (qi >= 1))
    def _():
        for unit in units:
            consume(nearest, 1, unit)

    long_bridge = rest >= 3

    @pl.when((rest >= 1) & jnp.logical_not(long_bridge))
    def _():
        stage(first, 0, nearest, 1)
        stage(jnp.minimum(first + 1, qi), 1, first, 0)

    @pl.when(long_bridge)
    def _():
        stage(first, 0, nearest, 1)
        stage(first + 1, 1, first, 0)
        stage(first + 2, 0, first + 1, 1)
        stage(jnp.minimum(first + 3, qi), 1, first + 2, 0)

    bridged = jnp.where(long_bridge, 3, 1)
    start = first + bridged
    count = jnp.maximum(rest - bridged, 0)

    def block_pair(b0):
        stage(b0 + 1, 0, b0, 1)
        stage(jnp.minimum(b0 + 2, qi), 1, b0 + 1, 0)

    def block_oct(t, carry):
        for pair in range(4):
            block_pair(start + 8 * t + 2 * pair)
        return carry

    lax.fori_loop(0, count // 8, block_oct, 0)

    @pl.when(count % 8 >= 4)
    def _():
        block_pair(start + count // 8 * 8)
        block_pair(start + count // 8 * 8 + 2)

    @pl.when(count % 4 >= 2)
    def _():
        block_pair(start + count // 4 * 4)

    @pl.when(count % 2 == 1)
    def _():
        for unit in units:
            consume(qi - 2, 1, unit)

    outs = [acc_sc[hh, :HEAD_DIM, :] / acc_sc[hh, HEAD_DIM:HEAD_DIM + 1, :] for hh in range(2)]
    o_ref[...] = jnp.concatenate(outs, axis=0).astype(BF16)


def _fox(q3t, k3, vt4, knorm, cmin):
    b, s, _ = k3.shape
    tq = FOX_TQ
    assert knorm.shape == cmin.shape == (b, s // tq, 1, LANES)
    stat_spec = pl.BlockSpec((None, s // tq, 1, LANES), lambda bi, hp, qi: (bi, 0, 0, 0))
    return pl.pallas_call(
        _fox_kernel,
        grid=(b, FOX_PAIRS, s // tq),
        in_specs=[pl.BlockSpec((None, 2, FOX_DEPTH, tq), lambda bi, hp, qi: (bi, hp, 0, qi)),
                  pl.BlockSpec((None, s, 2 * FOX_DEPTH), lambda bi, hp, qi: (bi, 0, hp)),
                  pl.BlockSpec((None, 2, VT_ROWS, s), lambda bi, hp, qi: (bi, hp, 0, 0)),
                  stat_spec, stat_spec],
        out_specs=pl.BlockSpec((None, LANES, tq), lambda bi, hp, qi: (bi, hp, qi)),
        out_shape=jax.ShapeDtypeStruct((b, FOX_W, s), BF16),
        scratch_shapes=[pltpu.VMEM((2, 2, tq, tq), F32),
                        pltpu.VMEM((2, 2, 1, tq), F32),
                        pltpu.VMEM((2, 1, tq), F32),
                        pltpu.VMEM((2, VT_ROWS, tq), F32)],
        compiler_params=pltpu.CompilerParams(dimension_semantics=("parallel", "parallel", "parallel"),
                                             vmem_limit_bytes=VMEM_LIMIT),
        name="fox",
    )(q3t, k3, vt4, knorm, cmin)


def _post_kernel(x_ref, ya_ref, ybt_ref, gates_ref, p_ref, wa_ref, wb_ref, wo_ref, gm_ref, w1_ref, w2_ref,
                 gp_ref, wpg_ref, wpp_ref, gf_ref, o_ref):
    y_a = jnp.dot(ya_ref[...], wa_ref[...], preferred_element_type=F32)
    y_b = jnp.dot(ybt_ref[...].astype(F32).T.astype(BF16), wb_ref[...], preferred_element_type=F32)
    gates = gates_ref[...].astype(F32)
    mixed = gates[:, :D_MODEL] * y_a + gates[:, D_MODEL:] * y_b
    h = x_ref[...] + jnp.dot(mixed.astype(BF16), wo_ref[...], preferred_element_type=F32)
    u = _rms(h, gm_ref[...]).astype(BF16)
    acc = jnp.zeros_like(h)
    for c in range(D_FF // FF_CHUNK):
        cols = slice(c * FF_CHUNK, (c + 1) * FF_CHUNK)
        a = jnp.dot(u, w1_ref[:, cols], preferred_element_type=F32)
        a = jnp.square(jnp.maximum(a, 0.0)).astype(BF16)
        acc = acc + jnp.dot(a, w2_ref[cols, :], preferred_element_type=F32)
    h = h + acc
    gate = jax.nn.sigmoid(jnp.dot(_rms(h, gp_ref[...]).astype(BF16), wpg_ref[...], preferred_element_type=F32))
    proj = jnp.dot(p_ref[...].astype(BF16), wpp_ref[...], preferred_element_type=F32)
    h = h + gate * proj
    o_ref[...] = _rms(h, gf_ref[...])


def _post(x2, att_a, att_bt, gates, p2, w_a, w_b, w_o, g_mlp, w1, w2, g_ple, w_pg, w_pp, g_final):
    t = x2.shape[0]
    tps = att_bt.shape[2] // TM
    row_spec = lambda w: pl.BlockSpec((TM, w), lambda i: (i, 0))
    bt_spec = pl.BlockSpec((None, FOX_W, TM), lambda i: (i // tps, 0, i % tps))
    weights = (w_a, w_b, w_o, g_mlp, w1, w2, g_ple, w_pg, w_pp, g_final)
    return pl.pallas_call(
        _post_kernel,
        grid=(t // TM,),
        in_specs=[row_spec(D_MODEL), row_spec(SWA_Q), bt_spec, row_spec(2 * D_MODEL), row_spec(PLE_DIM)]
                 + [_const_spec(w.shape) for w in weights],
        out_specs=row_spec(D_MODEL),
        out_shape=jax.ShapeDtypeStruct((t, D_MODEL), F32),
        compiler_params=pltpu.CompilerParams(dimension_semantics=("parallel",), vmem_limit_bytes=POST_VMEM_LIMIT),
        name="post",
    )(x2, att_a, att_bt, gates, p2, *weights)


def kernel(x, p, g_mix, w_in, b_forget, swa_sinks, w_br_swa, w_br_fox, w_mix_out,
           g_mlp, w_ff1, w_ff2, g_ple, w_ple_gate, w_ple_proj, g_final):
    b, s, d = x.shape
    assert d == D_MODEL and w_in.shape[0] == 1, "single-layer trunk with D_MODEL channels only"
    assert s % TM == 0 and FOX_TQ == TM and s % SWA_TILE == 0 and SWA_TILE % SWA_BLOCK == 0
    t = b * s
    x2 = x.reshape(t, d)

    w = jnp.swapaxes(w_in, 1, 2)[0]
    bf_pad = jnp.pad(b_forget[0], (0, LANES - FOX_HEADS)).reshape(1, LANES)

    later = (w_br_swa[0], w_br_fox[0], w_mix_out[0], w_ff1[0], w_ff2[0], w_ple_gate[0])
    qat, ka, vta, q3t, k3, vtb, gates, knorm, cmin, *later_bf16 = _in_proj(x2, g_mix[0].reshape(1, d), w, bf_pad, b, s, later)
    wa_b, wb_b, wo_b, w1_b, w2_b, wpg_b = later_bf16

    att_a = _swa(qat, ka.reshape(b, s, SWA_KV), vta, swa_sinks[0].reshape(1, SWA_HEADS),
                 jnp.asarray(_swa_bias_table()))
    blocks = (b, s // FOX_TQ, 1, LANES)
    att_b = _fox(q3t, k3.reshape(b, s, FOX_HEADS * FOX_DEPTH), vtb.reshape(b, FOX_HEADS, VT_ROWS, s),
                 knorm.reshape(blocks), cmin.reshape(blocks))

    out = _post(x2, att_a.reshape(t, SWA_Q), att_b, gates, p[0].reshape(t, PLE_DIM),
                wa_b, wb_b, wo_b, g_mlp[0].reshape(1, d), w1_b, w2_b,
                g_ple[0].reshape(1, d), wpg_b, w_ple_proj[0].astype(BF16), g_final.reshape(1, d))
    return out.reshape(b, s, d)
```
